```python
import jax, jax.numpy as jnp
from jax import lax
import numpy as np

D_MODEL = 1024
BATCH = 8
SEQ = 2048
DEPTH = 2

HEAD_DIM = 64
N_HEADS_FOX = 6
N_HEADS_MOBA = 6
DILATED_PAIRS = ((128, 1), (512, 4), (2048, 16))
N_SLOTS_DIL = 4
N_HEADS_DIL = N_SLOTS_DIL * len(DILATED_PAIRS)
N_HEADS = N_HEADS_FOX + N_HEADS_MOBA + N_HEADS_DIL
MIX_WIDTH = N_HEADS * HEAD_DIM
N_BRANCH = 3
IN_COLS = 3 * MIX_WIDTH + N_HEADS_FOX + N_BRANCH * D_MODEL
FOX_BLOCK = 128
MOBA_BLOCK = 256
MOBA_TOPK = 3
MOBA_Q_CHUNK = 32
WIN_BLOCK = 128
N_ALIBI = N_HEADS_MOBA + N_HEADS_DIL
DIL_SLOPE_OFFSETS = (0, N_SLOTS_DIL, 2 * N_SLOTS_DIL + N_HEADS_MOBA)
MOBA_SLOPE_OFFSET = 2 * N_SLOTS_DIL
D_FF = 2816
N_EXPERTS = 8
TOP_K = 2
D_FF_EXPERT = 2816
N_DENSE = (DEPTH + 1) // 2
N_MOE = DEPTH // 2
RMS_EPS = 1e-6
NEG_INF = -1e30

kernel_name = 'hybrid_fox_moba_dilated_moe_adaln'


def _rmsnorm(x, g):
    xf = x.astype(jnp.float32)
    y = xf * lax.rsqrt(jnp.mean(xf * xf, axis=-1, keepdims=True) + RMS_EPS)
    return (y * g.astype(jnp.float32)).astype(x.dtype)


def _alibi_slopes(n):
    return jnp.asarray(2.0 ** (-8.0 * np.arange(1, n + 1) / n), dtype=jnp.float32)


def _merge_heads(o):
    B, H, S, dh = o.shape
    return o.transpose(0, 2, 1, 3).reshape(B, S, H * dh)


def _fox_attention(q, k, v, f_logit):
    B, H, S, dh = q.shape
    cum = jnp.cumsum(jax.nn.log_sigmoid(f_logit.astype(jnp.float32)), axis=1).transpose(0, 2, 1)
    nqb = S // FOX_BLOCK
    qb = q.reshape(B, H, nqb, FOX_BLOCK, dh).transpose(2, 0, 1, 3, 4)
    cb = cum.reshape(B, H, nqb, FOX_BLOCK).transpose(2, 0, 1, 3)
    key_pos = jnp.arange(S)
    scale = dh ** -0.5

    def block(args):
        i, q_i, c_i = args
        s = jnp.einsum('bhqd,bhkd->bhqk', q_i, k, preferred_element_type=jnp.float32) * scale
        s = s + c_i[..., :, None] - cum[:, :, None, :]
        qpos = i * FOX_BLOCK + jnp.arange(FOX_BLOCK)
        s = jnp.where(key_pos[None, :] <= qpos[:, None], s, NEG_INF)
        p = jax.nn.softmax(s, axis=-1)
        return jnp.einsum('bhqk,bhkd->bhqd', p.astype(v.dtype), v)

    out = lax.map(block, (jnp.arange(nqb), qb, cb))
    return out.transpose(1, 2, 0, 3, 4).reshape(B, H, S, dh)


def _moba_attention(q, k, v, slopes):
    B, H, S, dh = q.shape
    Sp = -(-S // MOBA_BLOCK) * MOBA_BLOCK
    pad = ((0, 0), (0, 0), (0, Sp - S), (0, 0))
    q, k, v = (jnp.pad(t, pad) for t in (q, k, v))
    nb = Sp // MOBA_BLOCK
    n_sel = min(MOBA_TOPK, nb)
    kb = k.reshape(B, H, nb, MOBA_BLOCK, dh)
    vb = v.reshape(B, H, nb, MOBA_BLOCK, dh)
    k_mean = jnp.mean(kb.astype(jnp.float32), axis=3)
    gate = jnp.einsum('bhsd,bhnd->bhsn', q.astype(jnp.float32), k_mean)
    q_blk = jnp.arange(Sp) // MOBA_BLOCK
    gate = jnp.where(jnp.arange(nb)[None, :] < q_blk[:, None], gate, NEG_INF)
    _, sel = lax.top_k(gate, n_sel)
    sel_ok = sel < q_blk[:, None]
    scale = dh ** -0.5
    b_idx = jnp.arange(B)[:, None, None, None]
    h_idx = jnp.arange(H)[None, :, None, None]
    blk_pos = jnp.arange(MOBA_BLOCK)

    def chunk(i):
        start = i * MOBA_Q_CHUNK
        q_c = lax.dynamic_slice_in_dim(q, start, MOBA_Q_CHUNK, axis=2)
        sel_c = lax.dynamic_slice_in_dim(sel, start, MOBA_Q_CHUNK, axis=2)
        ok_c = lax.dynamic_slice_in_dim(sel_ok, start, MOBA_Q_CHUNK, axis=2)
        qpos = start + jnp.arange(MOBA_Q_CHUNK)
        own = start // MOBA_BLOCK
        k_own = lax.dynamic_slice_in_dim(k, own * MOBA_BLOCK, MOBA_BLOCK, axis=2)
        v_own = lax.dynamic_slice_in_dim(v, own * MOBA_BLOCK, MOBA_BLOCK, axis=2)
        dist_own = (qpos[:, None] - (own * MOBA_BLOCK + blk_pos)[None, :])
        s_own = jnp.einsum('bhcd,bhld->bhcl', q_c, k_own, preferred_element_type=jnp.float32) * scale
        s_own = s_own - slopes[:, None, None] * dist_own.astype(jnp.float32)
        s_own = jnp.where(dist_own >= 0, s_own, NEG_INF)
        k_sel = kb[b_idx, h_idx, sel_c]
        v_sel = vb[b_idx, h_idx, sel_c]
        s_sel = jnp.einsum('bhcd,bhcnld->bhcnl', q_c, k_sel, preferred_element_type=jnp.float32) * scale
        dist_sel = qpos[:, None, None] - (sel_c[..., None] * MOBA_BLOCK + blk_pos)
        s_sel = s_sel - slopes[:, None, None, None] * dist_sel.astype(jnp.float32)
        s_sel = jnp.where(ok_c[..., None], s_sel, NEG_INF)
        s = jnp.concatenate([s_own, s_sel.reshape(B, H, MOBA_Q_CHUNK, n_sel * MOBA_BLOCK)], axis=-1)
        p = jax.nn.softmax(s, axis=-1).astype(v.dtype)
        p_own = p[..., :MOBA_BLOCK]
        p_sel = p[..., MOBA_BLOCK:].reshape(B, H, MOBA_Q_CHUNK, n_sel, MOBA_BLOCK)
        return (jnp.einsum('bhcl,bhld->bhcd', p_own, v_own)
                + jnp.einsum('bhcnl,bhcnld->bhcd', p_sel, v_sel))

    out = lax.map(chunk, jnp.arange(Sp // MOBA_Q_CHUNK))
    return out.transpose(1, 2, 0, 3, 4).reshape(B, H, Sp, dh)[:, :, :S]


def _dilated_group(q, k, v, slopes, window, dil):
    B, H, S, dh = q.shape
    n_back = window // dil
    L = S // dil
    Lp = -(-L // WIN_BLOCK) * WIN_BLOCK
    nblk = Lp // WIN_BLOCK

    def to_blocks(t):
        t = t.reshape(B, H, L, dil, dh).transpose(0, 1, 3, 2, 4)
        t = jnp.pad(t, ((0, 0), (0, 0), (0, 0), (0, Lp - L), (0, 0)))
        return t.reshape(B, H, dil, nblk, WIN_BLOCK, dh)

    def band(t):
        prev = jnp.concatenate([jnp.zeros_like(t[:, :, :, :1]), t[:, :, :, :-1]], axis=3)
        return jnp.concatenate([prev, t], axis=4)

    qb = to_blocks(q)
    kband = band(to_blocks(k))
    vband = band(to_blocks(v))
    s = jnp.einsum('bhrnqd,bhrnkd->bhrnqk', qb, kband, preferred_element_type=jnp.float32) * (dh ** -0.5)
    steps = (jnp.arange(WIN_BLOCK)[:, None] + WIN_BLOCK) - jnp.arange(2 * WIN_BLOCK)[None, :]
    key_sub = (jnp.arange(nblk)[:, None, None] - 1) * WIN_BLOCK + jnp.arange(2 * WIN_BLOCK)[None, None, :]
    valid = (steps >= 0) & (steps <= n_back) & (key_sub >= 0)
    s = s - slopes[:, None, None, None, None] * (steps * dil).astype(jnp.float32)
    s = jnp.where(valid, s, NEG_INF)
    lse = jax.nn.logsumexp(s, axis=-1)
    p = jnp.exp(s - lse[..., None])
    o = jnp.einsum('bhrnqk,bhrnkd->bhrnqd', p.astype(v.dtype), vband)

    def from_blocks(t):
        t = t.reshape((B, H, dil, Lp) + t.shape[5:])[:, :, :, :L]
        t = jnp.moveaxis(t, 2, 3)
        return t.reshape((B, H, S) + t.shape[4:])

    return from_blocks(o), from_blocks(lse)


def _token_mixer(h, w_in, b_fgate, q_gain, k_gain, w_br_fox, w_br_moba, w_br_dil, w_out):
    B, S, _ = h.shape
    proj = h @ w_in
    qkv = proj[..., :3 * MIX_WIDTH].reshape(B, S, 3, N_HEADS, HEAD_DIM)
    f_logit = proj[..., 3 * MIX_WIDTH:3 * MIX_WIDTH + N_HEADS_FOX] + b_fgate
    gates = jax.nn.sigmoid(proj[..., 3 * MIX_WIDTH + N_HEADS_FOX:].astype(jnp.float32))
    gates = gates.reshape(B, S, N_BRANCH, D_MODEL).astype(h.dtype)
    q = _rmsnorm(qkv[:, :, 0], q_gain).transpose(0, 2, 1, 3)
    k = _rmsnorm(qkv[:, :, 1], k_gain).transpose(0, 2, 1, 3)
    v = qkv[:, :, 2].transpose(0, 2, 1, 3)
    slopes = _alibi_slopes(N_ALIBI)

    a1 = N_HEADS_FOX
    b1 = a1 + N_HEADS_MOBA
    o_fox = _fox_attention(q[:, :a1], k[:, :a1], v[:, :a1], f_logit)
    o_moba = _moba_attention(q[:, a1:b1], k[:, a1:b1], v[:, a1:b1],
                             slopes[MOBA_SLOPE_OFFSET:MOBA_SLOPE_OFFSET + N_HEADS_MOBA])
    outs, lses = [], []
    for g, (window, dil) in enumerate(DILATED_PAIRS):
        lo = b1 + g * N_SLOTS_DIL
        so = DIL_SLOPE_OFFSETS[g]
        o_g, l_g = _dilated_group(q[:, lo:lo + N_SLOTS_DIL], k[:, lo:lo + N_SLOTS_DIL],
                                  v[:, lo:lo + N_SLOTS_DIL], slopes[so:so + N_SLOTS_DIL], window, dil)
        outs.append(o_g)
        lses.append(l_g)
    w_dil = jax.nn.softmax(jnp.stack(lses, axis=0), axis=0)
    o_dil = jnp.einsum('gbhs,gbhsd->bhsd', w_dil.astype(v.dtype), jnp.stack(outs, axis=0))

    y = (gates[:, :, 0] * (_merge_heads(o_fox) @ w_br_fox)
         + gates[:, :, 1] * (_merge_heads(o_moba) @ w_br_moba)
         + gates[:, :, 2] * (_merge_heads(o_dil) @ w_br_dil))
    return y @ w_out


def _swiglu(h, w_gate, w_up, w_down):
    return (jax.nn.silu(h @ w_gate) * (h @ w_up)) @ w_down


def _moe_swiglu(h, w_router, b_router, w_gate, w_up, w_down):
    logits = (h @ w_router).astype(jnp.float32) + b_router.astype(jnp.float32)
    top_val, top_idx = lax.top_k(logits, TOP_K)
    top_w = jax.nn.softmax(top_val, axis=-1)
    combine = jnp.sum(jax.nn.one_hot(top_idx, N_EXPERTS, dtype=jnp.float32) * top_w[..., None], axis=-2)
    combine = combine.astype(h.dtype)
    out = jnp.zeros_like(h)
    for e in range(N_EXPERTS):
        out = out + combine[..., e:e + 1] * _swiglu(h, w_gate[e], w_up[e], w_down[e])
    return out


def setup_inputs(seed: int = 0) -> dict:
    key = jax.random.key(seed)
    ks = jax.random.split(key, 22)
    D = D_MODEL

    def nrm(k, shape, scale):
        return jax.random.normal(k, shape, jnp.float32) * scale

    wf = N_HEADS_FOX * HEAD_DIM
    wm = N_HEADS_MOBA * HEAD_DIM
    wd = N_SLOTS_DIL * HEAD_DIM
    return {
        'x': nrm(ks[0], (BATCH, SEQ, D), 1.0),
        'c': nrm(ks[1], (BATCH, D), 1.0),
        'w_ada': nrm(ks[2], (DEPTH, D, 6 * D), 0.5 * D ** -0.5),
        'b_ada': nrm(ks[3], (DEPTH, 6 * D), 0.05),
        'norm_mix': 1.0 + nrm(ks[4], (DEPTH, D), 0.05),
        'norm_ffn': 1.0 + nrm(ks[5], (DEPTH, D), 0.05),
        'w_in': nrm(ks[6], (DEPTH, D, IN_COLS), D ** -0.5),
        'b_fgate': jax.random.uniform(ks[7], (DEPTH, N_HEADS_FOX), jnp.float32, 1.0, 6.0),
        'q_gain': 1.0 + nrm(ks[8], (DEPTH, N_HEADS, HEAD_DIM), 0.05),
        'k_gain': 1.0 + nrm(ks[9], (DEPTH, N_HEADS, HEAD_DIM), 0.05),
        'w_br_fox': nrm(ks[10], (DEPTH, wf, D), wf ** -0.5),
        'w_br_moba': nrm(ks[11], (DEPTH, wm, D), wm ** -0.5),
        'w_br_dil': nrm(ks[12], (DEPTH, wd, D), wd ** -0.5),
        'w_out': nrm(ks[13], (DEPTH, D, D), D ** -0.5),
        'w_ffn_gate': nrm(ks[14], (N_DENSE, D, D_FF), D ** -0.5),
        'w_ffn_up': nrm(ks[15], (N_DENSE, D, D_FF), D ** -0.5),
        'w_ffn_down': nrm(ks[16], (N_DENSE, D_FF, D), D_FF ** -0.5),
        'w_router': nrm(ks[17], (N_MOE, D, N_EXPERTS), D ** -0.5),
        'b_router': nrm(ks[18], (N_MOE, N_EXPERTS), 0.01),
        'w_exp_gate': nrm(ks[19], (N_MOE, N_EXPERTS, D, D_FF_EXPERT), D ** -0.5),
        'w_exp_up': nrm(ks[20], (N_MOE, N_EXPERTS, D, D_FF_EXPERT), D ** -0.5),
        'w_exp_down': nrm(ks[21], (N_MOE, N_EXPERTS, D_FF_EXPERT, D), D_FF_EXPERT ** -0.5),
    }


def reference(x, c, w_ada, b_ada, norm_mix, norm_ffn, w_in, b_fgate, q_gain, k_gain,
              w_br_fox, w_br_moba, w_br_dil, w_out, w_ffn_gate, w_ffn_up, w_ffn_down,
              w_router, b_router, w_exp_gate, w_exp_up, w_exp_down):
    cond = jax.nn.silu(c)
    for l in range(DEPTH):
        mod = (cond @ w_ada[l] + b_ada[l])[:, None, :]
        shift1, scale1, gate1, shift2, scale2, gate2 = jnp.split(mod, 6, axis=-1)
        h = _rmsnorm(x, norm_mix[l]) * (1 + scale1) + shift1
        x = x + gate1 * _token_mixer(h, w_in[l], b_fgate[l], q_gain[l], k_gain[l],
                                     w_br_fox[l], w_br_moba[l], w_br_dil[l], w_out[l])
        h = _rmsnorm(x, norm_ffn[l]) * (1 + scale2) + shift2
        i = l // 2
        if l % 2 == 0:
            y = _swiglu(h, w_ffn_gate[i], w_ffn_up[i], w_ffn_down[i])
        else:
            y = _moe_swiglu(h, w_router[i], b_router[i], w_exp_gate[i], w_exp_up[i], w_exp_down[i])
        x = x + gate2 * y
    return x
```

```python
import functools

import numpy as np
import jax
import jax.numpy as jnp
from jax import lax
from jax.experimental import pallas as pl
from jax.experimental.pallas import tpu as pltpu

HEAD_DIM = 64
N_HEADS_FOX = 6
N_HEADS_MOBA = 6
DILATED_PAIRS = ((128, 1), (512, 4), (2048, 16))
N_SLOTS_DIL = 4
N_HEADS_DIL = N_SLOTS_DIL * len(DILATED_PAIRS)
N_HEADS = N_HEADS_FOX + N_HEADS_MOBA + N_HEADS_DIL
MIX_WIDTH = N_HEADS * HEAD_DIM
N_BRANCH = 3
MOBA_BLOCK = 256
MOBA_TOPK = 3
WIN_BLOCK = 128
N_ALIBI = N_HEADS_MOBA + N_HEADS_DIL
DIL_SLOPE_OFFSETS = (0, N_SLOTS_DIL, 2 * N_SLOTS_DIL + N_HEADS_MOBA)
MOBA_SLOPE_OFFSET = 2 * N_SLOTS_DIL
TOP_K = 2
RMS_EPS = 1e-6
NEG_INF = -1e30

LANES = 128
PAIR = 2 * HEAD_DIM
VMEM_LIMIT = 56 * 1024 * 1024

F32 = jnp.float32
BF16 = jnp.bfloat16
_NT = (((1,), (1,)), ((), ()))


def _alibi_slopes(n):
    return (2.0 ** (-8.0 * np.arange(1, n + 1) / n)).astype(np.float32)


def _sigmoid(x):
    return 1.0 / (1.0 + jnp.exp(-x))


def _params(*sem):
    return pltpu.CompilerParams(dimension_semantics=sem, vmem_limit_bytes=VMEM_LIMIT)


def _adaln_kernel(c_ref, w_ref, b_ref, o_ref):
    c = c_ref[...]
    cond = c * _sigmoid(c)
    o_ref[0] = jnp.dot(cond, w_ref[0], precision=lax.Precision.HIGHEST,
                       preferred_element_type=F32) + b_ref[0]


def _adaln(c, w_ada, b_ada):
    depth, d, n = w_ada.shape
    b = c.shape[0]
    tn = 1536
    return pl.pallas_call(
        _adaln_kernel,
        grid=(depth, n // tn),
        in_specs=[pl.BlockSpec((b, d), lambda l, j: (0, 0)),
                  pl.BlockSpec((1, d, tn), lambda l, j: (l, 0, j)),
                  pl.BlockSpec((1, 1, tn), lambda l, j: (l, 0, j))],
        out_specs=pl.BlockSpec((1, b, tn), lambda l, j: (l, 0, j)),
        out_shape=jax.ShapeDtypeStruct((depth, b, n), F32),
        compiler_params=_params("arbitrary", "arbitrary"),
        name="adaln",
    )(c, w_ada, b_ada.reshape(depth, 1, n))


def _mod_norm(x, g, scale, shift):
    ms = jnp.mean(x * x, axis=-1, keepdims=True)
    return (x * lax.rsqrt(ms + RMS_EPS) * g) * (1.0 + scale) + shift


def _inproj_kernel(x_ref, mod_ref, g_ref, w_ref, qg_ref, kg_ref, bf_ref, e_ref,
                   q_ref, k_ref, v_ref, qd1, kd1, vd1, qd2, kd2, vd2, qd3, kd3, vd3,
                   gate_ref, lf_ref, *, n_main):
    h = _mod_norm(x_ref[...], g_ref[...], mod_ref[0, 1:2, :], mod_ref[0, 0:1, :]).astype(BF16)
    qk_chunk = 4 * HEAD_DIM
    q_outs = (q_ref, qd1, qd2, qd3)
    k_outs = (k_ref, kd1, kd2, kd3)
    v_outs = (v_ref, vd1, vd2, vd3)

    def dst(outs, col):
        if col < n_main:
            return outs[0], col
        g = (col - n_main) // qk_chunk
        return outs[1 + g], (col - n_main) % qk_chunk

    def normed(col0, gain_ref, gcol, scale):
        y = jnp.dot(h, w_ref[:, col0:col0 + qk_chunk], preferred_element_type=F32)
        ss = jnp.dot((y * y).astype(BF16), e_ref[...], preferred_element_type=F32)
        r = lax.rsqrt(ss * (1.0 / HEAD_DIM) + RMS_EPS)
        return (y * r * gain_ref[:, gcol:gcol + qk_chunk]) * scale

    for c in range(MIX_WIDTH // qk_chunk):
        col = c * qk_chunk
        ref, rc = dst(q_outs, col)
        ref[:, rc:rc + qk_chunk] = normed(col, qg_ref, col, HEAD_DIM ** -0.5).astype(BF16)
        ref, rc = dst(k_outs, col)
        ref[:, rc:rc + qk_chunk] = normed(MIX_WIDTH + col, kg_ref, col, 1.0).astype(BF16)
        ref, rc = dst(v_outs, col)
        y = jnp.dot(h, w_ref[:, 2 * MIX_WIDTH + col:2 * MIX_WIDTH + col + qk_chunk],
                    preferred_element_type=F32)
        ref[:, rc:rc + qk_chunk] = y.astype(BF16)

    d_model = x_ref.shape[1]
    g0 = 3 * MIX_WIDTH
    gchunk = 512
    for c in range(N_BRANCH * d_model // gchunk):
        y = jnp.dot(h, w_ref[:, g0 + c * gchunk:g0 + (c + 1) * gchunk], preferred_element_type=F32)
        gate_ref[:, c * gchunk:(c + 1) * gchunk] = _sigmoid(y).astype(BF16)

    f0 = g0 + N_BRANCH * d_model
    f = jnp.dot(h, w_ref[:, f0:f0 + LANES], preferred_element_type=F32) + bf_ref[...]
    lf_ref[...] = jnp.minimum(f, 0.0) - jnp.log(1.0 + jnp.exp(-jnp.abs(f)))


def _inproj(x2, mod_l, g_norm, w_all, q_gain, k_gain, b_f, seq, tm):
    t, d = x2.shape
    n_main = (N_HEADS_FOX + N_HEADS_MOBA) * HEAD_DIM
    dil_w = N_SLOTS_DIL * HEAD_DIM
    e = (np.arange(dil_w)[:, None] // HEAD_DIM == np.arange(dil_w)[None, :] // HEAD_DIM)
    e = jnp.asarray(e, BF16)
    per_b = seq // tm
    row = lambda i: (i, 0)
    fix = lambda i: (0, 0)
    qkv_shapes = []
    qkv_specs = []
    for w in (n_main, dil_w, dil_w, dil_w):
        for _ in range(3):
            qkv_shapes.append(jax.ShapeDtypeStruct((t, w), BF16))
            qkv_specs.append(pl.BlockSpec((tm, w), row))
    out_shapes = qkv_shapes + [jax.ShapeDtypeStruct((t, N_BRANCH * d), BF16),
                               jax.ShapeDtypeStruct((t, LANES), F32)]
    out_specs = qkv_specs + [pl.BlockSpec((tm, N_BRANCH * d), row), pl.BlockSpec((tm, LANES), row)]
    return pl.pallas_call(
        functools.partial(_inproj_kernel, n_main=n_main),
        grid=(t // tm,),
        in_specs=[pl.BlockSpec((tm, d), row),
                  pl.BlockSpec((1, 6, d), lambda i: (i // per_b, 0, 0)),
                  pl.BlockSpec((1, d), fix),
                  pl.BlockSpec(w_all.shape, fix, pipeline_mode=pl.Buffered(1)),
                  pl.BlockSpec((1, MIX_WIDTH), fix),
                  pl.BlockSpec((1, MIX_WIDTH), fix),
                  pl.BlockSpec((1, LANES), fix),
                  pl.BlockSpec(e.shape, fix)],
        out_specs=out_specs,
        out_shape=out_shapes,
        compiler_params=_params("arbitrary"),
        name="inproj",
    )(x2, mod_l, g_norm, w_all, q_gain, k_gain, b_f, e)


def _decay_kernel(lf_ref, u_ref, o_ref, *, blk):
    lf_t = lf_ref[0].T
    seq = lf_t.shape[1]
    carry = jnp.zeros((8, 1), F32)
    for j in range(seq // blk):
        c = jnp.dot(lf_t[0:8, j * blk:(j + 1) * blk], u_ref[...], precision=lax.Precision.HIGHEST,
                    preferred_element_type=F32) + carry
        o_ref[0, :, j * blk:(j + 1) * blk] = c
        carry = c[:, blk - 1:blk]


def _decay(lf, blk=256):
    b, seq, _ = lf.shape
    u = jnp.asarray(np.arange(blk)[:, None] <= np.arange(blk)[None, :], F32)
    return pl.pallas_call(
        functools.partial(_decay_kernel, blk=blk),
        grid=(b,),
        in_specs=[pl.BlockSpec((1, seq, LANES), lambda i: (i, 0, 0)),
                  pl.BlockSpec((blk, blk), lambda i: (0, 0))],
        out_specs=pl.BlockSpec((1, 8, seq), lambda i: (i, 0, 0)),
        out_shape=jax.ShapeDtypeStruct((b, 8, seq), F32),
        compiler_params=_params("arbitrary"),
        name="fox_decay",
    )(lf, u)


def _head_masks():
    lane = lax.broadcasted_iota(jnp.int32, (1, PAIR), 1)
    return lane < HEAD_DIM, lane >= HEAD_DIM


def _online_update(s, vb, m_scr, l_scr, acc_scr):
    m_prev = m_scr[...]
    m_new = jnp.maximum(m_prev, jnp.max(s, axis=-1, keepdims=True))
    p = jnp.exp(s - m_new)
    alpha = jnp.exp(m_prev - m_new)
    l_scr[...] = alpha * l_scr[...] + jnp.sum(p, axis=-1, keepdims=True)
    acc_scr[...] = alpha * acc_scr[...] + jnp.dot(p.astype(BF16), vb, preferred_element_type=F32)
    m_scr[...] = m_new


def _reset(m_scr, l_scr, acc_scr):
    m_scr[...] = jnp.full(m_scr.shape, NEG_INF, F32)
    l_scr[...] = jnp.zeros(l_scr.shape, F32)
    acc_scr[...] = jnp.zeros(acc_scr.shape, F32)


def _fox_kernel(q_ref, k_ref, v_ref, ck_ref, o_ref, m_scr, l_scr, acc_scr, *, tq, tk):
    i = pl.program_id(2)
    q = q_ref[0]
    n_full = (i * tq) // tk
    n_all = ((i + 1) * tq + tk - 1) // tk
    row = lax.broadcasted_iota(jnp.int32, (tq, tk), 0) + i * tq
    col = lax.broadcasted_iota(jnp.int32, (tq, tk), 1)
    outs = []
    for hh, hmask in enumerate(_head_masks()):
        qh = jnp.where(hmask, q, jnp.zeros_like(q))
        _reset(m_scr, l_scr, acc_scr)

        def step(j, carry, masked):
            k0 = pl.multiple_of(j * tk, tk)
            s = lax.dot_general(qh, k_ref[0, pl.ds(k0, tk), :], _NT, preferred_element_type=F32)
            s = s - ck_ref[0, 0, j][hh:hh + 1, :]
            if masked:
                s = jnp.where(col + j * tk <= row, s, NEG_INF)
            _online_update(s, v_ref[0, pl.ds(k0, tk), :], m_scr, l_scr, acc_scr)
            return carry

        lax.fori_loop(0, n_full, functools.partial(step, masked=False), 0)
        lax.fori_loop(n_full, n_all, functools.partial(step, masked=True), 0)
        outs.append(acc_scr[...] / l_scr[...])
    o_ref[0] = jnp.where(_head_masks()[0], outs[0], outs[1]).astype(BF16)


def _fox(q, k, v, ck, col0, n_pairs, tq=256, tk=256):
    b, seq, _ = q.shape
    return pl.pallas_call(
        functools.partial(_fox_kernel, tq=tq, tk=tk),
        grid=(b, n_pairs, seq // tq),
        in_specs=[pl.BlockSpec((1, tq, PAIR), lambda bi, p, i: (bi, i, col0 + p)),
                  pl.BlockSpec((1, seq, PAIR), lambda bi, p, i: (bi, 0, col0 + p)),
                  pl.BlockSpec((1, seq, PAIR), lambda bi, p, i: (bi, 0, col0 + p)),
                  pl.BlockSpec((1, 1, seq // tk, 2, tk), lambda bi, p, i: (bi, p, 0, 0, 0))],
        out_specs=pl.BlockSpec((1, tq, PAIR), lambda bi, p, i: (bi, i, p)),
        out_shape=jax.ShapeDtypeStruct((b, seq, n_pairs * PAIR), BF16),
        scratch_shapes=[pltpu.VMEM((tq, 1), F32), pltpu.VMEM((tq, 1), F32), pltpu.VMEM((tq, PAIR), F32)],
        compiler_params=_params("arbitrary", "arbitrary", "arbitrary"),
        name="fox_attn",
    )(q, k, v, ck)


def _moba_kernel(q_ref, k_ref, v_ref, sl_ref, o_ref, km_scr, sel_scr, m_scr, l_scr, acc_scr):
    blk = MOBA_BLOCK
    i = pl.program_id(2)
    n_blk = k_ref.shape[1] // blk

    @pl.when(i == 0)
    def _():
        km_scr[...] = jnp.zeros(km_scr.shape, F32)
        for n in range(n_blk):
            kb = k_ref[0, n * blk:(n + 1) * blk, :].astype(F32)
            km_scr[n:n + 1, :] = jnp.sum(kb, axis=0, keepdims=True) * (1.0 / blk)

    q = q_ref[0]
    rel = (lax.broadcasted_iota(jnp.int32, (blk, blk), 0)
           - lax.broadcasted_iota(jnp.int32, (blk, blk), 1))
    rel_f = rel.astype(F32)
    lane_n = lax.broadcasted_iota(jnp.int32, (blk, LANES), 1)
    k_own = k_ref[0, pl.ds(pl.multiple_of(i * blk, blk), blk), :]
    v_own = v_ref[0, pl.ds(pl.multiple_of(i * blk, blk), blk), :]
    outs = []
    for hh, hmask in enumerate(_head_masks()):
        qh = jnp.where(hmask, q, jnp.zeros_like(q))
        slope = sl_ref[0, hh:hh + 1, :]
        srel = slope * rel_f

        g = lax.dot_general(qh.astype(F32), km_scr[...], _NT, precision=lax.Precision.HIGHEST,
                            preferred_element_type=F32)
        cnt = jnp.zeros((blk, LANES), F32)
        for m in range(n_blk):
            gm = g[:, m:m + 1]
            beats = (gm > g) | ((gm == g) & (lane_n > m))
            cnt = cnt + jnp.where(beats, (m < i).astype(F32), 0.0)
        sel_scr[...] = jnp.where((lane_n < i) & (cnt < float(MOBA_TOPK)), 1.0, 0.0)

        _reset(m_scr, l_scr, acc_scr)
        s = lax.dot_general(qh, k_own, _NT, preferred_element_type=F32) - srel
        _online_update(jnp.where(rel >= 0, s, NEG_INF), v_own, m_scr, l_scr, acc_scr)

        def step(j, carry):
            k0 = pl.multiple_of(j * blk, blk)
            picked = jnp.sum(jnp.where(lane_n == j, sel_scr[...], 0.0), axis=-1, keepdims=True)
            off = ((i - j) * blk).astype(F32)
            s = lax.dot_general(qh, k_ref[0, pl.ds(k0, blk), :], _NT, preferred_element_type=F32)
            s = s - srel - slope * off
            s = jnp.where(picked > 0.5, s, NEG_INF)
            _online_update(s, v_ref[0, pl.ds(k0, blk), :], m_scr, l_scr, acc_scr)
            return carry

        lax.fori_loop(0, i, step, 0)
        outs.append(acc_scr[...] / l_scr[...])
    o_ref[0] = jnp.where(_head_masks()[0], outs[0], outs[1]).astype(BF16)


def _moba(q, k, v, col0, n_pairs):
    b, seq, _ = q.shape
    blk = MOBA_BLOCK
    assert seq % blk == 0 and seq // blk <= 8
    slopes = _alibi_slopes(N_ALIBI)[MOBA_SLOPE_OFFSET:MOBA_SLOPE_OFFSET + 2 * n_pairs]
    sl = jnp.asarray(np.broadcast_to(slopes.reshape(n_pairs, 2, 1), (n_pairs, 2, blk)).copy())
    return pl.pallas_call(
        _moba_kernel,
        grid=(b, n_pairs, seq // blk),
        in_specs=[pl.BlockSpec((1, blk, PAIR), lambda bi, p, i: (bi, i, col0 + p)),
                  pl.BlockSpec((1, seq, PAIR), lambda bi, p, i: (bi, 0, col0 + p)),
                  pl.BlockSpec((1, seq, PAIR), lambda bi, p, i: (bi, 0, col0 + p)),
                  pl.BlockSpec((1, 2, blk), lambda bi, p, i: (p, 0, 0))],
        out_specs=pl.BlockSpec((1, blk, PAIR), lambda bi, p, i: (bi, i, p)),
        out_shape=jax.ShapeDtypeStruct((b, seq, n_pairs * PAIR), BF16),
        scratch_shapes=[pltpu.VMEM((LANES, PAIR), F32), pltpu.VMEM((blk, LANES), F32),
                        pltpu.VMEM((blk, 1), F32), pltpu.VMEM((blk, 1), F32),
                        pltpu.VMEM((blk, PAIR), F32)],
        compiler_params=_params("arbitrary", "arbitrary", "arbitrary"),
        name="moba_attn",
    )(q, k, v, sl)


def _dil_kernel(q_ref, k_ref, v_ref, o_ref, lse_ref, *, dil, slopes):
    wb = WIN_BLOCK
    n_q = q_ref.shape[1] // wb
    steps = (lax.broadcasted_iota(jnp.int32, (wb, 2 * wb), 0) + wb
             - lax.broadcasted_iota(jnp.int32, (wb, 2 * wb), 1))
    band_ok = (steps >= 0) & (steps <= wb)
    steps_f = (steps * dil).astype(F32)
    first, second = _head_masks()

    def attend(rows, krows, c0, first_block):
        q = q_ref[0, rows, c0:c0 + PAIR]
        kb = k_ref[0, krows, c0:c0 + PAIR]
        vb = v_ref[0, krows, c0:c0 + PAIR]
        outs, lses = [], []
        for hh, hmask in enumerate((first, second)):
            slope = slopes[(c0 % (2 * PAIR)) // HEAD_DIM + hh]
            qh = jnp.where(hmask, q, jnp.zeros_like(q))
            s = lax.dot_general(qh, kb, _NT, preferred_element_type=F32)
            if first_block:
                s = jnp.where(band_ok[:, wb:], s - slope * steps_f[:, wb:], NEG_INF)
            else:
                s = jnp.where(band_ok, s - slope * steps_f, NEG_INF)
            m = jnp.max(s, axis=-1, keepdims=True)
            p = jnp.exp(s - m)
            l = jnp.sum(p, axis=-1, keepdims=True)
            outs.append(jnp.dot(p.astype(BF16), vb, preferred_element_type=F32) / l)
            lses.append(m + jnp.log(l))
        o_ref[0, rows, c0:c0 + PAIR] = jnp.where(first, outs[0], outs[1]).astype(BF16)
        lse_ref[0, rows, c0:c0 + PAIR] = jnp.where(first, lses[0], lses[1])

    for r in range(dil):
        for pair in range(2):
            c0 = r * 2 * PAIR + pair * PAIR
            attend(pl.ds(0, wb), pl.ds(0, wb), c0, True)
            if n_q > 1:
                def body(i, carry, c0=c0):
                    attend(pl.ds(pl.multiple_of(i * wb, wb), wb),
                           pl.ds(pl.multiple_of((i - 1) * wb, wb), 2 * wb), c0, False)
                    return carry
                lax.fori_loop(1, n_q, body, 0)


def _dilated(q, k, v, group):
    window, dil = DILATED_PAIRS[group]
    assert window // dil == WIN_BLOCK
    b, seq, w = q.shape
    l_sub = seq // dil
    so = DIL_SLOPE_OFFSETS[group]
    slopes = tuple(float(s) for s in _alibi_slopes(N_ALIBI)[so:so + N_SLOTS_DIL])
    view = lambda t: t.reshape(b, l_sub, dil * w)
    spec = pl.BlockSpec((1, l_sub, dil * w), lambda bi: (bi, 0, 0))
    o, lse = pl.pallas_call(
        functools.partial(_dil_kernel, dil=dil, slopes=slopes),
        grid=(b,),
        in_specs=[spec, spec, spec],
        out_specs=[spec, spec],
        out_shape=[jax.ShapeDtypeStruct((b, l_sub, dil * w), BF16),
                   jax.ShapeDtypeStruct((b, l_sub, dil * w), F32)],
        compiler_params=_params("arbitrary"),
        name=f"dilated_attn_{dil}",
    )(view(q), view(k), view(v))
    return o.reshape(b * seq, w), lse.reshape(b * seq, w)


def _outproj_kernel(x_ref, mod_ref, g_ref, of_ref, om_ref, o1, l1, o2, l2, o3, l3, gate_ref,
                    wf_ref, wm_ref, wd_ref, wo_ref, x1_ref, h2_ref):
    d = x_ref.shape[1]
    lmax = jnp.maximum(jnp.maximum(l1[...], l2[...]), l3[...])
    e1, e2, e3 = jnp.exp(l1[...] - lmax), jnp.exp(l2[...] - lmax), jnp.exp(l3[...] - lmax)
    den = e1 + e2 + e3
    o_dil = ((e1 / den) * o1[...].astype(F32) + (e2 / den) * o2[...].astype(F32)
             + (e3 / den) * o3[...].astype(F32))
    y = (gate_ref[:, 0:d].astype(F32) * jnp.dot(of_ref[...], wf_ref[...], preferred_element_type=F32)
         + gate_ref[:, d:2 * d].astype(F32) * jnp.dot(om_ref[...], wm_ref[...], preferred_element_type=F32)
         + gate_ref[:, 2 * d:3 * d].astype(F32)
         * jnp.dot(o_dil.astype(BF16), wd_ref[...], preferred_element_type=F32))
    out = jnp.dot(y.astype(BF16), wo_ref[...], preferred_element_type=F32)
    x1 = x_ref[...] + mod_ref[0, 2:3, :] * out
    x1_ref[...] = x1
    h2_ref[...] = _mod_norm(x1, g_ref[...], mod_ref[0, 4:5, :], mod_ref[0, 3:4, :]).astype(BF16)


def _outproj(x2, mod_l, g_norm, o_fox, o_moba, dil_outs, gates, w_f, w_m, w_d, w_o, seq, tm):
    t, d = x2.shape
    per_b = seq // tm
    row = lambda i: (i, 0)
    fix = lambda i: (0, 0)
    dil_args, dil_specs = [], []
    for o, lse in dil_outs:
        dil_args += [o, lse]
        dil_specs += [pl.BlockSpec((tm, o.shape[1]), row), pl.BlockSpec((tm, lse.shape[1]), row)]
    return pl.pallas_call(
        _outproj_kernel,
        grid=(t // tm,),
        in_specs=[pl.BlockSpec((tm, d), row),
                  pl.BlockSpec((1, 6, d), lambda i: (i // per_b, 0, 0)),
                  pl.BlockSpec((1, d), fix),
                  pl.BlockSpec((tm, o_fox.shape[1]), row),
                  pl.BlockSpec((tm, o_moba.shape[1]), row)] + dil_specs + [
                  pl.BlockSpec((tm, N_BRANCH * d), row),
                  pl.BlockSpec(w_f.shape, fix), pl.BlockSpec(w_m.shape, fix),
                  pl.BlockSpec(w_d.shape, fix), pl.BlockSpec(w_o.shape, fix)],
        out_specs=[pl.BlockSpec((tm, d), row), pl.BlockSpec((tm, d), row)],
        out_shape=[jax.ShapeDtypeStruct((t, d), F32), jax.ShapeDtypeStruct((t, d), BF16)],
        compiler_params=_params("arbitrary"),
        name="outproj",
    )(x2, mod_l, g_norm, o_fox, o_moba, *dil_args, gates, w_f, w_m, w_d, w_o)


def _router_kernel(h_ref, w_ref, b_ref, o_ref, *, n_exp):
    logits = jnp.dot(h_ref[...], w_ref[...], preferred_element_type=F32) + b_ref[...]
    lane = lax.broadcasted_iota(jnp.int32, logits.shape, 1)
    logits = jnp.where(lane < n_exp, logits, -jnp.inf)

    def top(vals):
        m = jnp.max(vals, axis=-1, keepdims=True)
        idx = jnp.min(jnp.where(vals == m, lane, LANES), axis=-1, keepdims=True)
        return m, lane == idx

    m1, hot1 = top(logits)
    m2, hot2 = top(jnp.where(hot1, -jnp.inf, logits))
    e2 = jnp.exp(m2 - m1)
    o_ref[...] = jnp.where(hot1, 1.0 / (1.0 + e2), 0.0) + jnp.where(hot2, e2 / (1.0 + e2), 0.0)


def _router(h2, w_r, b_r, n_exp, tm):
    t, d = h2.shape
    return pl.pallas_call(
        functools.partial(_router_kernel, n_exp=n_exp),
        grid=(t // tm,),
        in_specs=[pl.BlockSpec((tm, d), lambda i: (i, 0)),
                  pl.BlockSpec((d, LANES), lambda i: (0, 0)),
                  pl.BlockSpec((1, LANES), lambda i: (0, 0))],
        out_specs=pl.BlockSpec((tm, LANES), lambda i: (i, 0)),
        out_shape=jax.ShapeDtypeStruct((t, LANES), F32),
        compiler_params=_params("arbitrary"),
        name="router",
    )(h2, w_r, b_r)


def _ffn_kernel(x_ref, h_ref, mod_ref, cw_ref, wg_ref, wu_ref, wd_ref, o_ref, acc_scr):
    e, f = pl.program_id(1), pl.program_id(2)

    @pl.when((e == 0) & (f == 0))
    def _():
        acc_scr[...] = jnp.zeros(acc_scr.shape, F32)

    h = h_ref[...]
    g = jnp.dot(h, wg_ref[0], preferred_element_type=F32)
    u = jnp.dot(h, wu_ref[0], preferred_element_type=F32)
    lane = lax.broadcasted_iota(jnp.int32, cw_ref.shape, 1)
    cw = jnp.sum(jnp.where(lane == e, cw_ref[...], 0.0), axis=-1, keepdims=True)
    a = ((g * _sigmoid(g)) * u).astype(BF16)
    acc_scr[...] += cw * jnp.dot(a, wd_ref[0], preferred_element_type=F32)

    @pl.when((e == pl.num_programs(1) - 1) & (f == pl.num_programs(2) - 1))
    def _():
        o_ref[...] = x_ref[...] + mod_ref[0, 5:6, :] * acc_scr[...]


def _ffn(x1, h2, mod_l, cw, w_g, w_u, w_d, seq, tm, n_f):
    t, d = x1.shape
    n_exp, _, ff = w_g.shape
    tf = ff // n_f
    per_b = seq // tm
    return pl.pallas_call(
        _ffn_kernel,
        grid=(t // tm, n_exp, n_f),
        in_specs=[pl.BlockSpec((tm, d), lambda i, e, f: (i, 0)),
                  pl.BlockSpec((tm, d), lambda i, e, f: (i, 0)),
                  pl.BlockSpec((1, 6, d), lambda i, e, f: (i // per_b, 0, 0)),
                  pl.BlockSpec((tm, LANES), lambda i, e, f: (i, 0)),
                  pl.BlockSpec((1, d, tf), lambda i, e, f: (e, 0, f)),
                  pl.BlockSpec((1, d, tf), lambda i, e, f: (e, 0, f)),
                  pl.BlockSpec((1, tf, d), lambda i, e, f: (e, f, 0))],
        out_specs=pl.BlockSpec((tm, d), lambda i, e, f: (i, 0)),
        out_shape=jax.ShapeDtypeStruct((t, d), F32),
        scratch_shapes=[pltpu.VMEM((tm, d), F32)],
        compiler_params=_params("arbitrary", "arbitrary", "arbitrary"),
        name="swiglu",
    )(x1, h2, mod_l, cw, w_g, w_u, w_d)


def _pad_cols(a, width):
    return jnp.pad(a, ((0, 0), (0, width - a.shape[1])))


def kernel(x, c, w_ada, b_ada, norm_mix, norm_ffn, w_in, b_fgate, q_gain, k_gain, w_br_fox, w_br_moba,
           w_br_dil, w_out, w_ffn_gate, w_ffn_up, w_ffn_down, w_router, b_router, w_exp_gate,
           w_exp_up, w_exp_down):
    b, seq, d = x.shape
    depth = w_ada.shape[0]
    t = b * seq
    tm = 512
    n_pairs_fox = N_HEADS_FOX // 2
    n_pairs_moba = N_HEADS_MOBA // 2
    tk = 256

    mod = _adaln(c, w_ada, b_ada).reshape(depth, b, 6, d)
    x2 = x.reshape(t, d)
    for l in range(depth):
        wl = w_in[l]
        f0 = 3 * MIX_WIDTH
        w_all = jnp.concatenate(
            [wl[:, :f0], wl[:, f0 + N_HEADS_FOX:], _pad_cols(wl[:, f0:f0 + N_HEADS_FOX], LANES)],
            axis=1).astype(BF16)
        b_f = _pad_cols(b_fgate[l].reshape(1, -1), LANES)
        outs = _inproj(x2, mod[l], norm_mix[l].reshape(1, d), w_all, q_gain[l].reshape(1, -1),
                       k_gain[l].reshape(1, -1), b_f, seq, tm)
        qkv = [tuple(o.reshape(b, seq, -1) for o in outs[3 * g:3 * g + 3]) for g in range(4)]
        gates, lf = outs[12], outs[13]

        cum = _decay(lf.reshape(b, seq, LANES))[:, :N_HEADS_FOX]
        ck = cum.reshape(b, n_pairs_fox, 2, seq // tk, tk).transpose(0, 1, 3, 2, 4)
        q_m, k_m, v_m = qkv[0]
        o_fox = _fox(q_m, k_m, v_m, ck, 0, n_pairs_fox, tk=tk).reshape(t, -1)
        o_moba = _moba(q_m, k_m, v_m, n_pairs_fox, n_pairs_moba).reshape(t, -1)
        dil_outs = [_dilated(*qkv[1 + g], g) for g in range(len(DILATED_PAIRS))]

        x1, h2 = _outproj(x2, mod[l], norm_ffn[l].reshape(1, d), o_fox, o_moba, dil_outs, gates,
                          w_br_fox[l].astype(BF16), w_br_moba[l].astype(BF16),
                          w_br_dil[l].astype(BF16), w_out[l].astype(BF16), seq, tm)
        i = l // 2
        if l % 2 == 0:
            cw = jnp.ones((t, LANES), F32)
            x2 = _ffn(x1, h2, mod[l], cw, w_ffn_gate[i:i + 1].astype(BF16),
                      w_ffn_up[i:i + 1].astype(BF16), w_ffn_down[i:i + 1].astype(BF16), seq, 1024, 2)
        else:
            n_exp = w_router.shape[2]
            cw = _router(h2, _pad_cols(w_router[i], LANES).astype(BF16),
                         _pad_cols(b_router[i].reshape(1, -1), LANES), n_exp, tm)
            x2 = _ffn(x1, h2, mod[l], cw, w_exp_gate[i].astype(BF16), w_exp_up[i].astype(BF16),
                      w_exp_down[i].astype(BF16), seq, 1024, 2)
    return x2.reshape(b, seq, d)
```

```python
import functools

import numpy as np
import jax
import jax.numpy as jnp
from jax import lax
from jax.experimental import pallas as pl
from jax.experimental.pallas import tpu as pltpu

HEAD_DIM = 64
N_HEADS_FOX = 6
N_HEADS_MOBA = 6
DILATED_PAIRS = ((128, 1), (512, 4), (2048, 16))
N_SLOTS_DIL = 4
N_HEADS_DIL = N_SLOTS_DIL * len(DILATED_PAIRS)
N_HEADS = N_HEADS_FOX + N_HEADS_MOBA + N_HEADS_DIL
MIX_WIDTH = N_HEADS * HEAD_DIM
N_BRANCH = 3
MOBA_BLOCK = 256
MOBA_TOPK = 3
WIN_BLOCK = 128
N_ALIBI = N_HEADS_MOBA + N_HEADS_DIL
DIL_SLOPE_OFFSETS = (0, N_SLOTS_DIL, 2 * N_SLOTS_DIL + N_HEADS_MOBA)
MOBA_SLOPE_OFFSET = 2 * N_SLOTS_DIL
TOP_K = 2
RMS_EPS = 1e-6
NEG_INF = -1e30

LANES = 128
PAIR = 2 * HEAD_DIM
VMEM_LIMIT = 56 * 1024 * 1024

F32 = jnp.float32
BF16 = jnp.bfloat16
_NT = (((1,), (1,)), ((), ()))


def _alibi_slopes(n):
    return (2.0 ** (-8.0 * np.arange(1, n + 1) / n)).astype(np.float32)


def _sigmoid(x):
    return 1.0 / (1.0 + jnp.exp(-x))


def _params(*sem):
    return pltpu.CompilerParams(dimension_semantics=sem, vmem_limit_bytes=VMEM_LIMIT)


def _adaln_kernel(c_ref, w_ref, b_ref, o_ref):
    c = c_ref[...]
    cond = c * _sigmoid(c)
    o_ref[0] = jnp.dot(cond, w_ref[0], precision=lax.Precision.HIGHEST,
                       preferred_element_type=F32) + b_ref[0]


def _adaln(c, w_ada, b_ada):
    depth, d, n = w_ada.shape
    b = c.shape[0]
    tn = 1536
    return pl.pallas_call(
        _adaln_kernel,
        grid=(depth, n // tn),
        in_specs=[pl.BlockSpec((b, d), lambda l, j: (0, 0)),
                  pl.BlockSpec((1, d, tn), lambda l, j: (l, 0, j)),
                  pl.BlockSpec((1, 1, tn), lambda l, j: (l, 0, j))],
        out_specs=pl.BlockSpec((1, b, tn), lambda l, j: (l, 0, j)),
        out_shape=jax.ShapeDtypeStruct((depth, b, n), F32),
        compiler_params=_params("arbitrary", "arbitrary"),
        name="adaln",
    )(c, w_ada, b_ada.reshape(depth, 1, n))


def _mod_norm(x, g, scale, shift):
    ms = jnp.mean(x * x, axis=-1, keepdims=True)
    return (x * lax.rsqrt(ms + RMS_EPS) * g) * (1.0 + scale) + shift


def _inproj_kernel(x_ref, mod_ref, g_ref, w_ref, wg_ref, wf_ref, qg_ref, kg_ref, bf_ref, e_ref,
                   q_ref, k_ref, v_ref, qd1, kd1, vd1, qd2, kd2, vd2, qd3, kd3, vd3,
                   gate_ref, lf_ref, *, n_main):
    h = _mod_norm(x_ref[...], g_ref[...], mod_ref[0, 1:2, :], mod_ref[0, 0:1, :]).astype(BF16)
    qk_chunk = 4 * HEAD_DIM
    q_outs = (q_ref, qd1, qd2, qd3)
    k_outs = (k_ref, kd1, kd2, kd3)
    v_outs = (v_ref, vd1, vd2, vd3)

    def dst(outs, col):
        if col < n_main:
            return outs[0], col
        g = (col - n_main) // qk_chunk
        return outs[1 + g], (col - n_main) % qk_chunk

    def normed(col0, gain_ref, gcol, scale):
        y = jnp.dot(h, w_ref[:, col0:col0 + qk_chunk], preferred_element_type=F32)
        ss = jnp.dot((y * y).astype(BF16), e_ref[...], preferred_element_type=F32)
        r = lax.rsqrt(ss * (1.0 / HEAD_DIM) + RMS_EPS)
        return (y * r * gain_ref[:, gcol:gcol + qk_chunk]) * scale

    for c in range(MIX_WIDTH // qk_chunk):
        col = c * qk_chunk
        ref, rc = dst(q_outs, col)
        ref[:, rc:rc + qk_chunk] = normed(col, qg_ref, col, HEAD_DIM ** -0.5).astype(BF16)
        ref, rc = dst(k_outs, col)
        ref[:, rc:rc + qk_chunk] = normed(MIX_WIDTH + col, kg_ref, col, 1.0).astype(BF16)
        ref, rc = dst(v_outs, col)
        y = jnp.dot(h, w_ref[:, 2 * MIX_WIDTH + col:2 * MIX_WIDTH + col + qk_chunk],
                    preferred_element_type=F32)
        ref[:, rc:rc + qk_chunk] = y.astype(BF16)

    gchunk = 512
    for c in range(wg_ref.shape[1] // gchunk):
        y = jnp.dot(h, wg_ref[:, c * gchunk:(c + 1) * gchunk], preferred_element_type=F32)
        gate_ref[:, c * gchunk:(c + 1) * gchunk] = _sigmoid(y).astype(BF16)

    f = jnp.dot(h, wf_ref[...], preferred_element_type=F32) + bf_ref[...]
    lf_ref[...] = jnp.minimum(f, 0.0) - jnp.log(1.0 + jnp.exp(-jnp.abs(f)))


def _inproj(x2, mod_l, g_norm, w_qkv, w_gate, w_f, q_gain, k_gain, b_f, seq, tm):
    t, d = x2.shape
    n_main = (N_HEADS_FOX + N_HEADS_MOBA) * HEAD_DIM
    dil_w = N_SLOTS_DIL * HEAD_DIM
    e = (np.arange(dil_w)[:, None] // HEAD_DIM == np.arange(dil_w)[None, :] // HEAD_DIM)
    e = jnp.asarray(e, BF16)
    per_b = seq // tm
    row = lambda i: (i, 0)
    fix = lambda i: (0, 0)
    qkv_shapes = []
    qkv_specs = []
    for w in (n_main, dil_w, dil_w, dil_w):
        for _ in range(3):
            qkv_shapes.append(jax.ShapeDtypeStruct((t, w), BF16))
            qkv_specs.append(pl.BlockSpec((tm, w), row))
    out_shapes = qkv_shapes + [jax.ShapeDtypeStruct((t, N_BRANCH * d), BF16),
                               jax.ShapeDtypeStruct((t, LANES), F32)]
    out_specs = qkv_specs + [pl.BlockSpec((tm, N_BRANCH * d), row), pl.BlockSpec((tm, LANES), row)]
    return pl.pallas_call(
        functools.partial(_inproj_kernel, n_main=n_main),
        grid=(t // tm,),
        in_specs=[pl.BlockSpec((tm, d), row),
                  pl.BlockSpec((1, 6, d), lambda i: (i // per_b, 0, 0)),
                  pl.BlockSpec((1, d), fix),
                  pl.BlockSpec(w_qkv.shape, fix, pipeline_mode=pl.Buffered(1)),
                  pl.BlockSpec(w_gate.shape, fix, pipeline_mode=pl.Buffered(1)),
                  pl.BlockSpec(w_f.shape, fix),
                  pl.BlockSpec((1, MIX_WIDTH), fix),
                  pl.BlockSpec((1, MIX_WIDTH), fix),
                  pl.BlockSpec((1, LANES), fix),
                  pl.BlockSpec(e.shape, fix)],
        out_specs=out_specs,
        out_shape=out_shapes,
        compiler_params=_params("arbitrary"),
        name="inproj",
    )(x2, mod_l, g_norm, w_qkv, w_gate, w_f, q_gain, k_gain, b_f, e)


def _decay_kernel(lf_ref, u_ref, o_ref, *, blk):
    lf_t = lf_ref[0].T
    seq = lf_t.shape[1]
    carry = jnp.zeros((8, 1), F32)
    for j in range(seq // blk):
        c = jnp.dot(lf_t[0:8, j * blk:(j + 1) * blk], u_ref[...], precision=lax.Precision.HIGHEST,
                    preferred_element_type=F32) + carry
        o_ref[0, :, j * blk:(j + 1) * blk] = c
        carry = c[:, blk - 1:blk]


def _decay(lf, blk=256):
    b, seq, _ = lf.shape
    u = jnp.asarray(np.arange(blk)[:, None] <= np.arange(blk)[None, :], F32)
    return pl.pallas_call(
        functools.partial(_decay_kernel, blk=blk),
        grid=(b,),
        in_specs=[pl.BlockSpec((1, seq, LANES), lambda i: (i, 0, 0)),
                  pl.BlockSpec((blk, blk), lambda i: (0, 0))],
        out_specs=pl.BlockSpec((1, 8, seq), lambda i: (i, 0, 0)),
        out_shape=jax.ShapeDtypeStruct((b, 8, seq), F32),
        compiler_params=_params("arbitrary"),
        name="fox_decay",
    )(lf, u)


def _head_masks():
    lane = lax.broadcasted_iota(jnp.int32, (1, PAIR), 1)
    return lane < HEAD_DIM, lane >= HEAD_DIM


def _split_heads(q):
    return [jnp.where(m, q, jnp.zeros_like(q)) for m in _head_masks()]


def _online_update(s, vb, m_ref, l_ref, acc_ref):
    m_prev = m_ref[...]
    m_new = jnp.maximum(m_prev, jnp.max(s, axis=-1, keepdims=True))
    p = jnp.exp(s - jnp.concatenate([m_new] * (s.shape[1] // LANES), axis=1))
    alpha = jnp.exp(m_prev - m_new)
    l_ref[...] = alpha * l_ref[...] + jnp.sum(p, axis=-1, keepdims=True)
    acc_ref[...] = alpha * acc_ref[...] + jnp.dot(p.astype(BF16), vb, preferred_element_type=F32)
    m_ref[...] = m_new


def _reset(m_scr, l_scr, acc_scr):
    m_scr[...] = jnp.full(m_scr.shape, NEG_INF, F32)
    l_scr[...] = jnp.zeros(l_scr.shape, F32)
    acc_scr[...] = jnp.zeros(acc_scr.shape, F32)


def _merge_pair(l_scr, acc_scr):
    first = _head_masks()[0]
    return jnp.where(first, acc_scr[0] / l_scr[0], acc_scr[1] / l_scr[1]).astype(BF16)


def _stat_scratch(tq):
    return [pltpu.VMEM((2, tq, LANES), F32), pltpu.VMEM((2, tq, LANES), F32),
            pltpu.VMEM((2, tq, PAIR), F32)]


def _fox_kernel(q_ref, k_ref, v_ref, ck_ref, o_ref, m_scr, l_scr, acc_scr, *, tq, tk):
    i = pl.program_id(2)
    qs = _split_heads(q_ref[0])
    n_full = (i * tq) // tk
    n_all = ((i + 1) * tq + tk - 1) // tk
    row = lax.broadcasted_iota(jnp.int32, (tq, tk), 0) + i * tq
    col = lax.broadcasted_iota(jnp.int32, (tq, tk), 1)
    _reset(m_scr, l_scr, acc_scr)

    def step(j, carry, masked):
        k0 = pl.multiple_of(j * tk, tk)
        kb = k_ref[0, pl.ds(k0, tk), :]
        vb = v_ref[0, pl.ds(k0, tk), :]
        ck = ck_ref[0, 0, j]
        for hh in range(2):
            s = lax.dot_general(qs[hh], kb, _NT, preferred_element_type=F32) - ck[hh:hh + 1, :]
            if masked:
                s = jnp.where(col + j * tk <= row, s, NEG_INF)
            _online_update(s, vb, m_scr.at[hh], l_scr.at[hh], acc_scr.at[hh])
        return carry

    lax.fori_loop(0, n_full, functools.partial(step, masked=False), 0)
    lax.fori_loop(n_full, n_all, functools.partial(step, masked=True), 0)
    o_ref[0] = _merge_pair(l_scr, acc_scr)


def _fox(q, k, v, ck, col0, n_pairs, tq, tk):
    b, seq, _ = q.shape
    return pl.pallas_call(
        functools.partial(_fox_kernel, tq=tq, tk=tk),
        grid=(b, n_pairs, seq // tq),
        in_specs=[pl.BlockSpec((1, tq, PAIR), lambda bi, p, i: (bi, i, col0 + p)),
                  pl.BlockSpec((1, seq, PAIR), lambda bi, p, i: (bi, 0, col0 + p)),
                  pl.BlockSpec((1, seq, PAIR), lambda bi, p, i: (bi, 0, col0 + p)),
                  pl.BlockSpec((1, 1, seq // tk, 2, tk), lambda bi, p, i: (bi, p, 0, 0, 0))],
        out_specs=pl.BlockSpec((1, tq, PAIR), lambda bi, p, i: (bi, i, p)),
        out_shape=jax.ShapeDtypeStruct((b, seq, n_pairs * PAIR), BF16),
        scratch_shapes=_stat_scratch(tq),
        compiler_params=_params("arbitrary", "arbitrary", "arbitrary"),
        name="fox_attn",
    )(q, k, v, ck)


def _moba_kernel(q_ref, k_ref, v_ref, sl_ref, o_ref, km_scr, sel_scr, srel_scr, sown_scr, m_scr,
                 l_scr, acc_scr):
    blk = MOBA_BLOCK
    i = pl.program_id(2)
    n_blk = k_ref.shape[1] // blk

    @pl.when(i == 0)
    def _():
        km_scr[...] = jnp.zeros(km_scr.shape, F32)
        for n in range(n_blk):
            kb = k_ref[0, n * blk:(n + 1) * blk, :].astype(F32)
            km_scr[n:n + 1, :] = jnp.sum(kb, axis=0, keepdims=True) * (1.0 / blk)

    tq = q_ref.shape[1]
    per = tq // blk
    qs = _split_heads(q_ref[0])
    _reset(m_scr, l_scr, acc_scr)

    @pl.when(i == 0)
    def _():
        row = lax.broadcasted_iota(jnp.int32, (tq, tq), 0)
        col = lax.broadcasted_iota(jnp.int32, (tq, tq), 1)
        rel_f = (row - col).astype(F32)
        hidden = (col // blk > row // blk) | ((col // blk == row // blk) & (col > row))
        for hh in range(2):
            srel = sl_ref[0, hh:hh + 1, :] * rel_f
            srel_scr[hh] = srel
            sown_scr[hh] = jnp.where(hidden, -NEG_INF, srel)

    nbp = 8
    blk_n = lax.broadcasted_iota(jnp.int32, (nbp, tq), 0)
    q_blk = i * per + lax.broadcasted_iota(jnp.int32, (nbp, tq), 1) // blk
    for hh in range(2):
        g = lax.dot_general(km_scr[...], qs[hh].astype(F32), _NT, precision=lax.Precision.HIGHEST,
                            preferred_element_type=F32)[0:nbp]
        cnt = jnp.zeros((nbp, tq), F32)
        for m in range(n_blk):
            gm = g[m:m + 1, :]
            beats = ((gm > g) | ((gm == g) & (blk_n > m))) & (q_blk > m)
            cnt = cnt + jnp.where(beats, 1.0, 0.0)
        sel = jnp.where(((blk_n < q_blk) & (cnt < float(MOBA_TOPK))) | (blk_n == q_blk), 1.0, 0.0)
        sel_scr[hh] = jnp.concatenate([sel, jnp.zeros((LANES - nbp, tq), F32)], axis=0).T

    lane_n = lax.broadcasted_iota(jnp.int32, (tq, LANES), 1)

    def chunk(j, bias_scr):
        k0 = pl.multiple_of(j * tq, tq)
        kb = k_ref[0, pl.ds(k0, tq), :]
        vb = v_ref[0, pl.ds(k0, tq), :]
        for hh in range(2):
            soff = sl_ref[0, hh:hh + 1, 0:LANES] * ((i - j) * tq).astype(F32)
            t = lax.dot_general(qs[hh], kb, _NT, preferred_element_type=F32) - bias_scr[hh]
            sel = sel_scr[hh]
            cols = [jnp.broadcast_to(jnp.sum(jnp.where(lane_n == j * per + c, sel, 0.0), axis=-1,
                                             keepdims=True), (tq, LANES)) > 0.5 for c in range(per)]
            m_prev = m_scr[hh]
            m_new = m_prev
            for c in range(per):
                mx = jnp.max(t[:, c * blk:(c + 1) * blk], axis=-1, keepdims=True)
                m_new = jnp.maximum(m_new, jnp.where(cols[c], mx - soff, NEG_INF))
            shift = jnp.concatenate(
                [jnp.where(cols[c], m_new + soff, -NEG_INF) for c in range(per)
                 for _ in range(blk // LANES)], axis=1)
            p = jnp.exp(t - shift)
            alpha = jnp.exp(m_prev - m_new)
            l_scr[hh] = alpha * l_scr[hh] + jnp.sum(p, axis=-1, keepdims=True)
            acc_scr[hh] = alpha * acc_scr[hh] + jnp.dot(p.astype(BF16), vb, preferred_element_type=F32)
            m_scr[hh] = m_new

    chunk(i, sown_scr)

    def step(j, carry):
        chunk(j, srel_scr)
        return carry

    lax.fori_loop(0, i, step, 0)
    o_ref[0] = _merge_pair(l_scr, acc_scr)


def _moba(q, k, v, col0, n_pairs, tq):
    b, seq, _ = q.shape
    blk = MOBA_BLOCK
    assert seq % tq == 0 and tq % blk == 0 and seq // blk <= 8
    slopes = _alibi_slopes(N_ALIBI)[MOBA_SLOPE_OFFSET:MOBA_SLOPE_OFFSET + 2 * n_pairs]
    sl = jnp.asarray(np.broadcast_to(slopes.reshape(n_pairs, 2, 1), (n_pairs, 2, tq)).copy())
    return pl.pallas_call(
        _moba_kernel,
        grid=(b, n_pairs, seq // tq),
        in_specs=[pl.BlockSpec((1, tq, PAIR), lambda bi, p, i: (bi, i, col0 + p)),
                  pl.BlockSpec((1, seq, PAIR), lambda bi, p, i: (bi, 0, col0 + p)),
                  pl.BlockSpec((1, seq, PAIR), lambda bi, p, i: (bi, 0, col0 + p)),
                  pl.BlockSpec((1, 2, tq), lambda bi, p, i: (p, 0, 0))],
        out_specs=pl.BlockSpec((1, tq, PAIR), lambda bi, p, i: (bi, i, p)),
        out_shape=jax.ShapeDtypeStruct((b, seq, n_pairs * PAIR), BF16),
        scratch_shapes=[pltpu.VMEM((LANES, PAIR), F32), pltpu.VMEM((2, tq, LANES), F32),
                        pltpu.VMEM((2, tq, tq), F32), pltpu.VMEM((2, tq, tq), F32)] + _stat_scratch(tq),
        compiler_params=_params("arbitrary", "arbitrary", "arbitrary"),
        name="moba_attn",
    )(q, k, v, sl)


def _dil_kernel(q_ref, k_ref, v_ref, o_ref, lse_ref, *, dil, slopes):
    wb = WIN_BLOCK
    n_q = q_ref.shape[1] // wb
    steps = (lax.broadcasted_iota(jnp.int32, (wb, 2 * wb), 0) + wb
             - lax.broadcasted_iota(jnp.int32, (wb, 2 * wb), 1))
    band_ok = (steps >= 0) & (steps <= wb)
    steps_f = (steps * dil).astype(F32)
    first, second = _head_masks()

    def attend(rows, krows, c0, first_block):
        q = q_ref[0, rows, c0:c0 + PAIR]
        kb = k_ref[0, krows, c0:c0 + PAIR]
        vb = v_ref[0, krows, c0:c0 + PAIR]
        outs, lses = [], []
        for hh, hmask in enumerate((first, second)):
            slope = slopes[(c0 % (2 * PAIR)) // HEAD_DIM + hh]
            qh = jnp.where(hmask, q, jnp.zeros_like(q))
            s = lax.dot_general(qh, kb, _NT, preferred_element_type=F32)
            if first_block:
                s = jnp.where(band_ok[:, wb:], s - slope * steps_f[:, wb:], NEG_INF)
            else:
                s = jnp.where(band_ok, s - slope * steps_f, NEG_INF)
            m = jnp.max(s, axis=-1, keepdims=True)
            p = jnp.exp(s - m)
            l = jnp.sum(p, axis=-1, keepdims=True)
            outs.append(jnp.dot(p.astype(BF16), vb, preferred_element_type=F32) / l)
            lses.append(m + jnp.log(l))
        o_ref[0, rows, c0:c0 + PAIR] = jnp.where(first, outs[0], outs[1]).astype(BF16)
        lse_ref[0, rows, c0:c0 + PAIR] = jnp.where(first, lses[0], lses[1])

    for r in range(dil):
        for pair in range(2):
            c0 = r * 2 * PAIR + pair * PAIR
            attend(pl.ds(0, wb), pl.ds(0, wb), c0, True)
            if n_q > 1:
                def body(i, carry, c0=c0):
                    attend(pl.ds(pl.multiple_of(i * wb, wb), wb),
                           pl.ds(pl.multiple_of((i - 1) * wb, wb), 2 * wb), c0, False)
                    return carry
                lax.fori_loop(1, n_q, body, 0)


def _dilated(q, k, v, group):
    window, dil = DILATED_PAIRS[group]
    assert window // dil == WIN_BLOCK
    b, seq, w = q.shape
    l_sub = seq // dil
    so = DIL_SLOPE_OFFSETS[group]
    slopes = tuple(float(s) for s in _alibi_slopes(N_ALIBI)[so:so + N_SLOTS_DIL])
    view = lambda t: t.reshape(b, l_sub, dil * w)
    spec = pl.BlockSpec((1, l_sub, dil * w), lambda bi: (bi, 0, 0))
    o, lse = pl.pallas_call(
        functools.partial(_dil_kernel, dil=dil, slopes=slopes),
        grid=(b,),
        in_specs=[spec, spec, spec],
        out_specs=[spec, spec],
        out_shape=[jax.ShapeDtypeStruct((b, l_sub, dil * w), BF16),
                   jax.ShapeDtypeStruct((b, l_sub, dil * w), F32)],
        compiler_params=_params("arbitrary"),
        name=f"dilated_attn_{dil}",
    )(view(q), view(k), view(v))
    return o.reshape(b * seq, w), lse.reshape(b * seq, w)


def _outproj_kernel(x_ref, mod_ref, g_ref, of_ref, om_ref, o1, l1, o2, l2, o3, l3, gate_ref,
                    wf_ref, wm_ref, wd_ref, wo_ref, x1_ref, h2_ref):
    d = x_ref.shape[1]
    lmax = jnp.maximum(jnp.maximum(l1[...], l2[...]), l3[...])
    e1, e2, e3 = jnp.exp(l1[...] - lmax), jnp.exp(l2[...] - lmax), jnp.exp(l3[...] - lmax)
    den = e1 + e2 + e3
    o_dil = ((e1 / den) * o1[...].astype(F32) + (e2 / den) * o2[...].astype(F32)
             + (e3 / den) * o3[...].astype(F32))
    y = (gate_ref[:, 0:d].astype(F32) * jnp.dot(of_ref[...], wf_ref[...], preferred_element_type=F32)
         + gate_ref[:, d:2 * d].astype(F32) * jnp.dot(om_ref[...], wm_ref[...], preferred_element_type=F32)
         + gate_ref[:, 2 * d:3 * d].astype(F32)
         * jnp.dot(o_dil.astype(BF16), wd_ref[...], preferred_element_type=F32))
    out = jnp.dot(y.astype(BF16), wo_ref[...], preferred_element_type=F32)
    x1 = x_ref[...] + mod_ref[0, 2:3, :] * out
    x1_ref[...] = x1
    h2_ref[...] = _mod_norm(x1, g_ref[...], mod_ref[0, 4:5, :], mod_ref[0, 3:4, :]).astype(BF16)


def _outproj(x2, mod_l, g_norm, o_fox, o_moba, dil_outs, gates, w_f, w_m, w_d, w_o, seq, tm):
    t, d = x2.shape
    per_b = seq // tm
    row = lambda i: (i, 0)
    fix = lambda i: (0, 0)
    dil_args, dil_specs = [], []
    for o, lse in dil_outs:
        dil_args += [o, lse]
        dil_specs += [pl.BlockSpec((tm, o.shape[1]), row), pl.BlockSpec((tm, lse.shape[1]), row)]
    return pl.pallas_call(
        _outproj_kernel,
        grid=(t // tm,),
        in_specs=[pl.BlockSpec((tm, d), row),
                  pl.BlockSpec((1, 6, d), lambda i: (i // per_b, 0, 0)),
                  pl.BlockSpec((1, d), fix),
                  pl.BlockSpec((tm, o_fox.shape[1]), row),
                  pl.BlockSpec((tm, o_moba.shape[1]), row)] + dil_specs + [
                  pl.BlockSpec((tm, N_BRANCH * d), row),
                  pl.BlockSpec(w_f.shape, fix), pl.BlockSpec(w_m.shape, fix),
                  pl.BlockSpec(w_d.shape, fix), pl.BlockSpec(w_o.shape, fix)],
        out_specs=[pl.BlockSpec((tm, d), row), pl.BlockSpec((tm, d), row)],
        out_shape=[jax.ShapeDtypeStruct((t, d), F32), jax.ShapeDtypeStruct((t, d), BF16)],
        compiler_params=_params("arbitrary"),
        name="outproj",
    )(x2, mod_l, g_norm, o_fox, o_moba, *dil_args, gates, w_f, w_m, w_d, w_o)


def _router_kernel(h_ref, w_ref, b_ref, o_ref, *, n_exp):
    logits = jnp.dot(h_ref[...], w_ref[...], preferred_element_type=F32) + b_ref[...]
    lane = lax.broadcasted_iota(jnp.int32, logits.shape, 1)
    logits = jnp.where(lane < n_exp, logits, -jnp.inf)

    def top(vals):
        m = jnp.max(vals, axis=-1, keepdims=True)
        idx = jnp.min(jnp.where(vals == m, lane, LANES), axis=-1, keepdims=True)
        return m, lane == idx

    m1, hot1 = top(logits)
    m2, hot2 = top(jnp.where(hot1, -jnp.inf, logits))
    e2 = jnp.exp(m2 - m1)
    o_ref[...] = jnp.where(hot1, 1.0 / (1.0 + e2), 0.0) + jnp.where(hot2, e2 / (1.0 + e2), 0.0)


def _router(h2, w_r, b_r, n_exp, tm):
    t, d = h2.shape
    return pl.pallas_call(
        functools.partial(_router_kernel, n_exp=n_exp),
        grid=(t // tm,),
        in_specs=[pl.BlockSpec((tm, d), lambda i: (i, 0)),
                  pl.BlockSpec((d, LANES), lambda i: (0, 0)),
                  pl.BlockSpec((1, LANES), lambda i: (0, 0))],
        out_specs=pl.BlockSpec((tm, LANES), lambda i: (i, 0)),
        out_shape=jax.ShapeDtypeStruct((t, LANES), F32),
        compiler_params=_params("arbitrary"),
        name="router",
    )(h2, w_r, b_r)


def _ffn_kernel(x_ref, h_ref, mod_ref, cw_ref, wg_ref, wu_ref, wd_ref, o_ref, acc_scr):
    e, f = pl.program_id(1), pl.program_id(2)

    @pl.when((e == 0) & (f == 0))
    def _():
        acc_scr[...] = jnp.zeros(acc_scr.shape, F32)

    h = h_ref[...]
    g = jnp.dot(h, wg_ref[0], preferred_element_type=F32)
    u = jnp.dot(h, wu_ref[0], preferred_element_type=F32)
    lane = lax.broadcasted_iota(jnp.int32, cw_ref.shape, 1)
    cw = jnp.sum(jnp.where(lane == e, cw_ref[...], 0.0), axis=-1, keepdims=True)
    a = ((g * _sigmoid(g)) * u).astype(BF16)
    acc_scr[...] += cw * jnp.dot(a, wd_ref[0], preferred_element_type=F32)

    @pl.when((e == pl.num_programs(1) - 1) & (f == pl.num_programs(2) - 1))
    def _():
        o_ref[...] = x_ref[...] + mod_ref[0, 5:6, :] * acc_scr[...]


def _ffn(x1, h2, mod_l, cw, w_g, w_u, w_d, seq, tm, n_f):
    t, d = x1.shape
    n_exp, _, ff = w_g.shape
    tf = ff // n_f
    per_b = seq // tm
    return pl.pallas_call(
        _ffn_kernel,
        grid=(t // tm, n_exp, n_f),
        in_specs=[pl.BlockSpec((tm, d), lambda i, e, f: (i, 0)),
                  pl.BlockSpec((tm, d), lambda i, e, f: (i, 0)),
                  pl.BlockSpec((1, 6, d), lambda i, e, f: (i // per_b, 0, 0)),
                  pl.BlockSpec((tm, LANES), lambda i, e, f: (i, 0)),
                  pl.BlockSpec((1, d, tf), lambda i, e, f: (e, 0, f)),
                  pl.BlockSpec((1, d, tf), lambda i, e, f: (e, 0, f)),
                  pl.BlockSpec((1, tf, d), lambda i, e, f: (e, f, 0))],
        out_specs=pl.BlockSpec((tm, d), lambda i, e, f: (i, 0)),
        out_shape=jax.ShapeDtypeStruct((t, d), F32),
        scratch_shapes=[pltpu.VMEM((tm, d), F32)],
        compiler_params=_params("arbitrary", "arbitrary", "arbitrary"),
        name="swiglu",
    )(x1, h2, mod_l, cw, w_g, w_u, w_d)


def _pad_cols(a, width):
    return jnp.pad(a, ((0, 0), (0, width - a.shape[1])))


def kernel(x, c, w_ada, b_ada, norm_mix, norm_ffn, w_in, b_fgate, q_gain, k_gain, w_br_fox, w_br_moba,
           w_br_dil, w_out, w_ffn_gate, w_ffn_up, w_ffn_down, w_router, b_router, w_exp_gate,
           w_exp_up, w_exp_down):
    b, seq, d = x.shape
    depth = w_ada.shape[0]
    t = b * seq
    tm = 512
    n_pairs_fox = N_HEADS_FOX // 2
    n_pairs_moba = N_HEADS_MOBA // 2
    tq, tk = 512, 512

    mod = _adaln(c, w_ada, b_ada).reshape(depth, b, 6, d)
    x2 = x.reshape(t, d)
    for l in range(depth):
        wl = w_in[l]
        f0 = 3 * MIX_WIDTH
        w_qkv = wl[:, :f0].astype(BF16)
        w_gate = wl[:, f0 + N_HEADS_FOX:].astype(BF16)
        w_f = _pad_cols(wl[:, f0:f0 + N_HEADS_FOX], LANES).astype(BF16)
        b_f = _pad_cols(b_fgate[l].reshape(1, -1), LANES)
        outs = _inproj(x2, mod[l], norm_mix[l].reshape(1, d), w_qkv, w_gate, w_f, q_gain[l].reshape(1, -1),
                       k_gain[l].reshape(1, -1), b_f, seq, tm)
        qkv = [tuple(o.reshape(b, seq, -1) for o in outs[3 * g:3 * g + 3]) for g in range(4)]
        gates, lf = outs[12], outs[13]

        cum = _decay(lf.reshape(b, seq, LANES))[:, :N_HEADS_FOX]
        ck = cum.reshape(b, n_pairs_fox, 2, seq // tk, tk).transpose(0, 1, 3, 2, 4)
        q_m, k_m, v_m = qkv[0]
        o_fox = _fox(q_m, k_m, v_m, ck, 0, n_pairs_fox, tq, tk).reshape(t, -1)
        o_moba = _moba(q_m, k_m, v_m, n_pairs_fox, n_pairs_moba, tq).reshape(t, -1)
        dil_outs = [_dilated(*qkv[1 + g], g) for g in range(len(DILATED_PAIRS))]

        x1, h2 = _outproj(x2, mod[l], norm_ffn[l].reshape(1, d), o_fox, o_moba, dil_outs, gates,
                          w_br_fox[l].astype(BF16), w_br_moba[l].astype(BF16),
                          w_br_dil[l].astype(BF16), w_out[l].astype(BF16), seq, tm)
        i = l // 2
        if l % 2 == 0:
            cw = jnp.ones((t, LANES), F32)
            x2 = _ffn(x1, h2, mod[l], cw, w_ffn_gate[i:i + 1].astype(BF16),
                      w_ffn_up[i:i + 1].astype(BF16), w_ffn_down[i:i + 1].astype(BF16), seq, 1024, 2)
        else:
            n_exp = w_router.shape[2]
            cw = _router(h2, _pad_cols(w_router[i], LANES).astype(BF16),
                         _pad_cols(b_router[i].reshape(1, -1), LANES), n_exp, tm)
            x2 = _ffn(x1, h2, mod[l], cw, w_exp_gate[i].astype(BF16), w_exp_up[i].astype(BF16),
                      w_exp_down[i].astype(BF16), seq, 1024, 2)
    return x2.reshape(b, seq, d)
```

```python
import functools

import numpy as np
import jax
import jax.numpy as jnp
from jax import lax
from jax.experimental import pallas as pl
from jax.experimental.pallas import tpu as pltpu

HEAD_DIM = 64
N_HEADS_FOX = 6
N_HEADS_MOBA = 6
DILATED_PAIRS = ((128, 1), (512, 4), (2048, 16))
N_SLOTS_DIL = 4
N_HEADS_DIL = N_SLOTS_DIL * len(DILATED_PAIRS)
N_HEADS = N_HEADS_FOX + N_HEADS_MOBA + N_HEADS_DIL
MIX_WIDTH = N_HEADS * HEAD_DIM
N_BRANCH = 3
MOBA_BLOCK = 256
MOBA_TOPK = 3
WIN_BLOCK = 128
N_ALIBI = N_HEADS_MOBA + N_HEADS_DIL
DIL_SLOPE_OFFSETS = (0, N_SLOTS_DIL, 2 * N_SLOTS_DIL + N_HEADS_MOBA)
MOBA_SLOPE_OFFSET = 2 * N_SLOTS_DIL
TOP_K = 2
RMS_EPS = 1e-6
NEG_INF = -1e30

LANES = 128
PAIR = 2 * HEAD_DIM
VMEM_LIMIT = 56 * 1024 * 1024

F32 = jnp.float32
BF16 = jnp.bfloat16
_NT = (((1,), (1,)), ((), ()))


def _alibi_slopes(n):
    return (2.0 ** (-8.0 * np.arange(1, n + 1) / n)).astype(np.float32)


def _sigmoid(x):
    return 1.0 / (1.0 + jnp.exp(-x))


def _params(*sem):
    return pltpu.CompilerParams(dimension_semantics=sem, vmem_limit_bytes=VMEM_LIMIT)


def _adaln_kernel(c_ref, w_ref, b_ref, o_ref):
    c = c_ref[...]
    cond = c * _sigmoid(c)
    o_ref[0] = jnp.dot(cond, w_ref[0], precision=lax.Precision.HIGHEST,
                       preferred_element_type=F32) + b_ref[0]


def _adaln(c, w_ada, b_ada):
    depth, d, n = w_ada.shape
    b = c.shape[0]
    tn = 1536
    return pl.pallas_call(
        _adaln_kernel,
        grid=(depth, n // tn),
        in_specs=[pl.BlockSpec((b, d), lambda l, j: (0, 0)),
                  pl.BlockSpec((1, d, tn), lambda l, j: (l, 0, j)),
                  pl.BlockSpec((1, 1, tn), lambda l, j: (l, 0, j))],
        out_specs=pl.BlockSpec((1, b, tn), lambda l, j: (l, 0, j)),
        out_shape=jax.ShapeDtypeStruct((depth, b, n), F32),
        compiler_params=_params("arbitrary", "arbitrary"),
        name="adaln",
    )(c, w_ada, b_ada.reshape(depth, 1, n))


def _mod_norm(x, g, scale, shift):
    ms = jnp.mean(x * x, axis=-1, keepdims=True)
    return (x * lax.rsqrt(ms + RMS_EPS) * g) * (1.0 + scale) + shift


def _inproj_kernel(x_ref, mod_ref, g_ref, w_ref, wg_ref, wf_ref, qg_ref, kg_ref, bf_ref, e_ref,
                   q_ref, k_ref, v_ref, qd1, kd1, vd1, qd2, kd2, vd2, qd3, kd3, vd3,
                   gate_ref, lf_ref, perm_scr, *, n_main):
    h = _mod_norm(x_ref[...], g_ref[...], mod_ref[0, 1:2, :], mod_ref[0, 0:1, :]).astype(BF16)
    tm = x_ref.shape[0]
    qk_chunk = 4 * HEAD_DIM
    q_outs = (q_ref, qd1, qd2, qd3)
    k_outs = (k_ref, kd1, kd2, kd3)
    v_outs = (v_ref, vd1, vd2, vd3)

    def put(outs, col, val):
        if col < n_main:
            outs[0][:, col:col + qk_chunk] = val.astype(BF16)
            return
        g = (col - n_main) // qk_chunk
        dil = DILATED_PAIRS[g][1]
        if dil == 1:
            outs[1 + g][0, 0] = val.astype(BF16)
            return
        halves = qk_chunk // LANES
        for c in range(halves):
            perm_scr[c] = val[:, c * LANES:(c + 1) * LANES]
        for r in range(dil):
            outs[1 + g][0, r] = jnp.concatenate(
                [perm_scr[c, pl.ds(r, tm // dil, stride=dil), :] for c in range(halves)],
                axis=1).astype(BF16)

    def proj(wt_ref, col0, width):
        return lax.dot_general(h, wt_ref[col0:col0 + width, :], _NT, preferred_element_type=F32)

    def normed(col0, gain_ref, gcol, scale):
        y = proj(w_ref, col0, qk_chunk)
        ss = jnp.dot((y * y).astype(BF16), e_ref[...], preferred_element_type=F32)
        r = lax.rsqrt(ss * (1.0 / HEAD_DIM) + RMS_EPS)
        return (y * r * gain_ref[:, gcol:gcol + qk_chunk]) * scale

    for c in range(MIX_WIDTH // qk_chunk):
        col = c * qk_chunk
        put(q_outs, col, normed(col, qg_ref, col, HEAD_DIM ** -0.5))
        put(k_outs, col, normed(MIX_WIDTH + col, kg_ref, col, 1.0))
        put(v_outs, col, proj(w_ref, 2 * MIX_WIDTH + col, qk_chunk))

    gchunk = 512
    for c in range(wg_ref.shape[0] // gchunk):
        y = proj(wg_ref, c * gchunk, gchunk)
        gate_ref[:, c * gchunk:(c + 1) * gchunk] = _sigmoid(y).astype(BF16)

    f = proj(wf_ref, 0, LANES) + bf_ref[...]
    lf_ref[...] = jnp.minimum(f, 0.0) - jnp.log(1.0 + jnp.exp(-jnp.abs(f)))


def _inproj(x2, mod_l, g_norm, w_qkv, w_gate, w_f, q_gain, k_gain, b_f, seq, tm):
    t, d = x2.shape
    n_main = (N_HEADS_FOX + N_HEADS_MOBA) * HEAD_DIM
    dil_w = N_SLOTS_DIL * HEAD_DIM
    e = (np.arange(dil_w)[:, None] // HEAD_DIM == np.arange(dil_w)[None, :] // HEAD_DIM)
    e = jnp.asarray(e, BF16)
    per_b = seq // tm
    row = lambda i: (i, 0)
    fix = lambda i: (0, 0)
    qkv_shapes = [jax.ShapeDtypeStruct((t, n_main), BF16)] * 3
    qkv_specs = [pl.BlockSpec((tm, n_main), row)] * 3
    for _, dil in DILATED_PAIRS:
        assert tm % (16 * dil) == 0
        qkv_shapes += [jax.ShapeDtypeStruct((t // seq, dil, seq // dil, dil_w), BF16)] * 3
        qkv_specs += [pl.BlockSpec((1, dil, tm // dil, dil_w),
                                   lambda i: (i // per_b, 0, i % per_b, 0))] * 3
    out_shapes = qkv_shapes + [jax.ShapeDtypeStruct((t, N_BRANCH * d), BF16),
                               jax.ShapeDtypeStruct((t, LANES), F32)]
    out_specs = qkv_specs + [pl.BlockSpec((tm, N_BRANCH * d), row), pl.BlockSpec((tm, LANES), row)]
    return pl.pallas_call(
        functools.partial(_inproj_kernel, n_main=n_main),
        grid=(t // tm,),
        in_specs=[pl.BlockSpec((tm, d), row),
                  pl.BlockSpec((1, 6, d), lambda i: (i // per_b, 0, 0)),
                  pl.BlockSpec((1, d), fix),
                  pl.BlockSpec(w_qkv.shape, fix, pipeline_mode=pl.Buffered(1)),
                  pl.BlockSpec(w_gate.shape, fix, pipeline_mode=pl.Buffered(1)),
                  pl.BlockSpec(w_f.shape, fix),
                  pl.BlockSpec((1, MIX_WIDTH), fix),
                  pl.BlockSpec((1, MIX_WIDTH), fix),
                  pl.BlockSpec((1, LANES), fix),
                  pl.BlockSpec(e.shape, fix)],
        out_specs=out_specs,
        out_shape=out_shapes,
        scratch_shapes=[pltpu.VMEM((dil_w // LANES, tm, LANES), F32)],
        compiler_params=_params("arbitrary"),
        name="inproj",
    )(x2, mod_l, g_norm, w_qkv, w_gate, w_f, q_gain, k_gain, b_f, e)


def _decay_kernel(lf_ref, u_ref, o_ref, *, blk):
    lf_t = lf_ref[0].T
    seq = lf_t.shape[1]
    carry = jnp.zeros((8, 1), F32)
    for j in range(seq // blk):
        c = jnp.dot(lf_t[0:8, j * blk:(j + 1) * blk], u_ref[...], precision=lax.Precision.HIGHEST,
                    preferred_element_type=F32) + carry
        o_ref[0, :, j * blk:(j + 1) * blk] = c
        carry = c[:, blk - 1:blk]


def _decay(lf, blk=256):
    b, seq, _ = lf.shape
    u = jnp.asarray(np.arange(blk)[:, None] <= np.arange(blk)[None, :], F32)
    return pl.pallas_call(
        functools.partial(_decay_kernel, blk=blk),
        grid=(b,),
        in_specs=[pl.BlockSpec((1, seq, LANES), lambda i: (i, 0, 0)),
                  pl.BlockSpec((blk, blk), lambda i: (0, 0))],
        out_specs=pl.BlockSpec((1, 8, seq), lambda i: (i, 0, 0)),
        out_shape=jax.ShapeDtypeStruct((b, 8, seq), F32),
        compiler_params=_params("arbitrary"),
        name="fox_decay",
    )(lf, u)


def _head_masks():
    lane = lax.broadcasted_iota(jnp.int32, (1, PAIR), 1)
    return lane < HEAD_DIM, lane >= HEAD_DIM


def _split_heads(q):
    return [jnp.where(m, q, jnp.zeros_like(q)) for m in _head_masks()]


def _online_update(s, vb, m_ref, l_ref, acc_ref):
    m_prev = m_ref[...]
    m_new = jnp.maximum(m_prev, jnp.max(s, axis=-1, keepdims=True))
    p = jnp.exp(s - jnp.concatenate([m_new] * (s.shape[1] // LANES), axis=1))
    alpha = jnp.exp(m_prev - m_new)
    l_ref[...] = alpha * l_ref[...] + jnp.sum(p, axis=-1, keepdims=True)
    acc_ref[...] = alpha * acc_ref[...] + jnp.dot(p.astype(BF16), vb, preferred_element_type=F32)
    m_ref[...] = m_new


def _reset(m_scr, l_scr, acc_scr):
    m_scr[...] = jnp.full(m_scr.shape, NEG_INF, F32)
    l_scr[...] = jnp.zeros(l_scr.shape, F32)
    acc_scr[...] = jnp.zeros(acc_scr.shape, F32)


def _merge_pair(l_scr, acc_scr):
    first = _head_masks()[0]
    return jnp.where(first, acc_scr[0] / l_scr[0], acc_scr[1] / l_scr[1]).astype(BF16)


def _stat_scratch(tq):
    return [pltpu.VMEM((2, tq, LANES), F32), pltpu.VMEM((2, tq, LANES), F32),
            pltpu.VMEM((2, tq, PAIR), F32)]


def _fox_kernel(q_ref, k_ref, v_ref, ck_ref, o_ref, m_scr, l_scr, acc_scr, *, tq, tk):
    i = pl.program_id(2)
    qs = _split_heads(q_ref[0])
    n_full = (i * tq) // tk
    n_all = ((i + 1) * tq + tk - 1) // tk
    row = lax.broadcasted_iota(jnp.int32, (tq, tk), 0) + i * tq
    col = lax.broadcasted_iota(jnp.int32, (tq, tk), 1)
    _reset(m_scr, l_scr, acc_scr)

    def step(j, carry, masked):
        k0 = pl.multiple_of(j * tk, tk)
        kb = k_ref[0, pl.ds(k0, tk), :]
        vb = v_ref[0, pl.ds(k0, tk), :]
        ck = ck_ref[0, 0, j]
        for hh in range(2):
            s = lax.dot_general(qs[hh], kb, _NT, preferred_element_type=F32) - ck[hh:hh + 1, :]
            if masked:
                s = jnp.where(col + j * tk <= row, s, NEG_INF)
            _online_update(s, vb, m_scr.at[hh], l_scr.at[hh], acc_scr.at[hh])
        return carry

    lax.fori_loop(0, n_full, functools.partial(step, masked=False), 0)
    lax.fori_loop(n_full, n_all, functools.partial(step, masked=True), 0)
    o_ref[0] = _merge_pair(l_scr, acc_scr)


def _fox(q, k, v, ck, col0, n_pairs, tq, tk):
    b, seq, _ = q.shape
    return pl.pallas_call(
        functools.partial(_fox_kernel, tq=tq, tk=tk),
        grid=(b, n_pairs, seq // tq),
        in_specs=[pl.BlockSpec((1, tq, PAIR), lambda bi, p, i: (bi, i, col0 + p)),
                  pl.BlockSpec((1, seq, PAIR), lambda bi, p, i: (bi, 0, col0 + p)),
                  pl.BlockSpec((1, seq, PAIR), lambda bi, p, i: (bi, 0, col0 + p)),
                  pl.BlockSpec((1, 1, seq // tk, 2, tk), lambda bi, p, i: (bi, p, 0, 0, 0))],
        out_specs=pl.BlockSpec((1, tq, PAIR), lambda bi, p, i: (bi, i, p)),
        out_shape=jax.ShapeDtypeStruct((b, seq, n_pairs * PAIR), BF16),
        scratch_shapes=_stat_scratch(tq),
        compiler_params=_params("arbitrary", "arbitrary", "arbitrary"),
        name="fox_attn",
    )(q, k, v, ck)


def _moba_kernel(q_ref, k_ref, v_ref, sl_ref, o_ref, km_scr, sel_scr, srel_scr, sown_scr, m_scr,
                 l_scr, acc_scr):
    blk = MOBA_BLOCK
    i = pl.program_id(2)
    n_blk = k_ref.shape[1] // blk

    @pl.when(i == 0)
    def _():
        km_scr[...] = jnp.zeros(km_scr.shape, F32)
        for n in range(n_blk):
            kb = k_ref[0, n * blk:(n + 1) * blk, :].astype(F32)
            km_scr[n:n + 1, :] = jnp.sum(kb, axis=0, keepdims=True) * (1.0 / blk)

    tq = q_ref.shape[1]
    per = tq // blk
    qs = _split_heads(q_ref[0])
    _reset(m_scr, l_scr, acc_scr)

    @pl.when(i == 0)
    def _():
        row = lax.broadcasted_iota(jnp.int32, (tq, tq), 0)
        col = lax.broadcasted_iota(jnp.int32, (tq, tq), 1)
        rel_f = (row - col).astype(F32)
        hidden = (col // blk > row // blk) | ((col // blk == row // blk) & (col > row))
        for hh in range(2):
            srel = sl_ref[0, hh:hh + 1, :] * rel_f
            srel_scr[hh] = srel
            sown_scr[hh] = jnp.where(hidden, -NEG_INF, srel)

    nbp = 8
    blk_n = lax.broadcasted_iota(jnp.int32, (nbp, tq), 0)
    q_blk = i * per + lax.broadcasted_iota(jnp.int32, (nbp, tq), 1) // blk
    for hh in range(2):
        g = lax.dot_general(km_scr[...], qs[hh].astype(F32), _NT, precision=lax.Precision.HIGHEST,
                            preferred_element_type=F32)[0:nbp]
        cnt = jnp.zeros((nbp, tq), F32)
        for m in range(n_blk):
            gm = g[m:m + 1, :]
            beats = ((gm > g) | ((gm == g) & (blk_n > m))) & (q_blk > m)
            cnt = cnt + jnp.where(beats, 1.0, 0.0)
        sel = jnp.where(((blk_n < q_blk) & (cnt < float(MOBA_TOPK))) | (blk_n == q_blk), 1.0, 0.0)
        sel_scr[hh] = jnp.concatenate([sel, jnp.zeros((LANES - nbp, tq), F32)], axis=0).T

    lane_n = lax.broadcasted_iota(jnp.int32, (tq, LANES), 1)

    def chunk(j, bias_scr):
        k0 = pl.multiple_of(j * tq, tq)
        kb = k_ref[0, pl.ds(k0, tq), :]
        vb = v_ref[0, pl.ds(k0, tq), :]
        for hh in range(2):
            soff = sl_ref[0, hh:hh + 1, 0:LANES] * ((i - j) * tq).astype(F32)
            t = lax.dot_general(qs[hh], kb, _NT, preferred_element_type=F32) - bias_scr[hh]
            sel = sel_scr[hh]
            cols = [jnp.broadcast_to(jnp.sum(jnp.where(lane_n == j * per + c, sel, 0.0), axis=-1,
                                             keepdims=True), (tq, LANES)) > 0.5 for c in range(per)]
            m_prev = m_scr[hh]
            m_new = m_prev
            for c in range(per):
                mx = jnp.max(t[:, c * blk:(c + 1) * blk], axis=-1, keepdims=True)
                m_new = jnp.maximum(m_new, jnp.where(cols[c], mx - soff, NEG_INF))
            shift = jnp.concatenate(
                [jnp.where(cols[c], m_new + soff, -NEG_INF) for c in range(per)
                 for _ in range(blk // LANES)], axis=1)
            p = jnp.exp(t - shift)
            alpha = jnp.exp(m_prev - m_new)
            l_scr[hh] = alpha * l_scr[hh] + jnp.sum(p, axis=-1, keepdims=True)
            acc_scr[hh] = alpha * acc_scr[hh] + jnp.dot(p.astype(BF16), vb, preferred_element_type=F32)
            m_scr[hh] = m_new

    chunk(i, sown_scr)

    def step(j, carry):
        chunk(j, srel_scr)
        return carry

    lax.fori_loop(0, i, step, 0)
    o_ref[0] = _merge_pair(l_scr, acc_scr)


def _moba(q, k, v, col0, n_pairs, tq):
    b, seq, _ = q.shape
    blk = MOBA_BLOCK
    assert seq % tq == 0 and tq % blk == 0 and seq // blk <= 8
    slopes = _alibi_slopes(N_ALIBI)[MOBA_SLOPE_OFFSET:MOBA_SLOPE_OFFSET + 2 * n_pairs]
    sl = jnp.asarray(np.broadcast_to(slopes.reshape(n_pairs, 2, 1), (n_pairs, 2, tq)).copy())
    return pl.pallas_call(
        _moba_kernel,
        grid=(b, n_pairs, seq // tq),
        in_specs=[pl.BlockSpec((1, tq, PAIR), lambda bi, p, i: (bi, i, col0 + p)),
                  pl.BlockSpec((1, seq, PAIR), lambda bi, p, i: (bi, 0, col0 + p)),
                  pl.BlockSpec((1, seq, PAIR), lambda bi, p, i: (bi, 0, col0 + p)),
                  pl.BlockSpec((1, 2, tq), lambda bi, p, i: (p, 0, 0))],
        out_specs=pl.BlockSpec((1, tq, PAIR), lambda bi, p, i: (bi, i, p)),
        out_shape=jax.ShapeDtypeStruct((b, seq, n_pairs * PAIR), BF16),
        scratch_shapes=[pltpu.VMEM((LANES, PAIR), F32), pltpu.VMEM((2, tq, LANES), F32),
                        pltpu.VMEM((2, tq, tq), F32), pltpu.VMEM((2, tq, tq), F32)] + _stat_scratch(tq),
        compiler_params=_params("arbitrary", "arbitrary", "arbitrary"),
        name="moba_attn",
    )(q, k, v, sl)


def _dil_kernel(q_ref, k_ref, v_ref, o_ref, lse_ref, bias_scr, *, dil, slopes):
    wb = WIN_BLOCK
    n_q = q_ref.shape[2] // wb
    steps = (lax.broadcasted_iota(jnp.int32, (wb, 2 * wb), 0) + wb
             - lax.broadcasted_iota(jnp.int32, (wb, 2 * wb), 1))
    band_ok = (steps >= 0) & (steps <= wb)
    steps_f = (steps * dil).astype(F32)
    for h in range(len(slopes)):
        bias_scr[h] = jnp.where(band_ok, slopes[h] * steps_f, -NEG_INF)
    first = _head_masks()[0]

    def attend(r, rows, krows, first_block):
        for pair in range(2):
            cols = slice(pair * PAIR, (pair + 1) * PAIR)
            qs = _split_heads(q_ref[0, r, rows, cols])
            kb = k_ref[0, r, krows, cols]
            vb = v_ref[0, r, krows, cols]
            outs, lses = [], []
            for hh in range(2):
                bias = bias_scr[2 * pair + hh]
                s = lax.dot_general(qs[hh], kb, _NT, preferred_element_type=F32)
                s = s - (bias[:, wb:] if first_block else bias)
                m = jnp.max(s, axis=-1, keepdims=True)
                p = jnp.exp(s - m)
                l = jnp.sum(p, axis=-1, keepdims=True)
                outs.append(jnp.dot(p.astype(BF16), vb, preferred_element_type=F32) / l)
                lses.append(m + jnp.log(l))
            o_ref[0, r, rows, cols] = jnp.where(first, outs[0], outs[1]).astype(BF16)
            lse_ref[0, r, rows, cols] = jnp.where(first, lses[0], lses[1])

    for r in range(dil):
        attend(r, pl.ds(0, wb), pl.ds(0, wb), True)
    if n_q > 1:
        def body(i, carry):
            for r in range(dil):
                attend(r, pl.ds(pl.multiple_of(i * wb, wb), wb),
                       pl.ds(pl.multiple_of((i - 1) * wb, wb), 2 * wb), False)
            return carry
        lax.fori_loop(1, n_q, body, 0)


def _dilated(q, k, v, group):
    window, dil = DILATED_PAIRS[group]
    assert window // dil == WIN_BLOCK
    b, _, l_sub, w = q.shape
    so = DIL_SLOPE_OFFSETS[group]
    slopes = tuple(float(s) for s in _alibi_slopes(N_ALIBI)[so:so + N_SLOTS_DIL])
    spec = pl.BlockSpec((1, dil, l_sub, w), lambda bi: (bi, 0, 0, 0))
    return pl.pallas_call(
        functools.partial(_dil_kernel, dil=dil, slopes=slopes),
        grid=(b,),
        in_specs=[spec, spec, spec],
        out_specs=[spec, spec],
        out_shape=[jax.ShapeDtypeStruct(q.shape, BF16), jax.ShapeDtypeStruct(q.shape, F32)],
        scratch_shapes=[pltpu.VMEM((N_SLOTS_DIL, WIN_BLOCK, 2 * WIN_BLOCK), F32)],
        compiler_params=_params("arbitrary"),
        name=f"dilated_attn_{dil}",
    )(q, k, v)


def _outproj_kernel(x_ref, mod_ref, g_ref, of_ref, om_ref, o1, l1, o2, l2, o3, l3, gate_ref,
                    wf_ref, wm_ref, wd_ref, wo_ref, x1_ref, h2_ref, perm_scr):
    d = x_ref.shape[1]

    def natural(ref):
        dil, n = ref.shape[1], ref.shape[2]
        if dil == 1:
            return ref[0, 0].astype(F32)
        halves = ref.shape[3] // LANES
        for r in range(dil):
            blk = ref[0, r].astype(F32)
            for c in range(halves):
                perm_scr[c, pl.ds(r, n, stride=dil), :] = blk[:, c * LANES:(c + 1) * LANES]
        return jnp.concatenate([perm_scr[c] for c in range(halves)], axis=1)

    lses = [natural(l) for l in (l1, l2, l3)]
    lmax = jnp.maximum(jnp.maximum(lses[0], lses[1]), lses[2])
    e1, e2, e3 = [jnp.exp(l - lmax) for l in lses]
    den = e1 + e2 + e3
    o_dil = (e1 / den) * natural(o1) + (e2 / den) * natural(o2) + (e3 / den) * natural(o3)
    y = (gate_ref[:, 0:d].astype(F32) * jnp.dot(of_ref[...], wf_ref[...], preferred_element_type=F32)
         + gate_ref[:, d:2 * d].astype(F32) * jnp.dot(om_ref[...], wm_ref[...], preferred_element_type=F32)
         + gate_ref[:, 2 * d:3 * d].astype(F32)
         * jnp.dot(o_dil.astype(BF16), wd_ref[...], preferred_element_type=F32))
    out = jnp.dot(y.astype(BF16), wo_ref[...], preferred_element_type=F32)
    x1 = x_ref[...] + mod_ref[0, 2:3, :] * out
    x1_ref[...] = x1
    h2_ref[...] = _mod_norm(x1, g_ref[...], mod_ref[0, 4:5, :], mod_ref[0, 3:4, :]).astype(BF16)


def _outproj(x2, mod_l, g_norm, o_fox, o_moba, dil_outs, gates, w_f, w_m, w_d, w_o, seq, tm):
    t, d = x2.shape
    per_b = seq // tm
    row = lambda i: (i, 0)
    fix = lambda i: (0, 0)
    dil_args, dil_specs = [], []
    for o, lse in dil_outs:
        _, dil, _, w = o.shape
        dil_args += [o, lse]
        dil_specs += [pl.BlockSpec((1, dil, tm // dil, w), lambda i: (i // per_b, 0, i % per_b, 0))] * 2
    return pl.pallas_call(
        _outproj_kernel,
        grid=(t // tm,),
        in_specs=[pl.BlockSpec((tm, d), row),
                  pl.BlockSpec((1, 6, d), lambda i: (i // per_b, 0, 0)),
                  pl.BlockSpec((1, d), fix),
                  pl.BlockSpec((tm, o_fox.shape[1]), row),
                  pl.BlockSpec((tm, o_moba.shape[1]), row)] + dil_specs + [
                  pl.BlockSpec((tm, N_BRANCH * d), row),
                  pl.BlockSpec(w_f.shape, fix), pl.BlockSpec(w_m.shape, fix),
                  pl.BlockSpec(w_d.shape, fix), pl.BlockSpec(w_o.shape, fix)],
        out_specs=[pl.BlockSpec((tm, d), row), pl.BlockSpec((tm, d), row)],
        out_shape=[jax.ShapeDtypeStruct((t, d), F32), jax.ShapeDtypeStruct((t, d), BF16)],
        scratch_shapes=[pltpu.VMEM((N_SLOTS_DIL * HEAD_DIM // LANES, tm, LANES), F32)],
        compiler_params=_params("arbitrary"),
        name="outproj",
    )(x2, mod_l, g_norm, o_fox, o_moba, *dil_args, gates, w_f, w_m, w_d, w_o)


def _router_kernel(h_ref, w_ref, b_ref, o_ref, *, n_exp):
    logits = jnp.dot(h_ref[...], w_ref[...], preferred_element_type=F32) + b_ref[...]
    lane = lax.broadcasted_iota(jnp.int32, logits.shape, 1)
    logits = jnp.where(lane < n_exp, logits, -jnp.inf)

    def top(vals):
        m = jnp.max(vals, axis=-1, keepdims=True)
        idx = jnp.min(jnp.where(vals == m, lane, LANES), axis=-1, keepdims=True)
        return m, lane == idx

    m1, hot1 = top(logits)
    m2, hot2 = top(jnp.where(hot1, -jnp.inf, logits))
    e2 = jnp.exp(m2 - m1)
    o_ref[...] = jnp.where(hot1, 1.0 / (1.0 + e2), 0.0) + jnp.where(hot2, e2 / (1.0 + e2), 0.0)


def _router(h2, w_r, b_r, n_exp, tm):
    t, d = h2.shape
    return pl.pallas_call(
        functools.partial(_router_kernel, n_exp=n_exp),
        grid=(t // tm,),
        in_specs=[pl.BlockSpec((tm, d), lambda i: (i, 0)),
                  pl.BlockSpec((d, LANES), lambda i: (0, 0)),
                  pl.BlockSpec((1, LANES), lambda i: (0, 0))],
        out_specs=pl.BlockSpec((tm, LANES), lambda i: (i, 0)),
        out_shape=jax.ShapeDtypeStruct((t, LANES), F32),
        compiler_params=_params("arbitrary"),
        name="router",
    )(h2, w_r, b_r)


def _ffn_kernel(x_ref, h_ref, mod_ref, cw_ref, wg_ref, wu_ref, wd_ref, o_ref, acc_scr):
    e, f = pl.program_id(1), pl.program_id(2)

    @pl.when((e == 0) & (f == 0))
    def _():
        acc_scr[...] = jnp.zeros(acc_scr.shape, F32)

    h = h_ref[...]
    g = jnp.dot(h, wg_ref[0], preferred_element_type=F32)
    u = jnp.dot(h, wu_ref[0], preferred_element_type=F32)
    lane = lax.broadcasted_iota(jnp.int32, cw_ref.shape, 1)
    cw = jnp.sum(jnp.where(lane == e, cw_ref[...], 0.0), axis=-1, keepdims=True)
    a = ((g * _sigmoid(g)) * u).astype(BF16)
    acc_scr[...] += cw * jnp.dot(a, wd_ref[0], preferred_element_type=F32)

    @pl.when((e == pl.num_programs(1) - 1) & (f == pl.num_programs(2) - 1))
    def _():
        o_ref[...] = x_ref[...] + mod_ref[0, 5:6, :] * acc_scr[...]


def _ffn(x1, h2, mod_l, cw, w_g, w_u, w_d, seq, tm, n_f):
    t, d = x1.shape
    n_exp, _, ff = w_g.shape
    tf = ff // n_f
    per_b = seq // tm
    return pl.pallas_call(
        _ffn_kernel,
        grid=(t // tm, n_exp, n_f),
        in_specs=[pl.BlockSpec((tm, d), lambda i, e, f: (i, 0)),
                  pl.BlockSpec((tm, d), lambda i, e, f: (i, 0)),
                  pl.BlockSpec((1, 6, d), lambda i, e, f: (i // per_b, 0, 0)),
                  pl.BlockSpec((tm, LANES), lambda i, e, f: (i, 0)),
                  pl.BlockSpec((1, d, tf), lambda i, e, f: (e, 0, f)),
                  pl.BlockSpec((1, d, tf), lambda i, e, f: (e, 0, f)),
                  pl.BlockSpec((1, tf, d), lambda i, e, f: (e, f, 0))],
        out_specs=pl.BlockSpec((tm, d), lambda i, e, f: (i, 0)),
        out_shape=jax.ShapeDtypeStruct((t, d), F32),
        scratch_shapes=[pltpu.VMEM((tm, d), F32)],
        compiler_params=_params("arbitrary", "arbitrary", "arbitrary"),
        name="swiglu",
    )(x1, h2, mod_l, cw, w_g, w_u, w_d)


def _pad_cols(a, width):
    return jnp.pad(a, ((0, 0), (0, width - a.shape[1])))


def kernel(x, c, w_ada, b_ada, norm_mix, norm_ffn, w_in, b_fgate, q_gain, k_gain, w_br_fox, w_br_moba,
           w_br_dil, w_out, w_ffn_gate, w_ffn_up, w_ffn_down, w_router, b_router, w_exp_gate,
           w_exp_up, w_exp_down):
    b, seq, d = x.shape
    depth = w_ada.shape[0]
    t = b * seq
    tm = 512
    n_pairs_fox = N_HEADS_FOX // 2
    n_pairs_moba = N_HEADS_MOBA // 2
    tq, tk = 512, 512

    mod = _adaln(c, w_ada, b_ada).reshape(depth, b, 6, d)
    x2 = x.reshape(t, d)
    w_in_t = jnp.transpose(w_in, (2, 0, 1))
    for l in range(depth):
        wl = w_in_t[:, l, :].astype(BF16)
        f0 = 3 * MIX_WIDTH
        w_qkv, w_gate = wl[:f0], wl[f0 + N_HEADS_FOX:]
        w_f = jnp.pad(wl[f0:f0 + N_HEADS_FOX], ((0, LANES - N_HEADS_FOX), (0, 0)))
        b_f = _pad_cols(b_fgate[l].reshape(1, -1), LANES)
        outs = _inproj(x2, mod[l], norm_mix[l].reshape(1, d), w_qkv, w_gate, w_f, q_gain[l].reshape(1, -1),
                       k_gain[l].reshape(1, -1), b_f, seq, tm)
        q_m, k_m, v_m = (o.reshape(b, seq, -1) for o in outs[0:3])
        gates, lf = outs[12], outs[13]

        cum = _decay(lf.reshape(b, seq, LANES))[:, :N_HEADS_FOX]
        ck = cum.reshape(b, n_pairs_fox, 2, seq // tk, tk).transpose(0, 1, 3, 2, 4)
        o_fox = _fox(q_m, k_m, v_m, ck, 0, n_pairs_fox, tq, tk).reshape(t, -1)
        o_moba = _moba(q_m, k_m, v_m, n_pairs_fox, n_pairs_moba, tq).reshape(t, -1)
        dil_outs = [_dilated(*outs[3 + 3 * g:6 + 3 * g], g) for g in range(len(DILATED_PAIRS))]

        x1, h2 = _outproj(x2, mod[l], norm_ffn[l].reshape(1, d), o_fox, o_moba, dil_outs, gates,
                          w_br_fox[l].astype(BF16), w_br_moba[l].astype(BF16),
                          w_br_dil[l].astype(BF16), w_out[l].astype(BF16), seq, tm)
        i = l // 2
        if l % 2 == 0:
            cw = jnp.ones((t, LANES), F32)
            x2 = _ffn(x1, h2, mod[l], cw, w_ffn_gate[i:i + 1].astype(BF16),
                      w_ffn_up[i:i + 1].astype(BF16), w_ffn_down[i:i + 1].astype(BF16), seq, 1024, 2)
        else:
            n_exp = w_router.shape[2]
            cw = _router(h2, _pad_cols(w_router[i], LANES).astype(BF16),
                         _pad_cols(b_router[i].reshape(1, -1), LANES), n_exp, tm)
            x2 = _ffn(x1, h2, mod[l], cw, w_exp_gate[i].astype(BF16), w_exp_up[i].astype(BF16),
                      w_exp_down[i].astype(BF16), seq, 1024, 2)
    return x2.reshape(b, seq, d)
```

```python
import functools

import numpy as np
import jax
import jax.numpy as jnp
from jax import lax
from jax.experimental import pallas as pl
from jax.experimental.pallas import tpu as pltpu

HEAD_DIM = 64
N_HEADS_FOX = 6
N_HEADS_MOBA = 6
DILATED_PAIRS = ((128, 1), (512, 4), (2048, 16))
N_SLOTS_DIL = 4
N_HEADS_DIL = N_SLOTS_DIL * len(DILATED_PAIRS)
N_HEADS = N_HEADS_FOX + N_HEADS_MOBA + N_HEADS_DIL
MIX_WIDTH = N_HEADS * HEAD_DIM
N_BRANCH = 3
MOBA_BLOCK = 256
MOBA_TOPK = 3
WIN_BLOCK = 128
N_ALIBI = N_HEADS_MOBA + N_HEADS_DIL
DIL_SLOPE_OFFSETS = (0, N_SLOTS_DIL, 2 * N_SLOTS_DIL + N_HEADS_MOBA)
MOBA_SLOPE_OFFSET = 2 * N_SLOTS_DIL
TOP_K = 2
RMS_EPS = 1e-6
NEG_INF = -1e30

LANES = 128
PAIR = 2 * HEAD_DIM
VMEM_LIMIT = 56 * 1024 * 1024

F32 = jnp.float32
BF16 = jnp.bfloat16
_NT = (((1,), (1,)), ((), ()))


def _alibi_slopes(n):
    return (2.0 ** (-8.0 * np.arange(1, n + 1) / n)).astype(np.float32)


def _sigmoid(x):
    return 1.0 / (1.0 + jnp.exp(-x))


def _params(*sem):
    return pltpu.CompilerParams(dimension_semantics=sem, vmem_limit_bytes=VMEM_LIMIT)


def _adaln_kernel(c_ref, w_ref, b_ref, o_ref):
    c = c_ref[...]
    cond = c * _sigmoid(c)
    o_ref[0] = jnp.dot(cond, w_ref[0], precision=lax.Precision.HIGHEST,
                       preferred_element_type=F32) + b_ref[0]


def _adaln(c, w_ada, b_ada):
    depth, d, n = w_ada.shape
    b = c.shape[0]
    tn = 1536
    return pl.pallas_call(
        _adaln_kernel,
        grid=(depth, n // tn),
        in_specs=[pl.BlockSpec((b, d), lambda l, j: (0, 0)),
                  pl.BlockSpec((1, d, tn), lambda l, j: (l, 0, j)),
                  pl.BlockSpec((1, 1, tn), lambda l, j: (l, 0, j))],
        out_specs=pl.BlockSpec((1, b, tn), lambda l, j: (l, 0, j)),
        out_shape=jax.ShapeDtypeStruct((depth, b, n), F32),
        compiler_params=_params("arbitrary", "arbitrary"),
        name="adaln",
    )(c, w_ada, b_ada.reshape(depth, 1, n))


def _mod_norm(x, g, scale, shift):
    ms = jnp.mean(x * x, axis=-1, keepdims=True)
    return (x * lax.rsqrt(ms + RMS_EPS) * g) * (1.0 + scale) + shift


def _inproj_kernel(x_ref, mod_ref, g_ref, w_ref, wg_ref, wf_ref, qg_ref, kg_ref, bf_ref, e_ref,
                   q_ref, k_ref, v_ref, qd1, kd1, vd1, qd2, kd2, vd2, qd3, kd3, vd3,
                   gate_ref, lf_ref, perm_scr, *, n_main):
    h = _mod_norm(x_ref[...], g_ref[...], mod_ref[0, 1:2, :], mod_ref[0, 0:1, :]).astype(BF16)
    tm = x_ref.shape[0]
    qk_chunk = 4 * HEAD_DIM
    q_outs = (q_ref, qd1, qd2, qd3)
    k_outs = (k_ref, kd1, kd2, kd3)
    v_outs = (v_ref, vd1, vd2, vd3)

    def put(outs, col, val):
        if col < n_main:
            outs[0][:, col:col + qk_chunk] = val.astype(BF16)
            return
        g = (col - n_main) // qk_chunk
        dil = DILATED_PAIRS[g][1]
        if dil == 1:
            outs[1 + g][0, 0] = val.astype(BF16)
            return
        halves = qk_chunk // LANES
        for c in range(halves):
            perm_scr[c] = val[:, c * LANES:(c + 1) * LANES]
        for r in range(dil):
            outs[1 + g][0, r] = jnp.concatenate(
                [perm_scr[c, pl.ds(r, tm // dil, stride=dil), :] for c in range(halves)],
                axis=1).astype(BF16)

    def proj(wt_ref, col0, width):
        return lax.dot_general(h, wt_ref[col0:col0 + width, :], _NT, preferred_element_type=F32)

    def normed(col0, gain_ref, gcol, scale):
        y = proj(w_ref, col0, qk_chunk)
        ss = jnp.dot((y * y).astype(BF16), e_ref[...], preferred_element_type=F32)
        r = lax.rsqrt(ss * (1.0 / HEAD_DIM) + RMS_EPS)
        return (y * r * gain_ref[:, gcol:gcol + qk_chunk]) * scale

    for c in range(MIX_WIDTH // qk_chunk):
        col = c * qk_chunk
        put(q_outs, col, normed(col, qg_ref, col, HEAD_DIM ** -0.5))
        put(k_outs, col, normed(MIX_WIDTH + col, kg_ref, col, 1.0))
        put(v_outs, col, proj(w_ref, 2 * MIX_WIDTH + col, qk_chunk))

    gchunk = 512
    for c in range(wg_ref.shape[0] // gchunk):
        y = proj(wg_ref, c * gchunk, gchunk)
        gate_ref[:, c * gchunk:(c + 1) * gchunk] = _sigmoid(y).astype(BF16)

    f = proj(wf_ref, 0, LANES) + bf_ref[...]
    lf_ref[...] = jnp.minimum(f, 0.0) - jnp.log(1.0 + jnp.exp(-jnp.abs(f)))


def _inproj(x2, mod_l, g_norm, w_qkv, w_gate, w_f, q_gain, k_gain, b_f, seq, tm):
    t, d = x2.shape
    n_main = (N_HEADS_FOX + N_HEADS_MOBA) * HEAD_DIM
    dil_w = N_SLOTS_DIL * HEAD_DIM
    e = (np.arange(dil_w)[:, None] // HEAD_DIM == np.arange(dil_w)[None, :] // HEAD_DIM)
    e = jnp.asarray(e, BF16)
    per_b = seq // tm
    row = lambda i: (i, 0)
    fix = lambda i: (0, 0)
    qkv_shapes = [jax.ShapeDtypeStruct((t, n_main), BF16)] * 3
    qkv_specs = [pl.BlockSpec((tm, n_main), row)] * 3
    for _, dil in DILATED_PAIRS:
        assert tm % (16 * dil) == 0
        qkv_shapes += [jax.ShapeDtypeStruct((t // seq, dil, seq // dil, dil_w), BF16)] * 3
        qkv_specs += [pl.BlockSpec((1, dil, tm // dil, dil_w),
                                   lambda i: (i // per_b, 0, i % per_b, 0))] * 3
    out_shapes = qkv_shapes + [jax.ShapeDtypeStruct((t, N_BRANCH * d), BF16),
                               jax.ShapeDtypeStruct((t, LANES), F32)]
    out_specs = qkv_specs + [pl.BlockSpec((tm, N_BRANCH * d), row), pl.BlockSpec((tm, LANES), row)]
    return pl.pallas_call(
        functools.partial(_inproj_kernel, n_main=n_main),
        grid=(t // tm,),
        in_specs=[pl.BlockSpec((tm, d), row),
                  pl.BlockSpec((1, 6, d), lambda i: (i // per_b, 0, 0)),
                  pl.BlockSpec((1, d), fix),
                  pl.BlockSpec(w_qkv.shape, fix, pipeline_mode=pl.Buffered(1)),
                  pl.BlockSpec(w_gate.shape, fix, pipeline_mode=pl.Buffered(1)),
                  pl.BlockSpec(w_f.shape, fix),
                  pl.BlockSpec((1, MIX_WIDTH), fix),
                  pl.BlockSpec((1, MIX_WIDTH), fix),
                  pl.BlockSpec((1, LANES), fix),
                  pl.BlockSpec(e.shape, fix)],
        out_specs=out_specs,
        out_shape=out_shapes,
        scratch_shapes=[pltpu.VMEM((dil_w // LANES, tm, LANES), F32)],
        compiler_params=_params("arbitrary"),
        name="inproj",
    )(x2, mod_l, g_norm, w_qkv, w_gate, w_f, q_gain, k_gain, b_f, e)


def _decay_kernel(lf_ref, u_ref, o_ref, *, blk):
    lf_t = lf_ref[0].T
    seq = lf_t.shape[1]
    carry = jnp.zeros((8, 1), F32)
    for j in range(seq // blk):
        c = jnp.dot(lf_t[0:8, j * blk:(j + 1) * blk], u_ref[...], precision=lax.Precision.HIGHEST,
                    preferred_element_type=F32) + carry
        o_ref[0, :, j * blk:(j + 1) * blk] = c
        carry = c[:, blk - 1:blk]


def _decay(lf, blk=256):
    b, seq, _ = lf.shape
    u = jnp.asarray(np.arange(blk)[:, None] <= np.arange(blk)[None, :], F32)
    return pl.pallas_call(
        functools.partial(_decay_kernel, blk=blk),
        grid=(b,),
        in_specs=[pl.BlockSpec((1, seq, LANES), lambda i: (i, 0, 0)),
                  pl.BlockSpec((blk, blk), lambda i: (0, 0))],
        out_specs=pl.BlockSpec((1, 8, seq), lambda i: (i, 0, 0)),
        out_shape=jax.ShapeDtypeStruct((b, 8, seq), F32),
        compiler_params=_params("arbitrary"),
        name="fox_decay",
    )(lf, u)


def _head_masks():
    lane = lax.broadcasted_iota(jnp.int32, (1, PAIR), 1)
    return lane < HEAD_DIM, lane >= HEAD_DIM


def _split_heads(q):
    return [jnp.where(m, q, jnp.zeros_like(q)) for m in _head_masks()]


def _online_update(s, vb, m_ref, l_ref, acc_ref):
    m_prev = m_ref[...]
    m_new = jnp.maximum(m_prev, jnp.max(s, axis=-1, keepdims=True))
    p = jnp.exp(s - jnp.concatenate([m_new] * (s.shape[1] // LANES), axis=1))
    alpha = jnp.exp(m_prev - m_new)
    l_ref[...] = alpha * l_ref[...] + jnp.sum(p, axis=-1, keepdims=True)
    acc_ref[...] = alpha * acc_ref[...] + jnp.dot(p.astype(BF16), vb, preferred_element_type=F32)
    m_ref[...] = m_new


def _reset(m_scr, l_scr, acc_scr):
    m_scr[...] = jnp.full(m_scr.shape, NEG_INF, F32)
    l_scr[...] = jnp.zeros(l_scr.shape, F32)
    acc_scr[...] = jnp.zeros(acc_scr.shape, F32)


def _merge_pair(l_scr, acc_scr):
    first = _head_masks()[0]
    return jnp.where(first, acc_scr[0] / l_scr[0], acc_scr[1] / l_scr[1]).astype(BF16)


def _stat_scratch(tq):
    return [pltpu.VMEM((2, tq, LANES), F32), pltpu.VMEM((2, tq, LANES), F32),
            pltpu.VMEM((2, tq, PAIR), F32)]


def _fox_kernel(q_ref, k_ref, v_ref, ck_ref, o_ref, m_scr, l_scr, acc_scr, *, tq, tk):
    i = pl.program_id(2)
    qs = _split_heads(q_ref[0])
    n_full = (i * tq) // tk
    n_all = ((i + 1) * tq + tk - 1) // tk
    row = lax.broadcasted_iota(jnp.int32, (tq, tk), 0) + i * tq
    col = lax.broadcasted_iota(jnp.int32, (tq, tk), 1)
    _reset(m_scr, l_scr, acc_scr)

    def step(j, carry, masked):
        k0 = pl.multiple_of(j * tk, tk)
        kb = k_ref[0, pl.ds(k0, tk), :]
        vb = v_ref[0, pl.ds(k0, tk), :]
        ck = ck_ref[0, 0, j]
        for hh in range(2):
            s = lax.dot_general(qs[hh], kb, _NT, preferred_element_type=F32) - ck[hh:hh + 1, :]
            if masked:
                s = jnp.where(col + j * tk <= row, s, NEG_INF)
            _online_update(s, vb, m_scr.at[hh], l_scr.at[hh], acc_scr.at[hh])
        return carry

    lax.fori_loop(0, n_full, functools.partial(step, masked=False), 0)
    lax.fori_loop(n_full, n_all, functools.partial(step, masked=True), 0)
    o_ref[0] = _merge_pair(l_scr, acc_scr)


def _fox(q, k, v, ck, col0, n_pairs, tq, tk):
    b, seq, _ = q.shape
    return pl.pallas_call(
        functools.partial(_fox_kernel, tq=tq, tk=tk),
        grid=(b, n_pairs, seq // tq),
        in_specs=[pl.BlockSpec((1, tq, PAIR), lambda bi, p, i: (bi, i, col0 + p)),
                  pl.BlockSpec((1, seq, PAIR), lambda bi, p, i: (bi, 0, col0 + p)),
                  pl.BlockSpec((1, seq, PAIR), lambda bi, p, i: (bi, 0, col0 + p)),
                  pl.BlockSpec((1, 1, seq // tk, 2, tk), lambda bi, p, i: (bi, p, 0, 0, 0))],
        out_specs=pl.BlockSpec((1, tq, PAIR), lambda bi, p, i: (bi, i, p)),
        out_shape=jax.ShapeDtypeStruct((b, seq, n_pairs * PAIR), BF16),
        scratch_shapes=_stat_scratch(tq),
        compiler_params=_params("arbitrary", "arbitrary", "arbitrary"),
        name="fox_attn",
    )(q, k, v, ck)


def _moba_kernel(q_ref, k_ref, v_ref, sl_ref, o_ref, km_scr, sel_scr, srel_scr, sown_scr, m_scr,
                 l_scr, acc_scr):
    blk = MOBA_BLOCK
    i = pl.program_id(2)
    n_blk = k_ref.shape[1] // blk

    @pl.when(i == 0)
    def _():
        km_scr[...] = jnp.zeros(km_scr.shape, F32)
        for n in range(n_blk):
            kb = k_ref[0, n * blk:(n + 1) * blk, :].astype(F32)
            km_scr[n:n + 1, :] = jnp.sum(kb, axis=0, keepdims=True) * (1.0 / blk)

    tq = q_ref.shape[1]
    per = tq // blk
    qs = _split_heads(q_ref[0])
    _reset(m_scr, l_scr, acc_scr)

    @pl.when(i == 0)
    def _():
        row = lax.broadcasted_iota(jnp.int32, (tq, tq), 0)
        col = lax.broadcasted_iota(jnp.int32, (tq, tq), 1)
        rel_f = (row - col).astype(F32)
        hidden = (col // blk > row // blk) | ((col // blk == row // blk) & (col > row))
        for hh in range(2):
            srel = sl_ref[0, hh:hh + 1, :] * rel_f
            srel_scr[hh] = srel
            sown_scr[hh] = jnp.where(hidden, -NEG_INF, srel)

    nbp = 8
    blk_n = lax.broadcasted_iota(jnp.int32, (nbp, tq), 0)
    q_blk = i * per + lax.broadcasted_iota(jnp.int32, (nbp, tq), 1) // blk
    for hh in range(2):
        g = lax.dot_general(km_scr[...], qs[hh].astype(F32), _NT, precision=lax.Precision.HIGHEST,
                            preferred_element_type=F32)[0:nbp]
        cnt = jnp.zeros((nbp, tq), F32)
        for m in range(n_blk):
            gm = g[m:m + 1, :]
            beats = ((gm > g) | ((gm == g) & (blk_n > m))) & (q_blk > m)
            cnt = cnt + jnp.where(beats, 1.0, 0.0)
        sel = jnp.where(((blk_n < q_blk) & (cnt < float(MOBA_TOPK))) | (blk_n == q_blk), 1.0, 0.0)
        sel_scr[hh] = jnp.concatenate([sel, jnp.zeros((LANES - nbp, tq), F32)], axis=0).T

    lane_n = lax.broadcasted_iota(jnp.int32, (tq, LANES), 1)

    def chunk(j, bias_scr):
        k0 = pl.multiple_of(j * tq, tq)
        kb = k_ref[0, pl.ds(k0, tq), :]
        vb = v_ref[0, pl.ds(k0, tq), :]
        for hh in range(2):
            soff = sl_ref[0, hh:hh + 1, 0:LANES] * ((i - j) * tq).astype(F32)
            t = lax.dot_general(qs[hh], kb, _NT, preferred_element_type=F32) - bias_scr[hh]
            sel = sel_scr[hh]
            cols = [jnp.broadcast_to(jnp.sum(jnp.where(lane_n == j * per + c, sel, 0.0), axis=-1,
                                             keepdims=True), (tq, LANES)) > 0.5 for c in range(per)]
            m_prev = m_scr[hh]
            m_new = m_prev
            for c in range(per):
                mx = jnp.max(t[:, c * blk:(c + 1) * blk], axis=-1, keepdims=True)
                m_new = jnp.maximum(m_new, jnp.where(cols[c], mx - soff, NEG_INF))
            shift = jnp.concatenate(
                [jnp.where(cols[c], m_new + soff, -NEG_INF) for c in range(per)
                 for _ in range(blk // LANES)], axis=1)
            p = jnp.exp(t - shift)
            alpha = jnp.exp(m_prev - m_new)
            l_scr[hh] = alpha * l_scr[hh] + jnp.sum(p, axis=-1, keepdims=True)
            acc_scr[hh] = alpha * acc_scr[hh] + jnp.dot(p.astype(BF16), vb, preferred_element_type=F32)
            m_scr[hh] = m_new

    chunk(i, sown_scr)

    def step(j, carry):
        chunk(j, srel_scr)
        return carry

    lax.fori_loop(0, i, step, 0)
    o_ref[0] = _merge_pair(l_scr, acc_scr)


def _moba(q, k, v, col0, n_pairs, tq):
    b, seq, _ = q.shape
    blk = MOBA_BLOCK
    assert seq % tq == 0 and tq % blk == 0 and seq // blk <= 8
    slopes = _alibi_slopes(N_ALIBI)[MOBA_SLOPE_OFFSET:MOBA_SLOPE_OFFSET + 2 * n_pairs]
    sl = jnp.asarray(np.broadcast_to(slopes.reshape(n_pairs, 2, 1), (n_pairs, 2, tq)).copy())
    return pl.pallas_call(
        _moba_kernel,
        grid=(b, n_pairs, seq // tq),
        in_specs=[pl.BlockSpec((1, tq, PAIR), lambda bi, p, i: (bi, i, col0 + p)),
                  pl.BlockSpec((1, seq, PAIR), lambda bi, p, i: (bi, 0, col0 + p)),
                  pl.BlockSpec((1, seq, PAIR), lambda bi, p, i: (bi, 0, col0 + p)),
                  pl.BlockSpec((1, 2, tq), lambda bi, p, i: (p, 0, 0))],
        out_specs=pl.BlockSpec((1, tq, PAIR), lambda bi, p, i: (bi, i, p)),
        out_shape=jax.ShapeDtypeStruct((b, seq, n_pairs * PAIR), BF16),
        scratch_shapes=[pltpu.VMEM((LANES, PAIR), F32), pltpu.VMEM((2, tq, LANES), F32),
                        pltpu.VMEM((2, tq, tq), F32), pltpu.VMEM((2, tq, tq), F32)] + _stat_scratch(tq),
        compiler_params=_params("arbitrary", "arbitrary", "arbitrary"),
        name="moba_attn",
    )(q, k, v, sl)


def _dil_kernel(q_ref, k_ref, v_ref, o_ref, lse_ref, bias_scr, *, dil, slopes):
    wb = WIN_BLOCK
    n_q = q_ref.shape[2] // wb
    steps = (lax.broadcasted_iota(jnp.int32, (wb, 2 * wb), 0) + wb
             - lax.broadcasted_iota(jnp.int32, (wb, 2 * wb), 1))
    band_ok = (steps >= 0) & (steps <= wb)
    steps_f = (steps * dil).astype(F32)
    for h in range(len(slopes)):
        bias_scr[h] = jnp.where(band_ok, slopes[h] * steps_f, -NEG_INF)
    first = _head_masks()[0]

    def attend(r, rows, krows, first_block):
        for pair in range(2):
            cols = slice(pair * PAIR, (pair + 1) * PAIR)
            qs = _split_heads(q_ref[0, r, rows, cols])
            kb = k_ref[0, r, krows, cols]
            vb = v_ref[0, r, krows, cols]
            outs, lses = [], []
            for hh in range(2):
                bias = bias_scr[2 * pair + hh]
                s = lax.dot_general(qs[hh], kb, _NT, preferred_element_type=F32)
                s = s - (bias[:, wb:] if first_block else bias)
                m = jnp.max(s, axis=-1, keepdims=True)
                p = jnp.exp(s - m)
                l = jnp.sum(p, axis=-1, keepdims=True)
                outs.append(jnp.dot(p.astype(BF16), vb, preferred_element_type=F32) / l)
                lses.append(m + jnp.log(l))
            o_ref[0, r, rows, cols] = jnp.where(first, outs[0], outs[1]).astype(BF16)
            lse_ref[0, r, rows, cols] = jnp.where(first, lses[0], lses[1])

    for r in range(dil):
        attend(r, pl.ds(0, wb), pl.ds(0, wb), True)
    if n_q > 1:
        def body(i, carry):
            for r in range(dil):
                attend(r, pl.ds(pl.multiple_of(i * wb, wb), wb),
                       pl.ds(pl.multiple_of((i - 1) * wb, wb), 2 * wb), False)
            return carry
        lax.fori_loop(1, n_q, body, 0)


def _dilated(q, k, v, group):
    window, dil = DILATED_PAIRS[group]
    assert window // dil == WIN_BLOCK
    b, _, l_sub, w = q.shape
    so = DIL_SLOPE_OFFSETS[group]
    slopes = tuple(float(s) for s in _alibi_slopes(N_ALIBI)[so:so + N_SLOTS_DIL])
    spec = pl.BlockSpec((1, dil, l_sub, w), lambda bi: (bi, 0, 0, 0))
    return pl.pallas_call(
        functools.partial(_dil_kernel, dil=dil, slopes=slopes),
        grid=(b,),
        in_specs=[spec, spec, spec],
        out_specs=[spec, spec],
        out_shape=[jax.ShapeDtypeStruct(q.shape, BF16), jax.ShapeDtypeStruct(q.shape, F32)],
        scratch_shapes=[pltpu.VMEM((N_SLOTS_DIL, WIN_BLOCK, 2 * WIN_BLOCK), F32)],
        compiler_params=_params("arbitrary"),
        name=f"dilated_attn_{dil}",
    )(q, k, v)


def _outproj_kernel(x_ref, mod_ref, g_ref, of_ref, om_ref, o1, l1, o2, l2, o3, l3, gate_ref,
                    wf_ref, wm_ref, wd_ref, wo_ref, x1_ref, h2_ref, perm_scr):
    d = x_ref.shape[1]

    def natural(ref):
        dil, n = ref.shape[1], ref.shape[2]
        if dil == 1:
            return ref[0, 0].astype(F32)
        halves = ref.shape[3] // LANES
        for r in range(dil):
            blk = ref[0, r].astype(F32)
            for c in range(halves):
                perm_scr[c, pl.ds(r, n, stride=dil), :] = blk[:, c * LANES:(c + 1) * LANES]
        return jnp.concatenate([perm_scr[c] for c in range(halves)], axis=1)

    lses = [natural(l) for l in (l1, l2, l3)]
    lmax = jnp.maximum(jnp.maximum(lses[0], lses[1]), lses[2])
    e1, e2, e3 = [jnp.exp(l - lmax) for l in lses]
    den = e1 + e2 + e3
    o_dil = (e1 / den) * natural(o1) + (e2 / den) * natural(o2) + (e3 / den) * natural(o3)
    y = (gate_ref[:, 0:d].astype(F32) * jnp.dot(of_ref[...], wf_ref[...], preferred_element_type=F32)
         + gate_ref[:, d:2 * d].astype(F32) * jnp.dot(om_ref[...], wm_ref[...], preferred_element_type=F32)
         + gate_ref[:, 2 * d:3 * d].astype(F32)
         * jnp.dot(o_dil.astype(BF16), wd_ref[...], preferred_element_type=F32))
    out = jnp.dot(y.astype(BF16), wo_ref[...], preferred_element_type=F32)
    x1 = x_ref[...] + mod_ref[0, 2:3, :] * out
    x1_ref[...] = x1
    h2_ref[...] = _mod_norm(x1, g_ref[...], mod_ref[0, 4:5, :], mod_ref[0, 3:4, :]).astype(BF16)


def _outproj(x2, mod_l, g_norm, o_fox, o_moba, dil_outs, gates, w_f, w_m, w_d, w_o, seq, tm):
    t, d = x2.shape
    per_b = seq // tm
    row = lambda i: (i, 0)
    fix = lambda i: (0, 0)
    dil_args, dil_specs = [], []
    for o, lse in dil_outs:
        _, dil, _, w = o.shape
        dil_args += [o, lse]
        dil_specs += [pl.BlockSpec((1, dil, tm // dil, w), lambda i: (i // per_b, 0, i % per_b, 0))] * 2
    return pl.pallas_call(
        _outproj_kernel,
        grid=(t // tm,),
        in_specs=[pl.BlockSpec((tm, d), row),
                  pl.BlockSpec((1, 6, d), lambda i: (i // per_b, 0, 0)),
                  pl.BlockSpec((1, d), fix),
                  pl.BlockSpec((tm, o_fox.shape[1]), row),
                  pl.BlockSpec((tm, o_moba.shape[1]), row)] + dil_specs + [
                  pl.BlockSpec((tm, N_BRANCH * d), row),
                  pl.BlockSpec(w_f.shape, fix), pl.BlockSpec(w_m.shape, fix),
                  pl.BlockSpec(w_d.shape, fix), pl.BlockSpec(w_o.shape, fix)],
        out_specs=[pl.BlockSpec((tm, d), row), pl.BlockSpec((tm, d), row)],
        out_shape=[jax.ShapeDtypeStruct((t, d), F32), jax.ShapeDtypeStruct((t, d), BF16)],
        scratch_shapes=[pltpu.VMEM((N_SLOTS_DIL * HEAD_DIM // LANES, tm, LANES), F32)],
        compiler_params=_params("arbitrary"),
        name="outproj",
    )(x2, mod_l, g_norm, o_fox, o_moba, *dil_args, gates, w_f, w_m, w_d, w_o)


def _router_kernel(h_ref, w_ref, b_ref, o_ref, cnt_ref, *, n_exp):
    logits = jnp.dot(h_ref[...], w_ref[...], preferred_element_type=F32) + b_ref[...]
    lane = lax.broadcasted_iota(jnp.int32, logits.shape, 1)
    logits = jnp.where(lane < n_exp, logits, -jnp.inf)

    def top(vals):
        m = jnp.max(vals, axis=-1, keepdims=True)
        idx = jnp.min(jnp.where(vals == m, lane, LANES), axis=-1, keepdims=True)
        return m, lane == idx

    m1, hot1 = top(logits)
    m2, hot2 = top(jnp.where(hot1, -jnp.inf, logits))
    e2 = jnp.exp(m2 - m1)
    cw = jnp.where(hot1, 1.0 / (1.0 + e2), 0.0) + jnp.where(hot2, e2 / (1.0 + e2), 0.0)
    o_ref[...] = cw
    n_tok = jnp.sum(jnp.where(cw > 0.0, 1.0, 0.0), axis=0, keepdims=True)
    cnt_ref[...] = jnp.broadcast_to(n_tok, cnt_ref.shape).astype(jnp.int32)


def _router(h2, w_r, b_r, n_exp, tm):
    t, d = h2.shape
    cw, cnt = pl.pallas_call(
        functools.partial(_router_kernel, n_exp=n_exp),
        grid=(t // tm,),
        in_specs=[pl.BlockSpec((tm, d), lambda i: (i, 0)),
                  pl.BlockSpec((d, LANES), lambda i: (0, 0)),
                  pl.BlockSpec((1, LANES), lambda i: (0, 0))],
        out_specs=[pl.BlockSpec((tm, LANES), lambda i: (i, 0)),
                   pl.BlockSpec((8, LANES), lambda i: (i, 0))],
        out_shape=[jax.ShapeDtypeStruct((t, LANES), F32),
                   jax.ShapeDtypeStruct((t // tm * 8, LANES), jnp.int32)],
        compiler_params=_params("arbitrary"),
        name="router",
    )(h2, w_r, b_r)
    return cw, cnt.reshape(t // tm, 8, LANES)[:, 0, :n_exp]


MOE_TILE = 512
MOE_ROWS = 512
MOE_PUT = 128
MOE_GET = 256
MOE_ALIGN = 16


def _moe_layout(cnt, n_rows_static):
    n_tiles, n_exp = cnt.shape
    padded = (cnt + MOE_ALIGN - 1) // MOE_ALIGN * MOE_ALIGN
    length = jnp.sum(padded, axis=0)
    span = (length + (MOE_PUT - MOE_ALIGN) + MOE_ROWS - 1) // MOE_ROWS * MOE_ROWS
    start = jnp.cumsum(span) - span
    off = start[None, :] + jnp.cumsum(padded, axis=0) - padded
    n_steps = n_rows_static // MOE_ROWS
    first = start // MOE_ROWS
    step = jnp.arange(n_steps, dtype=jnp.int32)
    expert = jnp.sum((first[None, :] <= step[:, None]).astype(jnp.int32), axis=1) - 1
    active = step < (first + (length + MOE_ROWS - 1) // MOE_ROWS)[expert]
    return (off.reshape(-1).astype(jnp.int32), cnt.reshape(-1).astype(jnp.int32),
            expert.astype(jnp.int32), active.astype(jnp.int32))


def _dispatch_kernel(off_ref, cnt_ref, h_ref, cw_ref, u_ref, xs_in, xs_ref, stage, sems, *, n_exp):
    del xs_in
    i = pl.program_id(0)
    n_put = MOE_TILE // MOE_PUT
    routed = jnp.where(cw_ref[...].T[0:8] > 0.0, 1.0, 0.0)
    pos = jnp.dot(routed.astype(BF16), u_ref[...], preferred_element_type=F32)
    row = lax.broadcasted_iota(jnp.int32, (MOE_PUT, MOE_TILE), 0).astype(F32)
    h = h_ref[...]

    def copy(e, s):
        slot = e * n_put + s
        dst = pl.multiple_of(off_ref[i * n_exp + e] + s * MOE_PUT, MOE_ALIGN)
        return pltpu.make_async_copy(stage.at[slot], xs_ref.at[pl.ds(dst, MOE_PUT), :], sems.at[slot])

    for e in range(n_exp):
        for s in range(n_put):
            @pl.when(s * MOE_PUT < cnt_ref[i * n_exp + e])
            def _():
                take = (routed[e:e + 1, :] > 0.5) & (pos[e:e + 1, :] == row + float(s * MOE_PUT))
                onehot = jnp.where(take, 1.0, 0.0).astype(BF16)
                stage[e * n_put + s] = jnp.dot(onehot, h, preferred_element_type=F32).astype(BF16)
                copy(e, s).start()

    for e in range(n_exp):
        for s in range(n_put):
            @pl.when(s * MOE_PUT < cnt_ref[i * n_exp + e])
            def _():
                copy(e, s).wait()


def _dispatch(h2, cw, off, cnt, n_exp, n_rows):
    t, d = h2.shape
    u = jnp.asarray(np.arange(MOE_TILE)[:, None] < np.arange(MOE_TILE)[None, :], BF16)
    n_slots = n_exp * (MOE_TILE // MOE_PUT)
    return pl.pallas_call(
        functools.partial(_dispatch_kernel, n_exp=n_exp),
        grid_spec=pltpu.PrefetchScalarGridSpec(
            num_scalar_prefetch=2,
            grid=(t // MOE_TILE,),
            in_specs=[pl.BlockSpec((MOE_TILE, d), lambda i, o, c: (i, 0)),
                      pl.BlockSpec((MOE_TILE, LANES), lambda i, o, c: (i, 0)),
                      pl.BlockSpec((MOE_TILE, MOE_TILE), lambda i, o, c: (0, 0)),
                      pl.BlockSpec(memory_space=pl.ANY)],
            out_specs=pl.BlockSpec(memory_space=pl.ANY),
            scratch_shapes=[pltpu.VMEM((n_slots, MOE_PUT, d), BF16),
                            pltpu.SemaphoreType.DMA((n_slots,))]),
        out_shape=jax.ShapeDtypeStruct((n_rows, d), BF16),
        input_output_aliases={5: 0},
        compiler_params=_params("arbitrary"),
        name="moe_dispatch",
    )(off, cnt, h2, cw, u, jnp.zeros((n_rows, d), BF16))


def _experts_kernel(exp_ref, act_ref, xs_ref, wg_ref, wu_ref, wd_ref, ys_ref, *, n_chunks):
    g = pl.program_id(0)

    @pl.when(act_ref[g] > 0)
    def _():
        x = xs_ref[...]
        tf = wg_ref.shape[2] // n_chunks
        y = jnp.zeros(ys_ref.shape, F32)
        for c in range(n_chunks):
            gate = jnp.dot(x, wg_ref[0, :, c * tf:(c + 1) * tf], preferred_element_type=F32)
            up = jnp.dot(x, wu_ref[0, :, c * tf:(c + 1) * tf], preferred_element_type=F32)
            a = ((gate * _sigmoid(gate)) * up).astype(BF16)
            y = y + jnp.dot(a, wd_ref[0, c * tf:(c + 1) * tf, :], preferred_element_type=F32)
        ys_ref[...] = y.astype(BF16)

    @pl.when(act_ref[g] == 0)
    def _():
        ys_ref[...] = jnp.zeros(ys_ref.shape, BF16)


def _experts(xs, expert, active, w_g, w_u, w_d):
    n_rows, d = xs.shape
    _, _, ff = w_g.shape
    wmap = lambda g, ex, ac: (ex[g], 0, 0)
    return pl.pallas_call(
        functools.partial(_experts_kernel, n_chunks=2),
        grid_spec=pltpu.PrefetchScalarGridSpec(
            num_scalar_prefetch=2,
            grid=(n_rows // MOE_ROWS,),
            in_specs=[pl.BlockSpec((MOE_ROWS, d), lambda g, ex, ac: (g, 0)),
                      pl.BlockSpec((1, d, ff), wmap), pl.BlockSpec((1, d, ff), wmap),
                      pl.BlockSpec((1, ff, d), wmap)],
            out_specs=pl.BlockSpec((MOE_ROWS, d), lambda g, ex, ac: (g, 0))),
        out_shape=jax.ShapeDtypeStruct((n_rows, d), BF16),
        compiler_params=_params("arbitrary"),
        name="moe_experts",
    )(expert, active, xs, w_g, w_u, w_d)


def _combine_kernel(off_ref, cnt_ref, x_ref, mod_ref, cw_ref, l_ref, ys_ref, o_ref, buf, sems, acc_scr,
                    *, n_exp):
    i = pl.program_id(0)
    n_get = MOE_TILE // MOE_GET

    def copy(e, s):
        slot = e * n_get + s
        src = pl.multiple_of(off_ref[i * n_exp + e] + s * MOE_GET, MOE_ALIGN)
        return pltpu.make_async_copy(ys_ref.at[pl.ds(src, MOE_GET), :], buf.at[slot], sems.at[slot])

    for e in range(n_exp):
        for s in range(n_get):
            @pl.when(s * MOE_GET < cnt_ref[i * n_exp + e])
            def _():
                copy(e, s).start()

    cw = cw_ref[...]
    routed = jnp.where(cw > 0.0, 1.0, 0.0)
    pos = jnp.dot(l_ref[...], routed.astype(BF16), preferred_element_type=F32)
    col = lax.broadcasted_iota(jnp.int32, (MOE_TILE, MOE_GET), 1).astype(F32)
    acc_scr[...] = jnp.zeros(acc_scr.shape, F32)
    for e in range(n_exp):
        for s in range(n_get):
            @pl.when(s * MOE_GET < cnt_ref[i * n_exp + e])
            def _():
                copy(e, s).wait()
                take = (cw[:, e:e + 1] > 0.0) & (pos[:, e:e + 1] == col + float(s * MOE_GET))
                onehot = jnp.where(take, 1.0, 0.0).astype(BF16)
                acc_scr[...] += cw[:, e:e + 1] * jnp.dot(onehot, buf[e * n_get + s],
                                                         preferred_element_type=F32)
    o_ref[...] = x_ref[...] + mod_ref[0, 5:6, :] * acc_scr[...]


def _combine(x1, mod_l, cw, ys, off, cnt, n_exp, seq):
    t, d = x1.shape
    per_b = seq // MOE_TILE
    low = jnp.asarray(np.arange(MOE_TILE)[:, None] > np.arange(MOE_TILE)[None, :], BF16)
    n_slots = n_exp * (MOE_TILE // MOE_GET)
    return pl.pallas_call(
        functools.partial(_combine_kernel, n_exp=n_exp),
        grid_spec=pltpu.PrefetchScalarGridSpec(
            num_scalar_prefetch=2,
            grid=(t // MOE_TILE,),
            in_specs=[pl.BlockSpec((MOE_TILE, d), lambda i, o, c: (i, 0)),
                      pl.BlockSpec((1, 6, d), lambda i, o, c: (i // per_b, 0, 0)),
                      pl.BlockSpec((MOE_TILE, LANES), lambda i, o, c: (i, 0)),
                      pl.BlockSpec((MOE_TILE, MOE_TILE), lambda i, o, c: (0, 0)),
                      pl.BlockSpec(memory_space=pl.ANY)],
            out_specs=pl.BlockSpec((MOE_TILE, d), lambda i, o, c: (i, 0)),
            scratch_shapes=[pltpu.VMEM((n_slots, MOE_GET, d), BF16),
                            pltpu.SemaphoreType.DMA((n_slots,)),
                            pltpu.VMEM((MOE_TILE, d), F32)]),
        out_shape=jax.ShapeDtypeStruct((t, d), F32),
        compiler_params=_params("arbitrary"),
        name="moe_combine",
    )(off, cnt, x1, mod_l, cw, low, ys)


def _moe(x1, h2, mod_l, w_r, b_r, w_g, w_u, w_d, seq):
    t, d = x1.shape
    n_exp = w_g.shape[0]
    n_tiles = t // MOE_TILE
    bound = (TOP_K * t + n_tiles * n_exp * (MOE_ALIGN - 1)
             + n_exp * (MOE_PUT - MOE_ALIGN + MOE_ROWS - 1))
    n_rows = (bound + MOE_ROWS - 1) // MOE_ROWS * MOE_ROWS + MOE_ROWS
    cw, cnt = _router(h2, w_r, b_r, n_exp, MOE_TILE)
    off, cnt, expert, active = _moe_layout(cnt, n_rows)
    xs = _dispatch(h2, cw, off, cnt, n_exp, n_rows)
    ys = _experts(xs, expert, active, w_g, w_u, w_d)
    return _combine(x1, mod_l, cw, ys, off, cnt, n_exp, seq)


def _ffn_kernel(x_ref, h_ref, mod_ref, cw_ref, wg_ref, wu_ref, wd_ref, o_ref, acc_scr):
    e, f = pl.program_id(1), pl.program_id(2)

    @pl.when((e == 0) & (f == 0))
    def _():
        acc_scr[...] = jnp.zeros(acc_scr.shape, F32)

    h = h_ref[...]
    g = jnp.dot(h, wg_ref[0], preferred_element_type=F32)
    u = jnp.dot(h, wu_ref[0], preferred_element_type=F32)
    lane = lax.broadcasted_iota(jnp.int32, cw_ref.shape, 1)
    cw = jnp.sum(jnp.where(lane == e, cw_ref[...], 0.0), axis=-1, keepdims=True)
    a = ((g * _sigmoid(g)) * u).astype(BF16)
    acc_scr[...] += cw * jnp.dot(a, wd_ref[0], preferred_element_type=F32)

    @pl.when((e == pl.num_programs(1) - 1) & (f == pl.num_programs(2) - 1))
    def _():
        o_ref[...] = x_ref[...] + mod_ref[0, 5:6, :] * acc_scr[...]


def _ffn(x1, h2, mod_l, cw, w_g, w_u, w_d, seq, tm, n_f):
    t, d = x1.shape
    n_exp, _, ff = w_g.shape
    tf = ff // n_f
    per_b = seq // tm
    return pl.pallas_call(
        _ffn_kernel,
        grid=(t // tm, n_exp, n_f),
        in_specs=[pl.BlockSpec((tm, d), lambda i, e, f: (i, 0)),
                  pl.BlockSpec((tm, d), lambda i, e, f: (i, 0)),
                  pl.BlockSpec((1, 6, d), lambda i, e, f: (i // per_b, 0, 0)),
                  pl.BlockSpec((tm, LANES), lambda i, e, f: (i, 0)),
                  pl.BlockSpec((1, d, tf), lambda i, e, f: (e, 0, f)),
                  pl.BlockSpec((1, d, tf), lambda i, e, f: (e, 0, f)),
                  pl.BlockSpec((1, tf, d), lambda i, e, f: (e, f, 0))],
        out_specs=pl.BlockSpec((tm, d), lambda i, e, f: (i, 0)),
        out_shape=jax.ShapeDtypeStruct((t, d), F32),
        scratch_shapes=[pltpu.VMEM((tm, d), F32)],
        compiler_params=_params("arbitrary", "arbitrary", "arbitrary"),
        name="swiglu",
    )(x1, h2, mod_l, cw, w_g, w_u, w_d)


def _pad_cols(a, width):
    return jnp.pad(a, ((0, 0), (0, width - a.shape[1])))


def kernel(x, c, w_ada, b_ada, norm_mix, norm_ffn, w_in, b_fgate, q_gain, k_gain, w_br_fox, w_br_moba,
           w_br_dil, w_out, w_ffn_gate, w_ffn_up, w_ffn_down, w_router, b_router, w_exp_gate,
           w_exp_up, w_exp_down):
    b, seq, d = x.shape
    depth = w_ada.shape[0]
    t = b * seq
    tm = 512
    n_pairs_fox = N_HEADS_FOX // 2
    n_pairs_moba = N_HEADS_MOBA // 2
    tq, tk = 512, 512

    mod = _adaln(c, w_ada, b_ada).reshape(depth, b, 6, d)
    x2 = x.reshape(t, d)
    w_in_t = jnp.transpose(w_in, (2, 0, 1))
    for l in range(depth):
        wl = w_in_t[:, l, :].astype(BF16)
        f0 = 3 * MIX_WIDTH
        w_qkv, w_gate = wl[:f0], wl[f0 + N_HEADS_FOX:]
        w_f = jnp.pad(wl[f0:f0 + N_HEADS_FOX], ((0, LANES - N_HEADS_FOX), (0, 0)))
        b_f = _pad_cols(b_fgate[l].reshape(1, -1), LANES)
        outs = _inproj(x2, mod[l], norm_mix[l].reshape(1, d), w_qkv, w_gate, w_f, q_gain[l].reshape(1, -1),
                       k_gain[l].reshape(1, -1), b_f, seq, tm)
        q_m, k_m, v_m = (o.reshape(b, seq, -1) for o in outs[0:3])
        gates, lf = outs[12], outs[13]

        cum = _decay(lf.reshape(b, seq, LANES))[:, :N_HEADS_FOX]
        ck = cum.reshape(b, n_pairs_fox, 2, seq // tk, tk).transpose(0, 1, 3, 2, 4)
        o_fox = _fox(q_m, k_m, v_m, ck, 0, n_pairs_fox, tq, tk).reshape(t, -1)
        o_moba = _moba(q_m, k_m, v_m, n_pairs_fox, n_pairs_moba, tq).reshape(t, -1)
        dil_outs = [_dilated(*outs[3 + 3 * g:6 + 3 * g], g) for g in range(len(DILATED_PAIRS))]

        x1, h2 = _outproj(x2, mod[l], norm_ffn[l].reshape(1, d), o_fox, o_moba, dil_outs, gates,
                          w_br_fox[l].astype(BF16), w_br_moba[l].astype(BF16),
                          w_br_dil[l].astype(BF16), w_out[l].astype(BF16), seq, tm)
        i = l // 2
        if l % 2 == 0:
            cw = jnp.ones((t, LANES), F32)
            x2 = _ffn(x1, h2, mod[l], cw, w_ffn_gate[i:i + 1].astype(BF16),
                      w_ffn_up[i:i + 1].astype(BF16), w_ffn_down[i:i + 1].astype(BF16), seq, 1024, 2)
        else:
            x2 = _moe(x1, h2, mod[l], _pad_cols(w_router[i], LANES).astype(BF16),
                      _pad_cols(b_router[i].reshape(1, -1), LANES), w_exp_gate[i].astype(BF16),
                      w_exp_up[i].astype(BF16), w_exp_down[i].astype(BF16), seq)
    return x2.reshape(b, seq, d)
```

```python
import functools

import numpy as np
import jax
import jax.numpy as jnp
from jax import lax
from jax.experimental import pallas as pl
from jax.experimental.pallas import tpu as pltpu

HEAD_DIM = 64
N_HEADS_FOX = 6
N_HEADS_MOBA = 6
DILATED_PAIRS = ((128, 1), (512, 4), (2048, 16))
N_SLOTS_DIL = 4
N_HEADS_DIL = N_SLOTS_DIL * len(DILATED_PAIRS)
N_HEADS = N_HEADS_FOX + N_HEADS_MOBA + N_HEADS_DIL
MIX_WIDTH = N_HEADS * HEAD_DIM
N_BRANCH = 3
MOBA_BLOCK = 256
MOBA_TOPK = 3
WIN_BLOCK = 128
N_ALIBI = N_HEADS_MOBA + N_HEADS_DIL
DIL_SLOPE_OFFSETS = (0, N_SLOTS_DIL, 2 * N_SLOTS_DIL + N_HEADS_MOBA)
MOBA_SLOPE_OFFSET = 2 * N_SLOTS_DIL
TOP_K = 2
RMS_EPS = 1e-6
NEG_INF = -1e30

LANES = 128
PAIR = 2 * HEAD_DIM
VMEM_LIMIT = 56 * 1024 * 1024

F32 = jnp.float32
BF16 = jnp.bfloat16
_NT = (((1,), (1,)), ((), ()))


def _alibi_slopes(n):
    return (2.0 ** (-8.0 * np.arange(1, n + 1) / n)).astype(np.float32)


def _sigmoid(x):
    return 1.0 / (1.0 + jnp.exp(-x))


def _params(*sem):
    return pltpu.CompilerParams(dimension_semantics=sem, vmem_limit_bytes=VMEM_LIMIT)


def _adaln_kernel(c_ref, w_ref, b_ref, o_ref):
    c = c_ref[...]
    cond = c * _sigmoid(c)
    o_ref[0] = jnp.dot(cond, w_ref[0], precision=lax.Precision.HIGHEST,
                       preferred_element_type=F32) + b_ref[0]


def _adaln(c, w_ada, b_ada):
    depth, d, n = w_ada.shape
    b = c.shape[0]
    tn = 1536
    return pl.pallas_call(
        _adaln_kernel,
        grid=(depth, n // tn),
        in_specs=[pl.BlockSpec((b, d), lambda l, j: (0, 0)),
                  pl.BlockSpec((1, d, tn), lambda l, j: (l, 0, j)),
                  pl.BlockSpec((1, 1, tn), lambda l, j: (l, 0, j))],
        out_specs=pl.BlockSpec((1, b, tn), lambda l, j: (l, 0, j)),
        out_shape=jax.ShapeDtypeStruct((depth, b, n), F32),
        compiler_params=_params("arbitrary", "arbitrary"),
        name="adaln",
    )(c, w_ada, b_ada.reshape(depth, 1, n))


def _mod_norm(x, g, scale, shift):
    ms = jnp.mean(x * x, axis=-1, keepdims=True)
    return (x * lax.rsqrt(ms + RMS_EPS) * g) * (1.0 + scale) + shift


def _inproj_kernel(x_ref, mod_ref, g_ref, w_ref, wg_ref, wf_ref, qg_ref, kg_ref, bf_ref, e_ref,
                   q_ref, k_ref, v_ref, qd1, kd1, vd1, qd2, kd2, vd2, qd3, kd3, vd3,
                   gate_ref, lf_ref, perm_scr, *, n_main):
    h = _mod_norm(x_ref[...], g_ref[...], mod_ref[0, 1:2, :], mod_ref[0, 0:1, :]).astype(BF16)
    tm = x_ref.shape[0]
    qk_chunk = 4 * HEAD_DIM
    q_outs = (q_ref, qd1, qd2, qd3)
    k_outs = (k_ref, kd1, kd2, kd3)
    v_outs = (v_ref, vd1, vd2, vd3)

    def put(outs, col, val):
        if col < n_main:
            outs[0][:, col:col + qk_chunk] = val.astype(BF16)
            return
        g = (col - n_main) // qk_chunk
        dil = DILATED_PAIRS[g][1]
        if dil == 1:
            outs[1 + g][0, 0] = val.astype(BF16)
            return
        halves = qk_chunk // LANES
        for c in range(halves):
            perm_scr[c] = val[:, c * LANES:(c + 1) * LANES]
        for r in range(dil):
            outs[1 + g][0, r] = jnp.concatenate(
                [perm_scr[c, pl.ds(r, tm // dil, stride=dil), :] for c in range(halves)],
                axis=1).astype(BF16)

    def proj(wt_ref, col0, width):
        return lax.dot_general(h, wt_ref[col0:col0 + width, :], _NT, preferred_element_type=F32)

    def normed(col0, gain_ref, gcol, scale):
        y = proj(w_ref, col0, qk_chunk)
        ss = jnp.dot((y * y).astype(BF16), e_ref[...], preferred_element_type=F32)
        r = lax.rsqrt(ss * (1.0 / HEAD_DIM) + RMS_EPS)
        return (y * r * gain_ref[:, gcol:gcol + qk_chunk]) * scale

    for c in range(MIX_WIDTH // qk_chunk):
        col = c * qk_chunk
        put(q_outs, col, normed(col, qg_ref, col, HEAD_DIM ** -0.5))
        put(k_outs, col, normed(MIX_WIDTH + col, kg_ref, col, 1.0))
        put(v_outs, col, proj(w_ref, 2 * MIX_WIDTH + col, qk_chunk))

    gchunk = 512
    for c in range(wg_ref.shape[0] // gchunk):
        y = proj(wg_ref, c * gchunk, gchunk)
        gate_ref[:, c * gchunk:(c + 1) * gchunk] = _sigmoid(y).astype(BF16)

    f = proj(wf_ref, 0, LANES) + bf_ref[...]
    lf_ref[...] = jnp.minimum(f, 0.0) - jnp.log(1.0 + jnp.exp(-jnp.abs(f)))


def _inproj(x2, mod_l, g_norm, w_qkv, w_gate, w_f, q_gain, k_gain, b_f, seq, tm):
    t, d = x2.shape
    n_main = (N_HEADS_FOX + N_HEADS_MOBA) * HEAD_DIM
    dil_w = N_SLOTS_DIL * HEAD_DIM
    e = (np.arange(dil_w)[:, None] // HEAD_DIM == np.arange(dil_w)[None, :] // HEAD_DIM)
    e = jnp.asarray(e, BF16)
    per_b = seq // tm
    row = lambda i: (i, 0)
    fix = lambda i: (0, 0)
    qkv_shapes = [jax.ShapeDtypeStruct((t, n_main), BF16)] * 3
    qkv_specs = [pl.BlockSpec((tm, n_main), row)] * 3
    for _, dil in DILATED_PAIRS:
        assert tm % (16 * dil) == 0
        qkv_shapes += [jax.ShapeDtypeStruct((t // seq, dil, seq // dil, dil_w), BF16)] * 3
        qkv_specs += [pl.BlockSpec((1, dil, tm // dil, dil_w),
                                   lambda i: (i // per_b, 0, i % per_b, 0))] * 3
    out_shapes = qkv_shapes + [jax.ShapeDtypeStruct((t, N_BRANCH * d), BF16),
                               jax.ShapeDtypeStruct((t, LANES), F32)]
    out_specs = qkv_specs + [pl.BlockSpec((tm, N_BRANCH * d), row), pl.BlockSpec((tm, LANES), row)]
    return pl.pallas_call(
        functools.partial(_inproj_kernel, n_main=n_main),
        grid=(t // tm,),
        in_specs=[pl.BlockSpec((tm, d), row),
                  pl.BlockSpec((1, 6, d), lambda i: (i // per_b, 0, 0)),
                  pl.BlockSpec((1, d), fix),
                  pl.BlockSpec(w_qkv.shape, fix, pipeline_mode=pl.Buffered(1)),
                  pl.BlockSpec(w_gate.shape, fix, pipeline_mode=pl.Buffered(1)),
                  pl.BlockSpec(w_f.shape, fix),
                  pl.BlockSpec((1, MIX_WIDTH), fix),
                  pl.BlockSpec((1, MIX_WIDTH), fix),
                  pl.BlockSpec((1, LANES), fix),
                  pl.BlockSpec(e.shape, fix)],
        out_specs=out_specs,
        out_shape=out_shapes,
        scratch_shapes=[pltpu.VMEM((dil_w // LANES, tm, LANES), F32)],
        compiler_params=_params("arbitrary"),
        name="inproj",
    )(x2, mod_l, g_norm, w_qkv, w_gate, w_f, q_gain, k_gain, b_f, e)


def _decay_kernel(lf_ref, u_ref, o_ref, *, blk):
    lf_t = lf_ref[0].T
    seq = lf_t.shape[1]
    carry = jnp.zeros((8, 1), F32)
    for j in range(seq // blk):
        c = jnp.dot(lf_t[0:8, j * blk:(j + 1) * blk], u_ref[...], precision=lax.Precision.HIGHEST,
                    preferred_element_type=F32) + carry
        o_ref[0, :, j * blk:(j + 1) * blk] = c
        carry = c[:, blk - 1:blk]


def _decay(lf, blk=256):
    b, seq, _ = lf.shape
    u = jnp.asarray(np.arange(blk)[:, None] <= np.arange(blk)[None, :], F32)
    return pl.pallas_call(
        functools.partial(_decay_kernel, blk=blk),
        grid=(b,),
        in_specs=[pl.BlockSpec((1, seq, LANES), lambda i: (i, 0, 0)),
                  pl.BlockSpec((blk, blk), lambda i: (0, 0))],
        out_specs=pl.BlockSpec((1, 8, seq), lambda i: (i, 0, 0)),
        out_shape=jax.ShapeDtypeStruct((b, 8, seq), F32),
        compiler_params=_params("arbitrary"),
        name="fox_decay",
    )(lf, u)


def _head_masks():
    lane = lax.broadcasted_iota(jnp.int32, (1, PAIR), 1)
    return lane < HEAD_DIM, lane >= HEAD_DIM


def _split_heads(q):
    return [jnp.where(m, q, jnp.zeros_like(q)) for m in _head_masks()]


def _softmax_stage(s, m_ref, l_ref, p_ref, a_ref):
    m_prev = m_ref[...]
    m_new = jnp.maximum(m_prev, jnp.max(s, axis=-1, keepdims=True))
    p = jnp.exp(s - jnp.concatenate([m_new] * (s.shape[1] // LANES), axis=1))
    alpha = jnp.exp(m_prev - m_new)
    l_ref[...] = alpha * l_ref[...] + jnp.sum(p, axis=-1, keepdims=True)
    m_ref[...] = m_new
    p_ref[...] = p.astype(BF16)
    a_ref[...] = alpha


def _value_stage(vb, acc_ref, p_ref, a_ref):
    acc_ref[...] = a_ref[...] * acc_ref[...] + jnp.dot(p_ref[...], vb, preferred_element_type=F32)


def _reset(m_scr, l_scr, acc_scr):
    m_scr[...] = jnp.full(m_scr.shape, NEG_INF, F32)
    l_scr[...] = jnp.zeros(l_scr.shape, F32)
    acc_scr[...] = jnp.zeros(acc_scr.shape, F32)


def _merge_pair(l_scr, acc_scr):
    first = _head_masks()[0]
    return jnp.where(first, acc_scr[0] / l_scr[0], acc_scr[1] / l_scr[1]).astype(BF16)


def _stat_scratch(tq):
    return [pltpu.VMEM((2, tq, LANES), F32), pltpu.VMEM((2, tq, LANES), F32),
            pltpu.VMEM((2, tq, PAIR), F32), pltpu.VMEM((2, 2, tq, tq), BF16),
            pltpu.VMEM((2, 2, tq, LANES), F32)]


def _fox_kernel(q_ref, k_ref, v_ref, ck_ref, o_ref, m_scr, l_scr, acc_scr, p_scr, a_scr, *, tq):
    i = pl.program_id(2)
    qs = _split_heads(q_ref[0])
    causal = (lax.broadcasted_iota(jnp.int32, (tq, tq), 1)
              <= lax.broadcasted_iota(jnp.int32, (tq, tq), 0))
    _reset(m_scr, l_scr, acc_scr)

    def scores(j, masked):
        kb = k_ref[0, j * tq:(j + 1) * tq, :]
        ck = ck_ref[0, 0, j]
        for hh in range(2):
            s = lax.dot_general(qs[hh], kb, _NT, preferred_element_type=F32) - ck[hh:hh + 1, :]
            if masked:
                s = jnp.where(causal, s, NEG_INF)
            _softmax_stage(s, m_scr.at[hh], l_scr.at[hh], p_scr.at[j % 2, hh], a_scr.at[j % 2, hh])

    def values(j):
        vb = v_ref[0, j * tq:(j + 1) * tq, :]
        for hh in range(2):
            _value_stage(vb, acc_scr.at[hh], p_scr.at[j % 2, hh], a_scr.at[j % 2, hh])

    for c in range(k_ref.shape[1] // tq):
        @pl.when(i == c)
        def _(c=c):
            scores(0, c == 0)
            for j in range(1, c + 1):
                scores(j, j == c)
                values(j - 1)
            values(c)

    o_ref[0] = _merge_pair(l_scr, acc_scr)


def _fox(q, k, v, ck, col0, n_pairs, tq):
    b, seq, _ = q.shape
    tk = tq
    return pl.pallas_call(
        functools.partial(_fox_kernel, tq=tq),
        grid=(b, n_pairs, seq // tq),
        in_specs=[pl.BlockSpec((1, tq, PAIR), lambda bi, p, i: (bi, i, col0 + p)),
                  pl.BlockSpec((1, seq, PAIR), lambda bi, p, i: (bi, 0, col0 + p)),
                  pl.BlockSpec((1, seq, PAIR), lambda bi, p, i: (bi, 0, col0 + p)),
                  pl.BlockSpec((1, 1, seq // tk, 2, tk), lambda bi, p, i: (bi, p, 0, 0, 0))],
        out_specs=pl.BlockSpec((1, tq, PAIR), lambda bi, p, i: (bi, i, p)),
        out_shape=jax.ShapeDtypeStruct((b, seq, n_pairs * PAIR), BF16),
        scratch_shapes=_stat_scratch(tq),
        compiler_params=_params("arbitrary", "arbitrary", "arbitrary"),
        name="fox_attn",
    )(q, k, v, ck)


def _moba_kernel(q_ref, k_ref, v_ref, sl_ref, o_ref, km_scr, sel_scr, srel_scr, sown_scr, m_scr,
                 l_scr, acc_scr, p_scr, a_scr):
    blk = MOBA_BLOCK
    i = pl.program_id(2)
    n_blk = k_ref.shape[1] // blk

    @pl.when(i == 0)
    def _():
        km_scr[...] = jnp.zeros(km_scr.shape, F32)
        for n in range(n_blk):
            kb = k_ref[0, n * blk:(n + 1) * blk, :].astype(F32)
            km_scr[n:n + 1, :] = jnp.sum(kb, axis=0, keepdims=True) * (1.0 / blk)

    tq = q_ref.shape[1]
    per = tq // blk
    qs = _split_heads(q_ref[0])
    _reset(m_scr, l_scr, acc_scr)

    @pl.when(i == 0)
    def _():
        row = lax.broadcasted_iota(jnp.int32, (tq, tq), 0)
        col = lax.broadcasted_iota(jnp.int32, (tq, tq), 1)
        rel_f = (row - col).astype(F32)
        hidden = (col // blk > row // blk) | ((col // blk == row // blk) & (col > row))
        for hh in range(2):
            srel = sl_ref[0, hh:hh + 1, :] * rel_f
            srel_scr[hh] = srel
            sown_scr[hh] = jnp.where(hidden, -NEG_INF, srel)

    nbp = 8
    blk_n = lax.broadcasted_iota(jnp.int32, (nbp, tq), 0)
    q_blk = i * per + lax.broadcasted_iota(jnp.int32, (nbp, tq), 1) // blk
    for hh in range(2):
        g = lax.dot_general(km_scr[...], qs[hh].astype(F32), _NT, precision=lax.Precision.HIGHEST,
                            preferred_element_type=F32)[0:nbp]
        cnt = jnp.zeros((nbp, tq), F32)
        for m in range(n_blk):
            gm = g[m:m + 1, :]
            beats = ((gm > g) | ((gm == g) & (blk_n > m))) & (q_blk > m)
            cnt = cnt + jnp.where(beats, 1.0, 0.0)
        sel = jnp.where(((blk_n < q_blk) & (cnt < float(MOBA_TOPK))) | (blk_n == q_blk), 1.0, 0.0)
        sel_scr[hh] = jnp.concatenate([sel, jnp.zeros((LANES - nbp, tq), F32)], axis=0).T

    lane_n = lax.broadcasted_iota(jnp.int32, (tq, LANES), 1)

    def scores(qc, j, bias_scr, slot):
        kb = k_ref[0, j * tq:(j + 1) * tq, :]
        for hh in range(2):
            soff = sl_ref[0, hh:hh + 1, 0:LANES] * float((qc - j) * tq)
            t = lax.dot_general(qs[hh], kb, _NT, preferred_element_type=F32) - bias_scr[hh]
            sel = sel_scr[hh]
            cols = [jnp.broadcast_to(jnp.sum(jnp.where(lane_n == j * per + c, sel, 0.0), axis=-1,
                                             keepdims=True), (tq, LANES)) > 0.5 for c in range(per)]
            m_prev = m_scr[hh]
            m_new = m_prev
            for c in range(per):
                mx = jnp.max(t[:, c * blk:(c + 1) * blk], axis=-1, keepdims=True)
                m_new = jnp.maximum(m_new, jnp.where(cols[c], mx - soff, NEG_INF))
            shift = jnp.concatenate(
                [jnp.where(cols[c], m_new + soff, -NEG_INF) for c in range(per)
                 for _ in range(blk // LANES)], axis=1)
            p = jnp.exp(t - shift)
            alpha = jnp.exp(m_prev - m_new)
            l_scr[hh] = alpha * l_scr[hh] + jnp.sum(p, axis=-1, keepdims=True)
            m_scr[hh] = m_new
            p_scr[slot, hh] = p.astype(BF16)
            a_scr[slot, hh] = alpha

    def values(j, slot):
        vb = v_ref[0, j * tq:(j + 1) * tq, :]
        for hh in range(2):
            _value_stage(vb, acc_scr.at[hh], p_scr.at[slot, hh], a_scr.at[slot, hh])

    for c in range(k_ref.shape[1] // tq):
        @pl.when(i == c)
        def _(c=c):
            order = [c] + list(range(c))
            scores(c, c, sown_scr, 0)
            for n in range(1, len(order)):
                scores(c, order[n], srel_scr, n % 2)
                values(order[n - 1], (n - 1) % 2)
            values(order[-1], (len(order) - 1) % 2)

    o_ref[0] = _merge_pair(l_scr, acc_scr)


def _moba(q, k, v, col0, n_pairs, tq):
    b, seq, _ = q.shape
    blk = MOBA_BLOCK
    assert seq % tq == 0 and tq % blk == 0 and seq // blk <= 8
    slopes = _alibi_slopes(N_ALIBI)[MOBA_SLOPE_OFFSET:MOBA_SLOPE_OFFSET + 2 * n_pairs]
    sl = jnp.asarray(np.broadcast_to(slopes.reshape(n_pairs, 2, 1), (n_pairs, 2, tq)).copy())
    return pl.pallas_call(
        _moba_kernel,
        grid=(b, n_pairs, seq // tq),
        in_specs=[pl.BlockSpec((1, tq, PAIR), lambda bi, p, i: (bi, i, col0 + p)),
                  pl.BlockSpec((1, seq, PAIR), lambda bi, p, i: (bi, 0, col0 + p)),
                  pl.BlockSpec((1, seq, PAIR), lambda bi, p, i: (bi, 0, col0 + p)),
                  pl.BlockSpec((1, 2, tq), lambda bi, p, i: (p, 0, 0))],
        out_specs=pl.BlockSpec((1, tq, PAIR), lambda bi, p, i: (bi, i, p)),
        out_shape=jax.ShapeDtypeStruct((b, seq, n_pairs * PAIR), BF16),
        scratch_shapes=[pltpu.VMEM((LANES, PAIR), F32), pltpu.VMEM((2, tq, LANES), F32),
                        pltpu.VMEM((2, tq, tq), F32), pltpu.VMEM((2, tq, tq), F32)] + _stat_scratch(tq),
        compiler_params=_params("arbitrary", "arbitrary", "arbitrary"),
        name="moba_attn",
    )(q, k, v, sl)


def _dil_kernel(q_ref, k_ref, v_ref, o_ref, lse_ref, *, dil, slopes, n_q):
    wb = WIN_BLOCK
    seq = q_ref.shape[1]
    nb = seq // wb
    steps = (lax.broadcasted_iota(jnp.int32, (wb, 2 * wb), 0) + wb
             - lax.broadcasted_iota(jnp.int32, (wb, 2 * wb), 1))
    band_ok = (steps >= 0) & (steps <= wb)
    steps_f = (steps * dil).astype(F32)
    blk = lax.broadcasted_iota(jnp.int32, (nb, 1, 2 * wb), 0)
    key = lax.broadcasted_iota(jnp.int32, (nb, 1, 2 * wb), 2)
    no_prev = jnp.where((blk % n_q == 0) & (key < wb), -NEG_INF, 0.0)
    first = _head_masks()[0]

    def band(ref, cols):
        cur = ref[0, :, cols]
        prev = jnp.concatenate([cur[0:wb], cur[0:seq - wb]], axis=0)
        return jnp.concatenate([prev.reshape(nb, wb, PAIR), cur.reshape(nb, wb, PAIR)], axis=1)

    for pair in range(2):
        cols = slice(pair * PAIR, (pair + 1) * PAIR)
        qs = [qh.reshape(nb, wb, PAIR) for qh in _split_heads(q_ref[0, :, cols])]
        kband, vband = band(k_ref, cols), band(v_ref, cols)
        outs, lses = [], []
        for hh in range(2):
            bias = jnp.where(band_ok, slopes[2 * pair + hh] * steps_f, -NEG_INF)
            s = jnp.einsum("bqd,bkd->bqk", qs[hh], kband, preferred_element_type=F32)
            s = s - bias[None] - no_prev
            m = jnp.max(s, axis=-1, keepdims=True)
            p = jnp.exp(s - m)
            l = jnp.sum(p, axis=-1, keepdims=True)
            o = jnp.einsum("bqk,bkd->bqd", p.astype(BF16), vband, preferred_element_type=F32)
            outs.append(o / l)
            lses.append(jnp.broadcast_to(m + jnp.log(l), (nb, wb, PAIR)))
        o_ref[0, :, cols] = jnp.where(first, outs[0], outs[1]).reshape(seq, PAIR).astype(BF16)
        lse_ref[0, :, cols] = jnp.where(first, lses[0], lses[1]).reshape(seq, PAIR)


def _dilated(q, k, v, group):
    window, dil = DILATED_PAIRS[group]
    assert window // dil == WIN_BLOCK
    b, _, l_sub, w = q.shape
    seq = dil * l_sub
    so = DIL_SLOPE_OFFSETS[group]
    slopes = tuple(float(s) for s in _alibi_slopes(N_ALIBI)[so:so + N_SLOTS_DIL])
    spec = pl.BlockSpec((1, seq, w), lambda bi: (bi, 0, 0))
    flat = lambda a: a.reshape(b, seq, w)
    o, lse = pl.pallas_call(
        functools.partial(_dil_kernel, dil=dil, slopes=slopes, n_q=l_sub // WIN_BLOCK),
        grid=(b,),
        in_specs=[spec, spec, spec],
        out_specs=[spec, spec],
        out_shape=[jax.ShapeDtypeStruct((b, seq, w), BF16), jax.ShapeDtypeStruct((b, seq, w), F32)],
        compiler_params=_params("arbitrary"),
        name=f"dilated_attn_{dil}",
    )(flat(q), flat(k), flat(v))
    return o.reshape(q.shape), lse.reshape(q.shape)


def _outproj_kernel(x_ref, mod_ref, g_ref, of_ref, om_ref, o1, l1, o2, l2, o3, l3, gate_ref,
                    wf_ref, wm_ref, wd_ref, wo_ref, x1_ref, h2_ref, perm_scr):
    d = x_ref.shape[1]

    def natural(ref):
        dil, n = ref.shape[1], ref.shape[2]
        if dil == 1:
            return ref[0, 0].astype(F32)
        halves = ref.shape[3] // LANES
        for r in range(dil):
            blk = ref[0, r].astype(F32)
            for c in range(halves):
                perm_scr[c, pl.ds(r, n, stride=dil), :] = blk[:, c * LANES:(c + 1) * LANES]
        return jnp.concatenate([perm_scr[c] for c in range(halves)], axis=1)

    lses = [natural(l) for l in (l1, l2, l3)]
    lmax = jnp.maximum(jnp.maximum(lses[0], lses[1]), lses[2])
    e1, e2, e3 = [jnp.exp(l - lmax) for l in lses]
    den = e1 + e2 + e3
    o_dil = (e1 / den) * natural(o1) + (e2 / den) * natural(o2) + (e3 / den) * natural(o3)
    y = (gate_ref[:, 0:d].astype(F32) * jnp.dot(of_ref[...], wf_ref[...], preferred_element_type=F32)
         + gate_ref[:, d:2 * d].astype(F32) * jnp.dot(om_ref[...], wm_ref[...], preferred_element_type=F32)
         + gate_ref[:, 2 * d:3 * d].astype(F32)
         * jnp.dot(o_dil.astype(BF16), wd_ref[...], preferred_element_type=F32))
    out = jnp.dot(y.astype(BF16), wo_ref[...], preferred_element_type=F32)
    x1 = x_ref[...] + mod_ref[0, 2:3, :] * out
    x1_ref[...] = x1
    h2_ref[...] = _mod_norm(x1, g_ref[...], mod_ref[0, 4:5, :], mod_ref[0, 3:4, :]).astype(BF16)


def _outproj(x2, mod_l, g_norm, o_fox, o_moba, dil_outs, gates, w_f, w_m, w_d, w_o, seq, tm):
    t, d = x2.shape
    per_b = seq // tm
    row = lambda i: (i, 0)
    fix = lambda i: (0, 0)
    dil_args, dil_specs = [], []
    for o, lse in dil_outs:
        _, dil, _, w = o.shape
        dil_args += [o, lse]
        dil_specs += [pl.BlockSpec((1, dil, tm // dil, w), lambda i: (i // per_b, 0, i % per_b, 0))] * 2
    return pl.pallas_call(
        _outproj_kernel,
        grid=(t // tm,),
        in_specs=[pl.BlockSpec((tm, d), row),
                  pl.BlockSpec((1, 6, d), lambda i: (i // per_b, 0, 0)),
                  pl.BlockSpec((1, d), fix),
                  pl.BlockSpec((tm, o_fox.shape[1]), row),
                  pl.BlockSpec((tm, o_moba.shape[1]), row)] + dil_specs + [
                  pl.BlockSpec((tm, N_BRANCH * d), row),
                  pl.BlockSpec(w_f.shape, fix), pl.BlockSpec(w_m.shape, fix),
                  pl.BlockSpec(w_d.shape, fix), pl.BlockSpec(w_o.shape, fix)],
        out_specs=[pl.BlockSpec((tm, d), row), pl.BlockSpec((tm, d), row)],
        out_shape=[jax.ShapeDtypeStruct((t, d), F32), jax.ShapeDtypeStruct((t, d), BF16)],
        scratch_shapes=[pltpu.VMEM((N_SLOTS_DIL * HEAD_DIM // LANES, tm, LANES), F32)],
        compiler_params=_params("arbitrary"),
        name="outproj",
    )(x2, mod_l, g_norm, o_fox, o_moba, *dil_args, gates, w_f, w_m, w_d, w_o)


def _router_kernel(h_ref, w_ref, b_ref, o_ref, cnt_ref, *, n_exp):
    logits = jnp.dot(h_ref[...], w_ref[...], preferred_element_type=F32) + b_ref[...]
    lane = lax.broadcasted_iota(jnp.int32, logits.shape, 1)
    logits = jnp.where(lane < n_exp, logits, -jnp.inf)

    def top(vals):
        m = jnp.max(vals, axis=-1, keepdims=True)
        idx = jnp.min(jnp.where(vals == m, lane, LANES), axis=-1, keepdims=True)
        return m, lane == idx

    m1, hot1 = top(logits)
    m2, hot2 = top(jnp.where(hot1, -jnp.inf, logits))
    e2 = jnp.exp(m2 - m1)
    cw = jnp.where(hot1, 1.0 / (1.0 + e2), 0.0) + jnp.where(hot2, e2 / (1.0 + e2), 0.0)
    o_ref[...] = cw
    n_tok = jnp.sum(jnp.where(cw > 0.0, 1.0, 0.0), axis=0, keepdims=True)
    cnt_ref[...] = jnp.broadcast_to(n_tok, cnt_ref.shape).astype(jnp.int32)


def _router(h2, w_r, b_r, n_exp, tm):
    t, d = h2.shape
    cw, cnt = pl.pallas_call(
        functools.partial(_router_kernel, n_exp=n_exp),
        grid=(t // tm,),
        in_specs=[pl.BlockSpec((tm, d), lambda i: (i, 0)),
                  pl.BlockSpec((d, LANES), lambda i: (0, 0)),
                  pl.BlockSpec((1, LANES), lambda i: (0, 0))],
        out_specs=[pl.BlockSpec((tm, LANES), lambda i: (i, 0)),
                   pl.BlockSpec((8, LANES), lambda i: (i, 0))],
        out_shape=[jax.ShapeDtypeStruct((t, LANES), F32),
                   jax.ShapeDtypeStruct((t // tm * 8, LANES), jnp.int32)],
        compiler_params=_params("arbitrary"),
        name="router",
    )(h2, w_r, b_r)
    return cw, cnt.reshape(t // tm, 8, LANES)[:, 0, :n_exp]


MOE_TILE = 512
MOE_ROWS = 512
MOE_PUT = 128
MOE_GET = 256
MOE_ALIGN = 16


def _moe_layout(cnt, n_rows_static):
    n_tiles, n_exp = cnt.shape
    padded = (cnt + MOE_ALIGN - 1) // MOE_ALIGN * MOE_ALIGN
    length = jnp.sum(padded, axis=0)
    span = (length + (MOE_PUT - MOE_ALIGN) + MOE_ROWS - 1) // MOE_ROWS * MOE_ROWS
    start = jnp.cumsum(span) - span
    off = start[None, :] + jnp.cumsum(padded, axis=0) - padded
    n_steps = n_rows_static // MOE_ROWS
    first = start // MOE_ROWS
    step = jnp.arange(n_steps, dtype=jnp.int32)
    expert = jnp.sum((first[None, :] <= step[:, None]).astype(jnp.int32), axis=1) - 1
    active = step < (first + (length + MOE_ROWS - 1) // MOE_ROWS)[expert]
    return (off.reshape(-1).astype(jnp.int32), cnt.reshape(-1).astype(jnp.int32),
            expert.astype(jnp.int32), active.astype(jnp.int32))


def _dispatch_kernel(off_ref, cnt_ref, h_ref, cw_ref, u_ref, xs_in, xs_ref, stage, sems, *, n_exp):
    del xs_in
    i = pl.program_id(0)
    n_put = MOE_TILE // MOE_PUT
    routed = jnp.where(cw_ref[...].T[0:8] > 0.0, 1.0, 0.0)
    pos = jnp.dot(routed.astype(BF16), u_ref[...], preferred_element_type=F32)
    row = lax.broadcasted_iota(jnp.int32, (MOE_PUT, MOE_TILE), 0).astype(F32)
    h = h_ref[...]

    def copy(e, s):
        slot = e * n_put + s
        dst = pl.multiple_of(off_ref[i * n_exp + e] + s * MOE_PUT, MOE_ALIGN)
        return pltpu.make_async_copy(stage.at[slot], xs_ref.at[pl.ds(dst, MOE_PUT), :], sems.at[slot])

    for e in range(n_exp):
        for s in range(n_put):
            @pl.when(s * MOE_PUT < cnt_ref[i * n_exp + e])
            def _():
                take = (routed[e:e + 1, :] > 0.5) & (pos[e:e + 1, :] == row + float(s * MOE_PUT))
                onehot = jnp.where(take, 1.0, 0.0).astype(BF16)
                stage[e * n_put + s] = jnp.dot(onehot, h, preferred_element_type=F32).astype(BF16)
                copy(e, s).start()

    for e in range(n_exp):
        for s in range(n_put):
            @pl.when(s * MOE_PUT < cnt_ref[i * n_exp + e])
            def _():
                copy(e, s).wait()


def _dispatch(h2, cw, off, cnt, n_exp, n_rows):
    t, d = h2.shape
    u = jnp.asarray(np.arange(MOE_TILE)[:, None] < np.arange(MOE_TILE)[None, :], BF16)
    n_slots = n_exp * (MOE_TILE // MOE_PUT)
    return pl.pallas_call(
        functools.partial(_dispatch_kernel, n_exp=n_exp),
        grid_spec=pltpu.PrefetchScalarGridSpec(
            num_scalar_prefetch=2,
            grid=(t // MOE_TILE,),
            in_specs=[pl.BlockSpec((MOE_TILE, d), lambda i, o, c: (i, 0)),
                      pl.BlockSpec((MOE_TILE, LANES), lambda i, o, c: (i, 0)),
                      pl.BlockSpec((MOE_TILE, MOE_TILE), lambda i, o, c: (0, 0)),
                      pl.BlockSpec(memory_space=pl.ANY)],
            out_specs=pl.BlockSpec(memory_space=pl.ANY),
            scratch_shapes=[pltpu.VMEM((n_slots, MOE_PUT, d), BF16),
                            pltpu.SemaphoreType.DMA((n_slots,))]),
        out_shape=jax.ShapeDtypeStruct((n_rows, d), BF16),
        input_output_aliases={5: 0},
        compiler_params=_params("arbitrary"),
        name="moe_dispatch",
    )(off, cnt, h2, cw, u, jnp.zeros((n_rows, d), BF16))


def _experts_kernel(exp_ref, act_ref, xs_ref, wg_ref, wu_ref, wd_ref, ys_ref, *, n_chunks):
    g = pl.program_id(0)

    @pl.when(act_ref[g] > 0)
    def _():
        x = xs_ref[...]
        tf = wg_ref.shape[2] // n_chunks
        y = jnp.zeros(ys_ref.shape, F32)
        for c in range(n_chunks):
            gate = jnp.dot(x, wg_ref[0, :, c * tf:(c + 1) * tf], preferred_element_type=F32)
            up = jnp.dot(x, wu_ref[0, :, c * tf:(c + 1) * tf], preferred_element_type=F32)
            a = ((gate * _sigmoid(gate)) * up).astype(BF16)
            y = y + jnp.dot(a, wd_ref[0, c * tf:(c + 1) * tf, :], preferred_element_type=F32)
        ys_ref[...] = y.astype(BF16)

    @pl.when(act_ref[g] == 0)
    def _():
        ys_ref[...] = jnp.zeros(ys_ref.shape, BF16)


def _experts(xs, expert, active, w_g, w_u, w_d):
    n_rows, d = xs.shape
    _, _, ff = w_g.shape
    wmap = lambda g, ex, ac: (ex[g], 0, 0)
    return pl.pallas_call(
        functools.partial(_experts_kernel, n_chunks=2),
        grid_spec=pltpu.PrefetchScalarGridSpec(
            num_scalar_prefetch=2,
            grid=(n_rows // MOE_ROWS,),
            in_specs=[pl.BlockSpec((MOE_ROWS, d), lambda g, ex, ac: (g, 0)),
                      pl.BlockSpec((1, d, ff), wmap), pl.BlockSpec((1, d, ff), wmap),
                      pl.BlockSpec((1, ff, d), wmap)],
            out_specs=pl.BlockSpec((MOE_ROWS, d), lambda g, ex, ac: (g, 0))),
        out_shape=jax.ShapeDtypeStruct((n_rows, d), BF16),
        compiler_params=_params("arbitrary"),
        name="moe_experts",
    )(expert, active, xs, w_g, w_u, w_d)


def _combine_kernel(off_ref, cnt_ref, x_ref, mod_ref, cw_ref, l_ref, ys_ref, o_ref, buf, sems, acc_scr,
                    *, n_exp):
    i = pl.program_id(0)
    n_get = MOE_TILE // MOE_GET

    def copy(e, s):
        slot = e * n_get + s
        src = pl.multiple_of(off_ref[i * n_exp + e] + s * MOE_GET, MOE_ALIGN)
        return pltpu.make_async_copy(ys_ref.at[pl.ds(src, MOE_GET), :], buf.at[slot], sems.at[slot])

    for e in range(n_exp):
        for s in range(n_get):
            @pl.when(s * MOE_GET < cnt_ref[i * n_exp + e])
            def _():
                copy(e, s).start()

    cw = cw_ref[...]
    routed = jnp.where(cw > 0.0, 1.0, 0.0)
    pos = jnp.dot(l_ref[...], routed.astype(BF16), preferred_element_type=F32)
    col = lax.broadcasted_iota(jnp.int32, (MOE_TILE, MOE_GET), 1).astype(F32)
    acc_scr[...] = jnp.zeros(acc_scr.shape, F32)
    for e in range(n_exp):
        for s in range(n_get):
            @pl.when(s * MOE_GET < cnt_ref[i * n_exp + e])
            def _():
                copy(e, s).wait()
                take = (cw[:, e:e + 1] > 0.0) & (pos[:, e:e + 1] == col + float(s * MOE_GET))
                onehot = jnp.where(take, 1.0, 0.0).astype(BF16)
                acc_scr[...] += cw[:, e:e + 1] * jnp.dot(onehot, buf[e * n_get + s],
                                                         preferred_element_type=F32)
    o_ref[...] = x_ref[...] + mod_ref[0, 5:6, :] * acc_scr[...]


def _combine(x1, mod_l, cw, ys, off, cnt, n_exp, seq):
    t, d = x1.shape
    per_b = seq // MOE_TILE
    low = jnp.asarray(np.arange(MOE_TILE)[:, None] > np.arange(MOE_TILE)[None, :], BF16)
    n_slots = n_exp * (MOE_TILE // MOE_GET)
    return pl.pallas_call(
        functools.partial(_combine_kernel, n_exp=n_exp),
        grid_spec=pltpu.PrefetchScalarGridSpec(
            num_scalar_prefetch=2,
            grid=(t // MOE_TILE,),
            in_specs=[pl.BlockSpec((MOE_TILE, d), lambda i, o, c: (i, 0)),
                      pl.BlockSpec((1, 6, d), lambda i, o, c: (i // per_b, 0, 0)),
                      pl.BlockSpec((MOE_TILE, LANES), lambda i, o, c: (i, 0)),
                      pl.BlockSpec((MOE_TILE, MOE_TILE), lambda i, o, c: (0, 0)),
                      pl.BlockSpec(memory_space=pl.ANY)],
            out_specs=pl.BlockSpec((MOE_TILE, d), lambda i, o, c: (i, 0)),
            scratch_shapes=[pltpu.VMEM((n_slots, MOE_GET, d), BF16),
                            pltpu.SemaphoreType.DMA((n_slots,)),
                            pltpu.VMEM((MOE_TILE, d), F32)]),
        out_shape=jax.ShapeDtypeStruct((t, d), F32),
        compiler_params=_params("arbitrary"),
        name="moe_combine",
    )(off, cnt, x1, mod_l, cw, low, ys)


def _moe(x1, h2, mod_l, w_r, b_r, w_g, w_u, w_d, seq):
    t, d = x1.shape
    n_exp = w_g.shape[0]
    n_tiles = t // MOE_TILE
    bound = (TOP_K * t + n_tiles * n_exp * (MOE_ALIGN - 1)
             + n_exp * (MOE_PUT - MOE_ALIGN + MOE_ROWS - 1))
    n_rows = (bound + MOE_ROWS - 1) // MOE_ROWS * MOE_ROWS + MOE_ROWS
    cw, cnt = _router(h2, w_r, b_r, n_exp, MOE_TILE)
    off, cnt, expert, active = _moe_layout(cnt, n_rows)
    xs = _dispatch(h2, cw, off, cnt, n_exp, n_rows)
    ys = _experts(xs, expert, active, w_g, w_u, w_d)
    return _combine(x1, mod_l, cw, ys, off, cnt, n_exp, seq)


def _ffn_kernel(x_ref, h_ref, mod_ref, cw_ref, wg_ref, wu_ref, wd_ref, o_ref, acc_scr):
    e, f = pl.program_id(1), pl.program_id(2)

    @pl.when((e == 0) & (f == 0))
    def _():
        acc_scr[...] = jnp.zeros(acc_scr.shape, F32)

    h = h_ref[...]
    g = jnp.dot(h, wg_ref[0], preferred_element_type=F32)
    u = jnp.dot(h, wu_ref[0], preferred_element_type=F32)
    lane = lax.broadcasted_iota(jnp.int32, cw_ref.shape, 1)
    cw = jnp.sum(jnp.where(lane == e, cw_ref[...], 0.0), axis=-1, keepdims=True)
    a = ((g * _sigmoid(g)) * u).astype(BF16)
    acc_scr[...] += cw * jnp.dot(a, wd_ref[0], preferred_element_type=F32)

    @pl.when((e == pl.num_programs(1) - 1) & (f == pl.num_programs(2) - 1))
    def _():
        o_ref[...] = x_ref[...] + mod_ref[0, 5:6, :] * acc_scr[...]


def _ffn(x1, h2, mod_l, cw, w_g, w_u, w_d, seq, tm, n_f):
    t, d = x1.shape
    n_exp, _, ff = w_g.shape
    tf = ff // n_f
    per_b = seq // tm
    return pl.pallas_call(
        _ffn_kernel,
        grid=(t // tm, n_exp, n_f),
        in_specs=[pl.BlockSpec((tm, d), lambda i, e, f: (i, 0)),
                  pl.BlockSpec((tm, d), lambda i, e, f: (i, 0)),
                  pl.BlockSpec((1, 6, d), lambda i, e, f: (i // per_b, 0, 0)),
                  pl.BlockSpec((tm, LANES), lambda i, e, f: (i, 0)),
                  pl.BlockSpec((1, d, tf), lambda i, e, f: (e, 0, f)),
                  pl.BlockSpec((1, d, tf), lambda i, e, f: (e, 0, f)),
                  pl.BlockSpec((1, tf, d), lambda i, e, f: (e, f, 0))],
        out_specs=pl.BlockSpec((tm, d), lambda i, e, f: (i, 0)),
        out_shape=jax.ShapeDtypeStruct((t, d), F32),
        scratch_shapes=[pltpu.VMEM((tm, d), F32)],
        compiler_params=_params("arbitrary", "arbitrary", "arbitrary"),
        name="swiglu",
    )(x1, h2, mod_l, cw, w_g, w_u, w_d)


def _pad_cols(a, width):
    return jnp.pad(a, ((0, 0), (0, width - a.shape[1])))


def kernel(x, c, w_ada, b_ada, norm_mix, norm_ffn, w_in, b_fgate, q_gain, k_gain, w_br_fox, w_br_moba,
           w_br_dil, w_out, w_ffn_gate, w_ffn_up, w_ffn_down, w_router, b_router, w_exp_gate,
           w_exp_up, w_exp_down):
    b, seq, d = x.shape
    depth = w_ada.shape[0]
    t = b * seq
    tm = 512
    n_pairs_fox = N_HEADS_FOX // 2
    n_pairs_moba = N_HEADS_MOBA // 2
    tq, tk = 512, 512

    mod = _adaln(c, w_ada, b_ada).reshape(depth, b, 6, d)
    x2 = x.reshape(t, d)
    w_in_t = jnp.transpose(w_in, (2, 0, 1))
    for l in range(depth):
        wl = w_in_t[:, l, :].astype(BF16)
        f0 = 3 * MIX_WIDTH
        w_qkv, w_gate = wl[:f0], wl[f0 + N_HEADS_FOX:]
        w_f = jnp.pad(wl[f0:f0 + N_HEADS_FOX], ((0, LANES - N_HEADS_FOX), (0, 0)))
        b_f = _pad_cols(b_fgate[l].reshape(1, -1), LANES)
        outs = _inproj(x2, mod[l], norm_mix[l].reshape(1, d), w_qkv, w_gate, w_f, q_gain[l].reshape(1, -1),
                       k_gain[l].reshape(1, -1), b_f, seq, tm)
        q_m, k_m, v_m = (o.reshape(b, seq, -1) for o in outs[0:3])
        gates, lf = outs[12], outs[13]

        cum = _decay(lf.reshape(b, seq, LANES))[:, :N_HEADS_FOX]
        ck = cum.reshape(b, n_pairs_fox, 2, seq // tk, tk).transpose(0, 1, 3, 2, 4)
        o_fox = _fox(q_m, k_m, v_m, ck, 0, n_pairs_fox, tq).reshape(t, -1)
        o_moba = _moba(q_m, k_m, v_m, n_pairs_fox, n_pairs_moba, tq).reshape(t, -1)
        dil_outs = [_dilated(*outs[3 + 3 * g:6 + 3 * g], g) for g in range(len(DILATED_PAIRS))]

        x1, h2 = _outproj(x2, mod[l], norm_ffn[l].reshape(1, d), o_fox, o_moba, dil_outs, gates,
                          w_br_fox[l].astype(BF16), w_br_moba[l].astype(BF16),
                          w_br_dil[l].astype(BF16), w_out[l].astype(BF16), seq, tm)
        i = l // 2
        if l % 2 == 0:
            cw = jnp.ones((t, LANES), F32)
            x2 = _ffn(x1, h2, mod[l], cw, w_ffn_gate[i:i + 1].astype(BF16),
                      w_ffn_up[i:i + 1].astype(BF16), w_ffn_down[i:i + 1].astype(BF16), seq, 1024, 2)
        else:
            x2 = _moe(x1, h2, mod[l], _pad_cols(w_router[i], LANES).astype(BF16),
                      _pad_cols(b_router[i].reshape(1, -1), LANES), w_exp_gate[i].astype(BF16),
                      w_exp_up[i].astype(BF16), w_exp_down[i].astype(BF16), seq)
    return x2.reshape(b, seq, d)
```

```python
import functools

import numpy as np
import jax
import jax.numpy as jnp
from jax import lax
from jax.experimental import pallas as pl
from jax.experimental.pallas import tpu as pltpu

HEAD_DIM = 64
N_HEADS_FOX = 6
N_HEADS_MOBA = 6
DILATED_PAIRS = ((128, 1), (512, 4), (2048, 16))
N_SLOTS_DIL = 4
N_HEADS_DIL = N_SLOTS_DIL * len(DILATED_PAIRS)
N_HEADS = N_HEADS_FOX + N_HEADS_MOBA + N_HEADS_DIL
MIX_WIDTH = N_HEADS * HEAD_DIM
N_BRANCH = 3
MOBA_BLOCK = 256
MOBA_TOPK = 3
WIN_BLOCK = 128
N_ALIBI = N_HEADS_MOBA + N_HEADS_DIL
DIL_SLOPE_OFFSETS = (0, N_SLOTS_DIL, 2 * N_SLOTS_DIL + N_HEADS_MOBA)
MOBA_SLOPE_OFFSET = 2 * N_SLOTS_DIL
TOP_K = 2
RMS_EPS = 1e-6
NEG_INF = -1e30

LANES = 128
PAIR = 2 * HEAD_DIM
VMEM_LIMIT = 56 * 1024 * 1024

F32 = jnp.float32
BF16 = jnp.bfloat16
_NT = (((1,), (1,)), ((), ()))


def _alibi_slopes(n):
    return (2.0 ** (-8.0 * np.arange(1, n + 1) / n)).astype(np.float32)


def _sigmoid(x):
    return 1.0 / (1.0 + jnp.exp(-x))


def _params(*sem):
    return pltpu.CompilerParams(dimension_semantics=sem, vmem_limit_bytes=VMEM_LIMIT)


def _adaln_kernel(c_ref, w_ref, b_ref, o_ref):
    c = c_ref[...]
    cond = c * _sigmoid(c)
    o_ref[0] = jnp.dot(cond, w_ref[0], precision=lax.Precision.HIGHEST,
                       preferred_element_type=F32) + b_ref[0]


def _adaln(c, w_ada, b_ada):
    depth, d, n = w_ada.shape
    b = c.shape[0]
    tn = 1536
    return pl.pallas_call(
        _adaln_kernel,
        grid=(depth, n // tn),
        in_specs=[pl.BlockSpec((b, d), lambda l, j: (0, 0)),
                  pl.BlockSpec((1, d, tn), lambda l, j: (l, 0, j)),
                  pl.BlockSpec((1, 1, tn), lambda l, j: (l, 0, j))],
        out_specs=pl.BlockSpec((1, b, tn), lambda l, j: (l, 0, j)),
        out_shape=jax.ShapeDtypeStruct((depth, b, n), F32),
        compiler_params=_params("arbitrary", "arbitrary"),
        name="adaln",
    )(c, w_ada, b_ada.reshape(depth, 1, n))


def _mod_norm(x, g, scale, shift):
    ms = jnp.mean(x * x, axis=-1, keepdims=True)
    return (x * lax.rsqrt(ms + RMS_EPS) * g) * (1.0 + scale) + shift


def _inproj_kernel(x_ref, mod_ref, g_ref, w_ref, wg_ref, wf_ref, qg_ref, kg_ref, bf_ref, e_ref,
                   q_ref, k_ref, v_ref, qd1, kd1, vd1, qd2, kd2, vd2, qd3, kd3, vd3,
                   gate_ref, lf_ref, perm_scr, *, n_main):
    h = _mod_norm(x_ref[...], g_ref[...], mod_ref[0, 1:2, :], mod_ref[0, 0:1, :]).astype(BF16)
    tm = x_ref.shape[0]
    qk_chunk = 4 * HEAD_DIM
    q_outs = (q_ref, qd1, qd2, qd3)
    k_outs = (k_ref, kd1, kd2, kd3)
    v_outs = (v_ref, vd1, vd2, vd3)

    def put(outs, col, val):
        if col < n_main:
            outs[0][:, col:col + qk_chunk] = val.astype(BF16)
            return
        g = (col - n_main) // qk_chunk
        dil = DILATED_PAIRS[g][1]
        if dil == 1:
            outs[1 + g][0, 0] = val.astype(BF16)
            return
        halves = qk_chunk // LANES
        for c in range(halves):
            perm_scr[c] = val[:, c * LANES:(c + 1) * LANES]
        for r in range(dil):
            outs[1 + g][0, r] = jnp.concatenate(
                [perm_scr[c, pl.ds(r, tm // dil, stride=dil), :] for c in range(halves)],
                axis=1).astype(BF16)

    def proj(wt_ref, col0, width):
        return lax.dot_general(h, wt_ref[col0:col0 + width, :], _NT, preferred_element_type=F32)

    def normed(col0, gain_ref, gcol, scale):
        y = proj(w_ref, col0, qk_chunk)
        ss = jnp.dot((y * y).astype(BF16), e_ref[...], preferred_element_type=F32)
        r = lax.rsqrt(ss * (1.0 / HEAD_DIM) + RMS_EPS)
        return (y * r * gain_ref[:, gcol:gcol + qk_chunk]) * scale

    for c in range(MIX_WIDTH // qk_chunk):
        col = c * qk_chunk
        put(q_outs, col, normed(col, qg_ref, col, HEAD_DIM ** -0.5))
        put(k_outs, col, normed(MIX_WIDTH + col, kg_ref, col, 1.0))
        put(v_outs, col, proj(w_ref, 2 * MIX_WIDTH + col, qk_chunk))

    gchunk = 512
    for c in range(wg_ref.shape[0] // gchunk):
        y = proj(wg_ref, c * gchunk, gchunk)
        gate_ref[:, c * gchunk:(c + 1) * gchunk] = _sigmoid(y).astype(BF16)

    f = proj(wf_ref, 0, LANES) + bf_ref[...]
    lf_ref[...] = jnp.minimum(f, 0.0) - jnp.log(1.0 + jnp.exp(-jnp.abs(f)))


def _inproj(x2, mod_l, g_norm, w_qkv, w_gate, w_f, q_gain, k_gain, b_f, seq, tm):
    t, d = x2.shape
    n_main = (N_HEADS_FOX + N_HEADS_MOBA) * HEAD_DIM
    dil_w = N_SLOTS_DIL * HEAD_DIM
    e = (np.arange(dil_w)[:, None] // HEAD_DIM == np.arange(dil_w)[None, :] // HEAD_DIM)
    e = jnp.asarray(e, BF16)
    per_b = seq // tm
    row = lambda i: (i, 0)
    fix = lambda i: (0, 0)
    qkv_shapes = [jax.ShapeDtypeStruct((t, n_main), BF16)] * 3
    qkv_specs = [pl.BlockSpec((tm, n_main), row)] * 3
    for _, dil in DILATED_PAIRS:
        assert tm % (16 * dil) == 0
        qkv_shapes += [jax.ShapeDtypeStruct((t // seq, dil, seq // dil, dil_w), BF16)] * 3
        qkv_specs += [pl.BlockSpec((1, dil, tm // dil, dil_w),
                                   lambda i: (i // per_b, 0, i % per_b, 0))] * 3
    out_shapes = qkv_shapes + [jax.ShapeDtypeStruct((t, N_BRANCH * d), BF16),
                               jax.ShapeDtypeStruct((t, LANES), F32)]
    out_specs = qkv_specs + [pl.BlockSpec((tm, N_BRANCH * d), row), pl.BlockSpec((tm, LANES), row)]
    return pl.pallas_call(
        functools.partial(_inproj_kernel, n_main=n_main),
        grid=(t // tm,),
        in_specs=[pl.BlockSpec((tm, d), row),
                  pl.BlockSpec((1, 6, d), lambda i: (i // per_b, 0, 0)),
                  pl.BlockSpec((1, d), fix),
                  pl.BlockSpec(w_qkv.shape, fix, pipeline_mode=pl.Buffered(1)),
                  pl.BlockSpec(w_gate.shape, fix, pipeline_mode=pl.Buffered(1)),
                  pl.BlockSpec(w_f.shape, fix),
                  pl.BlockSpec((1, MIX_WIDTH), fix),
                  pl.BlockSpec((1, MIX_WIDTH), fix),
                  pl.BlockSpec((1, LANES), fix),
                  pl.BlockSpec(e.shape, fix)],
        out_specs=out_specs,
        out_shape=out_shapes,
        scratch_shapes=[pltpu.VMEM((dil_w // LANES, tm, LANES), F32)],
        compiler_params=_params("arbitrary"),
        name="inproj",
    )(x2, mod_l, g_norm, w_qkv, w_gate, w_f, q_gain, k_gain, b_f, e)


def _decay_kernel(lf_ref, tri_ref, o_ref, *, blk):
    carry = jnp.zeros((1, LANES), F32)
    for j in range(lf_ref.shape[1] // blk):
        c = jnp.dot(tri_ref[...], lf_ref[0, j * blk:(j + 1) * blk, :],
                    precision=lax.Precision.HIGHEST, preferred_element_type=F32) + carry
        o_ref[0, j * blk:(j + 1) * blk, :] = c
        carry = c[blk - 1:blk, :]


def _decay(lf, blk=256):
    b, seq, _ = lf.shape
    tri = jnp.asarray(np.arange(blk)[:, None] >= np.arange(blk)[None, :], F32)
    return pl.pallas_call(
        functools.partial(_decay_kernel, blk=blk),
        grid=(b,),
        in_specs=[pl.BlockSpec((1, seq, LANES), lambda i: (i, 0, 0)),
                  pl.BlockSpec((blk, blk), lambda i: (0, 0))],
        out_specs=pl.BlockSpec((1, seq, LANES), lambda i: (i, 0, 0)),
        out_shape=jax.ShapeDtypeStruct((b, seq, LANES), F32),
        compiler_params=_params("arbitrary"),
        name="fox_decay",
    )(lf, tri)


def _head_masks():
    lane = lax.broadcasted_iota(jnp.int32, (1, PAIR), 1)
    return lane < HEAD_DIM, lane >= HEAD_DIM


def _split_heads(q):
    return [jnp.where(m, q, jnp.zeros_like(q)) for m in _head_masks()]


def _feat_base(hh):
    return HEAD_DIM * (1 - hh)


def _three_bf16(x):
    hi = x.astype(BF16).astype(F32)
    mid = (x - hi).astype(BF16).astype(F32)
    lo = (x - hi - mid).astype(BF16).astype(F32)
    return hi, mid, lo


def _place(idx, base, parts):
    out = jnp.zeros(jnp.broadcast_shapes(idx.shape, jnp.shape(parts[0])), F32)
    for r, part in enumerate(parts):
        out = jnp.where(idx == base + r, part, out)
    return out


def _flash_pair(i, qx, kx_scr, v_ref, o_ref, keep_own, own_first, m_scr, acc_scr, p_scr, a_scr, tq):
    _reset(m_scr, acc_scr)

    def scores(j, own, slot):
        for hh in range(2):
            s = lax.dot_general(qx[hh], kx_scr[hh, j * tq:(j + 1) * tq, :], _NT,
                                preferred_element_type=F32)
            if own:
                s = jnp.where(keep_own, s, NEG_INF)
            _softmax_stage(s, m_scr.at[hh], p_scr.at[slot, hh], a_scr.at[slot, hh])

    def values(j, slot):
        vs = _with_ones(v_ref[0, j * tq:(j + 1) * tq, :])
        for hh in range(2):
            _value_stage(vs[hh], acc_scr.at[hh], p_scr.at[slot, hh], a_scr.at[slot, hh])

    for c in range(v_ref.shape[1] // tq):
        @pl.when(i == c)
        def _(c=c):
            order = [c] + list(range(c)) if own_first else list(range(c + 1))
            scores(order[0], order[0] == c, 0)
            for n in range(1, len(order)):
                scores(order[n], order[n] == c, n % 2)
                values(order[n - 1], (n - 1) % 2)
            values(order[-1], (len(order) - 1) % 2)

    o_ref[0] = _merge_pair(acc_scr)


def _with_ones(v):
    return [jnp.where(m, v, jnp.ones_like(v)) for m in _head_masks()]


def _softmax_stage(s, m_ref, p_ref, a_ref):
    m_prev = m_ref[...]
    m_new = jnp.maximum(m_prev, jnp.max(s, axis=-1, keepdims=True))
    p = jnp.exp(s - jnp.concatenate([m_new] * (s.shape[1] // LANES), axis=1))
    m_ref[...] = m_new
    p_ref[...] = p.astype(BF16)
    a_ref[...] = jnp.exp(m_prev - m_new)


def _value_stage(v_ones, acc_ref, p_ref, a_ref):
    acc_ref[...] = a_ref[...] * acc_ref[...] + jnp.dot(p_ref[...], v_ones, preferred_element_type=F32)


def _reset(m_scr, acc_scr):
    m_scr[...] = jnp.full(m_scr.shape, NEG_INF, F32)
    acc_scr[...] = jnp.zeros(acc_scr.shape, F32)


def _merge_pair(acc_scr):
    first = _head_masks()[0]
    a0, a1 = acc_scr[0], acc_scr[1]
    return jnp.where(first, a0 / a0[:, HEAD_DIM:HEAD_DIM + 1], a1 / a1[:, 0:1]).astype(BF16)


def _stat_scratch(tq):
    return [pltpu.VMEM((2, tq, LANES), F32), pltpu.VMEM((2, tq, PAIR), F32),
            pltpu.VMEM((2, 2, tq, tq), BF16), pltpu.VMEM((2, 2, tq, LANES), F32)]


def _fox_kernel(q_ref, k_ref, v_ref, cum_ref, o_ref, kx_scr, m_scr, acc_scr, p_scr, a_scr, *, tq):
    pair, i = pl.program_id(1), pl.program_id(2)
    masks = _head_masks()
    lane = lax.broadcasted_iota(jnp.int32, (1, PAIR), 1)

    @pl.when(i == 0)
    def _():
        k = k_ref[0]
        cum = cum_ref[0]
        lane_s = lax.broadcasted_iota(jnp.int32, cum.shape, 1)
        for hh in range(2):
            f_s = jnp.sum(jnp.where(lane_s == 2 * pair + hh, cum, 0.0), axis=-1, keepdims=True)
            feats = _place(lane, _feat_base(hh), _three_bf16(-f_s))
            kx_scr[hh] = jnp.where(masks[hh], k, feats.astype(BF16))

    q = q_ref[0]
    qx = []
    for hh in range(2):
        ones = (lane >= _feat_base(hh)) & (lane < _feat_base(hh) + 3)
        qx.append(jnp.where(masks[hh], q, jnp.where(ones, 1.0, 0.0).astype(BF16)))
    causal = (lax.broadcasted_iota(jnp.int32, (tq, tq), 1)
              <= lax.broadcasted_iota(jnp.int32, (tq, tq), 0))
    _flash_pair(i, qx, kx_scr, v_ref, o_ref, causal, False, m_scr, acc_scr, p_scr, a_scr, tq)


def _fox(q, k, v, cum, col0, n_pairs, tq):
    b, seq, _ = q.shape
    return pl.pallas_call(
        functools.partial(_fox_kernel, tq=tq),
        grid=(b, n_pairs, seq // tq),
        in_specs=[pl.BlockSpec((1, tq, PAIR), lambda bi, p, i: (bi, i, col0 + p)),
                  pl.BlockSpec((1, seq, PAIR), lambda bi, p, i: (bi, 0, col0 + p)),
                  pl.BlockSpec((1, seq, PAIR), lambda bi, p, i: (bi, 0, col0 + p)),
                  pl.BlockSpec((1, seq, LANES), lambda bi, p, i: (bi, 0, 0))],
        out_specs=pl.BlockSpec((1, tq, PAIR), lambda bi, p, i: (bi, i, p)),
        out_shape=jax.ShapeDtypeStruct((b, seq, n_pairs * PAIR), BF16),
        scratch_shapes=[pltpu.VMEM((2, seq, PAIR), BF16)] + _stat_scratch(tq),
        compiler_params=_params("arbitrary", "arbitrary", "arbitrary"),
        name="fox_attn",
    )(q, k, v, cum)


def _moba_kernel(q_ref, k_ref, v_ref, sl_ref, o_ref, km_scr, kx_scr, m_scr, acc_scr, p_scr, a_scr):
    blk = MOBA_BLOCK
    nbp = 8
    i = pl.program_id(2)
    seq = k_ref.shape[1]
    n_blk = seq // blk
    tq = q_ref.shape[1]
    per = tq // blk
    masks = _head_masks()

    @pl.when(i == 0)
    def _():
        km_scr[...] = jnp.zeros(km_scr.shape, F32)
        for n in range(n_blk):
            kb = k_ref[0, n * blk:(n + 1) * blk, :].astype(F32)
            km_scr[n:n + 1, :] = jnp.sum(kb, axis=0, keepdims=True) * (1.0 / blk)
        k = k_ref[0]
        lane = lax.broadcasted_iota(jnp.int32, (seq, PAIR), 1)
        pos = lax.broadcasted_iota(jnp.int32, (seq, PAIR), 0)
        for hh in range(2):
            base = _feat_base(hh)
            in_blk = (lane >= base) & (lane < base + nbp) & (pos // blk == lane - base)
            ones = (lane >= base + 11) & (lane < base + 14)
            feats = jnp.where(in_blk | ones, 1.0, 0.0) + _place(
                lane, base + nbp, _three_bf16(sl_ref[0, hh:hh + 1, 0:LANES] * pos.astype(F32)))
            kx_scr[hh] = jnp.where(masks[hh], k, feats.astype(BF16))

    q = q_ref[0]
    qs = _split_heads(q)
    blk_n = lax.broadcasted_iota(jnp.int32, (nbp, tq), 0)
    q_blk = i * per + lax.broadcasted_iota(jnp.int32, (nbp, tq), 1) // blk
    row = lax.broadcasted_iota(jnp.int32, (LANES, tq), 0)
    t_pos = (i * tq + lax.broadcasted_iota(jnp.int32, (LANES, tq), 1)).astype(F32)
    qx = []
    for hh in range(2):
        base = _feat_base(hh)
        g = lax.dot_general(km_scr[...], qs[hh].astype(F32), _NT, precision=lax.Precision.HIGHEST,
                            preferred_element_type=F32)[0:nbp]
        cnt = jnp.zeros((nbp, tq), F32)
        for m in range(n_blk):
            gm = g[m:m + 1, :]
            beats = ((gm > g) | ((gm == g) & (blk_n > m))) & (q_blk > m)
            cnt = cnt + jnp.where(beats, 1.0, 0.0)
        visible = ((blk_n < q_blk) & (cnt < float(MOBA_TOPK))) | (blk_n == q_blk)
        hide = jnp.where(visible, 0.0, NEG_INF)
        pads = [jnp.zeros((n, tq), F32) for n in (base, LANES - nbp - base)]
        feats = jnp.concatenate([a for a in (pads[0], hide, pads[1]) if a.shape[0]], axis=0)
        feats = feats + jnp.where((row >= base + nbp) & (row < base + 11), 1.0, 0.0) + _place(
            row, base + 11, _three_bf16(-sl_ref[0, hh:hh + 1, :] * t_pos))
        qx.append(jnp.where(masks[hh], q, feats.T.astype(BF16)))

    r = lax.broadcasted_iota(jnp.int32, (tq, tq), 0)
    c = lax.broadcasted_iota(jnp.int32, (tq, tq), 1)
    keep_own = (r // blk != c // blk) | (c <= r)
    _flash_pair(i, qx, kx_scr, v_ref, o_ref, keep_own, True, m_scr, acc_scr, p_scr, a_scr, tq)


def _moba(q, k, v, col0, n_pairs, tq):
    b, seq, _ = q.shape
    blk = MOBA_BLOCK
    assert seq % tq == 0 and tq % blk == 0 and seq // blk <= 8
    slopes = _alibi_slopes(N_ALIBI)[MOBA_SLOPE_OFFSET:MOBA_SLOPE_OFFSET + 2 * n_pairs]
    sl = jnp.asarray(np.broadcast_to(slopes.reshape(n_pairs, 2, 1), (n_pairs, 2, tq)).copy())
    return pl.pallas_call(
        _moba_kernel,
        grid=(b, n_pairs, seq // tq),
        in_specs=[pl.BlockSpec((1, tq, PAIR), lambda bi, p, i: (bi, i, col0 + p)),
                  pl.BlockSpec((1, seq, PAIR), lambda bi, p, i: (bi, 0, col0 + p)),
                  pl.BlockSpec((1, seq, PAIR), lambda bi, p, i: (bi, 0, col0 + p)),
                  pl.BlockSpec((1, 2, tq), lambda bi, p, i: (p, 0, 0))],
        out_specs=pl.BlockSpec((1, tq, PAIR), lambda bi, p, i: (bi, i, p)),
        out_shape=jax.ShapeDtypeStruct((b, seq, n_pairs * PAIR), BF16),
        scratch_shapes=[pltpu.VMEM((LANES, PAIR), F32), pltpu.VMEM((2, seq, PAIR), BF16)]
        + _stat_scratch(tq),
        compiler_params=_params("arbitrary", "arbitrary", "arbitrary"),
        name="moba_attn",
    )(q, k, v, sl)


def _dil_kernel(q_ref, k_ref, v_ref, o_ref, lse_ref, *, dil, slopes, n_q):
    wb = WIN_BLOCK
    seq = q_ref.shape[1]
    nb = seq // wb
    steps = (lax.broadcasted_iota(jnp.int32, (wb, 2 * wb), 0) + wb
             - lax.broadcasted_iota(jnp.int32, (wb, 2 * wb), 1))
    band_ok = (steps >= 0) & (steps <= wb)
    steps_f = (steps * dil).astype(F32)
    blk = lax.broadcasted_iota(jnp.int32, (nb, 1, 2 * wb), 0)
    key = lax.broadcasted_iota(jnp.int32, (nb, 1, 2 * wb), 2)
    no_prev = jnp.where((blk % n_q == 0) & (key < wb), -NEG_INF, 0.0)
    first = _head_masks()[0]

    def band(ref, cols):
        cur = ref[0, :, cols]
        prev = jnp.concatenate([cur[0:wb], cur[0:seq - wb]], axis=0)
        return jnp.concatenate([prev.reshape(nb, wb, PAIR), cur.reshape(nb, wb, PAIR)], axis=1)

    for pair in range(2):
        cols = slice(pair * PAIR, (pair + 1) * PAIR)
        qs = [qh.reshape(nb, wb, PAIR) for qh in _split_heads(q_ref[0, :, cols])]
        kband, vband = band(k_ref, cols), band(v_ref, cols)
        outs, lses = [], []
        for hh in range(2):
            bias = jnp.where(band_ok, slopes[2 * pair + hh] * steps_f, -NEG_INF)
            s = jnp.einsum("bqd,bkd->bqk", qs[hh], kband, preferred_element_type=F32)
            s = s - bias[None] - no_prev
            m = jnp.max(s, axis=-1, keepdims=True)
            p = jnp.exp(s - m)
            l = jnp.sum(p, axis=-1, keepdims=True)
            o = jnp.einsum("bqk,bkd->bqd", p.astype(BF16), vband, preferred_element_type=F32)
            outs.append(o / l)
            lses.append(jnp.broadcast_to(m + jnp.log(l), (nb, wb, PAIR)))
        o_ref[0, :, cols] = jnp.where(first, outs[0], outs[1]).reshape(seq, PAIR).astype(BF16)
        lse_ref[0, :, cols] = jnp.where(first, lses[0], lses[1]).reshape(seq, PAIR)


def _dilated(q, k, v, group):
    window, dil = DILATED_PAIRS[group]
    assert window // dil == WIN_BLOCK
    b, _, l_sub, w = q.shape
    seq = dil * l_sub
    so = DIL_SLOPE_OFFSETS[group]
    slopes = tuple(float(s) for s in _alibi_slopes(N_ALIBI)[so:so + N_SLOTS_DIL])
    spec = pl.BlockSpec((1, seq, w), lambda bi: (bi, 0, 0))
    flat = lambda a: a.reshape(b, seq, w)
    o, lse = pl.pallas_call(
        functools.partial(_dil_kernel, dil=dil, slopes=slopes, n_q=l_sub // WIN_BLOCK),
        grid=(b,),
        in_specs=[spec, spec, spec],
        out_specs=[spec, spec],
        out_shape=[jax.ShapeDtypeStruct((b, seq, w), BF16), jax.ShapeDtypeStruct((b, seq, w), F32)],
        compiler_params=_params("arbitrary"),
        name=f"dilated_attn_{dil}",
    )(flat(q), flat(k), flat(v))
    return o.reshape(q.shape), lse.reshape(q.shape)


def _outproj_kernel(x_ref, mod_ref, g_ref, of_ref, om_ref, o1, l1, o2, l2, o3, l3, gate_ref,
                    wf_ref, wm_ref, wd_ref, wo_ref, x1_ref, h2_ref, perm_scr):
    d = x_ref.shape[1]

    def natural(ref):
        dil, n = ref.shape[1], ref.shape[2]
        if dil == 1:
            return ref[0, 0].astype(F32)
        halves = ref.shape[3] // LANES
        for r in range(dil):
            blk = ref[0, r].astype(F32)
            for c in range(halves):
                perm_scr[c, pl.ds(r, n, stride=dil), :] = blk[:, c * LANES:(c + 1) * LANES]
        return jnp.concatenate([perm_scr[c] for c in range(halves)], axis=1)

    lses = [natural(l) for l in (l1, l2, l3)]
    lmax = jnp.maximum(jnp.maximum(lses[0], lses[1]), lses[2])
    e1, e2, e3 = [jnp.exp(l - lmax) for l in lses]
    den = e1 + e2 + e3
    o_dil = (e1 / den) * natural(o1) + (e2 / den) * natural(o2) + (e3 / den) * natural(o3)
    y = (gate_ref[:, 0:d].astype(F32) * jnp.dot(of_ref[...], wf_ref[...], preferred_element_type=F32)
         + gate_ref[:, d:2 * d].astype(F32) * jnp.dot(om_ref[...], wm_ref[...], preferred_element_type=F32)
         + gate_ref[:, 2 * d:3 * d].astype(F32)
         * jnp.dot(o_dil.astype(BF16), wd_ref[...], preferred_element_type=F32))
    out = jnp.dot(y.astype(BF16), wo_ref[...], preferred_element_type=F32)
    x1 = x_ref[...] + mod_ref[0, 2:3, :] * out
    x1_ref[...] = x1
    h2_ref[...] = _mod_norm(x1, g_ref[...], mod_ref[0, 4:5, :], mod_ref[0, 3:4, :]).astype(BF16)


def _outproj(x2, mod_l, g_norm, o_fox, o_moba, dil_outs, gates, w_f, w_m, w_d, w_o, seq, tm):
    t, d = x2.shape
    per_b = seq // tm
    row = lambda i: (i, 0)
    fix = lambda i: (0, 0)
    dil_args, dil_specs = [], []
    for o, lse in dil_outs:
        _, dil, _, w = o.shape
        dil_args += [o, lse]
        dil_specs += [pl.BlockSpec((1, dil, tm // dil, w), lambda i: (i // per_b, 0, i % per_b, 0))] * 2
    return pl.pallas_call(
        _outproj_kernel,
        grid=(t // tm,),
        in_specs=[pl.BlockSpec((tm, d), row),
                  pl.BlockSpec((1, 6, d), lambda i: (i // per_b, 0, 0)),
                  pl.BlockSpec((1, d), fix),
                  pl.BlockSpec((tm, o_fox.shape[1]), row),
                  pl.BlockSpec((tm, o_moba.shape[1]), row)] + dil_specs + [
                  pl.BlockSpec((tm, N_BRANCH * d), row),
                  pl.BlockSpec(w_f.shape, fix), pl.BlockSpec(w_m.shape, fix),
                  pl.BlockSpec(w_d.shape, fix), pl.BlockSpec(w_o.shape, fix)],
        out_specs=[pl.BlockSpec((tm, d), row), pl.BlockSpec((tm, d), row)],
        out_shape=[jax.ShapeDtypeStruct((t, d), F32), jax.ShapeDtypeStruct((t, d), BF16)],
        scratch_shapes=[pltpu.VMEM((N_SLOTS_DIL * HEAD_DIM // LANES, tm, LANES), F32)],
        compiler_params=_params("arbitrary"),
        name="outproj",
    )(x2, mod_l, g_norm, o_fox, o_moba, *dil_args, gates, w_f, w_m, w_d, w_o)


def _router_kernel(h_ref, w_ref, b_ref, o_ref, cnt_ref, *, n_exp):
    logits = jnp.dot(h_ref[...], w_ref[...], preferred_element_type=F32) + b_ref[...]
    lane = lax.broadcasted_iota(jnp.int32, logits.shape, 1)
    logits = jnp.where(lane < n_exp, logits, -jnp.inf)

    def top(vals):
        m = jnp.max(vals, axis=-1, keepdims=True)
        idx = jnp.min(jnp.where(vals == m, lane, LANES), axis=-1, keepdims=True)
        return m, lane == idx

    m1, hot1 = top(logits)
    m2, hot2 = top(jnp.where(hot1, -jnp.inf, logits))
    e2 = jnp.exp(m2 - m1)
    cw = jnp.where(hot1, 1.0 / (1.0 + e2), 0.0) + jnp.where(hot2, e2 / (1.0 + e2), 0.0)
    o_ref[...] = cw
    n_tok = jnp.sum(jnp.where(cw > 0.0, 1.0, 0.0), axis=0, keepdims=True)
    cnt_ref[...] = jnp.broadcast_to(n_tok, cnt_ref.shape).astype(jnp.int32)


def _router(h2, w_r, b_r, n_exp, tm):
    t, d = h2.shape
    cw, cnt = pl.pallas_call(
        functools.partial(_router_kernel, n_exp=n_exp),
        grid=(t // tm,),
        in_specs=[pl.BlockSpec((tm, d), lambda i: (i, 0)),
                  pl.BlockSpec((d, LANES), lambda i: (0, 0)),
                  pl.BlockSpec((1, LANES), lambda i: (0, 0))],
        out_specs=[pl.BlockSpec((tm, LANES), lambda i: (i, 0)),
                   pl.BlockSpec((8, LANES), lambda i: (i, 0))],
        out_shape=[jax.ShapeDtypeStruct((t, LANES), F32),
                   jax.ShapeDtypeStruct((t // tm * 8, LANES), jnp.int32)],
        compiler_params=_params("arbitrary"),
        name="router",
    )(h2, w_r, b_r)
    return cw, cnt.reshape(t // tm, 8, LANES)[:, 0, :n_exp]


MOE_TILE = 512
MOE_ROWS = 512
MOE_PUT = 128
MOE_GET = 256
MOE_ALIGN = 16


def _moe_layout(cnt, n_rows_static):
    n_tiles, n_exp = cnt.shape
    padded = (cnt + MOE_ALIGN - 1) // MOE_ALIGN * MOE_ALIGN
    length = jnp.sum(padded, axis=0)
    span = (length + (MOE_PUT - MOE_ALIGN) + MOE_ROWS - 1) // MOE_ROWS * MOE_ROWS
    start = jnp.cumsum(span) - span
    off = start[None, :] + jnp.cumsum(padded, axis=0) - padded
    n_steps = n_rows_static // MOE_ROWS
    first = start // MOE_ROWS
    step = jnp.arange(n_steps, dtype=jnp.int32)
    expert = jnp.sum((first[None, :] <= step[:, None]).astype(jnp.int32), axis=1) - 1
    active = step < (first + (length + MOE_ROWS - 1) // MOE_ROWS)[expert]
    return (off.reshape(-1).astype(jnp.int32), cnt.reshape(-1).astype(jnp.int32),
            expert.astype(jnp.int32), active.astype(jnp.int32))


def _dispatch_kernel(off_ref, cnt_ref, h_ref, cw_ref, u_ref, xs_in, xs_ref, stage, sems, *, n_exp):
    del xs_in
    i = pl.program_id(0)
    n_put = MOE_TILE // MOE_PUT
    routed = jnp.where(cw_ref[...].T[0:8] > 0.0, 1.0, 0.0)
    pos = jnp.dot(routed.astype(BF16), u_ref[...], preferred_element_type=F32)
    row = lax.broadcasted_iota(jnp.int32, (MOE_PUT, MOE_TILE), 0).astype(F32)
    h = h_ref[...]

    def copy(e, s):
        slot = e * n_put + s
        dst = pl.multiple_of(off_ref[i * n_exp + e] + s * MOE_PUT, MOE_ALIGN)
        return pltpu.make_async_copy(stage.at[slot], xs_ref.at[pl.ds(dst, MOE_PUT), :], sems.at[slot])

    for e in range(n_exp):
        for s in range(n_put):
            @pl.when(s * MOE_PUT < cnt_ref[i * n_exp + e])
            def _():
                take = (routed[e:e + 1, :] > 0.5) & (pos[e:e + 1, :] == row + float(s * MOE_PUT))
                onehot = jnp.where(take, 1.0, 0.0).astype(BF16)
                stage[e * n_put + s] = jnp.dot(onehot, h, preferred_element_type=F32).astype(BF16)
                copy(e, s).start()

    for e in range(n_exp):
        for s in range(n_put):
            @pl.when(s * MOE_PUT < cnt_ref[i * n_exp + e])
            def _():
                copy(e, s).wait()


def _dispatch(h2, cw, off, cnt, n_exp, n_rows):
    t, d = h2.shape
    u = jnp.asarray(np.arange(MOE_TILE)[:, None] < np.arange(MOE_TILE)[None, :], BF16)
    n_slots = n_exp * (MOE_TILE // MOE_PUT)
    return pl.pallas_call(
        functools.partial(_dispatch_kernel, n_exp=n_exp),
        grid_spec=pltpu.PrefetchScalarGridSpec(
            num_scalar_prefetch=2,
            grid=(t // MOE_TILE,),
            in_specs=[pl.BlockSpec((MOE_TILE, d), lambda i, o, c: (i, 0)),
                      pl.BlockSpec((MOE_TILE, LANES), lambda i, o, c: (i, 0)),
                      pl.BlockSpec((MOE_TILE, MOE_TILE), lambda i, o, c: (0, 0)),
                      pl.BlockSpec(memory_space=pl.ANY)],
            out_specs=pl.BlockSpec(memory_space=pl.ANY),
            scratch_shapes=[pltpu.VMEM((n_slots, MOE_PUT, d), BF16),
                            pltpu.SemaphoreType.DMA((n_slots,))]),
        out_shape=jax.ShapeDtypeStruct((n_rows, d), BF16),
        input_output_aliases={5: 0},
        compiler_params=_params("arbitrary"),
        name="moe_dispatch",
    )(off, cnt, h2, cw, u, jnp.zeros((n_rows, d), BF16))


def _experts_kernel(exp_ref, act_ref, xs_ref, wg_ref, wu_ref, wd_ref, ys_ref, *, n_chunks):
    g = pl.program_id(0)

    @pl.when(act_ref[g] > 0)
    def _():
        x = xs_ref[...]
        tf = wg_ref.shape[2] // n_chunks
        y = jnp.zeros(ys_ref.shape, F32)
        for c in range(n_chunks):
            gate = jnp.dot(x, wg_ref[0, :, c * tf:(c + 1) * tf], preferred_element_type=F32)
            up = jnp.dot(x, wu_ref[0, :, c * tf:(c + 1) * tf], preferred_element_type=F32)
            a = ((gate * _sigmoid(gate)) * up).astype(BF16)
            y = y + jnp.dot(a, wd_ref[0, c * tf:(c + 1) * tf, :], preferred_element_type=F32)
        ys_ref[...] = y.astype(BF16)

    @pl.when(act_ref[g] == 0)
    def _():
        ys_ref[...] = jnp.zeros(ys_ref.shape, BF16)


def _experts(xs, expert, active, w_g, w_u, w_d):
    n_rows, d = xs.shape
    _, _, ff = w_g.shape
    wmap = lambda g, ex, ac: (ex[g], 0, 0)
    return pl.pallas_call(
        functools.partial(_experts_kernel, n_chunks=2),
        grid_spec=pltpu.PrefetchScalarGridSpec(
            num_scalar_prefetch=2,
            grid=(n_rows // MOE_ROWS,),
            in_specs=[pl.BlockSpec((MOE_ROWS, d), lambda g, ex, ac: (g, 0)),
                      pl.BlockSpec((1, d, ff), wmap), pl.BlockSpec((1, d, ff), wmap),
                      pl.BlockSpec((1, ff, d), wmap)],
            out_specs=pl.BlockSpec((MOE_ROWS, d), lambda g, ex, ac: (g, 0))),
        out_shape=jax.ShapeDtypeStruct((n_rows, d), BF16),
        compiler_params=_params("arbitrary"),
        name="moe_experts",
    )(expert, active, xs, w_g, w_u, w_d)


def _combine_kernel(off_ref, cnt_ref, x_ref, mod_ref, cw_ref, l_ref, ys_ref, o_ref, buf, sems, acc_scr,
                    *, n_exp):
    i = pl.program_id(0)
    n_get = MOE_TILE // MOE_GET

    def copy(e, s):
        slot = e * n_get + s
        src = pl.multiple_of(off_ref[i * n_exp + e] + s * MOE_GET, MOE_ALIGN)
        return pltpu.make_async_copy(ys_ref.at[pl.ds(src, MOE_GET), :], buf.at[slot], sems.at[slot])

    for e in range(n_exp):
        for s in range(n_get):
            @pl.when(s * MOE_GET < cnt_ref[i * n_exp + e])
            def _():
                copy(e, s).start()

    cw = cw_ref[...]
    routed = jnp.where(cw > 0.0, 1.0, 0.0)
    pos = jnp.dot(l_ref[...], routed.astype(BF16), preferred_element_type=F32)
    col = lax.broadcasted_iota(jnp.int32, (MOE_TILE, MOE_GET), 1).astype(F32)
    acc_scr[...] = jnp.zeros(acc_scr.shape, F32)
    for e in range(n_exp):
        for s in range(n_get):
            @pl.when(s * MOE_GET < cnt_ref[i * n_exp + e])
            def _():
                copy(e, s).wait()
                take = (cw[:, e:e + 1] > 0.0) & (pos[:, e:e + 1] == col + float(s * MOE_GET))
                onehot = jnp.where(take, 1.0, 0.0).astype(BF16)
                acc_scr[...] += cw[:, e:e + 1] * jnp.dot(onehot, buf[e * n_get + s],
                                                         preferred_element_type=F32)
    o_ref[...] = x_ref[...] + mod_ref[0, 5:6, :] * acc_scr[...]


def _combine(x1, mod_l, cw, ys, off, cnt, n_exp, seq):
    t, d = x1.shape
    per_b = seq // MOE_TILE
    low = jnp.asarray(np.arange(MOE_TILE)[:, None] > np.arange(MOE_TILE)[None, :], BF16)
    n_slots = n_exp * (MOE_TILE // MOE_GET)
    return pl.pallas_call(
        functools.partial(_combine_kernel, n_exp=n_exp),
        grid_spec=pltpu.PrefetchScalarGridSpec(
            num_scalar_prefetch=2,
            grid=(t // MOE_TILE,),
            in_specs=[pl.BlockSpec((MOE_TILE, d), lambda i, o, c: (i, 0)),
                      pl.BlockSpec((1, 6, d), lambda i, o, c: (i // per_b, 0, 0)),
                      pl.BlockSpec((MOE_TILE, LANES), lambda i, o, c: (i, 0)),
                      pl.BlockSpec((MOE_TILE, MOE_TILE), lambda i, o, c: (0, 0)),
                      pl.BlockSpec(memory_space=pl.ANY)],
            out_specs=pl.BlockSpec((MOE_TILE, d), lambda i, o, c: (i, 0)),
            scratch_shapes=[pltpu.VMEM((n_slots, MOE_GET, d), BF16),
                            pltpu.SemaphoreType.DMA((n_slots,)),
                            pltpu.VMEM((MOE_TILE, d), F32)]),
        out_shape=jax.ShapeDtypeStruct((t, d), F32),
        compiler_params=_params("arbitrary"),
        name="moe_combine",
    )(off, cnt, x1, mod_l, cw, low, ys)


def _moe(x1, h2, mod_l, w_r, b_r, w_g, w_u, w_d, seq):
    t, d = x1.shape
    n_exp = w_g.shape[0]
    n_tiles = t // MOE_TILE
    bound = (TOP_K * t + n_tiles * n_exp * (MOE_ALIGN - 1)
             + n_exp * (MOE_PUT - MOE_ALIGN + MOE_ROWS - 1))
    n_rows = (bound + MOE_ROWS - 1) // MOE_ROWS * MOE_ROWS + MOE_ROWS
    cw, cnt = _router(h2, w_r, b_r, n_exp, MOE_TILE)
    off, cnt, expert, active = _moe_layout(cnt, n_rows)
    xs = _dispatch(h2, cw, off, cnt, n_exp, n_rows)
    ys = _experts(xs, expert, active, w_g, w_u, w_d)
    return _combine(x1, mod_l, cw, ys, off, cnt, n_exp, seq)


def _ffn_kernel(x_ref, h_ref, mod_ref, cw_ref, wg_ref, wu_ref, wd_ref, o_ref, acc_scr):
    e, f = pl.program_id(1), pl.program_id(2)

    @pl.when((e == 0) & (f == 0))
    def _():
        acc_scr[...] = jnp.zeros(acc_scr.shape, F32)

    h = h_ref[...]
    g = jnp.dot(h, wg_ref[0], preferred_element_type=F32)
    u = jnp.dot(h, wu_ref[0], preferred_element_type=F32)
    lane = lax.broadcasted_iota(jnp.int32, cw_ref.shape, 1)
    cw = jnp.sum(jnp.where(lane == e, cw_ref[...], 0.0), axis=-1, keepdims=True)
    a = ((g * _sigmoid(g)) * u).astype(BF16)
    acc_scr[...] += cw * jnp.dot(a, wd_ref[0], preferred_element_type=F32)

    @pl.when((e == pl.num_programs(1) - 1) & (f == pl.num_programs(2) - 1))
    def _():
        o_ref[...] = x_ref[...] + mod_ref[0, 5:6, :] * acc_scr[...]


def _ffn(x1, h2, mod_l, cw, w_g, w_u, w_d, seq, tm, n_f):
    t, d = x1.shape
    n_exp, _, ff = w_g.shape
    tf = ff // n_f
    per_b = seq // tm
    return pl.pallas_call(
        _ffn_kernel,
        grid=(t // tm, n_exp, n_f),
        in_specs=[pl.BlockSpec((tm, d), lambda i, e, f: (i, 0)),
                  pl.BlockSpec((tm, d), lambda i, e, f: (i, 0)),
                  pl.BlockSpec((1, 6, d), lambda i, e, f: (i // per_b, 0, 0)),
                  pl.BlockSpec((tm, LANES), lambda i, e, f: (i, 0)),
                  pl.BlockSpec((1, d, tf), lambda i, e, f: (e, 0, f)),
                  pl.BlockSpec((1, d, tf), lambda i, e, f: (e, 0, f)),
                  pl.BlockSpec((1, tf, d), lambda i, e, f: (e, f, 0))],
        out_specs=pl.BlockSpec((tm, d), lambda i, e, f: (i, 0)),
        out_shape=jax.ShapeDtypeStruct((t, d), F32),
        scratch_shapes=[pltpu.VMEM((tm, d), F32)],
        compiler_params=_params("arbitrary", "arbitrary", "arbitrary"),
        name="swiglu",
    )(x1, h2, mod_l, cw, w_g, w_u, w_d)


def _pad_cols(a, width):
    return jnp.pad(a, ((0, 0), (0, width - a.shape[1])))


def kernel(x, c, w_ada, b_ada, norm_mix, norm_ffn, w_in, b_fgate, q_gain, k_gain, w_br_fox, w_br_moba,
           w_br_dil, w_out, w_ffn_gate, w_ffn_up, w_ffn_down, w_router, b_router, w_exp_gate,
           w_exp_up, w_exp_down):
    b, seq, d = x.shape
    depth = w_ada.shape[0]
    t = b * seq
    tm = 512
    n_pairs_fox = N_HEADS_FOX // 2
    n_pairs_moba = N_HEADS_MOBA // 2
    tq, tk = 512, 512

    mod = _adaln(c, w_ada, b_ada).reshape(depth, b, 6, d)
    x2 = x.reshape(t, d)
    w_in_t = jnp.transpose(w_in, (2, 0, 1))
    for l in range(depth):
        wl = w_in_t[:, l, :].astype(BF16)
        f0 = 3 * MIX_WIDTH
        w_qkv, w_gate = wl[:f0], wl[f0 + N_HEADS_FOX:]
        w_f = jnp.pad(wl[f0:f0 + N_HEADS_FOX], ((0, LANES - N_HEADS_FOX), (0, 0)))
        b_f = _pad_cols(b_fgate[l].reshape(1, -1), LANES)
        outs = _inproj(x2, mod[l], norm_mix[l].reshape(1, d), w_qkv, w_gate, w_f, q_gain[l].reshape(1, -1),
                       k_gain[l].reshape(1, -1), b_f, seq, tm)
        q_m, k_m, v_m = (o.reshape(b, seq, -1) for o in outs[0:3])
        gates, lf = outs[12], outs[13]

        cum = _decay(lf.reshape(b, seq, LANES))
        o_fox = _fox(q_m, k_m, v_m, cum, 0, n_pairs_fox, tq).reshape(t, -1)
        o_moba = _moba(q_m, k_m, v_m, n_pairs_fox, n_pairs_moba, tq).reshape(t, -1)
        dil_outs = [_dilated(*outs[3 + 3 * g:6 + 3 * g], g) for g in range(len(DILATED_PAIRS))]

        x1, h2 = _outproj(x2, mod[l], norm_ffn[l].reshape(1, d), o_fox, o_moba, dil_outs, gates,
                          w_br_fox[l].astype(BF16), w_br_moba[l].astype(BF16),
                          w_br_dil[l].astype(BF16), w_out[l].astype(BF16), seq, tm)
        i = l // 2
        if l % 2 == 0:
            cw = jnp.ones((t, LANES), F32)
            x2 = _ffn(x1, h2, mod[l], cw, w_ffn_gate[i:i + 1].astype(BF16),
                      w_ffn_up[i:i + 1].astype(BF16), w_ffn_down[i:i + 1].astype(BF16), seq, 1024, 2)
        else:
            x2 = _moe(x1, h2, mod[l], _pad_cols(w_router[i], LANES).astype(BF16),
                      _pad_cols(b_router[i].reshape(1, -1), LANES), w_exp_gate[i].astype(BF16),
                      w_exp_up[i].astype(BF16), w_exp_down[i].astype(BF16), seq)
    return x2.reshape(b, seq, d)
```

```python
import functools

import numpy as np
import jax
import jax.numpy as jnp
from jax import lax
from jax.experimental import pallas as pl
from jax.experimental.pallas import tpu as pltpu

HEAD_DIM = 64
N_HEADS_FOX = 6
N_HEADS_MOBA = 6
DILATED_PAIRS = ((128, 1), (512, 4), (2048, 16))
N_SLOTS_DIL = 4
N_HEADS_DIL = N_SLOTS_DIL * len(DILATED_PAIRS)
N_HEADS = N_HEADS_FOX + N_HEADS_MOBA + N_HEADS_DIL
MIX_WIDTH = N_HEADS * HEAD_DIM
N_BRANCH = 3
MOBA_BLOCK = 256
MOBA_TOPK = 3
WIN_BLOCK = 128
N_ALIBI = N_HEADS_MOBA + N_HEADS_DIL
DIL_SLOPE_OFFSETS = (0, N_SLOTS_DIL, 2 * N_SLOTS_DIL + N_HEADS_MOBA)
MOBA_SLOPE_OFFSET = 2 * N_SLOTS_DIL
TOP_K = 2
RMS_EPS = 1e-6
NEG_INF = -1e30

LANES = 128
PAIR = 2 * HEAD_DIM
VMEM_LIMIT = 56 * 1024 * 1024

F32 = jnp.float32
BF16 = jnp.bfloat16
_NT = (((1,), (1,)), ((), ()))


def _alibi_slopes(n):
    return (2.0 ** (-8.0 * np.arange(1, n + 1) / n)).astype(np.float32)


def _sigmoid(x):
    return 1.0 / (1.0 + jnp.exp(-x))


def _params(*sem):
    return pltpu.CompilerParams(dimension_semantics=sem, vmem_limit_bytes=VMEM_LIMIT)


def _adaln_kernel(c_ref, w_ref, b_ref, o_ref):
    c = c_ref[...]
    cond = c * _sigmoid(c)
    o_ref[0] = jnp.dot(cond, w_ref[0], precision=lax.Precision.HIGHEST,
                       preferred_element_type=F32) + b_ref[0]


def _adaln(c, w_ada, b_ada):
    depth, d, n = w_ada.shape
    b = c.shape[0]
    tn = 1536
    return pl.pallas_call(
        _adaln_kernel,
        grid=(depth, n // tn),
        in_specs=[pl.BlockSpec((b, d), lambda l, j: (0, 0)),
                  pl.BlockSpec((1, d, tn), lambda l, j: (l, 0, j)),
                  pl.BlockSpec((1, 1, tn), lambda l, j: (l, 0, j))],
        out_specs=pl.BlockSpec((1, b, tn), lambda l, j: (l, 0, j)),
        out_shape=jax.ShapeDtypeStruct((depth, b, n), F32),
        compiler_params=_params("arbitrary", "arbitrary"),
        name="adaln",
    )(c, w_ada, b_ada.reshape(depth, 1, n))


def _mod_norm(x, g, scale, shift):
    ms = jnp.mean(x * x, axis=-1, keepdims=True)
    return (x * lax.rsqrt(ms + RMS_EPS) * g) * (1.0 + scale) + shift


def _inproj_kernel(x_ref, mod_ref, g_ref, w_ref, wg_ref, wf_ref, qg_ref, kg_ref, bf_ref, e_ref,
                   q_ref, k_ref, v_ref, qd1, kd1, vd1, qd2, kd2, vd2, qd3, kd3, vd3,
                   gate_ref, lf_ref, perm_scr, *, n_main):
    h = _mod_norm(x_ref[...], g_ref[...], mod_ref[0, 1:2, :], mod_ref[0, 0:1, :]).astype(BF16)
    tm = x_ref.shape[0]
    qk_chunk = 4 * HEAD_DIM
    q_outs = (q_ref, qd1, qd2, qd3)
    k_outs = (k_ref, kd1, kd2, kd3)
    v_outs = (v_ref, vd1, vd2, vd3)

    def put(outs, col, val):
        if col < n_main:
            outs[0][:, col:col + qk_chunk] = val.astype(BF16)
            return
        g = (col - n_main) // qk_chunk
        dil = DILATED_PAIRS[g][1]
        if dil == 1:
            outs[1 + g][0, 0] = val.astype(BF16)
            return
        halves = qk_chunk // LANES
        for c in range(halves):
            perm_scr[c] = val[:, c * LANES:(c + 1) * LANES]
        for r in range(dil):
            outs[1 + g][0, r] = jnp.concatenate(
                [perm_scr[c, pl.ds(r, tm // dil, stride=dil), :] for c in range(halves)],
                axis=1).astype(BF16)

    def proj(wt_ref, col0, width):
        return lax.dot_general(h, wt_ref[col0:col0 + width, :], _NT, preferred_element_type=F32)

    wide = MIX_WIDTH // 2
    for half in range(2):
        yq = proj(w_ref, half * wide, wide)
        yk = proj(w_ref, MIX_WIDTH + half * wide, wide)
        yv = proj(w_ref, 2 * MIX_WIDTH + half * wide, wide)
        for c in range(wide // qk_chunk):
            col = half * wide + c * qk_chunk
            yq_c = yq[:, c * qk_chunk:(c + 1) * qk_chunk]
            yk_c = yk[:, c * qk_chunk:(c + 1) * qk_chunk]
            sq = jnp.concatenate([yq_c * yq_c, yk_c * yk_c], axis=0).astype(BF16)
            ss = jnp.dot(sq, e_ref[...], preferred_element_type=F32)
            r = lax.rsqrt(ss * (1.0 / HEAD_DIM) + RMS_EPS)
            put(q_outs, col, (yq_c * r[0:tm] * qg_ref[:, col:col + qk_chunk]) * HEAD_DIM ** -0.5)
            put(k_outs, col, yk_c * r[tm:2 * tm] * kg_ref[:, col:col + qk_chunk])
            put(v_outs, col, yv[:, c * qk_chunk:(c + 1) * qk_chunk])

    gchunk = 1024
    for c in range(wg_ref.shape[0] // gchunk):
        y = proj(wg_ref, c * gchunk, gchunk)
        gate_ref[:, c * gchunk:(c + 1) * gchunk] = _sigmoid(y).astype(BF16)

    f = proj(wf_ref, 0, LANES) + bf_ref[...]
    lf_ref[...] = jnp.minimum(f, 0.0) - jnp.log(1.0 + jnp.exp(-jnp.abs(f)))


def _inproj(x2, mod_l, g_norm, w_qkv, w_gate, w_f, q_gain, k_gain, b_f, seq, tm):
    t, d = x2.shape
    n_main = (N_HEADS_FOX + N_HEADS_MOBA) * HEAD_DIM
    dil_w = N_SLOTS_DIL * HEAD_DIM
    e = (np.arange(dil_w)[:, None] // HEAD_DIM == np.arange(dil_w)[None, :] // HEAD_DIM)
    e = jnp.asarray(e, BF16)
    per_b = seq // tm
    row = lambda i: (i, 0)
    fix = lambda i: (0, 0)
    qkv_shapes = [jax.ShapeDtypeStruct((t, n_main), BF16)] * 3
    qkv_specs = [pl.BlockSpec((tm, n_main), row)] * 3
    for _, dil in DILATED_PAIRS:
        assert tm % (16 * dil) == 0
        qkv_shapes += [jax.ShapeDtypeStruct((t // seq, dil, seq // dil, dil_w), BF16)] * 3
        qkv_specs += [pl.BlockSpec((1, dil, tm // dil, dil_w),
                                   lambda i: (i // per_b, 0, i % per_b, 0))] * 3
    out_shapes = qkv_shapes + [jax.ShapeDtypeStruct((t, N_BRANCH * d), BF16),
                               jax.ShapeDtypeStruct((t, LANES), F32)]
    out_specs = qkv_specs + [pl.BlockSpec((tm, N_BRANCH * d), row), pl.BlockSpec((tm, LANES), row)]
    return pl.pallas_call(
        functools.partial(_inproj_kernel, n_main=n_main),
        grid=(t // tm,),
        in_specs=[pl.BlockSpec((tm, d), row),
                  pl.BlockSpec((1, 6, d), lambda i: (i // per_b, 0, 0)),
                  pl.BlockSpec((1, d), fix),
                  pl.BlockSpec(w_qkv.shape, fix, pipeline_mode=pl.Buffered(1)),
                  pl.BlockSpec(w_gate.shape, fix, pipeline_mode=pl.Buffered(1)),
                  pl.BlockSpec(w_f.shape, fix),
                  pl.BlockSpec((1, MIX_WIDTH), fix),
                  pl.BlockSpec((1, MIX_WIDTH), fix),
                  pl.BlockSpec((1, LANES), fix),
                  pl.BlockSpec(e.shape, fix)],
        out_specs=out_specs,
        out_shape=out_shapes,
        scratch_shapes=[pltpu.VMEM((dil_w // LANES, tm, LANES), F32)],
        compiler_params=_params("arbitrary"),
        name="inproj",
    )(x2, mod_l, g_norm, w_qkv, w_gate, w_f, q_gain, k_gain, b_f, e)


def _decay_kernel(lf_ref, tri_ref, o_ref, *, blk):
    carry = jnp.zeros((1, LANES), F32)
    for j in range(lf_ref.shape[1] // blk):
        c = jnp.dot(tri_ref[...], lf_ref[0, j * blk:(j + 1) * blk, :],
                    precision=lax.Precision.HIGHEST, preferred_element_type=F32) + carry
        o_ref[0, j * blk:(j + 1) * blk, :] = c
        carry = c[blk - 1:blk, :]


def _decay(lf, blk=256):
    b, seq, _ = lf.shape
    tri = jnp.asarray(np.arange(blk)[:, None] >= np.arange(blk)[None, :], F32)
    return pl.pallas_call(
        functools.partial(_decay_kernel, blk=blk),
        grid=(b,),
        in_specs=[pl.BlockSpec((1, seq, LANES), lambda i: (i, 0, 0)),
                  pl.BlockSpec((blk, blk), lambda i: (0, 0))],
        out_specs=pl.BlockSpec((1, seq, LANES), lambda i: (i, 0, 0)),
        out_shape=jax.ShapeDtypeStruct((b, seq, LANES), F32),
        compiler_params=_params("arbitrary"),
        name="fox_decay",
    )(lf, tri)


def _head_masks():
    lane = lax.broadcasted_iota(jnp.int32, (1, PAIR), 1)
    return lane < HEAD_DIM, lane >= HEAD_DIM


def _split_heads(q):
    return [jnp.where(m, q, jnp.zeros_like(q)) for m in _head_masks()]


def _feat_base(hh):
    return HEAD_DIM * (1 - hh)


def _three_bf16(x):
    hi = x.astype(BF16).astype(F32)
    mid = (x - hi).astype(BF16).astype(F32)
    lo = (x - hi - mid).astype(BF16).astype(F32)
    return hi, mid, lo


def _place(idx, base, parts):
    out = jnp.zeros(jnp.broadcast_shapes(idx.shape, jnp.shape(parts[0])), F32)
    for r, part in enumerate(parts):
        out = jnp.where(idx == base + r, part, out)
    return out


def _flash_pair(i, qx, kx_scr, v_ref, o_ref, keep_own, own_first, m_scr, acc_scr, p_scr, a_scr, tq):
    _reset(m_scr, acc_scr)

    def scores(j, own, slot):
        for hh in range(2):
            s = lax.dot_general(qx[hh], kx_scr[hh, j * tq:(j + 1) * tq, :], _NT,
                                preferred_element_type=F32)
            if own:
                s = jnp.where(keep_own, s, NEG_INF)
            _softmax_stage(s, m_scr.at[hh], p_scr.at[slot, hh], a_scr.at[slot, hh])

    def values(j, slot):
        vs = _with_ones(v_ref[0, j * tq:(j + 1) * tq, :])
        for hh in range(2):
            _value_stage(vs[hh], acc_scr.at[hh], p_scr.at[slot, hh], a_scr.at[slot, hh])

    for c in range(v_ref.shape[1] // tq):
        @pl.when(i == c)
        def _(c=c):
            order = [c] + list(range(c)) if own_first else list(range(c + 1))
            scores(order[0], order[0] == c, 0)
            for n in range(1, len(order)):
                scores(order[n], order[n] == c, n % 2)
                values(order[n - 1], (n - 1) % 2)
            values(order[-1], (len(order) - 1) % 2)

    o_ref[0] = _merge_pair(acc_scr)


def _with_ones(v):
    return [jnp.where(m, v, jnp.ones_like(v)) for m in _head_masks()]


def _softmax_stage(s, m_ref, p_ref, a_ref):
    m_prev = m_ref[...]
    m_new = jnp.maximum(m_prev, jnp.max(s, axis=-1, keepdims=True))
    p = jnp.exp(s - jnp.concatenate([m_new] * (s.shape[1] // LANES), axis=1))
    m_ref[...] = m_new
    p_ref[...] = p.astype(BF16)
    a_ref[...] = jnp.exp(m_prev - m_new)


def _value_stage(v_ones, acc_ref, p_ref, a_ref):
    acc_ref[...] = a_ref[...] * acc_ref[...] + jnp.dot(p_ref[...], v_ones, preferred_element_type=F32)


def _reset(m_scr, acc_scr):
    m_scr[...] = jnp.full(m_scr.shape, NEG_INF, F32)
    acc_scr[...] = jnp.zeros(acc_scr.shape, F32)


def _merge_pair(acc_scr):
    first = _head_masks()[0]
    a0, a1 = acc_scr[0], acc_scr[1]
    return jnp.where(first, a0 / a0[:, HEAD_DIM:HEAD_DIM + 1], a1 / a1[:, 0:1]).astype(BF16)


def _stat_scratch(tq):
    return [pltpu.VMEM((2, tq, LANES), F32), pltpu.VMEM((2, tq, PAIR), F32),
            pltpu.VMEM((2, 2, tq, tq), BF16), pltpu.VMEM((2, 2, tq, LANES), F32)]


def _fox_kernel(q_ref, k_ref, v_ref, cum_ref, o_ref, kx_scr, m_scr, acc_scr, p_scr, a_scr, *, tq):
    pair, i = pl.program_id(1), pl.program_id(2)
    masks = _head_masks()
    lane = lax.broadcasted_iota(jnp.int32, (1, PAIR), 1)

    @pl.when(i == 0)
    def _():
        k = k_ref[0]
        cum = cum_ref[0]
        lane_s = lax.broadcasted_iota(jnp.int32, cum.shape, 1)
        for hh in range(2):
            f_s = jnp.sum(jnp.where(lane_s == 2 * pair + hh, cum, 0.0), axis=-1, keepdims=True)
            feats = _place(lane, _feat_base(hh), _three_bf16(-f_s))
            kx_scr[hh] = jnp.where(masks[hh], k, feats.astype(BF16))

    q = q_ref[0]
    qx = []
    for hh in range(2):
        ones = (lane >= _feat_base(hh)) & (lane < _feat_base(hh) + 3)
        qx.append(jnp.where(masks[hh], q, jnp.where(ones, 1.0, 0.0).astype(BF16)))
    causal = (lax.broadcasted_iota(jnp.int32, (tq, tq), 1)
              <= lax.broadcasted_iota(jnp.int32, (tq, tq), 0))
    _flash_pair(i, qx, kx_scr, v_ref, o_ref, causal, False, m_scr, acc_scr, p_scr, a_scr, tq)


def _fox(q, k, v, cum, col0, n_pairs, tq):
    b, seq, _ = q.shape
    return pl.pallas_call(
        functools.partial(_fox_kernel, tq=tq),
        grid=(b, n_pairs, seq // tq),
        in_specs=[pl.BlockSpec((1, tq, PAIR), lambda bi, p, i: (bi, i, col0 + p)),
                  pl.BlockSpec((1, seq, PAIR), lambda bi, p, i: (bi, 0, col0 + p)),
                  pl.BlockSpec((1, seq, PAIR), lambda bi, p, i: (bi, 0, col0 + p)),
                  pl.BlockSpec((1, seq, LANES), lambda bi, p, i: (bi, 0, 0))],
        out_specs=pl.BlockSpec((1, tq, PAIR), lambda bi, p, i: (bi, i, p)),
        out_shape=jax.ShapeDtypeStruct((b, seq, n_pairs * PAIR), BF16),
        scratch_shapes=[pltpu.VMEM((2, seq, PAIR), BF16)] + _stat_scratch(tq),
        compiler_params=_params("arbitrary", "arbitrary", "arbitrary"),
        name="fox_attn",
    )(q, k, v, cum)


def _moba_kernel(q_ref, k_ref, v_ref, sl_ref, o_ref, km_scr, kx_scr, m_scr, acc_scr, p_scr, a_scr):
    blk = MOBA_BLOCK
    nbp = 8
    i = pl.program_id(2)
    seq = k_ref.shape[1]
    n_blk = seq // blk
    tq = q_ref.shape[1]
    per = tq // blk
    masks = _head_masks()

    @pl.when(i == 0)
    def _():
        km_scr[...] = jnp.zeros(km_scr.shape, F32)
        for n in range(n_blk):
            kb = k_ref[0, n * blk:(n + 1) * blk, :].astype(F32)
            km_scr[n:n + 1, :] = jnp.sum(kb, axis=0, keepdims=True) * (1.0 / blk)
        k = k_ref[0]
        lane = lax.broadcasted_iota(jnp.int32, (seq, PAIR), 1)
        pos = lax.broadcasted_iota(jnp.int32, (seq, PAIR), 0)
        for hh in range(2):
            base = _feat_base(hh)
            in_blk = (lane >= base) & (lane < base + nbp) & (pos // blk == lane - base)
            ones = (lane >= base + 11) & (lane < base + 14)
            feats = jnp.where(in_blk | ones, 1.0, 0.0) + _place(
                lane, base + nbp, _three_bf16(sl_ref[0, hh:hh + 1, 0:LANES] * pos.astype(F32)))
            kx_scr[hh] = jnp.where(masks[hh], k, feats.astype(BF16))

    q = q_ref[0]
    qs = _split_heads(q)
    blk_n = lax.broadcasted_iota(jnp.int32, (nbp, tq), 0)
    q_blk = i * per + lax.broadcasted_iota(jnp.int32, (nbp, tq), 1) // blk
    row = lax.broadcasted_iota(jnp.int32, (LANES, tq), 0)
    t_pos = (i * tq + lax.broadcasted_iota(jnp.int32, (LANES, tq), 1)).astype(F32)
    qx = []
    for hh in range(2):
        base = _feat_base(hh)
        g = lax.dot_general(km_scr[...], qs[hh].astype(F32), _NT, precision=lax.Precision.HIGHEST,
                            preferred_element_type=F32)[0:nbp]
        cnt = jnp.zeros((nbp, tq), F32)
        for m in range(n_blk):
            gm = g[m:m + 1, :]
            beats = ((gm > g) | ((gm == g) & (blk_n > m))) & (q_blk > m)
            cnt = cnt + jnp.where(beats, 1.0, 0.0)
        visible = ((blk_n < q_blk) & (cnt < float(MOBA_TOPK))) | (blk_n == q_blk)
        hide = jnp.where(visible, 0.0, NEG_INF)
        pads = [jnp.zeros((n, tq), F32) for n in (base, LANES - nbp - base)]
        feats = jnp.concatenate([a for a in (pads[0], hide, pads[1]) if a.shape[0]], axis=0)
        feats = feats + jnp.where((row >= base + nbp) & (row < base + 11), 1.0, 0.0) + _place(
            row, base + 11, _three_bf16(-sl_ref[0, hh:hh + 1, :] * t_pos))
        qx.append(jnp.where(masks[hh], q, feats.T.astype(BF16)))

    r = lax.broadcasted_iota(jnp.int32, (tq, tq), 0)
    c = lax.broadcasted_iota(jnp.int32, (tq, tq), 1)
    keep_own = (r // blk != c // blk) | (c <= r)
    _flash_pair(i, qx, kx_scr, v_ref, o_ref, keep_own, True, m_scr, acc_scr, p_scr, a_scr, tq)


def _moba(q, k, v, col0, n_pairs, tq):
    b, seq, _ = q.shape
    blk = MOBA_BLOCK
    assert seq % tq == 0 and tq % blk == 0 and seq // blk <= 8
    slopes = _alibi_slopes(N_ALIBI)[MOBA_SLOPE_OFFSET:MOBA_SLOPE_OFFSET + 2 * n_pairs]
    sl = jnp.asarray(np.broadcast_to(slopes.reshape(n_pairs, 2, 1), (n_pairs, 2, tq)).copy())
    return pl.pallas_call(
        _moba_kernel,
        grid=(b, n_pairs, seq // tq),
        in_specs=[pl.BlockSpec((1, tq, PAIR), lambda bi, p, i: (bi, i, col0 + p)),
                  pl.BlockSpec((1, seq, PAIR), lambda bi, p, i: (bi, 0, col0 + p)),
                  pl.BlockSpec((1, seq, PAIR), lambda bi, p, i: (bi, 0, col0 + p)),
                  pl.BlockSpec((1, 2, tq), lambda bi, p, i: (p, 0, 0))],
        out_specs=pl.BlockSpec((1, tq, PAIR), lambda bi, p, i: (bi, i, p)),
        out_shape=jax.ShapeDtypeStruct((b, seq, n_pairs * PAIR), BF16),
        scratch_shapes=[pltpu.VMEM((LANES, PAIR), F32), pltpu.VMEM((2, seq, PAIR), BF16)]
        + _stat_scratch(tq),
        compiler_params=_params("arbitrary", "arbitrary", "arbitrary"),
        name="moba_attn",
    )(q, k, v, sl)


def _dil_kernel(q_ref, k_ref, v_ref, o_ref, lse_ref, *, dil, slopes, n_q):
    wb = WIN_BLOCK
    seq = q_ref.shape[1]
    nb = seq // wb
    steps = (lax.broadcasted_iota(jnp.int32, (wb, 2 * wb), 0) + wb
             - lax.broadcasted_iota(jnp.int32, (wb, 2 * wb), 1))
    band_ok = (steps >= 0) & (steps <= wb)
    steps_f = (steps * dil).astype(F32)
    blk = lax.broadcasted_iota(jnp.int32, (nb, 1, 2 * wb), 0)
    key = lax.broadcasted_iota(jnp.int32, (nb, 1, 2 * wb), 2)
    no_prev = jnp.where((blk % n_q == 0) & (key < wb), -NEG_INF, 0.0)
    first = _head_masks()[0]

    def band(ref, cols):
        cur = ref[0, :, cols]
        prev = jnp.concatenate([cur[0:wb], cur[0:seq - wb]], axis=0)
        return jnp.concatenate([prev.reshape(nb, wb, PAIR), cur.reshape(nb, wb, PAIR)], axis=1)

    for pair in range(2):
        cols = slice(pair * PAIR, (pair + 1) * PAIR)
        qs = [qh.reshape(nb, wb, PAIR) for qh in _split_heads(q_ref[0, :, cols])]
        kband, vband = band(k_ref, cols), band(v_ref, cols)
        outs, lses = [], []
        for hh in range(2):
            bias = jnp.where(band_ok, slopes[2 * pair + hh] * steps_f, -NEG_INF)
            s = jnp.einsum("bqd,bkd->bqk", qs[hh], kband, preferred_element_type=F32)
            s = s - bias[None] - no_prev
            m = jnp.max(s, axis=-1, keepdims=True)
            p = jnp.exp(s - m)
            l = jnp.sum(p, axis=-1, keepdims=True)
            o = jnp.einsum("bqk,bkd->bqd", p.astype(BF16), vband, preferred_element_type=F32)
            outs.append(o / l)
            lses.append(jnp.broadcast_to(m + jnp.log(l), (nb, wb, PAIR)))
        o_ref[0, :, cols] = jnp.where(first, outs[0], outs[1]).reshape(seq, PAIR).astype(BF16)
        lse_ref[0, :, cols] = jnp.where(first, lses[0], lses[1]).reshape(seq, PAIR)


def _dilated(q, k, v, group):
    window, dil = DILATED_PAIRS[group]
    assert window // dil == WIN_BLOCK
    b, _, l_sub, w = q.shape
    seq = dil * l_sub
    so = DIL_SLOPE_OFFSETS[group]
    slopes = tuple(float(s) for s in _alibi_slopes(N_ALIBI)[so:so + N_SLOTS_DIL])
    spec = pl.BlockSpec((1, seq, w), lambda bi: (bi, 0, 0))
    flat = lambda a: a.reshape(b, seq, w)
    o, lse = pl.pallas_call(
        functools.partial(_dil_kernel, dil=dil, slopes=slopes, n_q=l_sub // WIN_BLOCK),
        grid=(b,),
        in_specs=[spec, spec, spec],
        out_specs=[spec, spec],
        out_shape=[jax.ShapeDtypeStruct((b, seq, w), BF16), jax.ShapeDtypeStruct((b, seq, w), F32)],
        compiler_params=_params("arbitrary"),
        name=f"dilated_attn_{dil}",
    )(flat(q), flat(k), flat(v))
    return o.reshape(q.shape), lse.reshape(q.shape)


def _outproj_kernel(x_ref, mod_ref, g_ref, of_ref, om_ref, o1, l1, o2, l2, o3, l3, gate_ref,
                    wf_ref, wm_ref, wd_ref, wo_ref, x1_ref, h2_ref, perm_scr):
    d = x_ref.shape[1]

    def natural(ref):
        dil, n = ref.shape[1], ref.shape[2]
        if dil == 1:
            return ref[0, 0].astype(F32)
        halves = ref.shape[3] // LANES
        for r in range(dil):
            blk = ref[0, r].astype(F32)
            for c in range(halves):
                perm_scr[c, pl.ds(r, n, stride=dil), :] = blk[:, c * LANES:(c + 1) * LANES]
        return jnp.concatenate([perm_scr[c] for c in range(halves)], axis=1)

    lses = [natural(l) for l in (l1, l2, l3)]
    lmax = jnp.maximum(jnp.maximum(lses[0], lses[1]), lses[2])
    e1, e2, e3 = [jnp.exp(l - lmax) for l in lses]
    den = e1 + e2 + e3
    o_dil = (e1 / den) * natural(o1) + (e2 / den) * natural(o2) + (e3 / den) * natural(o3)
    y = (gate_ref[:, 0:d].astype(F32) * jnp.dot(of_ref[...], wf_ref[...], preferred_element_type=F32)
         + gate_ref[:, d:2 * d].astype(F32) * jnp.dot(om_ref[...], wm_ref[...], preferred_element_type=F32)
         + gate_ref[:, 2 * d:3 * d].astype(F32)
         * jnp.dot(o_dil.astype(BF16), wd_ref[...], preferred_element_type=F32))
    out = jnp.dot(y.astype(BF16), wo_ref[...], preferred_element_type=F32)
    x1 = x_ref[...] + mod_ref[0, 2:3, :] * out
    x1_ref[...] = x1
    h2_ref[...] = _mod_norm(x1, g_ref[...], mod_ref[0, 4:5, :], mod_ref[0, 3:4, :]).astype(BF16)


def _outproj(x2, mod_l, g_norm, o_fox, o_moba, dil_outs, gates, w_f, w_m, w_d, w_o, seq, tm):
    t, d = x2.shape
    per_b = seq // tm
    row = lambda i: (i, 0)
    fix = lambda i: (0, 0)
    dil_args, dil_specs = [], []
    for o, lse in dil_outs:
        _, dil, _, w = o.shape
        dil_args += [o, lse]
        dil_specs += [pl.BlockSpec((1, dil, tm // dil, w), lambda i: (i // per_b, 0, i % per_b, 0))] * 2
    return pl.pallas_call(
        _outproj_kernel,
        grid=(t // tm,),
        in_specs=[pl.BlockSpec((tm, d), row),
                  pl.BlockSpec((1, 6, d), lambda i: (i // per_b, 0, 0)),
                  pl.BlockSpec((1, d), fix),
                  pl.BlockSpec((tm, o_fox.shape[1]), row),
                  pl.BlockSpec((tm, o_moba.shape[1]), row)] + dil_specs + [
                  pl.BlockSpec((tm, N_BRANCH * d), row),
                  pl.BlockSpec(w_f.shape, fix), pl.BlockSpec(w_m.shape, fix),
                  pl.BlockSpec(w_d.shape, fix), pl.BlockSpec(w_o.shape, fix)],
        out_specs=[pl.BlockSpec((tm, d), row), pl.BlockSpec((tm, d), row)],
        out_shape=[jax.ShapeDtypeStruct((t, d), F32), jax.ShapeDtypeStruct((t, d), BF16)],
        scratch_shapes=[pltpu.VMEM((N_SLOTS_DIL * HEAD_DIM // LANES, tm, LANES), F32)],
        compiler_params=_params("arbitrary"),
        name="outproj",
    )(x2, mod_l, g_norm, o_fox, o_moba, *dil_args, gates, w_f, w_m, w_d, w_o)


def _router_kernel(h_ref, w_ref, b_ref, o_ref, cnt_ref, *, n_exp):
    logits = jnp.dot(h_ref[...], w_ref[...], preferred_element_type=F32) + b_ref[...]
    lane = lax.broadcasted_iota(jnp.int32, logits.shape, 1)
    logits = jnp.where(lane < n_exp, logits, -jnp.inf)

    def top(vals):
        m = jnp.max(vals, axis=-1, keepdims=True)
        idx = jnp.min(jnp.where(vals == m, lane, LANES), axis=-1, keepdims=True)
        return m, lane == idx

    m1, hot1 = top(logits)
    m2, hot2 = top(jnp.where(hot1, -jnp.inf, logits))
    e2 = jnp.exp(m2 - m1)
    cw = jnp.where(hot1, 1.0 / (1.0 + e2), 0.0) + jnp.where(hot2, e2 / (1.0 + e2), 0.0)
    o_ref[...] = cw
    n_tok = jnp.sum(jnp.where(cw > 0.0, 1.0, 0.0), axis=0, keepdims=True)
    cnt_ref[...] = jnp.broadcast_to(n_tok, cnt_ref.shape).astype(jnp.int32)


def _router(h2, w_r, b_r, n_exp, tm):
    t, d = h2.shape
    cw, cnt = pl.pallas_call(
        functools.partial(_router_kernel, n_exp=n_exp),
        grid=(t // tm,),
        in_specs=[pl.BlockSpec((tm, d), lambda i: (i, 0)),
                  pl.BlockSpec((d, LANES), lambda i: (0, 0)),
                  pl.BlockSpec((1, LANES), lambda i: (0, 0))],
        out_specs=[pl.BlockSpec((tm, LANES), lambda i: (i, 0)),
                   pl.BlockSpec((8, LANES), lambda i: (i, 0))],
        out_shape=[jax.ShapeDtypeStruct((t, LANES), F32),
                   jax.ShapeDtypeStruct((t // tm * 8, LANES), jnp.int32)],
        compiler_params=_params("arbitrary"),
        name="router",
    )(h2, w_r, b_r)
    return cw, cnt.reshape(t // tm, 8, LANES)[:, 0, :n_exp]


MOE_TILE = 512
MOE_ROWS = 512
MOE_PUT = 128
MOE_GET = 256
MOE_ALIGN = 16


def _moe_layout(cnt, n_rows_static):
    n_tiles, n_exp = cnt.shape
    padded = (cnt + MOE_ALIGN - 1) // MOE_ALIGN * MOE_ALIGN
    length = jnp.sum(padded, axis=0)
    span = (length + (MOE_PUT - MOE_ALIGN) + MOE_ROWS - 1) // MOE_ROWS * MOE_ROWS
    start = jnp.cumsum(span) - span
    off = start[None, :] + jnp.cumsum(padded, axis=0) - padded
    n_steps = n_rows_static // MOE_ROWS
    first = start // MOE_ROWS
    step = jnp.arange(n_steps, dtype=jnp.int32)
    expert = jnp.sum((first[None, :] <= step[:, None]).astype(jnp.int32), axis=1) - 1
    active = step < (first + (length + MOE_ROWS - 1) // MOE_ROWS)[expert]
    return (off.reshape(-1).astype(jnp.int32), cnt.reshape(-1).astype(jnp.int32),
            expert.astype(jnp.int32), active.astype(jnp.int32))


def _dispatch_kernel(off_ref, cnt_ref, h_ref, cw_ref, u_ref, xs_in, xs_ref, stage, sems, *, n_exp):
    del xs_in
    i = pl.program_id(0)
    n_put = MOE_TILE // MOE_PUT
    routed = jnp.where(cw_ref[...].T[0:8] > 0.0, 1.0, 0.0)
    pos = jnp.dot(routed.astype(BF16), u_ref[...], preferred_element_type=F32)
    row = lax.broadcasted_iota(jnp.int32, (MOE_PUT, MOE_TILE), 0).astype(F32)
    h = h_ref[...]

    def copy(e, s):
        slot = e * n_put + s
        dst = pl.multiple_of(off_ref[i * n_exp + e] + s * MOE_PUT, MOE_ALIGN)
        return pltpu.make_async_copy(stage.at[slot], xs_ref.at[pl.ds(dst, MOE_PUT), :], sems.at[slot])

    for e in range(n_exp):
        for s in range(n_put):
            @pl.when(s * MOE_PUT < cnt_ref[i * n_exp + e])
            def _():
                take = (routed[e:e + 1, :] > 0.5) & (pos[e:e + 1, :] == row + float(s * MOE_PUT))
                onehot = jnp.where(take, 1.0, 0.0).astype(BF16)
                stage[e * n_put + s] = jnp.dot(onehot, h, preferred_element_type=F32).astype(BF16)
                copy(e, s).start()

    for e in range(n_exp):
        for s in range(n_put):
            @pl.when(s * MOE_PUT < cnt_ref[i * n_exp + e])
            def _():
                copy(e, s).wait()


def _dispatch(h2, cw, off, cnt, n_exp, n_rows):
    t, d = h2.shape
    u = jnp.asarray(np.arange(MOE_TILE)[:, None] < np.arange(MOE_TILE)[None, :], BF16)
    n_slots = n_exp * (MOE_TILE // MOE_PUT)
    return pl.pallas_call(
        functools.partial(_dispatch_kernel, n_exp=n_exp),
        grid_spec=pltpu.PrefetchScalarGridSpec(
            num_scalar_prefetch=2,
            grid=(t // MOE_TILE,),
            in_specs=[pl.BlockSpec((MOE_TILE, d), lambda i, o, c: (i, 0)),
                      pl.BlockSpec((MOE_TILE, LANES), lambda i, o, c: (i, 0)),
                      pl.BlockSpec((MOE_TILE, MOE_TILE), lambda i, o, c: (0, 0)),
                      pl.BlockSpec(memory_space=pl.ANY)],
            out_specs=pl.BlockSpec(memory_space=pl.ANY),
            scratch_shapes=[pltpu.VMEM((n_slots, MOE_PUT, d), BF16),
                            pltpu.SemaphoreType.DMA((n_slots,))]),
        out_shape=jax.ShapeDtypeStruct((n_rows, d), BF16),
        input_output_aliases={5: 0},
        compiler_params=_params("arbitrary"),
        name="moe_dispatch",
    )(off, cnt, h2, cw, u, jnp.zeros((n_rows, d), BF16))


FF_CHUNK = 768


def _swiglu(x, wg_ref, wu_ref, wd_ref):
    ff = wg_ref.shape[2]
    y = None
    for c0 in range(0, ff, FF_CHUNK):
        c1 = min(c0 + FF_CHUNK, ff)
        gate = jnp.dot(x, wg_ref[0, :, c0:c1], preferred_element_type=F32)
        up = jnp.dot(x, wu_ref[0, :, c0:c1], preferred_element_type=F32)
        a = ((gate * _sigmoid(gate)) * up).astype(BF16)
        part = jnp.dot(a, wd_ref[0, c0:c1, :], preferred_element_type=F32)
        y = part if y is None else y + part
    return y


def _experts_kernel(exp_ref, act_ref, xs_ref, wg_ref, wu_ref, wd_ref, ys_ref):
    g = pl.program_id(0)

    @pl.when(act_ref[g] > 0)
    def _():
        ys_ref[...] = _swiglu(xs_ref[...], wg_ref, wu_ref, wd_ref).astype(BF16)

    @pl.when(act_ref[g] == 0)
    def _():
        ys_ref[...] = jnp.zeros(ys_ref.shape, BF16)


def _experts(xs, expert, active, w_g, w_u, w_d):
    n_rows, d = xs.shape
    _, _, ff = w_g.shape
    wmap = lambda g, ex, ac: (ex[g], 0, 0)
    return pl.pallas_call(
        _experts_kernel,
        grid_spec=pltpu.PrefetchScalarGridSpec(
            num_scalar_prefetch=2,
            grid=(n_rows // MOE_ROWS,),
            in_specs=[pl.BlockSpec((MOE_ROWS, d), lambda g, ex, ac: (g, 0)),
                      pl.BlockSpec((1, d, ff), wmap), pl.BlockSpec((1, d, ff), wmap),
                      pl.BlockSpec((1, ff, d), wmap)],
            out_specs=pl.BlockSpec((MOE_ROWS, d), lambda g, ex, ac: (g, 0))),
        out_shape=jax.ShapeDtypeStruct((n_rows, d), BF16),
        compiler_params=_params("arbitrary"),
        name="moe_experts",
    )(expert, active, xs, w_g, w_u, w_d)


def _combine_kernel(off_ref, cnt_ref, x_ref, mod_ref, cw_ref, l_ref, ys_ref, o_ref, buf, sems, acc_scr,
                    *, n_exp):
    i = pl.program_id(0)
    n_get = MOE_TILE // MOE_GET

    def copy(e, s):
        slot = e * n_get + s
        src = pl.multiple_of(off_ref[i * n_exp + e] + s * MOE_GET, MOE_ALIGN)
        return pltpu.make_async_copy(ys_ref.at[pl.ds(src, MOE_GET), :], buf.at[slot], sems.at[slot])

    for e in range(n_exp):
        for s in range(n_get):
            @pl.when(s * MOE_GET < cnt_ref[i * n_exp + e])
            def _():
                copy(e, s).start()

    cw = cw_ref[...]
    routed = jnp.where(cw > 0.0, 1.0, 0.0)
    pos = jnp.dot(l_ref[...], routed.astype(BF16), preferred_element_type=F32)
    col = lax.broadcasted_iota(jnp.int32, (MOE_TILE, MOE_GET), 1).astype(F32)
    acc_scr[...] = jnp.zeros(acc_scr.shape, F32)
    for e in range(n_exp):
        for s in range(n_get):
            @pl.when(s * MOE_GET < cnt_ref[i * n_exp + e])
            def _():
                copy(e, s).wait()
                take = (cw[:, e:e + 1] > 0.0) & (pos[:, e:e + 1] == col + float(s * MOE_GET))
                onehot = jnp.where(take, 1.0, 0.0).astype(BF16)
                acc_scr[...] += cw[:, e:e + 1] * jnp.dot(onehot, buf[e * n_get + s],
                                                         preferred_element_type=F32)
    o_ref[...] = x_ref[...] + mod_ref[0, 5:6, :] * acc_scr[...]


def _combine(x1, mod_l, cw, ys, off, cnt, n_exp, seq):
    t, d = x1.shape
    per_b = seq // MOE_TILE
    low = jnp.asarray(np.arange(MOE_TILE)[:, None] > np.arange(MOE_TILE)[None, :], BF16)
    n_slots = n_exp * (MOE_TILE // MOE_GET)
    return pl.pallas_call(
        functools.partial(_combine_kernel, n_exp=n_exp),
        grid_spec=pltpu.PrefetchScalarGridSpec(
            num_scalar_prefetch=2,
            grid=(t // MOE_TILE,),
            in_specs=[pl.BlockSpec((MOE_TILE, d), lambda i, o, c: (i, 0)),
                      pl.BlockSpec((1, 6, d), lambda i, o, c: (i // per_b, 0, 0)),
                      pl.BlockSpec((MOE_TILE, LANES), lambda i, o, c: (i, 0)),
                      pl.BlockSpec((MOE_TILE, MOE_TILE), lambda i, o, c: (0, 0)),
                      pl.BlockSpec(memory_space=pl.ANY)],
            out_specs=pl.BlockSpec((MOE_TILE, d), lambda i, o, c: (i, 0)),
            scratch_shapes=[pltpu.VMEM((n_slots, MOE_GET, d), BF16),
                            pltpu.SemaphoreType.DMA((n_slots,)),
                            pltpu.VMEM((MOE_TILE, d), F32)]),
        out_shape=jax.ShapeDtypeStruct((t, d), F32),
        compiler_params=_params("arbitrary"),
        name="moe_combine",
    )(off, cnt, x1, mod_l, cw, low, ys)


def _moe(x1, h2, mod_l, w_r, b_r, w_g, w_u, w_d, seq):
    t, d = x1.shape
    n_exp = w_g.shape[0]
    n_tiles = t // MOE_TILE
    bound = (TOP_K * t + n_tiles * n_exp * (MOE_ALIGN - 1)
             + n_exp * (MOE_PUT - MOE_ALIGN + MOE_ROWS - 1))
    n_rows = (bound + MOE_ROWS - 1) // MOE_ROWS * MOE_ROWS + MOE_ROWS
    cw, cnt = _router(h2, w_r, b_r, n_exp, MOE_TILE)
    off, cnt, expert, active = _moe_layout(cnt, n_rows)
    xs = _dispatch(h2, cw, off, cnt, n_exp, n_rows)
    ys = _experts(xs, expert, active, w_g, w_u, w_d)
    return _combine(x1, mod_l, cw, ys, off, cnt, n_exp, seq)


def _ffn_kernel(x_ref, h_ref, mod_ref, wg_ref, wu_ref, wd_ref, o_ref):
    o_ref[...] = x_ref[...] + mod_ref[0, 5:6, :] * _swiglu(h_ref[...], wg_ref, wu_ref, wd_ref)


def _ffn(x1, h2, mod_l, w_g, w_u, w_d, seq, tm):
    t, d = x1.shape
    per_b = seq // tm
    row = lambda i: (i, 0)
    whole = lambda w: pl.BlockSpec(w.shape, lambda i: (0, 0, 0), pipeline_mode=pl.Buffered(1))
    return pl.pallas_call(
        _ffn_kernel,
        grid=(t // tm,),
        in_specs=[pl.BlockSpec((tm, d), row), pl.BlockSpec((tm, d), row),
                  pl.BlockSpec((1, 6, d), lambda i: (i // per_b, 0, 0)),
                  whole(w_g), whole(w_u), whole(w_d)],
        out_specs=pl.BlockSpec((tm, d), row),
        out_shape=jax.ShapeDtypeStruct((t, d), F32),
        compiler_params=_params("arbitrary"),
        name="swiglu",
    )(x1, h2, mod_l, w_g, w_u, w_d)


def _pad_cols(a, width):
    return jnp.pad(a, ((0, 0), (0, width - a.shape[1])))


def kernel(x, c, w_ada, b_ada, norm_mix, norm_ffn, w_in, b_fgate, q_gain, k_gain, w_br_fox, w_br_moba,
           w_br_dil, w_out, w_ffn_gate, w_ffn_up, w_ffn_down, w_router, b_router, w_exp_gate,
           w_exp_up, w_exp_down):
    b, seq, d = x.shape
    depth = w_ada.shape[0]
    t = b * seq
    tm = 512
    n_pairs_fox = N_HEADS_FOX // 2
    n_pairs_moba = N_HEADS_MOBA // 2
    tq, tk = 512, 512

    mod = _adaln(c, w_ada, b_ada).reshape(depth, b, 6, d)
    x2 = x.reshape(t, d)
    w_in_t = jnp.transpose(w_in, (2, 0, 1))
    for l in range(depth):
        wl = w_in_t[:, l, :].astype(BF16)
        f0 = 3 * MIX_WIDTH
        w_qkv, w_gate = wl[:f0], wl[f0 + N_HEADS_FOX:]
        w_f = jnp.pad(wl[f0:f0 + N_HEADS_FOX], ((0, LANES - N_HEADS_FOX), (0, 0)))
        b_f = _pad_cols(b_fgate[l].reshape(1, -1), LANES)
        outs = _inproj(x2, mod[l], norm_mix[l].reshape(1, d), w_qkv, w_gate, w_f, q_gain[l].reshape(1, -1),
                       k_gain[l].reshape(1, -1), b_f, seq, tm)
        q_m, k_m, v_m = (o.reshape(b, seq, -1) for o in outs[0:3])
        gates, lf = outs[12], outs[13]

        cum = _decay(lf.reshape(b, seq, LANES))
        o_fox = _fox(q_m, k_m, v_m, cum, 0, n_pairs_fox, tq).reshape(t, -1)
        o_moba = _moba(q_m, k_m, v_m, n_pairs_fox, n_pairs_moba, tq).reshape(t, -1)
        dil_outs = [_dilated(*outs[3 + 3 * g:6 + 3 * g], g) for g in range(len(DILATED_PAIRS))]

        x1, h2 = _outproj(x2, mod[l], norm_ffn[l].reshape(1, d), o_fox, o_moba, dil_outs, gates,
                          w_br_fox[l].astype(BF16), w_br_moba[l].astype(BF16),
                          w_br_dil[l].astype(BF16), w_out[l].astype(BF16), seq, tm)
        i = l // 2
        if l % 2 == 0:
            x2 = _ffn(x1, h2, mod[l], w_ffn_gate[i:i + 1].astype(BF16),
                      w_ffn_up[i:i + 1].astype(BF16), w_ffn_down[i:i + 1].astype(BF16), seq, 1024)
        else:
            x2 = _moe(x1, h2, mod[l], _pad_cols(w_router[i], LANES).astype(BF16),
                      _pad_cols(b_router[i].reshape(1, -1), LANES), w_exp_gate[i].astype(BF16),
                      w_exp_up[i].astype(BF16), w_exp_down[i].astype(BF16), seq)
    return x2.reshape(b, seq, d)
```

```python
import functools

import numpy as np
import jax
import jax.numpy as jnp
from jax import lax
from jax.experimental import pallas as pl
from jax.experimental.pallas import tpu as pltpu

HEAD_DIM = 64
N_HEADS_FOX = 6
N_HEADS_MOBA = 6
DILATED_PAIRS = ((128, 1), (512, 4), (2048, 16))
N_SLOTS_DIL = 4
N_HEADS_DIL = N_SLOTS_DIL * len(DILATED_PAIRS)
N_HEADS = N_HEADS_FOX + N_HEADS_MOBA + N_HEADS_DIL
MIX_WIDTH = N_HEADS * HEAD_DIM
N_BRANCH = 3
MOBA_BLOCK = 256
MOBA_TOPK = 3
WIN_BLOCK = 128
N_ALIBI = N_HEADS_MOBA + N_HEADS_DIL
DIL_SLOPE_OFFSETS = (0, N_SLOTS_DIL, 2 * N_SLOTS_DIL + N_HEADS_MOBA)
MOBA_SLOPE_OFFSET = 2 * N_SLOTS_DIL
TOP_K = 2
RMS_EPS = 1e-6
NEG_INF = -1e30

LANES = 128
PAIR = 2 * HEAD_DIM
VMEM_LIMIT = 56 * 1024 * 1024

F32 = jnp.float32
BF16 = jnp.bfloat16
_NT = (((1,), (1,)), ((), ()))


def _alibi_slopes(n):
    return (2.0 ** (-8.0 * np.arange(1, n + 1) / n)).astype(np.float32)


def _sigmoid(x):
    return 1.0 / (1.0 + jnp.exp(-x))


def _params(*sem):
    return pltpu.CompilerParams(dimension_semantics=sem, vmem_limit_bytes=VMEM_LIMIT)


def _adaln_kernel(c_ref, w_ref, b_ref, o_ref):
    c = c_ref[...]
    cond = c * _sigmoid(c)
    o_ref[0] = jnp.dot(cond, w_ref[0], precision=lax.Precision.HIGHEST,
                       preferred_element_type=F32) + b_ref[0]


def _adaln(c, w_ada, b_ada):
    depth, d, n = w_ada.shape
    b = c.shape[0]
    tn = 1536
    return pl.pallas_call(
        _adaln_kernel,
        grid=(depth, n // tn),
        in_specs=[pl.BlockSpec((b, d), lambda l, j: (0, 0)),
                  pl.BlockSpec((1, d, tn), lambda l, j: (l, 0, j)),
                  pl.BlockSpec((1, 1, tn), lambda l, j: (l, 0, j))],
        out_specs=pl.BlockSpec((1, b, tn), lambda l, j: (l, 0, j)),
        out_shape=jax.ShapeDtypeStruct((depth, b, n), F32),
        compiler_params=_params("arbitrary", "arbitrary"),
        name="adaln",
    )(c, w_ada, b_ada.reshape(depth, 1, n))


def _mod_norm(x, g, scale, shift):
    ms = jnp.mean(x * x, axis=-1, keepdims=True)
    return (x * lax.rsqrt(ms + RMS_EPS) * g) * (1.0 + scale) + shift


def _inproj_kernel(x_ref, mod_ref, g_ref, w_ref, wg_ref, wf_ref, qg_ref, kg_ref, bf_ref, e_ref,
                   q_ref, k_ref, v_ref, qd1, kd1, vd1, qd2, kd2, vd2, qd3, kd3, vd3,
                   gate_ref, lf_ref, perm_scr, *, n_main):
    h = _mod_norm(x_ref[...], g_ref[...], mod_ref[0, 1:2, :], mod_ref[0, 0:1, :]).astype(BF16)
    tm = x_ref.shape[0]
    qk_chunk = 4 * HEAD_DIM
    q_outs = (q_ref, qd1, qd2, qd3)
    k_outs = (k_ref, kd1, kd2, kd3)
    v_outs = (v_ref, vd1, vd2, vd3)

    def put(outs, col, val):
        if col < n_main:
            outs[0][:, col:col + qk_chunk] = val.astype(BF16)
            return
        g = (col - n_main) // qk_chunk
        dil = DILATED_PAIRS[g][1]
        if dil == 1:
            outs[1 + g][0, 0] = val.astype(BF16)
            return
        halves = qk_chunk // LANES
        for c in range(halves):
            perm_scr[c] = val[:, c * LANES:(c + 1) * LANES]
        for r in range(dil):
            outs[1 + g][0, r] = jnp.concatenate(
                [perm_scr[c, pl.ds(r, tm // dil, stride=dil), :] for c in range(halves)],
                axis=1).astype(BF16)

    def proj(wt_ref, col0, width):
        return lax.dot_general(h, wt_ref[col0:col0 + width, :], _NT, preferred_element_type=F32)

    wide = MIX_WIDTH // 2
    for half in range(2):
        yq = proj(w_ref, half * wide, wide)
        yk = proj(w_ref, MIX_WIDTH + half * wide, wide)
        yv = proj(w_ref, 2 * MIX_WIDTH + half * wide, wide)
        for c in range(wide // qk_chunk):
            col = half * wide + c * qk_chunk
            yq_c = yq[:, c * qk_chunk:(c + 1) * qk_chunk]
            yk_c = yk[:, c * qk_chunk:(c + 1) * qk_chunk]
            sq = jnp.concatenate([yq_c * yq_c, yk_c * yk_c], axis=0).astype(BF16)
            ss = jnp.dot(sq, e_ref[...], preferred_element_type=F32)
            r = lax.rsqrt(ss * (1.0 / HEAD_DIM) + RMS_EPS)
            put(q_outs, col, (yq_c * r[0:tm] * qg_ref[:, col:col + qk_chunk]) * HEAD_DIM ** -0.5)
            put(k_outs, col, yk_c * r[tm:2 * tm] * kg_ref[:, col:col + qk_chunk])
            put(v_outs, col, yv[:, c * qk_chunk:(c + 1) * qk_chunk])

    gchunk = 1024
    for c in range(wg_ref.shape[0] // gchunk):
        y = proj(wg_ref, c * gchunk, gchunk)
        gate_ref[:, c * gchunk:(c + 1) * gchunk] = _sigmoid(y).astype(BF16)

    f = proj(wf_ref, 0, LANES) + bf_ref[...]
    lf_ref[...] = jnp.minimum(f, 0.0) - jnp.log(1.0 + jnp.exp(-jnp.abs(f)))


def _inproj(x2, mod_l, g_norm, w_qkv, w_gate, w_f, q_gain, k_gain, b_f, seq, tm):
    t, d = x2.shape
    n_main = (N_HEADS_FOX + N_HEADS_MOBA) * HEAD_DIM
    dil_w = N_SLOTS_DIL * HEAD_DIM
    e = (np.arange(dil_w)[:, None] // HEAD_DIM == np.arange(dil_w)[None, :] // HEAD_DIM)
    e = jnp.asarray(e, BF16)
    per_b = seq // tm
    row = lambda i: (i, 0)
    fix = lambda i: (0, 0)
    qkv_shapes = [jax.ShapeDtypeStruct((t, n_main), BF16)] * 3
    qkv_specs = [pl.BlockSpec((tm, n_main), row)] * 3
    for _, dil in DILATED_PAIRS:
        assert tm % (16 * dil) == 0
        qkv_shapes += [jax.ShapeDtypeStruct((t // seq, dil, seq // dil, dil_w), BF16)] * 3
        qkv_specs += [pl.BlockSpec((1, dil, tm // dil, dil_w),
                                   lambda i: (i // per_b, 0, i % per_b, 0))] * 3
    out_shapes = qkv_shapes + [jax.ShapeDtypeStruct((t, N_BRANCH * d), BF16),
                               jax.ShapeDtypeStruct((t, LANES), F32)]
    out_specs = qkv_specs + [pl.BlockSpec((tm, N_BRANCH * d), row), pl.BlockSpec((tm, LANES), row)]
    return pl.pallas_call(
        functools.partial(_inproj_kernel, n_main=n_main),
        grid=(t // tm,),
        in_specs=[pl.BlockSpec((tm, d), row),
                  pl.BlockSpec((1, 6, d), lambda i: (i // per_b, 0, 0)),
                  pl.BlockSpec((1, d), fix),
                  pl.BlockSpec(w_qkv.shape, fix, pipeline_mode=pl.Buffered(1)),
                  pl.BlockSpec(w_gate.shape, fix, pipeline_mode=pl.Buffered(1)),
                  pl.BlockSpec(w_f.shape, fix),
                  pl.BlockSpec((1, MIX_WIDTH), fix),
                  pl.BlockSpec((1, MIX_WIDTH), fix),
                  pl.BlockSpec((1, LANES), fix),
                  pl.BlockSpec(e.shape, fix)],
        out_specs=out_specs,
        out_shape=out_shapes,
        scratch_shapes=[pltpu.VMEM((dil_w // LANES, tm, LANES), F32)],
        compiler_params=_params("arbitrary"),
        name="inproj",
    )(x2, mod_l, g_norm, w_qkv, w_gate, w_f, q_gain, k_gain, b_f, e)


def _decay_kernel(lf_ref, tri_ref, o_ref, *, blk):
    carry = jnp.zeros((1, LANES), F32)
    for j in range(lf_ref.shape[1] // blk):
        c = jnp.dot(tri_ref[...], lf_ref[0, j * blk:(j + 1) * blk, :],
                    precision=lax.Precision.HIGHEST, preferred_element_type=F32) + carry
        o_ref[0, j * blk:(j + 1) * blk, :] = c
        carry = c[blk - 1:blk, :]


def _decay(lf, blk=256):
    b, seq, _ = lf.shape
    tri = jnp.asarray(np.arange(blk)[:, None] >= np.arange(blk)[None, :], F32)
    return pl.pallas_call(
        functools.partial(_decay_kernel, blk=blk),
        grid=(b,),
        in_specs=[pl.BlockSpec((1, seq, LANES), lambda i: (i, 0, 0)),
                  pl.BlockSpec((blk, blk), lambda i: (0, 0))],
        out_specs=pl.BlockSpec((1, seq, LANES), lambda i: (i, 0, 0)),
        out_shape=jax.ShapeDtypeStruct((b, seq, LANES), F32),
        compiler_params=_params("arbitrary"),
        name="fox_decay",
    )(lf, tri)


def _head_masks():
    lane = lax.broadcasted_iota(jnp.int32, (1, PAIR), 1)
    return lane < HEAD_DIM, lane >= HEAD_DIM


def _split_heads(q):
    return [jnp.where(m, q, jnp.zeros_like(q)) for m in _head_masks()]


def _feat_base(hh):
    return HEAD_DIM * (1 - hh)


def _three_bf16(x):
    hi = x.astype(BF16).astype(F32)
    mid = (x - hi).astype(BF16).astype(F32)
    lo = (x - hi - mid).astype(BF16).astype(F32)
    return hi, mid, lo


def _place(idx, base, parts):
    out = jnp.zeros(jnp.broadcast_shapes(idx.shape, jnp.shape(parts[0])), F32)
    for r, part in enumerate(parts):
        out = jnp.where(idx == base + r, part, out)
    return out


def _flash_pair(i, qx, kx_scr, v_ref, o_ref, keep_own, own_first, m_scr, acc_scr, p_scr, a_scr, tq):
    _reset(m_scr, acc_scr)

    def scores(j, own, slot):
        for hh in range(2):
            s = lax.dot_general(qx[hh], kx_scr[hh, j * tq:(j + 1) * tq, :], _NT,
                                preferred_element_type=F32)
            if own:
                s = jnp.where(keep_own, s, NEG_INF)
            _softmax_stage(s, m_scr.at[hh], p_scr.at[slot, hh], a_scr.at[slot, hh])

    def values(j, slot):
        vs = _with_ones(v_ref[0, j * tq:(j + 1) * tq, :])
        for hh in range(2):
            _value_stage(vs[hh], acc_scr.at[hh], p_scr.at[slot, hh], a_scr.at[slot, hh])

    for c in range(v_ref.shape[1] // tq):
        @pl.when(i == c)
        def _(c=c):
            order = [c] + list(range(c)) if own_first else list(range(c + 1))
            scores(order[0], order[0] == c, 0)
            for n in range(1, len(order)):
                scores(order[n], order[n] == c, n % 2)
                values(order[n - 1], (n - 1) % 2)
            values(order[-1], (len(order) - 1) % 2)

    o_ref[0] = _merge_pair(acc_scr)


def _with_ones(v):
    return [jnp.where(m, v, jnp.ones_like(v)) for m in _head_masks()]


def _softmax_stage(s, m_ref, p_ref, a_ref):
    m_prev = m_ref[...]
    m_new = jnp.maximum(m_prev, jnp.max(s, axis=-1, keepdims=True))
    p = jnp.exp(s - jnp.concatenate([m_new] * (s.shape[1] // LANES), axis=1))
    m_ref[...] = m_new
    p_ref[...] = p.astype(BF16)
    a_ref[...] = jnp.exp(m_prev - m_new)


def _value_stage(v_ones, acc_ref, p_ref, a_ref):
    acc_ref[...] = a_ref[...] * acc_ref[...] + jnp.dot(p_ref[...], v_ones, preferred_element_type=F32)


def _reset(m_scr, acc_scr):
    m_scr[...] = jnp.full(m_scr.shape, NEG_INF, F32)
    acc_scr[...] = jnp.zeros(acc_scr.shape, F32)


def _merge_pair(acc_scr):
    first = _head_masks()[0]
    a0, a1 = acc_scr[0], acc_scr[1]
    return jnp.where(first, a0 / a0[:, HEAD_DIM:HEAD_DIM + 1], a1 / a1[:, 0:1]).astype(BF16)


def _stat_scratch(tq):
    return [pltpu.VMEM((2, tq, LANES), F32), pltpu.VMEM((2, tq, PAIR), F32),
            pltpu.VMEM((2, 2, tq, tq), BF16), pltpu.VMEM((2, 2, tq, LANES), F32)]


def _fox_kernel(q_ref, k_ref, v_ref, cum_ref, o_ref, kx_scr, m_scr, acc_scr, p_scr, a_scr, *, tq):
    pair, i = pl.program_id(1), pl.program_id(2)
    masks = _head_masks()
    lane = lax.broadcasted_iota(jnp.int32, (1, PAIR), 1)

    @pl.when(i == 0)
    def _():
        k = k_ref[0]
        cum = cum_ref[0]
        lane_s = lax.broadcasted_iota(jnp.int32, cum.shape, 1)
        for hh in range(2):
            f_s = jnp.sum(jnp.where(lane_s == 2 * pair + hh, cum, 0.0), axis=-1, keepdims=True)
            feats = _place(lane, _feat_base(hh), _three_bf16(-f_s))
            kx_scr[hh] = jnp.where(masks[hh], k, feats.astype(BF16))

    q = q_ref[0]
    qx = []
    for hh in range(2):
        ones = (lane >= _feat_base(hh)) & (lane < _feat_base(hh) + 3)
        qx.append(jnp.where(masks[hh], q, jnp.where(ones, 1.0, 0.0).astype(BF16)))
    causal = (lax.broadcasted_iota(jnp.int32, (tq, tq), 1)
              <= lax.broadcasted_iota(jnp.int32, (tq, tq), 0))
    _flash_pair(i, qx, kx_scr, v_ref, o_ref, causal, False, m_scr, acc_scr, p_scr, a_scr, tq)


def _fox(q, k, v, cum, col0, n_pairs, tq):
    b, seq, _ = q.shape
    return pl.pallas_call(
        functools.partial(_fox_kernel, tq=tq),
        grid=(b, n_pairs, seq // tq),
        in_specs=[pl.BlockSpec((1, tq, PAIR), lambda bi, p, i: (bi, i, col0 + p)),
                  pl.BlockSpec((1, seq, PAIR), lambda bi, p, i: (bi, 0, col0 + p)),
                  pl.BlockSpec((1, seq, PAIR), lambda bi, p, i: (bi, 0, col0 + p)),
                  pl.BlockSpec((1, seq, LANES), lambda bi, p, i: (bi, 0, 0))],
        out_specs=pl.BlockSpec((1, tq, PAIR), lambda bi, p, i: (bi, i, p)),
        out_shape=jax.ShapeDtypeStruct((b, seq, n_pairs * PAIR), BF16),
        scratch_shapes=[pltpu.VMEM((2, seq, PAIR), BF16)] + _stat_scratch(tq),
        compiler_params=_params("arbitrary", "arbitrary", "arbitrary"),
        name="fox_attn",
    )(q, k, v, cum)


MOBA_NBLK = 8
ALIBI_PARTS = 5


def _moba_feature_tables(slopes, seq):
    pos = np.arange(seq, dtype=np.float64)

    def parts(x):
        out, rest = [], x.copy()
        for _ in range(ALIBI_PARTS):
            p = rest.astype(np.float32).astype(jnp.bfloat16).astype(np.float64)
            out.append(p)
            rest = rest - p
        assert not rest.any()
        return out

    n_heads = len(slopes)
    qf = np.zeros((n_heads // 2, 2, seq, PAIR), np.float32)
    kf = np.zeros((n_heads // 2, 2, seq, PAIR), np.float32)
    for h, slope in enumerate(slopes):
        base = _feat_base(h % 2)
        q_t, k_t = qf[h // 2, h % 2], kf[h // 2, h % 2]
        k_t[np.arange(seq), base + np.arange(seq) // MOBA_BLOCK] = 1.0
        lo, mid, hi = base + MOBA_NBLK, base + MOBA_NBLK + ALIBI_PARTS, base + MOBA_NBLK + 2 * ALIBI_PARTS
        q_t[:, lo:mid] = 1.0
        k_t[:, lo:mid] = np.stack(parts(np.float64(slope) * pos), axis=1)
        q_t[:, mid:hi] = np.stack(parts(-np.float64(slope) * pos), axis=1)
        k_t[:, mid:hi] = 1.0
    return jnp.asarray(qf, BF16), jnp.asarray(kf, BF16)


def _moba_kernel(q_ref, k_ref, v_ref, qf_ref, kf_ref, o_ref, km_scr, kx_scr, m_scr, acc_scr, p_scr,
                 a_scr):
    blk = MOBA_BLOCK
    nbp = MOBA_NBLK
    i = pl.program_id(2)
    n_blk = k_ref.shape[1] // blk
    tq = q_ref.shape[1]
    per = tq // blk
    masks = _head_masks()

    @pl.when(i == 0)
    def _():
        km_scr[...] = jnp.zeros(km_scr.shape, F32)
        for n in range(n_blk):
            kb = k_ref[0, n * blk:(n + 1) * blk, :].astype(F32)
            km_scr[n:n + 1, :] = jnp.sum(kb, axis=0, keepdims=True) * (1.0 / blk)
        for hh in range(2):
            kx_scr[hh] = jnp.where(masks[hh], k_ref[0], kf_ref[0, hh])

    q = q_ref[0]
    qs = _split_heads(q)
    blk_n = lax.broadcasted_iota(jnp.int32, (nbp, tq), 0)
    q_blk = i * per + lax.broadcasted_iota(jnp.int32, (nbp, tq), 1) // blk
    lane = lax.broadcasted_iota(jnp.int32, (1, PAIR), 1)
    qx = []
    for hh in range(2):
        base = _feat_base(hh)
        g = lax.dot_general(km_scr[0:nbp, :], qs[hh].astype(F32), _NT,
                            precision=lax.Precision.HIGHEST, preferred_element_type=F32)
        cnt = jnp.zeros((nbp, tq), F32)
        for m in range(n_blk):
            gm = g[m:m + 1, :]
            beats = ((gm > g) | ((gm == g) & (blk_n > m))) & (q_blk > m)
            cnt = cnt + jnp.where(beats, 1.0, 0.0)
        visible = ((blk_n < q_blk) & (cnt < float(MOBA_TOPK))) | (blk_n == q_blk)
        hide = jnp.where(visible, 0.0, NEG_INF)
        pads = [jnp.zeros((n, tq), F32) for n in (base, LANES - nbp - base)]
        hide = jnp.concatenate([a for a in (pads[0], hide, pads[1]) if a.shape[0]], axis=0)
        feats = jnp.where((lane >= base) & (lane < base + nbp), hide.T.astype(BF16), qf_ref[0, hh])
        qx.append(jnp.where(masks[hh], q, feats))

    r = lax.broadcasted_iota(jnp.int32, (tq, tq), 0)
    c = lax.broadcasted_iota(jnp.int32, (tq, tq), 1)
    keep_own = (r // blk != c // blk) | (c <= r)
    _flash_pair(i, qx, kx_scr, v_ref, o_ref, keep_own, True, m_scr, acc_scr, p_scr, a_scr, tq)


def _moba(q, k, v, col0, n_pairs, tq):
    b, seq, _ = q.shape
    blk = MOBA_BLOCK
    assert seq % tq == 0 and tq % blk == 0 and seq // blk <= MOBA_NBLK
    slopes = _alibi_slopes(N_ALIBI)[MOBA_SLOPE_OFFSET:MOBA_SLOPE_OFFSET + 2 * n_pairs]
    q_feats, k_feats = _moba_feature_tables(slopes, seq)
    return pl.pallas_call(
        _moba_kernel,
        grid=(b, n_pairs, seq // tq),
        in_specs=[pl.BlockSpec((1, tq, PAIR), lambda bi, p, i: (bi, i, col0 + p)),
                  pl.BlockSpec((1, seq, PAIR), lambda bi, p, i: (bi, 0, col0 + p)),
                  pl.BlockSpec((1, seq, PAIR), lambda bi, p, i: (bi, 0, col0 + p)),
                  pl.BlockSpec((1, 2, tq, PAIR), lambda bi, p, i: (p, 0, i, 0)),
                  pl.BlockSpec((1, 2, seq, PAIR), lambda bi, p, i: (p, 0, 0, 0))],
        out_specs=pl.BlockSpec((1, tq, PAIR), lambda bi, p, i: (bi, i, p)),
        out_shape=jax.ShapeDtypeStruct((b, seq, n_pairs * PAIR), BF16),
        scratch_shapes=[pltpu.VMEM((LANES, PAIR), F32), pltpu.VMEM((2, seq, PAIR), BF16)]
        + _stat_scratch(tq),
        compiler_params=_params("arbitrary", "arbitrary", "arbitrary"),
        name="moba_attn",
    )(q, k, v, q_feats, k_feats)


def _dil_kernel(q_ref, k_ref, v_ref, o_ref, lse_ref, *, dil, slopes, n_q):
    wb = WIN_BLOCK
    seq = q_ref.shape[1]
    nb = seq // wb
    steps = (lax.broadcasted_iota(jnp.int32, (wb, 2 * wb), 0) + wb
             - lax.broadcasted_iota(jnp.int32, (wb, 2 * wb), 1))
    band_ok = (steps >= 0) & (steps <= wb)
    steps_f = (steps * dil).astype(F32)
    blk = lax.broadcasted_iota(jnp.int32, (nb, 1, 2 * wb), 0)
    key = lax.broadcasted_iota(jnp.int32, (nb, 1, 2 * wb), 2)
    no_prev = jnp.where((blk % n_q == 0) & (key < wb), -NEG_INF, 0.0)
    first = _head_masks()[0]

    def band(ref, cols):
        cur = ref[0, :, cols]
        prev = jnp.concatenate([cur[0:wb], cur[0:seq - wb]], axis=0)
        return jnp.concatenate([prev.reshape(nb, wb, PAIR), cur.reshape(nb, wb, PAIR)], axis=1)

    for pair in range(2):
        cols = slice(pair * PAIR, (pair + 1) * PAIR)
        qs = [qh.reshape(nb, wb, PAIR) for qh in _split_heads(q_ref[0, :, cols])]
        kband, vband = band(k_ref, cols), band(v_ref, cols)
        outs, lses = [], []
        for hh in range(2):
            bias = jnp.where(band_ok, slopes[2 * pair + hh] * steps_f, -NEG_INF)
            s = jnp.einsum("bqd,bkd->bqk", qs[hh], kband, preferred_element_type=F32)
            s = s - bias[None] - no_prev
            m = jnp.max(s, axis=-1, keepdims=True)
            p = jnp.exp(s - m)
            l = jnp.sum(p, axis=-1, keepdims=True)
            o = jnp.einsum("bqk,bkd->bqd", p.astype(BF16), vband, preferred_element_type=F32)
            outs.append(o / l)
            lses.append(jnp.broadcast_to(m + jnp.log(l), (nb, wb, PAIR)))
        o_ref[0, :, cols] = jnp.where(first, outs[0], outs[1]).reshape(seq, PAIR).astype(BF16)
        lse_ref[0, :, cols] = jnp.where(first, lses[0], lses[1]).reshape(seq, PAIR)


def _dilated(q, k, v, group):
    window, dil = DILATED_PAIRS[group]
    assert window // dil == WIN_BLOCK
    b, _, l_sub, w = q.shape
    seq = dil * l_sub
    so = DIL_SLOPE_OFFSETS[group]
    slopes = tuple(float(s) for s in _alibi_slopes(N_ALIBI)[so:so + N_SLOTS_DIL])
    spec = pl.BlockSpec((1, seq, w), lambda bi: (bi, 0, 0))
    flat = lambda a: a.reshape(b, seq, w)
    o, lse = pl.pallas_call(
        functools.partial(_dil_kernel, dil=dil, slopes=slopes, n_q=l_sub // WIN_BLOCK),
        grid=(b,),
        in_specs=[spec, spec, spec],
        out_specs=[spec, spec],
        out_shape=[jax.ShapeDtypeStruct((b, seq, w), BF16), jax.ShapeDtypeStruct((b, seq, w), F32)],
        compiler_params=_params("arbitrary"),
        name=f"dilated_attn_{dil}",
    )(flat(q), flat(k), flat(v))
    return o.reshape(q.shape), lse.reshape(q.shape)


def _outproj_kernel(x_ref, mod_ref, g_ref, of_ref, om_ref, o1, l1, o2, l2, o3, l3, gate_ref,
                    wf_ref, wm_ref, wd_ref, wo_ref, x1_ref, h2_ref, perm_scr):
    d = x_ref.shape[1]

    def natural(ref):
        dil, n = ref.shape[1], ref.shape[2]
        if dil == 1:
            return ref[0, 0].astype(F32)
        halves = ref.shape[3] // LANES
        for r in range(dil):
            blk = ref[0, r].astype(F32)
            for c in range(halves):
                perm_scr[c, pl.ds(r, n, stride=dil), :] = blk[:, c * LANES:(c + 1) * LANES]
        return jnp.concatenate([perm_scr[c] for c in range(halves)], axis=1)

    lses = [natural(l) for l in (l1, l2, l3)]
    lmax = jnp.maximum(jnp.maximum(lses[0], lses[1]), lses[2])
    e1, e2, e3 = [jnp.exp(l - lmax) for l in lses]
    den = e1 + e2 + e3
    o_dil = (e1 / den) * natural(o1) + (e2 / den) * natural(o2) + (e3 / den) * natural(o3)
    y = (gate_ref[:, 0:d].astype(F32) * jnp.dot(of_ref[...], wf_ref[...], preferred_element_type=F32)
         + gate_ref[:, d:2 * d].astype(F32) * jnp.dot(om_ref[...], wm_ref[...], preferred_element_type=F32)
         + gate_ref[:, 2 * d:3 * d].astype(F32)
         * jnp.dot(o_dil.astype(BF16), wd_ref[...], preferred_element_type=F32))
    out = jnp.dot(y.astype(BF16), wo_ref[...], preferred_element_type=F32)
    x1 = x_ref[...] + mod_ref[0, 2:3, :] * out
    x1_ref[...] = x1
    h2_ref[...] = _mod_norm(x1, g_ref[...], mod_ref[0, 4:5, :], mod_ref[0, 3:4, :]).astype(BF16)


def _outproj(x2, mod_l, g_norm, o_fox, o_moba, dil_outs, gates, w_f, w_m, w_d, w_o, seq, tm):
    t, d = x2.shape
    per_b = seq // tm
    row = lambda i: (i, 0)
    fix = lambda i: (0, 0)
    dil_args, dil_specs = [], []
    for o, lse in dil_outs:
        _, dil, _, w = o.shape
        dil_args += [o, lse]
        dil_specs += [pl.BlockSpec((1, dil, tm // dil, w), lambda i: (i // per_b, 0, i % per_b, 0))] * 2
    return pl.pallas_call(
        _outproj_kernel,
        grid=(t // tm,),
        in_specs=[pl.BlockSpec((tm, d), row),
                  pl.BlockSpec((1, 6, d), lambda i: (i // per_b, 0, 0)),
                  pl.BlockSpec((1, d), fix),
                  pl.BlockSpec((tm, o_fox.shape[1]), row),
                  pl.BlockSpec((tm, o_moba.shape[1]), row)] + dil_specs + [
                  pl.BlockSpec((tm, N_BRANCH * d), row),
                  pl.BlockSpec(w_f.shape, fix), pl.BlockSpec(w_m.shape, fix),
                  pl.BlockSpec(w_d.shape, fix), pl.BlockSpec(w_o.shape, fix)],
        out_specs=[pl.BlockSpec((tm, d), row), pl.BlockSpec((tm, d), row)],
        out_shape=[jax.ShapeDtypeStruct((t, d), F32), jax.ShapeDtypeStruct((t, d), BF16)],
        scratch_shapes=[pltpu.VMEM((N_SLOTS_DIL * HEAD_DIM // LANES, tm, LANES), F32)],
        compiler_params=_params("arbitrary"),
        name="outproj",
    )(x2, mod_l, g_norm, o_fox, o_moba, *dil_args, gates, w_f, w_m, w_d, w_o)


def _router_kernel(h_ref, w_ref, b_ref, o_ref, cnt_ref, *, n_exp):
    logits = jnp.dot(h_ref[...], w_ref[...], preferred_element_type=F32) + b_ref[...]
    lane = lax.broadcasted_iota(jnp.int32, logits.shape, 1)
    logits = jnp.where(lane < n_exp, logits, -jnp.inf)

    def top(vals):
        m = jnp.max(vals, axis=-1, keepdims=True)
        idx = jnp.min(jnp.where(vals == m, lane, LANES), axis=-1, keepdims=True)
        return m, lane == idx

    m1, hot1 = top(logits)
    m2, hot2 = top(jnp.where(hot1, -jnp.inf, logits))
    e2 = jnp.exp(m2 - m1)
    cw = jnp.where(hot1, 1.0 / (1.0 + e2), 0.0) + jnp.where(hot2, e2 / (1.0 + e2), 0.0)
    o_ref[...] = cw
    n_tok = jnp.sum(jnp.where(cw > 0.0, 1.0, 0.0), axis=0, keepdims=True)
    cnt_ref[...] = jnp.broadcast_to(n_tok, cnt_ref.shape).astype(jnp.int32)


def _router(h2, w_r, b_r, n_exp, tm):
    t, d = h2.shape
    cw, cnt = pl.pallas_call(
        functools.partial(_router_kernel, n_exp=n_exp),
        grid=(t // tm,),
        in_specs=[pl.BlockSpec((tm, d), lambda i: (i, 0)),
                  pl.BlockSpec((d, LANES), lambda i: (0, 0)),
                  pl.BlockSpec((1, LANES), lambda i: (0, 0))],
        out_specs=[pl.BlockSpec((tm, LANES), lambda i: (i, 0)),
                   pl.BlockSpec((8, LANES), lambda i: (i, 0))],
        out_shape=[jax.ShapeDtypeStruct((t, LANES), F32),
                   jax.ShapeDtypeStruct((t // tm * 8, LANES), jnp.int32)],
        compiler_params=_params("arbitrary"),
        name="router",
    )(h2, w_r, b_r)
    return cw, cnt.reshape(t // tm, 8, LANES)[:, 0, :n_exp]


MOE_TILE = 512
MOE_ROWS = 512
MOE_PUT = 128
MOE_GET = 256
MOE_ALIGN = 16


def _moe_layout(cnt, n_rows_static):
    n_tiles, n_exp = cnt.shape
    padded = (cnt + MOE_ALIGN - 1) // MOE_ALIGN * MOE_ALIGN
    length = jnp.sum(padded, axis=0)
    span = (length + (MOE_PUT - MOE_ALIGN) + MOE_ROWS - 1) // MOE_ROWS * MOE_ROWS
    start = jnp.cumsum(span) - span
    off = start[None, :] + jnp.cumsum(padded, axis=0) - padded
    n_steps = n_rows_static // MOE_ROWS
    first = start // MOE_ROWS
    step = jnp.arange(n_steps, dtype=jnp.int32)
    expert = jnp.sum((first[None, :] <= step[:, None]).astype(jnp.int32), axis=1) - 1
    active = step < (first + (length + MOE_ROWS - 1) // MOE_ROWS)[expert]
    return (off.reshape(-1).astype(jnp.int32), cnt.reshape(-1).astype(jnp.int32),
            expert.astype(jnp.int32), active.astype(jnp.int32))


def _dispatch_kernel(off_ref, cnt_ref, h_ref, cw_ref, u_ref, xs_in, xs_ref, stage, sems, *, n_exp):
    del xs_in
    i = pl.program_id(0)
    n_put = MOE_TILE // MOE_PUT
    routed = jnp.where(cw_ref[...].T[0:8] > 0.0, 1.0, 0.0)
    pos = jnp.dot(routed.astype(BF16), u_ref[...], preferred_element_type=F32)
    row = lax.broadcasted_iota(jnp.int32, (MOE_PUT, MOE_TILE), 0).astype(F32)
    h = h_ref[...]

    def copy(e, s):
        slot = e * n_put + s
        dst = pl.multiple_of(off_ref[i * n_exp + e] + s * MOE_PUT, MOE_ALIGN)
        return pltpu.make_async_copy(stage.at[slot], xs_ref.at[pl.ds(dst, MOE_PUT), :], sems.at[slot])

    for e in range(n_exp):
        for s in range(n_put):
            @pl.when(s * MOE_PUT < cnt_ref[i * n_exp + e])
            def _():
                take = (routed[e:e + 1, :] > 0.5) & (pos[e:e + 1, :] == row + float(s * MOE_PUT))
                onehot = jnp.where(take, 1.0, 0.0).astype(BF16)
                stage[e * n_put + s] = jnp.dot(onehot, h, preferred_element_type=F32).astype(BF16)
                copy(e, s).start()

    for e in range(n_exp):
        for s in range(n_put):
            @pl.when(s * MOE_PUT < cnt_ref[i * n_exp + e])
            def _():
                copy(e, s).wait()


def _dispatch(h2, cw, off, cnt, n_exp, n_rows):
    t, d = h2.shape
    u = jnp.asarray(np.arange(MOE_TILE)[:, None] < np.arange(MOE_TILE)[None, :], BF16)
    n_slots = n_exp * (MOE_TILE // MOE_PUT)
    return pl.pallas_call(
        functools.partial(_dispatch_kernel, n_exp=n_exp),
        grid_spec=pltpu.PrefetchScalarGridSpec(
            num_scalar_prefetch=2,
            grid=(t // MOE_TILE,),
            in_specs=[pl.BlockSpec((MOE_TILE, d), lambda i, o, c: (i, 0)),
                      pl.BlockSpec((MOE_TILE, LANES), lambda i, o, c: (i, 0)),
                      pl.BlockSpec((MOE_TILE, MOE_TILE), lambda i, o, c: (0, 0)),
                      pl.BlockSpec(memory_space=pl.ANY)],
            out_specs=pl.BlockSpec(memory_space=pl.ANY),
            scratch_shapes=[pltpu.VMEM((n_slots, MOE_PUT, d), BF16),
                            pltpu.SemaphoreType.DMA((n_slots,))]),
        out_shape=jax.ShapeDtypeStruct((n_rows, d), BF16),
        input_output_aliases={5: 0},
        compiler_params=_params("arbitrary"),
        name="moe_dispatch",
    )(off, cnt, h2, cw, u, jnp.zeros((n_rows, d), BF16))


FF_CHUNK = 768


def _swiglu(x, wg_ref, wu_ref, wd_ref):
    ff = wg_ref.shape[2]
    y = None
    for c0 in range(0, ff, FF_CHUNK):
        c1 = min(c0 + FF_CHUNK, ff)
        gate = jnp.dot(x, wg_ref[0, :, c0:c1], preferred_element_type=F32)
        up = jnp.dot(x, wu_ref[0, :, c0:c1], preferred_element_type=F32)
        a = ((gate * _sigmoid(gate)) * up).astype(BF16)
        part = jnp.dot(a, wd_ref[0, c0:c1, :], preferred_element_type=F32)
        y = part if y is None else y + part
    return y


def _experts_kernel(exp_ref, act_ref, xs_ref, wg_ref, wu_ref, wd_ref, ys_ref):
    g = pl.program_id(0)

    @pl.when(act_ref[g] > 0)
    def _():
        ys_ref[...] = _swiglu(xs_ref[...], wg_ref, wu_ref, wd_ref).astype(BF16)

    @pl.when(act_ref[g] == 0)
    def _():
        ys_ref[...] = jnp.zeros(ys_ref.shape, BF16)


def _experts(xs, expert, active, w_g, w_u, w_d):
    n_rows, d = xs.shape
    _, _, ff = w_g.shape
    wmap = lambda g, ex, ac: (ex[g], 0, 0)
    return pl.pallas_call(
        _experts_kernel,
        grid_spec=pltpu.PrefetchScalarGridSpec(
            num_scalar_prefetch=2,
            grid=(n_rows // MOE_ROWS,),
            in_specs=[pl.BlockSpec((MOE_ROWS, d), lambda g, ex, ac: (g, 0)),
                      pl.BlockSpec((1, d, ff), wmap), pl.BlockSpec((1, d, ff), wmap),
                      pl.BlockSpec((1, ff, d), wmap)],
            out_specs=pl.BlockSpec((MOE_ROWS, d), lambda g, ex, ac: (g, 0))),
        out_shape=jax.ShapeDtypeStruct((n_rows, d), BF16),
        compiler_params=_params("arbitrary"),
        name="moe_experts",
    )(expert, active, xs, w_g, w_u, w_d)


def _combine_kernel(off_ref, cnt_ref, x_ref, mod_ref, cw_ref, l_ref, ys_ref, o_ref, buf, sems, acc_scr,
                    *, n_exp):
    i = pl.program_id(0)
    n_get = MOE_TILE // MOE_GET

    def copy(e, s):
        slot = e * n_get + s
        src = pl.multiple_of(off_ref[i * n_exp + e] + s * MOE_GET, MOE_ALIGN)
        return pltpu.make_async_copy(ys_ref.at[pl.ds(src, MOE_GET), :], buf.at[slot], sems.at[slot])

    for e in range(n_exp):
        for s in range(n_get):
            @pl.when(s * MOE_GET < cnt_ref[i * n_exp + e])
            def _():
                copy(e, s).start()

    cw = cw_ref[...]
    routed = jnp.where(cw > 0.0, 1.0, 0.0)
    pos = jnp.dot(l_ref[...], routed.astype(BF16), preferred_element_type=F32)
    col = lax.broadcasted_iota(jnp.int32, (MOE_TILE, MOE_GET), 1).astype(F32)
    acc_scr[...] = jnp.zeros(acc_scr.shape, F32)
    for e in range(n_exp):
        for s in range(n_get):
            @pl.when(s * MOE_GET < cnt_ref[i * n_exp + e])
            def _():
                copy(e, s).wait()
                take = (cw[:, e:e + 1] > 0.0) & (pos[:, e:e + 1] == col + float(s * MOE_GET))
                onehot = jnp.where(take, 1.0, 0.0).astype(BF16)
                acc_scr[...] += cw[:, e:e + 1] * jnp.dot(onehot, buf[e * n_get + s],
                                                         preferred_element_type=F32)
    o_ref[...] = x_ref[...] + mod_ref[0, 5:6, :] * acc_scr[...]


def _combine(x1, mod_l, cw, ys, off, cnt, n_exp, seq):
    t, d = x1.shape
    per_b = seq // MOE_TILE
    low = jnp.asarray(np.arange(MOE_TILE)[:, None] > np.arange(MOE_TILE)[None, :], BF16)
    n_slots = n_exp * (MOE_TILE // MOE_GET)
    return pl.pallas_call(
        functools.partial(_combine_kernel, n_exp=n_exp),
        grid_spec=pltpu.PrefetchScalarGridSpec(
            num_scalar_prefetch=2,
            grid=(t // MOE_TILE,),
            in_specs=[pl.BlockSpec((MOE_TILE, d), lambda i, o, c: (i, 0)),
                      pl.BlockSpec((1, 6, d), lambda i, o, c: (i // per_b, 0, 0)),
                      pl.BlockSpec((MOE_TILE, LANES), lambda i, o, c: (i, 0)),
                      pl.BlockSpec((MOE_TILE, MOE_TILE), lambda i, o, c: (0, 0)),
                      pl.BlockSpec(memory_space=pl.ANY)],
            out_specs=pl.BlockSpec((MOE_TILE, d), lambda i, o, c: (i, 0)),
            scratch_shapes=[pltpu.VMEM((n_slots, MOE_GET, d), BF16),
                            pltpu.SemaphoreType.DMA((n_slots,)),
                            pltpu.VMEM((MOE_TILE, d), F32)]),
        out_shape=jax.ShapeDtypeStruct((t, d), F32),
        compiler_params=_params("arbitrary"),
        name="moe_combine",
    )(off, cnt, x1, mod_l, cw, low, ys)


def _moe(x1, h2, mod_l, w_r, b_r, w_g, w_u, w_d, seq):
    t, d = x1.shape
    n_exp = w_g.shape[0]
    n_tiles = t // MOE_TILE
    bound = (TOP_K * t + n_tiles * n_exp * (MOE_ALIGN - 1)
             + n_exp * (MOE_PUT - MOE_ALIGN + MOE_ROWS - 1))
    n_rows = (bound + MOE_ROWS - 1) // MOE_ROWS * MOE_ROWS + MOE_ROWS
    cw, cnt = _router(h2, w_r, b_r, n_exp, MOE_TILE)
    off, cnt, expert, active = _moe_layout(cnt, n_rows)
    xs = _dispatch(h2, cw, off, cnt, n_exp, n_rows)
    ys = _experts(xs, expert, active, w_g, w_u, w_d)
    return _combine(x1, mod_l, cw, ys, off, cnt, n_exp, seq)


def _ffn_kernel(x_ref, h_ref, mod_ref, wg_ref, wu_ref, wd_ref, o_ref):
    o_ref[...] = x_ref[...] + mod_ref[0, 5:6, :] * _swiglu(h_ref[...], wg_ref, wu_ref, wd_ref)


def _ffn(x1, h2, mod_l, w_g, w_u, w_d, seq, tm):
    t, d = x1.shape
    per_b = seq // tm
    row = lambda i: (i, 0)
    whole = lambda w: pl.BlockSpec(w.shape, lambda i: (0, 0, 0), pipeline_mode=pl.Buffered(1))
    return pl.pallas_call(
        _ffn_kernel,
        grid=(t // tm,),
        in_specs=[pl.BlockSpec((tm, d), row), pl.BlockSpec((tm, d), row),
                  pl.BlockSpec((1, 6, d), lambda i: (i // per_b, 0, 0)),
                  whole(w_g), whole(w_u), whole(w_d)],
        out_specs=pl.BlockSpec((tm, d), row),
        out_shape=jax.ShapeDtypeStruct((t, d), F32),
        compiler_params=_params("arbitrary"),
        name="swiglu",
    )(x1, h2, mod_l, w_g, w_u, w_d)


def _pad_cols(a, width):
    return jnp.pad(a, ((0, 0), (0, width - a.shape[1])))


def kernel(x, c, w_ada, b_ada, norm_mix, norm_ffn, w_in, b_fgate, q_gain, k_gain, w_br_fox, w_br_moba,
           w_br_dil, w_out, w_ffn_gate, w_ffn_up, w_ffn_down, w_router, b_router, w_exp_gate,
           w_exp_up, w_exp_down):
    b, seq, d = x.shape
    depth = w_ada.shape[0]
    t = b * seq
    tm = 512
    n_pairs_fox = N_HEADS_FOX // 2
    n_pairs_moba = N_HEADS_MOBA // 2
    tq, tk = 512, 512

    mod = _adaln(c, w_ada, b_ada).reshape(depth, b, 6, d)
    x2 = x.reshape(t, d)
    w_in_t = jnp.transpose(w_in, (2, 0, 1))
    for l in range(depth):
        wl = w_in_t[:, l, :].astype(BF16)
        f0 = 3 * MIX_WIDTH
        w_qkv, w_gate = wl[:f0], wl[f0 + N_HEADS_FOX:]
        w_f = jnp.pad(wl[f0:f0 + N_HEADS_FOX], ((0, LANES - N_HEADS_FOX), (0, 0)))
        b_f = _pad_cols(b_fgate[l].reshape(1, -1), LANES)
        outs = _inproj(x2, mod[l], norm_mix[l].reshape(1, d), w_qkv, w_gate, w_f, q_gain[l].reshape(1, -1),
                       k_gain[l].reshape(1, -1), b_f, seq, tm)
        q_m, k_m, v_m = (o.reshape(b, seq, -1) for o in outs[0:3])
        gates, lf = outs[12], outs[13]

        cum = _decay(lf.reshape(b, seq, LANES))
        o_fox = _fox(q_m, k_m, v_m, cum, 0, n_pairs_fox, tq).reshape(t, -1)
        o_moba = _moba(q_m, k_m, v_m, n_pairs_fox, n_pairs_moba, tq).reshape(t, -1)
        dil_outs = [_dilated(*outs[3 + 3 * g:6 + 3 * g], g) for g in range(len(DILATED_PAIRS))]

        x1, h2 = _outproj(x2, mod[l], norm_ffn[l].reshape(1, d), o_fox, o_moba, dil_outs, gates,
                          w_br_fox[l].astype(BF16), w_br_moba[l].astype(BF16),
                          w_br_dil[l].astype(BF16), w_out[l].astype(BF16), seq, tm)
        i = l // 2
        if l % 2 == 0:
            x2 = _ffn(x1, h2, mod[l], w_ffn_gate[i:i + 1].astype(BF16),
                      w_ffn_up[i:i + 1].astype(BF16), w_ffn_down[i:i + 1].astype(BF16), seq, 1024)
        else:
            x2 = _moe(x1, h2, mod[l], _pad_cols(w_router[i], LANES).astype(BF16),
                      _pad_cols(b_router[i].reshape(1, -1), LANES), w_exp_gate[i].astype(BF16),
                      w_exp_up[i].astype(BF16), w_exp_down[i].astype(BF16), seq)
    return x2.reshape(b, seq, d)
```

```python
import functools

import numpy as np
import jax
import jax.numpy as jnp
from jax import lax
from jax.experimental import pallas as pl
from jax.experimental.pallas import tpu as pltpu

HEAD_DIM = 64
N_HEADS_FOX = 6
N_HEADS_MOBA = 6
DILATED_PAIRS = ((128, 1), (512, 4), (2048, 16))
N_SLOTS_DIL = 4
N_HEADS_DIL = N_SLOTS_DIL * len(DILATED_PAIRS)
N_HEADS = N_HEADS_FOX + N_HEADS_MOBA + N_HEADS_DIL
MIX_WIDTH = N_HEADS * HEAD_DIM
N_BRANCH = 3
MOBA_BLOCK = 256
MOBA_TOPK = 3
WIN_BLOCK = 128
N_ALIBI = N_HEADS_MOBA + N_HEADS_DIL
DIL_SLOPE_OFFSETS = (0, N_SLOTS_DIL, 2 * N_SLOTS_DIL + N_HEADS_MOBA)
MOBA_SLOPE_OFFSET = 2 * N_SLOTS_DIL
TOP_K = 2
RMS_EPS = 1e-6
NEG_INF = -1e30

LANES = 128
PAIR = 2 * HEAD_DIM
VMEM_LIMIT = 56 * 1024 * 1024

F32 = jnp.float32
BF16 = jnp.bfloat16
_NT = (((1,), (1,)), ((), ()))


def _alibi_slopes(n):
    return (2.0 ** (-8.0 * np.arange(1, n + 1) / n)).astype(np.float32)


def _sigmoid(x):
    return 1.0 / (1.0 + jnp.exp(-x))


def _params(*sem):
    return pltpu.CompilerParams(dimension_semantics=sem, vmem_limit_bytes=VMEM_LIMIT)


def _adaln_kernel(c_ref, w_ref, b_ref, o_ref):
    c = c_ref[...]
    cond = c * _sigmoid(c)
    o_ref[0] = jnp.dot(cond, w_ref[0], precision=lax.Precision.HIGHEST,
                       preferred_element_type=F32) + b_ref[0]


def _adaln(c, w_ada, b_ada):
    depth, d, n = w_ada.shape
    b = c.shape[0]
    tn = 1536
    return pl.pallas_call(
        _adaln_kernel,
        grid=(depth, n // tn),
        in_specs=[pl.BlockSpec((b, d), lambda l, j: (0, 0)),
                  pl.BlockSpec((1, d, tn), lambda l, j: (l, 0, j)),
                  pl.BlockSpec((1, 1, tn), lambda l, j: (l, 0, j))],
        out_specs=pl.BlockSpec((1, b, tn), lambda l, j: (l, 0, j)),
        out_shape=jax.ShapeDtypeStruct((depth, b, n), F32),
        compiler_params=_params("arbitrary", "arbitrary"),
        name="adaln",
    )(c, w_ada, b_ada.reshape(depth, 1, n))


def _mod_norm(x, g, scale, shift):
    ms = jnp.mean(x * x, axis=-1, keepdims=True)
    return (x * lax.rsqrt(ms + RMS_EPS) * g) * (1.0 + scale) + shift


def _inproj_kernel(x_ref, mod_ref, g_ref, w_ref, wg_ref, wf_ref, qg_ref, kg_ref, bf_ref, e_ref,
                   q_ref, k_ref, v_ref, qd1, kd1, vd1, qd2, kd2, vd2, qd3, kd3, vd3,
                   gate_ref, lf_ref, perm_scr, *, n_main):
    h = _mod_norm(x_ref[...], g_ref[...], mod_ref[0, 1:2, :], mod_ref[0, 0:1, :]).astype(BF16)
    tm = x_ref.shape[0]
    qk_chunk = 4 * HEAD_DIM
    q_outs = (q_ref, qd1, qd2, qd3)
    k_outs = (k_ref, kd1, kd2, kd3)
    v_outs = (v_ref, vd1, vd2, vd3)

    def put(outs, col, val):
        if col < n_main:
            outs[0][:, col:col + qk_chunk] = val.astype(BF16)
            return
        g = (col - n_main) // qk_chunk
        dil = DILATED_PAIRS[g][1]
        if dil == 1:
            outs[1 + g][0, 0] = val.astype(BF16)
            return
        halves = qk_chunk // LANES
        for c in range(halves):
            perm_scr[c] = val[:, c * LANES:(c + 1) * LANES]
        for r in range(dil):
            outs[1 + g][0, r] = jnp.concatenate(
                [perm_scr[c, pl.ds(r, tm // dil, stride=dil), :] for c in range(halves)],
                axis=1).astype(BF16)

    def proj(wt_ref, col0, width):
        return lax.dot_general(h, wt_ref[col0:col0 + width, :], _NT, preferred_element_type=F32)

    wide = MIX_WIDTH // 2
    for half in range(2):
        yq = proj(w_ref, half * wide, wide)
        yk = proj(w_ref, MIX_WIDTH + half * wide, wide)
        yv = proj(w_ref, 2 * MIX_WIDTH + half * wide, wide)
        for c in range(wide // qk_chunk):
            col = half * wide + c * qk_chunk
            yq_c = yq[:, c * qk_chunk:(c + 1) * qk_chunk]
            yk_c = yk[:, c * qk_chunk:(c + 1) * qk_chunk]
            sq = jnp.concatenate([yq_c * yq_c, yk_c * yk_c], axis=0).astype(BF16)
            ss = jnp.dot(sq, e_ref[...], preferred_element_type=F32)
            r = lax.rsqrt(ss * (1.0 / HEAD_DIM) + RMS_EPS)
            put(q_outs, col, (yq_c * r[0:tm] * qg_ref[:, col:col + qk_chunk]) * HEAD_DIM ** -0.5)
            put(k_outs, col, yk_c * r[tm:2 * tm] * kg_ref[:, col:col + qk_chunk])
            put(v_outs, col, yv[:, c * qk_chunk:(c + 1) * qk_chunk])

    gchunk = 1024
    for c in range(wg_ref.shape[0] // gchunk):
        y = proj(wg_ref, c * gchunk, gchunk)
        gate_ref[:, c * gchunk:(c + 1) * gchunk] = _sigmoid(y).astype(BF16)

    f = proj(wf_ref, 0, LANES) + bf_ref[...]
    lf_ref[...] = jnp.minimum(f, 0.0) - jnp.log(1.0 + jnp.exp(-jnp.abs(f)))


def _inproj(x2, mod_l, g_norm, w_qkv, w_gate, w_f, q_gain, k_gain, b_f, seq, tm):
    t, d = x2.shape
    n_main = (N_HEADS_FOX + N_HEADS_MOBA) * HEAD_DIM
    dil_w = N_SLOTS_DIL * HEAD_DIM
    e = (np.arange(dil_w)[:, None] // HEAD_DIM == np.arange(dil_w)[None, :] // HEAD_DIM)
    e = jnp.asarray(e, BF16)
    per_b = seq // tm
    row = lambda i: (i, 0)
    fix = lambda i: (0, 0)
    qkv_shapes = [jax.ShapeDtypeStruct((t, n_main), BF16)] * 3
    qkv_specs = [pl.BlockSpec((tm, n_main), row)] * 3
    for _, dil in DILATED_PAIRS:
        assert tm % (16 * dil) == 0
        qkv_shapes += [jax.ShapeDtypeStruct((t // seq, dil, seq // dil, dil_w), BF16)] * 3
        qkv_specs += [pl.BlockSpec((1, dil, tm // dil, dil_w),
                                   lambda i: (i // per_b, 0, i % per_b, 0))] * 3
    out_shapes = qkv_shapes + [jax.ShapeDtypeStruct((t, N_BRANCH * d), BF16),
                               jax.ShapeDtypeStruct((t, LANES), F32)]
    out_specs = qkv_specs + [pl.BlockSpec((tm, N_BRANCH * d), row), pl.BlockSpec((tm, LANES), row)]
    return pl.pallas_call(
        functools.partial(_inproj_kernel, n_main=n_main),
        grid=(t // tm,),
        in_specs=[pl.BlockSpec((tm, d), row),
                  pl.BlockSpec((1, 6, d), lambda i: (i // per_b, 0, 0)),
                  pl.BlockSpec((1, d), fix),
                  pl.BlockSpec(w_qkv.shape, fix, pipeline_mode=pl.Buffered(1)),
                  pl.BlockSpec(w_gate.shape, fix, pipeline_mode=pl.Buffered(1)),
                  pl.BlockSpec(w_f.shape, fix),
                  pl.BlockSpec((1, MIX_WIDTH), fix),
                  pl.BlockSpec((1, MIX_WIDTH), fix),
                  pl.BlockSpec((1, LANES), fix),
                  pl.BlockSpec(e.shape, fix)],
        out_specs=out_specs,
        out_shape=out_shapes,
        scratch_shapes=[pltpu.VMEM((dil_w // LANES, tm, LANES), F32)],
        compiler_params=_params("arbitrary"),
        name="inproj",
    )(x2, mod_l, g_norm, w_qkv, w_gate, w_f, q_gain, k_gain, b_f, e)


def _decay_kernel(lf_ref, tri_ref, o_ref, *, blk):
    carry = jnp.zeros((1, LANES), F32)
    for j in range(lf_ref.shape[1] // blk):
        c = jnp.dot(tri_ref[...], lf_ref[0, j * blk:(j + 1) * blk, :],
                    precision=lax.Precision.HIGHEST, preferred_element_type=F32) + carry
        o_ref[0, j * blk:(j + 1) * blk, :] = c
        carry = c[blk - 1:blk, :]


def _decay(lf, blk=256):
    b, seq, _ = lf.shape
    tri = jnp.asarray(np.arange(blk)[:, None] >= np.arange(blk)[None, :], F32)
    return pl.pallas_call(
        functools.partial(_decay_kernel, blk=blk),
        grid=(b,),
        in_specs=[pl.BlockSpec((1, seq, LANES), lambda i: (i, 0, 0)),
                  pl.BlockSpec((blk, blk), lambda i: (0, 0))],
        out_specs=pl.BlockSpec((1, seq, LANES), lambda i: (i, 0, 0)),
        out_shape=jax.ShapeDtypeStruct((b, seq, LANES), F32),
        compiler_params=_params("arbitrary"),
        name="fox_decay",
    )(lf, tri)


def _head_masks():
    lane = lax.broadcasted_iota(jnp.int32, (1, PAIR), 1)
    return lane < HEAD_DIM, lane >= HEAD_DIM


def _split_heads(q):
    return [jnp.where(m, q, jnp.zeros_like(q)) for m in _head_masks()]


def _feat_base(hh):
    return HEAD_DIM * (1 - hh)


def _three_bf16(x):
    hi = x.astype(BF16).astype(F32)
    mid = (x - hi).astype(BF16).astype(F32)
    lo = (x - hi - mid).astype(BF16).astype(F32)
    return hi, mid, lo


def _place(idx, base, parts):
    out = jnp.zeros(jnp.broadcast_shapes(idx.shape, jnp.shape(parts[0])), F32)
    for r, part in enumerate(parts):
        out = jnp.where(idx == base + r, part, out)
    return out


def _flash_pair(c, qx, kx_scr, v_ref, o_ref, keep_own, own_first, m_scr, acc_scr, p_scr, a_scr, tq):
    _reset(m_scr, acc_scr)

    def scores(j, own, slot):
        for hh in range(2):
            s = lax.dot_general(qx[hh], kx_scr[hh, j * tq:(j + 1) * tq, :], _NT,
                                preferred_element_type=F32)
            if own:
                s = jnp.where(keep_own, s, NEG_INF)
            _softmax_stage(s, m_scr.at[hh], p_scr.at[slot, hh], a_scr.at[slot, hh])

    def values(j, slot):
        vs = _with_ones(v_ref[0, j * tq:(j + 1) * tq, :])
        for hh in range(2):
            _value_stage(vs[hh], acc_scr.at[hh], p_scr.at[slot, hh], a_scr.at[slot, hh])

    order = [c] + list(range(c)) if own_first else list(range(c + 1))
    scores(order[0], order[0] == c, 0)
    for n in range(1, len(order)):
        scores(order[n], order[n] == c, n % 2)
        values(order[n - 1], (n - 1) % 2)
    values(order[-1], (len(order) - 1) % 2)
    o_ref[0, c * tq:(c + 1) * tq, :] = _merge_pair(acc_scr)


def _with_ones(v):
    return [jnp.where(m, v, jnp.ones_like(v)) for m in _head_masks()]


def _softmax_stage(s, m_ref, p_ref, a_ref):
    m_prev = m_ref[...]
    m_new = jnp.maximum(m_prev, jnp.max(s, axis=-1, keepdims=True))
    p = jnp.exp(s - jnp.concatenate([m_new] * (s.shape[1] // LANES), axis=1))
    m_ref[...] = m_new
    p_ref[...] = p.astype(BF16)
    a_ref[...] = jnp.exp(m_prev - m_new)


def _value_stage(v_ones, acc_ref, p_ref, a_ref):
    acc_ref[...] = a_ref[...] * acc_ref[...] + jnp.dot(p_ref[...], v_ones, preferred_element_type=F32)


def _reset(m_scr, acc_scr):
    m_scr[...] = jnp.full(m_scr.shape, NEG_INF, F32)
    acc_scr[...] = jnp.zeros(acc_scr.shape, F32)


def _merge_pair(acc_scr):
    first = _head_masks()[0]
    a0, a1 = acc_scr[0], acc_scr[1]
    return jnp.where(first, a0 / a0[:, HEAD_DIM:HEAD_DIM + 1], a1 / a1[:, 0:1]).astype(BF16)


def _stat_scratch(tq):
    return [pltpu.VMEM((2, tq, LANES), F32), pltpu.VMEM((2, tq, PAIR), F32),
            pltpu.VMEM((2, 2, tq, tq), BF16), pltpu.VMEM((2, 2, tq, LANES), F32)]


def _fox_kernel(q_ref, k_ref, v_ref, cum_ref, o_ref, kx_scr, m_scr, acc_scr, p_scr, a_scr, *, tq):
    pair = pl.program_id(1)
    masks = _head_masks()
    lane = lax.broadcasted_iota(jnp.int32, (1, PAIR), 1)
    k = k_ref[0]
    cum = cum_ref[0]
    lane_s = lax.broadcasted_iota(jnp.int32, cum.shape, 1)
    for hh in range(2):
        f_s = jnp.sum(jnp.where(lane_s == 2 * pair + hh, cum, 0.0), axis=-1, keepdims=True)
        feats = _place(lane, _feat_base(hh), _three_bf16(-f_s))
        kx_scr[hh] = jnp.where(masks[hh], k, feats.astype(BF16))

    causal = (lax.broadcasted_iota(jnp.int32, (tq, tq), 1)
              <= lax.broadcasted_iota(jnp.int32, (tq, tq), 0))
    for c in range(q_ref.shape[1] // tq):
        q = q_ref[0, c * tq:(c + 1) * tq, :]
        qx = []
        for hh in range(2):
            ones = (lane >= _feat_base(hh)) & (lane < _feat_base(hh) + 3)
            qx.append(jnp.where(masks[hh], q, jnp.where(ones, 1.0, 0.0).astype(BF16)))
        _flash_pair(c, qx, kx_scr, v_ref, o_ref, causal, False, m_scr, acc_scr, p_scr, a_scr, tq)


def _fox(q, k, v, cum, col0, n_pairs, tq):
    b, seq, _ = q.shape
    head_cols = pl.BlockSpec((1, seq, PAIR), lambda bi, p: (bi, 0, col0 + p))
    return pl.pallas_call(
        functools.partial(_fox_kernel, tq=tq),
        grid=(b, n_pairs),
        in_specs=[head_cols, head_cols, head_cols,
                  pl.BlockSpec((1, seq, LANES), lambda bi, p: (bi, 0, 0))],
        out_specs=pl.BlockSpec((1, seq, PAIR), lambda bi, p: (bi, 0, p)),
        out_shape=jax.ShapeDtypeStruct((b, seq, n_pairs * PAIR), BF16),
        scratch_shapes=[pltpu.VMEM((2, seq, PAIR), BF16)] + _stat_scratch(tq),
        compiler_params=_params("arbitrary", "arbitrary"),
        name="fox_attn",
    )(q, k, v, cum)


MOBA_NBLK = 8
ALIBI_PARTS = 5


def _moba_feature_tables(slopes, seq):
    pos = np.arange(seq, dtype=np.float64)

    def parts(x):
        out, rest = [], x.copy()
        for _ in range(ALIBI_PARTS):
            p = rest.astype(np.float32).astype(jnp.bfloat16).astype(np.float64)
            out.append(p)
            rest = rest - p
        assert not rest.any()
        return out

    n_heads = len(slopes)
    qf = np.zeros((n_heads // 2, 2, seq, PAIR), np.float32)
    kf = np.zeros((n_heads // 2, 2, seq, PAIR), np.float32)
    for h, slope in enumerate(slopes):
        base = _feat_base(h % 2)
        q_t, k_t = qf[h // 2, h % 2], kf[h // 2, h % 2]
        k_t[np.arange(seq), base + np.arange(seq) // MOBA_BLOCK] = 1.0
        lo, mid, hi = base + MOBA_NBLK, base + MOBA_NBLK + ALIBI_PARTS, base + MOBA_NBLK + 2 * ALIBI_PARTS
        q_t[:, lo:mid] = 1.0
        k_t[:, lo:mid] = np.stack(parts(np.float64(slope) * pos), axis=1)
        q_t[:, mid:hi] = np.stack(parts(-np.float64(slope) * pos), axis=1)
        k_t[:, mid:hi] = 1.0
    return jnp.asarray(qf, BF16), jnp.asarray(kf, BF16)


def _moba_kernel(q_ref, k_ref, v_ref, qf_ref, kf_ref, o_ref, km_scr, kx_scr, m_scr, acc_scr, p_scr,
                 a_scr, *, tq):
    blk = MOBA_BLOCK
    nbp = MOBA_NBLK
    n_blk = k_ref.shape[1] // blk
    per = tq // blk
    masks = _head_masks()

    km_scr[...] = jnp.zeros(km_scr.shape, F32)
    for n in range(n_blk):
        kb = k_ref[0, n * blk:(n + 1) * blk, :].astype(F32)
        km_scr[n:n + 1, :] = jnp.sum(kb, axis=0, keepdims=True) * (1.0 / blk)
    for hh in range(2):
        kx_scr[hh] = jnp.where(masks[hh], k_ref[0], kf_ref[0, hh])

    blk_n = lax.broadcasted_iota(jnp.int32, (nbp, tq), 0)
    lane = lax.broadcasted_iota(jnp.int32, (1, PAIR), 1)
    r = lax.broadcasted_iota(jnp.int32, (tq, tq), 0)
    s = lax.broadcasted_iota(jnp.int32, (tq, tq), 1)
    keep_own = (r // blk != s // blk) | (s <= r)

    for c in range(q_ref.shape[1] // tq):
        q = q_ref[0, c * tq:(c + 1) * tq, :]
        qs = _split_heads(q)
        q_blk = c * per + lax.broadcasted_iota(jnp.int32, (nbp, tq), 1) // blk
        qx = []
        for hh in range(2):
            base = _feat_base(hh)
            g = lax.dot_general(km_scr[0:nbp, :], qs[hh].astype(F32), _NT,
                                precision=lax.Precision.HIGHEST, preferred_element_type=F32)
            cnt = jnp.zeros((nbp, tq), F32)
            for m in range(n_blk):
                gm = g[m:m + 1, :]
                beats = ((gm > g) | ((gm == g) & (blk_n > m))) & (q_blk > m)
                cnt = cnt + jnp.where(beats, 1.0, 0.0)
            visible = ((blk_n < q_blk) & (cnt < float(MOBA_TOPK))) | (blk_n == q_blk)
            hide = jnp.where(visible, 0.0, NEG_INF)
            pads = [jnp.zeros((n, tq), F32) for n in (base, LANES - nbp - base)]
            hide = jnp.concatenate([a for a in (pads[0], hide, pads[1]) if a.shape[0]], axis=0)
            feats = jnp.where((lane >= base) & (lane < base + nbp), hide.T.astype(BF16),
                              qf_ref[0, hh, c * tq:(c + 1) * tq, :])
            qx.append(jnp.where(masks[hh], q, feats))
        _flash_pair(c, qx, kx_scr, v_ref, o_ref, keep_own, True, m_scr, acc_scr, p_scr, a_scr, tq)


def _moba(q, k, v, col0, n_pairs, tq):
    b, seq, _ = q.shape
    blk = MOBA_BLOCK
    assert seq % tq == 0 and tq % blk == 0 and seq // blk <= MOBA_NBLK
    slopes = _alibi_slopes(N_ALIBI)[MOBA_SLOPE_OFFSET:MOBA_SLOPE_OFFSET + 2 * n_pairs]
    q_feats, k_feats = _moba_feature_tables(slopes, seq)
    head_cols = pl.BlockSpec((1, seq, PAIR), lambda bi, p: (bi, 0, col0 + p))
    table = pl.BlockSpec((1, 2, seq, PAIR), lambda bi, p: (p, 0, 0, 0))
    return pl.pallas_call(
        functools.partial(_moba_kernel, tq=tq),
        grid=(b, n_pairs),
        in_specs=[head_cols, head_cols, head_cols, table, table],
        out_specs=pl.BlockSpec((1, seq, PAIR), lambda bi, p: (bi, 0, p)),
        out_shape=jax.ShapeDtypeStruct((b, seq, n_pairs * PAIR), BF16),
        scratch_shapes=[pltpu.VMEM((LANES, PAIR), F32), pltpu.VMEM((2, seq, PAIR), BF16)]
        + _stat_scratch(tq),
        compiler_params=_params("arbitrary", "arbitrary"),
        name="moba_attn",
    )(q, k, v, q_feats, k_feats)


def _dil_kernel(q_ref, k_ref, v_ref, o_ref, lse_ref, *, dil, slopes, n_q):
    wb = WIN_BLOCK
    seq = q_ref.shape[1]
    nb = seq // wb
    steps = (lax.broadcasted_iota(jnp.int32, (wb, 2 * wb), 0) + wb
             - lax.broadcasted_iota(jnp.int32, (wb, 2 * wb), 1))
    band_ok = (steps >= 0) & (steps <= wb)
    steps_f = (steps * dil).astype(F32)
    blk = lax.broadcasted_iota(jnp.int32, (nb, 1, 2 * wb), 0)
    key = lax.broadcasted_iota(jnp.int32, (nb, 1, 2 * wb), 2)
    no_prev = jnp.where((blk % n_q == 0) & (key < wb), -NEG_INF, 0.0)
    first = _head_masks()[0]

    def band(ref, cols):
        cur = ref[0, :, cols]
        prev = jnp.concatenate([cur[0:wb], cur[0:seq - wb]], axis=0)
        return jnp.concatenate([prev.reshape(nb, wb, PAIR), cur.reshape(nb, wb, PAIR)], axis=1)

    for pair in range(2):
        cols = slice(pair * PAIR, (pair + 1) * PAIR)
        qs = [qh.reshape(nb, wb, PAIR) for qh in _split_heads(q_ref[0, :, cols])]
        kband, vband = band(k_ref, cols), band(v_ref, cols)
        outs, lses = [], []
        for hh in range(2):
            bias = jnp.where(band_ok, slopes[2 * pair + hh] * steps_f, -NEG_INF)
            s = jnp.einsum("bqd,bkd->bqk", qs[hh], kband, preferred_element_type=F32)
            s = s - bias[None] - no_prev
            m = jnp.max(s, axis=-1, keepdims=True)
            p = jnp.exp(s - m)
            l = jnp.sum(p, axis=-1, keepdims=True)
            o = jnp.einsum("bqk,bkd->bqd", p.astype(BF16), vband, preferred_element_type=F32)
            outs.append(o / l)
            lses.append(jnp.broadcast_to(m + jnp.log(l), (nb, wb, PAIR)))
        o_ref[0, :, cols] = jnp.where(first, outs[0], outs[1]).reshape(seq, PAIR).astype(BF16)
        lse_ref[0, :, cols] = jnp.where(first, lses[0], lses[1]).reshape(seq, PAIR)


def _dilated(q, k, v, group):
    window, dil = DILATED_PAIRS[group]
    assert window // dil == WIN_BLOCK
    b, _, l_sub, w = q.shape
    seq = dil * l_sub
    so = DIL_SLOPE_OFFSETS[group]
    slopes = tuple(float(s) for s in _alibi_slopes(N_ALIBI)[so:so + N_SLOTS_DIL])
    spec = pl.BlockSpec((1, seq, w), lambda bi: (bi, 0, 0))
    flat = lambda a: a.reshape(b, seq, w)
    o, lse = pl.pallas_call(
        functools.partial(_dil_kernel, dil=dil, slopes=slopes, n_q=l_sub // WIN_BLOCK),
        grid=(b,),
        in_specs=[spec, spec, spec],
        out_specs=[spec, spec],
        out_shape=[jax.ShapeDtypeStruct((b, seq, w), BF16), jax.ShapeDtypeStruct((b, seq, w), F32)],
        compiler_params=_params("arbitrary"),
        name=f"dilated_attn_{dil}",
    )(flat(q), flat(k), flat(v))
    return o.reshape(q.shape), lse.reshape(q.shape)


def _outproj_kernel(x_ref, mod_ref, g_ref, of_ref, om_ref, o1, l1, o2, l2, o3, l3, gate_ref,
                    wf_ref, wm_ref, wd_ref, wo_ref, x1_ref, h2_ref, perm_scr):
    d = x_ref.shape[1]

    def natural(ref):
        dil, n = ref.shape[1], ref.shape[2]
        if dil == 1:
            return ref[0, 0].astype(F32)
        halves = ref.shape[3] // LANES
        for r in range(dil):
            blk = ref[0, r].astype(F32)
            for c in range(halves):
                perm_scr[c, pl.ds(r, n, stride=dil), :] = blk[:, c * LANES:(c + 1) * LANES]
        return jnp.concatenate([perm_scr[c] for c in range(halves)], axis=1)

    lses = [natural(l) for l in (l1, l2, l3)]
    lmax = jnp.maximum(jnp.maximum(lses[0], lses[1]), lses[2])
    e1, e2, e3 = [jnp.exp(l - lmax) for l in lses]
    den = e1 + e2 + e3
    o_dil = (e1 / den) * natural(o1) + (e2 / den) * natural(o2) + (e3 / den) * natural(o3)
    y = (gate_ref[:, 0:d].astype(F32) * jnp.dot(of_ref[...], wf_ref[...], preferred_element_type=F32)
         + gate_ref[:, d:2 * d].astype(F32) * jnp.dot(om_ref[...], wm_ref[...], preferred_element_type=F32)
         + gate_ref[:, 2 * d:3 * d].astype(F32)
         * jnp.dot(o_dil.astype(BF16), wd_ref[...], preferred_element_type=F32))
    out = jnp.dot(y.astype(BF16), wo_ref[...], preferred_element_type=F32)
    x1 = x_ref[...] + mod_ref[0, 2:3, :] * out
    x1_ref[...] = x1
    h2_ref[...] = _mod_norm(x1, g_ref[...], mod_ref[0, 4:5, :], mod_ref[0, 3:4, :]).astype(BF16)


def _outproj(x2, mod_l, g_norm, o_fox, o_moba, dil_outs, gates, w_f, w_m, w_d, w_o, seq, tm):
    t, d = x2.shape
    per_b = seq // tm
    row = lambda i: (i, 0)
    fix = lambda i: (0, 0)
    dil_args, dil_specs = [], []
    for o, lse in dil_outs:
        _, dil, _, w = o.shape
        dil_args += [o, lse]
        dil_specs += [pl.BlockSpec((1, dil, tm // dil, w), lambda i: (i // per_b, 0, i % per_b, 0))] * 2
    return pl.pallas_call(
        _outproj_kernel,
        grid=(t // tm,),
        in_specs=[pl.BlockSpec((tm, d), row),
                  pl.BlockSpec((1, 6, d), lambda i: (i // per_b, 0, 0)),
                  pl.BlockSpec((1, d), fix),
                  pl.BlockSpec((tm, o_fox.shape[1]), row),
                  pl.BlockSpec((tm, o_moba.shape[1]), row)] + dil_specs + [
                  pl.BlockSpec((tm, N_BRANCH * d), row),
                  pl.BlockSpec(w_f.shape, fix), pl.BlockSpec(w_m.shape, fix),
                  pl.BlockSpec(w_d.shape, fix), pl.BlockSpec(w_o.shape, fix)],
        out_specs=[pl.BlockSpec((tm, d), row), pl.BlockSpec((tm, d), row)],
        out_shape=[jax.ShapeDtypeStruct((t, d), F32), jax.ShapeDtypeStruct((t, d), BF16)],
        scratch_shapes=[pltpu.VMEM((N_SLOTS_DIL * HEAD_DIM // LANES, tm, LANES), F32)],
        compiler_params=_params("arbitrary"),
        name="outproj",
    )(x2, mod_l, g_norm, o_fox, o_moba, *dil_args, gates, w_f, w_m, w_d, w_o)


def _router_kernel(h_ref, w_ref, b_ref, o_ref, cnt_ref, *, n_exp):
    logits = jnp.dot(h_ref[...], w_ref[...], preferred_element_type=F32) + b_ref[...]
    lane = lax.broadcasted_iota(jnp.int32, logits.shape, 1)
    logits = jnp.where(lane < n_exp, logits, -jnp.inf)

    def top(vals):
        m = jnp.max(vals, axis=-1, keepdims=True)
        idx = jnp.min(jnp.where(vals == m, lane, LANES), axis=-1, keepdims=True)
        return m, lane == idx

    m1, hot1 = top(logits)
    m2, hot2 = top(jnp.where(hot1, -jnp.inf, logits))
    e2 = jnp.exp(m2 - m1)
    cw = jnp.where(hot1, 1.0 / (1.0 + e2), 0.0) + jnp.where(hot2, e2 / (1.0 + e2), 0.0)
    o_ref[...] = cw
    n_tok = jnp.sum(jnp.where(cw > 0.0, 1.0, 0.0), axis=0, keepdims=True)
    cnt_ref[...] = jnp.broadcast_to(n_tok, cnt_ref.shape).astype(jnp.int32)


def _router(h2, w_r, b_r, n_exp, tm):
    t, d = h2.shape
    cw, cnt = pl.pallas_call(
        functools.partial(_router_kernel, n_exp=n_exp),
        grid=(t // tm,),
        in_specs=[pl.BlockSpec((tm, d), lambda i: (i, 0)),
                  pl.BlockSpec((d, LANES), lambda i: (0, 0)),
                  pl.BlockSpec((1, LANES), lambda i: (0, 0))],
        out_specs=[pl.BlockSpec((tm, LANES), lambda i: (i, 0)),
                   pl.BlockSpec((8, LANES), lambda i: (i, 0))],
        out_shape=[jax.ShapeDtypeStruct((t, LANES), F32),
                   jax.ShapeDtypeStruct((t // tm * 8, LANES), jnp.int32)],
        compiler_params=_params("arbitrary"),
        name="router",
    )(h2, w_r, b_r)
    return cw, cnt.reshape(t // tm, 8, LANES)[:, 0, :n_exp]


MOE_TILE = 512
MOE_ROWS = 512
MOE_PUT = 128
MOE_GET = 256
MOE_ALIGN = 16


def _moe_layout(cnt, n_rows_static):
    n_tiles, n_exp = cnt.shape
    padded = (cnt + MOE_ALIGN - 1) // MOE_ALIGN * MOE_ALIGN
    length = jnp.sum(padded, axis=0)
    span = (length + (MOE_PUT - MOE_ALIGN) + MOE_ROWS - 1) // MOE_ROWS * MOE_ROWS
    start = jnp.cumsum(span) - span
    off = start[None, :] + jnp.cumsum(padded, axis=0) - padded
    n_steps = n_rows_static // MOE_ROWS
    first = start // MOE_ROWS
    step = jnp.arange(n_steps, dtype=jnp.int32)
    expert = jnp.sum((first[None, :] <= step[:, None]).astype(jnp.int32), axis=1) - 1
    active = step < (first + (length + MOE_ROWS - 1) // MOE_ROWS)[expert]
    return (off.reshape(-1).astype(jnp.int32), cnt.reshape(-1).astype(jnp.int32),
            expert.astype(jnp.int32), active.astype(jnp.int32))


def _dispatch_kernel(off_ref, cnt_ref, h_ref, cw_ref, u_ref, xs_in, xs_ref, stage, sems, *, n_exp):
    del xs_in
    i = pl.program_id(0)
    n_put = MOE_TILE // MOE_PUT
    routed = jnp.where(cw_ref[...].T[0:8] > 0.0, 1.0, 0.0)
    pos = jnp.dot(routed.astype(BF16), u_ref[...], preferred_element_type=F32)
    row = lax.broadcasted_iota(jnp.int32, (MOE_PUT, MOE_TILE), 0).astype(F32)
    h = h_ref[...]

    def copy(e, s):
        slot = e * n_put + s
        dst = pl.multiple_of(off_ref[i * n_exp + e] + s * MOE_PUT, MOE_ALIGN)
        return pltpu.make_async_copy(stage.at[slot], xs_ref.at[pl.ds(dst, MOE_PUT), :], sems.at[slot])

    for e in range(n_exp):
        for s in range(n_put):
            @pl.when(s * MOE_PUT < cnt_ref[i * n_exp + e])
            def _():
                take = (routed[e:e + 1, :] > 0.5) & (pos[e:e + 1, :] == row + float(s * MOE_PUT))
                onehot = jnp.where(take, 1.0, 0.0).astype(BF16)
                stage[e * n_put + s] = jnp.dot(onehot, h, preferred_element_type=F32).astype(BF16)
                copy(e, s).start()

    for e in range(n_exp):
        for s in range(n_put):
            @pl.when(s * MOE_PUT < cnt_ref[i * n_exp + e])
            def _():
                copy(e, s).wait()


def _dispatch(h2, cw, off, cnt, n_exp, n_rows):
    t, d = h2.shape
    u = jnp.asarray(np.arange(MOE_TILE)[:, None] < np.arange(MOE_TILE)[None, :], BF16)
    n_slots = n_exp * (MOE_TILE // MOE_PUT)
    return pl.pallas_call(
        functools.partial(_dispatch_kernel, n_exp=n_exp),
        grid_spec=pltpu.PrefetchScalarGridSpec(
            num_scalar_prefetch=2,
            grid=(t // MOE_TILE,),
            in_specs=[pl.BlockSpec((MOE_TILE, d), lambda i, o, c: (i, 0)),
                      pl.BlockSpec((MOE_TILE, LANES), lambda i, o, c: (i, 0)),
                      pl.BlockSpec((MOE_TILE, MOE_TILE), lambda i, o, c: (0, 0)),
                      pl.BlockSpec(memory_space=pl.ANY)],
            out_specs=pl.BlockSpec(memory_space=pl.ANY),
            scratch_shapes=[pltpu.VMEM((n_slots, MOE_PUT, d), BF16),
                            pltpu.SemaphoreType.DMA((n_slots,))]),
        out_shape=jax.ShapeDtypeStruct((n_rows, d), BF16),
        input_output_aliases={5: 0},
        compiler_params=_params("arbitrary"),
        name="moe_dispatch",
    )(off, cnt, h2, cw, u, jnp.zeros((n_rows, d), BF16))


FF_CHUNK = 768


def _swiglu(x, wg_ref, wu_ref, wd_ref):
    ff = wg_ref.shape[2]
    y = None
    for c0 in range(0, ff, FF_CHUNK):
        c1 = min(c0 + FF_CHUNK, ff)
        gate = jnp.dot(x, wg_ref[0, :, c0:c1], preferred_element_type=F32)
        up = jnp.dot(x, wu_ref[0, :, c0:c1], preferred_element_type=F32)
        a = ((gate * _sigmoid(gate)) * up).astype(BF16)
        part = jnp.dot(a, wd_ref[0, c0:c1, :], preferred_element_type=F32)
        y = part if y is None else y + part
    return y


def _experts_kernel(exp_ref, act_ref, xs_ref, wg_ref, wu_ref, wd_ref, ys_ref):
    g = pl.program_id(0)

    @pl.when(act_ref[g] > 0)
    def _():
        ys_ref[...] = _swiglu(xs_ref[...], wg_ref, wu_ref, wd_ref).astype(BF16)

    @pl.when(act_ref[g] == 0)
    def _():
        ys_ref[...] = jnp.zeros(ys_ref.shape, BF16)


def _experts(xs, expert, active, w_g, w_u, w_d):
    n_rows, d = xs.shape
    _, _, ff = w_g.shape
    wmap = lambda g, ex, ac: (ex[g], 0, 0)
    return pl.pallas_call(
        _experts_kernel,
        grid_spec=pltpu.PrefetchScalarGridSpec(
            num_scalar_prefetch=2,
            grid=(n_rows // MOE_ROWS,),
            in_specs=[pl.BlockSpec((MOE_ROWS, d), lambda g, ex, ac: (g, 0)),
                      pl.BlockSpec((1, d, ff), wmap), pl.BlockSpec((1, d, ff), wmap),
                      pl.BlockSpec((1, ff, d), wmap)],
            out_specs=pl.BlockSpec((MOE_ROWS, d), lambda g, ex, ac: (g, 0))),
        out_shape=jax.ShapeDtypeStruct((n_rows, d), BF16),
        compiler_params=_params("arbitrary"),
        name="moe_experts",
    )(expert, active, xs, w_g, w_u, w_d)


def _combine_kernel(off_ref, cnt_ref, x_ref, mod_ref, cw_ref, l_ref, ys_ref, o_ref, buf, sems, acc_scr,
                    *, n_exp):
    i = pl.program_id(0)
    n_get = MOE_TILE // MOE_GET

    def copy(e, s):
        slot = e * n_get + s
        src = pl.multiple_of(off_ref[i * n_exp + e] + s * MOE_GET, MOE_ALIGN)
        return pltpu.make_async_copy(ys_ref.at[pl.ds(src, MOE_GET), :], buf.at[slot], sems.at[slot])

    for e in range(n_exp):
        for s in range(n_get):
            @pl.when(s * MOE_GET < cnt_ref[i * n_exp + e])
            def _():
                copy(e, s).start()

    cw = cw_ref[...]
    routed = jnp.where(cw > 0.0, 1.0, 0.0)
    pos = jnp.dot(l_ref[...], routed.astype(BF16), preferred_element_type=F32)
    col = lax.broadcasted_iota(jnp.int32, (MOE_TILE, MOE_GET), 1).astype(F32)
    acc_scr[...] = jnp.zeros(acc_scr.shape, F32)
    for e in range(n_exp):
        for s in range(n_get):
            @pl.when(s * MOE_GET < cnt_ref[i * n_exp + e])
            def _():
                copy(e, s).wait()
                take = (cw[:, e:e + 1] > 0.0) & (pos[:, e:e + 1] == col + float(s * MOE_GET))
                onehot = jnp.where(take, 1.0, 0.0).astype(BF16)
                acc_scr[...] += cw[:, e:e + 1] * jnp.dot(onehot, buf[e * n_get + s],
                                                         preferred_element_type=F32)
    o_ref[...] = x_ref[...] + mod_ref[0, 5:6, :] * acc_scr[...]


def _combine(x1, mod_l, cw, ys, off, cnt, n_exp, seq):
    t, d = x1.shape
    per_b = seq // MOE_TILE
    low = jnp.asarray(np.arange(MOE_TILE)[:, None] > np.arange(MOE_TILE)[None, :], BF16)
    n_slots = n_exp * (MOE_TILE // MOE_GET)
    return pl.pallas_call(
        functools.partial(_combine_kernel, n_exp=n_exp),
        grid_spec=pltpu.PrefetchScalarGridSpec(
            num_scalar_prefetch=2,
            grid=(t // MOE_TILE,),
            in_specs=[pl.BlockSpec((MOE_TILE, d), lambda i, o, c: (i, 0)),
                      pl.BlockSpec((1, 6, d), lambda i, o, c: (i // per_b, 0, 0)),
                      pl.BlockSpec((MOE_TILE, LANES), lambda i, o, c: (i, 0)),
                      pl.BlockSpec((MOE_TILE, MOE_TILE), lambda i, o, c: (0, 0)),
                      pl.BlockSpec(memory_space=pl.ANY)],
            out_specs=pl.BlockSpec((MOE_TILE, d), lambda i, o, c: (i, 0)),
            scratch_shapes=[pltpu.VMEM((n_slots, MOE_GET, d), BF16),
                            pltpu.SemaphoreType.DMA((n_slots,)),
                            pltpu.VMEM((MOE_TILE, d), F32)]),
        out_shape=jax.ShapeDtypeStruct((t, d), F32),
        compiler_params=_params("arbitrary"),
        name="moe_combine",
    )(off, cnt, x1, mod_l, cw, low, ys)


def _moe(x1, h2, mod_l, w_r, b_r, w_g, w_u, w_d, seq):
    t, d = x1.shape
    n_exp = w_g.shape[0]
    n_tiles = t // MOE_TILE
    bound = (TOP_K * t + n_tiles * n_exp * (MOE_ALIGN - 1)
             + n_exp * (MOE_PUT - MOE_ALIGN + MOE_ROWS - 1))
    n_rows = (bound + MOE_ROWS - 1) // MOE_ROWS * MOE_ROWS + MOE_ROWS
    cw, cnt = _router(h2, w_r, b_r, n_exp, MOE_TILE)
    off, cnt, expert, active = _moe_layout(cnt, n_rows)
    xs = _dispatch(h2, cw, off, cnt, n_exp, n_rows)
    ys = _experts(xs, expert, active, w_g, w_u, w_d)
    return _combine(x1, mod_l, cw, ys, off, cnt, n_exp, seq)


def _ffn_kernel(x_ref, h_ref, mod_ref, wg_ref, wu_ref, wd_ref, o_ref):
    o_ref[...] = x_ref[...] + mod_ref[0, 5:6, :] * _swiglu(h_ref[...], wg_ref, wu_ref, wd_ref)


def _ffn(x1, h2, mod_l, w_g, w_u, w_d, seq, tm):
    t, d = x1.shape
    per_b = seq // tm
    row = lambda i: (i, 0)
    whole = lambda w: pl.BlockSpec(w.shape, lambda i: (0, 0, 0), pipeline_mode=pl.Buffered(1))
    return pl.pallas_call(
        _ffn_kernel,
        grid=(t // tm,),
        in_specs=[pl.BlockSpec((tm, d), row), pl.BlockSpec((tm, d), row),
                  pl.BlockSpec((1, 6, d), lambda i: (i // per_b, 0, 0)),
                  whole(w_g), whole(w_u), whole(w_d)],
        out_specs=pl.BlockSpec((tm, d), row),
        out_shape=jax.ShapeDtypeStruct((t, d), F32),
        compiler_params=_params("arbitrary"),
        name="swiglu",
    )(x1, h2, mod_l, w_g, w_u, w_d)


def _pad_cols(a, width):
    return jnp.pad(a, ((0, 0), (0, width - a.shape[1])))


def kernel(x, c, w_ada, b_ada, norm_mix, norm_ffn, w_in, b_fgate, q_gain, k_gain, w_br_fox, w_br_moba,
           w_br_dil, w_out, w_ffn_gate, w_ffn_up, w_ffn_down, w_router, b_router, w_exp_gate,
           w_exp_up, w_exp_down):
    b, seq, d = x.shape
    depth = w_ada.shape[0]
    t = b * seq
    tm = 512
    n_pairs_fox = N_HEADS_FOX // 2
    n_pairs_moba = N_HEADS_MOBA // 2
    tq, tk = 512, 512

    mod = _adaln(c, w_ada, b_ada).reshape(depth, b, 6, d)
    x2 = x.reshape(t, d)
    w_in_t = jnp.transpose(w_in, (2, 0, 1))
    for l in range(depth):
        wl = w_in_t[:, l, :].astype(BF16)
        f0 = 3 * MIX_WIDTH
        w_qkv, w_gate = wl[:f0], wl[f0 + N_HEADS_FOX:]
        w_f = jnp.pad(wl[f0:f0 + N_HEADS_FOX], ((0, LANES - N_HEADS_FOX), (0, 0)))
        b_f = _pad_cols(b_fgate[l].reshape(1, -1), LANES)
        outs = _inproj(x2, mod[l], norm_mix[l].reshape(1, d), w_qkv, w_gate, w_f, q_gain[l].reshape(1, -1),
                       k_gain[l].reshape(1, -1), b_f, seq, tm)
        q_m, k_m, v_m = (o.reshape(b, seq, -1) for o in outs[0:3])
        gates, lf = outs[12], outs[13]

        cum = _decay(lf.reshape(b, seq, LANES))
        o_fox = _fox(q_m, k_m, v_m, cum, 0, n_pairs_fox, tq).reshape(t, -1)
        o_moba = _moba(q_m, k_m, v_m, n_pairs_fox, n_pairs_moba, tq).reshape(t, -1)
        dil_outs = [_dilated(*outs[3 + 3 * g:6 + 3 * g], g) for g in range(len(DILATED_PAIRS))]

        x1, h2 = _outproj(x2, mod[l], norm_ffn[l].reshape(1, d), o_fox, o_moba, dil_outs, gates,
                          w_br_fox[l].astype(BF16), w_br_moba[l].astype(BF16),
                          w_br_dil[l].astype(BF16), w_out[l].astype(BF16), seq, tm)
        i = l // 2
        if l % 2 == 0:
            x2 = _ffn(x1, h2, mod[l], w_ffn_gate[i:i + 1].astype(BF16),
                      w_ffn_up[i:i + 1].astype(BF16), w_ffn_down[i:i + 1].astype(BF16), seq, 1024)
        else:
            x2 = _moe(x1, h2, mod[l], _pad_cols(w_router[i], LANES).astype(BF16),
                      _pad_cols(b_router[i].reshape(1, -1), LANES), w_exp_gate[i].astype(BF16),
                      w_exp_up[i].astype(BF16), w_exp_down[i].astype(BF16), seq)
    return x2.reshape(b, seq, d)
```

```python
import functools

import numpy as np
import jax
import jax.numpy as jnp
from jax import lax
from jax.experimental import pallas as pl
from jax.experimental.pallas import tpu as pltpu

HEAD_DIM = 64
N_HEADS_FOX = 6
N_HEADS_MOBA = 6
DILATED_PAIRS = ((128, 1), (512, 4), (2048, 16))
N_SLOTS_DIL = 4
N_HEADS_DIL = N_SLOTS_DIL * len(DILATED_PAIRS)
N_HEADS = N_HEADS_FOX + N_HEADS_MOBA + N_HEADS_DIL
MIX_WIDTH = N_HEADS * HEAD_DIM
N_BRANCH = 3
MOBA_BLOCK = 256
MOBA_TOPK = 3
WIN_BLOCK = 128
N_ALIBI = N_HEADS_MOBA + N_HEADS_DIL
DIL_SLOPE_OFFSETS = (0, N_SLOTS_DIL, 2 * N_SLOTS_DIL + N_HEADS_MOBA)
MOBA_SLOPE_OFFSET = 2 * N_SLOTS_DIL
TOP_K = 2
RMS_EPS = 1e-6
NEG_INF = -1e30

LANES = 128
PAIR = 2 * HEAD_DIM
VMEM_LIMIT = 56 * 1024 * 1024
TOKEN_TILE = 512
ATTN_CHUNK = 512
FFN_TILE = 1024

F32 = jnp.float32
BF16 = jnp.bfloat16
_NT = (((1,), (1,)), ((), ()))


def _alibi_slopes(n):
    return (2.0 ** (-8.0 * np.arange(1, n + 1) / n)).astype(np.float32)


def _sigmoid(x):
    return 1.0 / (1.0 + jnp.exp(-x))


def _params(*sem):
    return pltpu.CompilerParams(dimension_semantics=sem, vmem_limit_bytes=VMEM_LIMIT)


def _adaln_kernel(c_ref, w_ref, b_ref, o_ref):
    c = c_ref[...]
    cond = c * _sigmoid(c)
    o_ref[0] = jnp.dot(cond, w_ref[0], precision=lax.Precision.HIGHEST,
                       preferred_element_type=F32) + b_ref[0]


def _adaln(c, w_ada, b_ada):
    depth, d, n = w_ada.shape
    b = c.shape[0]
    tn = 1536
    return pl.pallas_call(
        _adaln_kernel,
        grid=(depth, n // tn),
        in_specs=[pl.BlockSpec((b, d), lambda l, j: (0, 0)),
                  pl.BlockSpec((1, d, tn), lambda l, j: (l, 0, j)),
                  pl.BlockSpec((1, 1, tn), lambda l, j: (l, 0, j))],
        out_specs=pl.BlockSpec((1, b, tn), lambda l, j: (l, 0, j)),
        out_shape=jax.ShapeDtypeStruct((depth, b, n), F32),
        compiler_params=_params("arbitrary", "arbitrary"),
        name="adaln",
    )(c, w_ada, b_ada.reshape(depth, 1, n))


def _mod_norm(x, g, scale, shift):
    ms = jnp.mean(x * x, axis=-1, keepdims=True)
    return (x * lax.rsqrt(ms + RMS_EPS) * g) * (1.0 + scale) + shift


def _inproj_kernel(x_ref, mod_ref, g_ref, w_ref, wg_ref, wf_ref, qg_ref, kg_ref, bf_ref, e_ref,
                   q_ref, k_ref, v_ref, qd1, kd1, vd1, qd2, kd2, vd2, qd3, kd3, vd3,
                   gate_ref, lf_ref, perm_scr, *, n_main):
    h = _mod_norm(x_ref[...], g_ref[...], mod_ref[0, 1:2, :], mod_ref[0, 0:1, :]).astype(BF16)
    tm = x_ref.shape[0]
    qk_chunk = 4 * HEAD_DIM
    q_outs = (q_ref, qd1, qd2, qd3)
    k_outs = (k_ref, kd1, kd2, kd3)
    v_outs = (v_ref, vd1, vd2, vd3)

    def put(outs, col, val):
        if col < n_main:
            outs[0][:, col:col + qk_chunk] = val.astype(BF16)
            return
        g = (col - n_main) // qk_chunk
        dil = DILATED_PAIRS[g][1]
        if dil == 1:
            outs[1 + g][0, 0] = val.astype(BF16)
            return
        halves = qk_chunk // LANES
        for c in range(halves):
            perm_scr[c] = val[:, c * LANES:(c + 1) * LANES]
        for r in range(dil):
            outs[1 + g][0, r] = jnp.concatenate(
                [perm_scr[c, pl.ds(r, tm // dil, stride=dil), :] for c in range(halves)],
                axis=1).astype(BF16)

    def proj(wt_ref, col0, width):
        return lax.dot_general(h, wt_ref[col0:col0 + width, :], _NT, preferred_element_type=F32)

    wide = MIX_WIDTH // 2
    for half in range(2):
        yq = proj(w_ref, half * wide, wide)
        yk = proj(w_ref, MIX_WIDTH + half * wide, wide)
        yv = proj(w_ref, 2 * MIX_WIDTH + half * wide, wide)
        for c in range(wide // qk_chunk):
            col = half * wide + c * qk_chunk
            yq_c = yq[:, c * qk_chunk:(c + 1) * qk_chunk]
            yk_c = yk[:, c * qk_chunk:(c + 1) * qk_chunk]
            sq = jnp.concatenate([yq_c * yq_c, yk_c * yk_c], axis=0).astype(BF16)
            ss = jnp.dot(sq, e_ref[...], preferred_element_type=F32)
            r = lax.rsqrt(ss * (1.0 / HEAD_DIM) + RMS_EPS)
            put(q_outs, col, (yq_c * r[0:tm] * qg_ref[:, col:col + qk_chunk]) * HEAD_DIM ** -0.5)
            put(k_outs, col, yk_c * r[tm:2 * tm] * kg_ref[:, col:col + qk_chunk])
            put(v_outs, col, yv[:, c * qk_chunk:(c + 1) * qk_chunk])

    gchunk = 1024
    for c in range(wg_ref.shape[0] // gchunk):
        y = proj(wg_ref, c * gchunk, gchunk)
        gate_ref[:, c * gchunk:(c + 1) * gchunk] = _sigmoid(y).astype(BF16)

    f = proj(wf_ref, 0, LANES) + bf_ref[...]
    lf_ref[...] = jnp.minimum(f, 0.0) - jnp.log(1.0 + jnp.exp(-jnp.abs(f)))


def _inproj(x2, mod_l, g_norm, w_qkv, w_gate, w_f, q_gain, k_gain, b_f, seq, tm):
    t, d = x2.shape
    n_main = (N_HEADS_FOX + N_HEADS_MOBA) * HEAD_DIM
    dil_w = N_SLOTS_DIL * HEAD_DIM
    e = (np.arange(dil_w)[:, None] // HEAD_DIM == np.arange(dil_w)[None, :] // HEAD_DIM)
    e = jnp.asarray(e, BF16)
    per_b = seq // tm
    row = lambda i: (i, 0)
    fix = lambda i: (0, 0)
    qkv_shapes = [jax.ShapeDtypeStruct((t, n_main), BF16)] * 3
    qkv_specs = [pl.BlockSpec((tm, n_main), row)] * 3
    for _, dil in DILATED_PAIRS:
        assert tm % (16 * dil) == 0
        qkv_shapes += [jax.ShapeDtypeStruct((t // seq, dil, seq // dil, dil_w), BF16)] * 3
        qkv_specs += [pl.BlockSpec((1, dil, tm // dil, dil_w),
                                   lambda i: (i // per_b, 0, i % per_b, 0))] * 3
    out_shapes = qkv_shapes + [jax.ShapeDtypeStruct((t, N_BRANCH * d), BF16),
                               jax.ShapeDtypeStruct((t, LANES), F32)]
    out_specs = qkv_specs + [pl.BlockSpec((tm, N_BRANCH * d), row), pl.BlockSpec((tm, LANES), row)]
    return pl.pallas_call(
        functools.partial(_inproj_kernel, n_main=n_main),
        grid=(t // tm,),
        in_specs=[pl.BlockSpec((tm, d), row),
                  pl.BlockSpec((1, 6, d), lambda i: (i // per_b, 0, 0)),
                  pl.BlockSpec((1, d), fix),
                  pl.BlockSpec(w_qkv.shape, fix, pipeline_mode=pl.Buffered(1)),
                  pl.BlockSpec(w_gate.shape, fix, pipeline_mode=pl.Buffered(1)),
                  pl.BlockSpec(w_f.shape, fix),
                  pl.BlockSpec((1, MIX_WIDTH), fix),
                  pl.BlockSpec((1, MIX_WIDTH), fix),
                  pl.BlockSpec((1, LANES), fix),
                  pl.BlockSpec(e.shape, fix)],
        out_specs=out_specs,
        out_shape=out_shapes,
        scratch_shapes=[pltpu.VMEM((dil_w // LANES, tm, LANES), F32)],
        compiler_params=_params("arbitrary"),
        name="inproj",
    )(x2, mod_l, g_norm, w_qkv, w_gate, w_f, q_gain, k_gain, b_f, e)


def _decay_kernel(lf_ref, tri_ref, o_ref, *, blk):
    carry = jnp.zeros((1, LANES), F32)
    for j in range(lf_ref.shape[1] // blk):
        c = jnp.dot(tri_ref[...], lf_ref[0, j * blk:(j + 1) * blk, :],
                    precision=lax.Precision.HIGHEST, preferred_element_type=F32) + carry
        o_ref[0, j * blk:(j + 1) * blk, :] = c
        carry = c[blk - 1:blk, :]


def _decay(lf, blk=256):
    b, seq, _ = lf.shape
    tri = jnp.asarray(np.arange(blk)[:, None] >= np.arange(blk)[None, :], F32)
    return pl.pallas_call(
        functools.partial(_decay_kernel, blk=blk),
        grid=(b,),
        in_specs=[pl.BlockSpec((1, seq, LANES), lambda i: (i, 0, 0)),
                  pl.BlockSpec((blk, blk), lambda i: (0, 0))],
        out_specs=pl.BlockSpec((1, seq, LANES), lambda i: (i, 0, 0)),
        out_shape=jax.ShapeDtypeStruct((b, seq, LANES), F32),
        compiler_params=_params("arbitrary"),
        name="fox_decay",
    )(lf, tri)


def _head_masks():
    lane = lax.broadcasted_iota(jnp.int32, (1, PAIR), 1)
    return lane < HEAD_DIM, lane >= HEAD_DIM


def _split_heads(q):
    return [jnp.where(m, q, jnp.zeros_like(q)) for m in _head_masks()]


def _feat_base(hh):
    return HEAD_DIM * (1 - hh)


def _three_bf16(x):
    hi = x.astype(BF16).astype(F32)
    mid = (x - hi).astype(BF16).astype(F32)
    lo = (x - hi - mid).astype(BF16).astype(F32)
    return hi, mid, lo


def _place(idx, base, parts):
    out = jnp.zeros(jnp.broadcast_shapes(idx.shape, jnp.shape(parts[0])), F32)
    for r, part in enumerate(parts):
        out = jnp.where(idx == base + r, part, out)
    return out


def _flash_pair(c, qx, kx_scr, v_ref, o_ref, keep_own, own_first, m_scr, acc_scr, p_scr, a_scr, tq):
    _reset(m_scr, acc_scr)

    def scores(j, own, slot):
        for hh in range(2):
            s = lax.dot_general(qx[hh], kx_scr[hh, j * tq:(j + 1) * tq, :], _NT,
                                preferred_element_type=F32)
            if own:
                s = jnp.where(keep_own, s, NEG_INF)
            _softmax_stage(s, m_scr.at[hh], p_scr.at[slot, hh], a_scr.at[slot, hh])

    def values(j, slot):
        vs = _with_ones(v_ref[0, j * tq:(j + 1) * tq, :])
        for hh in range(2):
            _value_stage(vs[hh], acc_scr.at[hh], p_scr.at[slot, hh], a_scr.at[slot, hh])

    order = [c] + list(range(c)) if own_first else list(range(c + 1))
    scores(order[0], order[0] == c, 0)
    for n in range(1, len(order)):
        scores(order[n], order[n] == c, n % 2)
        values(order[n - 1], (n - 1) % 2)
    values(order[-1], (len(order) - 1) % 2)
    o_ref[0, c * tq:(c + 1) * tq, :] = _merge_pair(acc_scr)


def _with_ones(v):
    return [jnp.where(m, v, jnp.ones_like(v)) for m in _head_masks()]


def _softmax_stage(s, m_ref, p_ref, a_ref):
    m_prev = m_ref[...]
    m_new = jnp.maximum(m_prev, jnp.max(s, axis=-1, keepdims=True))
    p = jnp.exp(s - jnp.concatenate([m_new] * (s.shape[1] // LANES), axis=1))
    m_ref[...] = m_new
    p_ref[...] = p.astype(BF16)
    a_ref[...] = jnp.exp(m_prev - m_new)


def _value_stage(v_ones, acc_ref, p_ref, a_ref):
    acc_ref[...] = a_ref[...] * acc_ref[...] + jnp.dot(p_ref[...], v_ones, preferred_element_type=F32)


def _reset(m_scr, acc_scr):
    m_scr[...] = jnp.full(m_scr.shape, NEG_INF, F32)
    acc_scr[...] = jnp.zeros(acc_scr.shape, F32)


def _merge_pair(acc_scr):
    first = _head_masks()[0]
    a0, a1 = acc_scr[0], acc_scr[1]
    return jnp.where(first, a0 / a0[:, HEAD_DIM:HEAD_DIM + 1], a1 / a1[:, 0:1]).astype(BF16)


def _stat_scratch(tq):
    return [pltpu.VMEM((2, tq, LANES), F32), pltpu.VMEM((2, tq, PAIR), F32),
            pltpu.VMEM((2, 2, tq, tq), BF16), pltpu.VMEM((2, 2, tq, LANES), F32)]


def _fox_kernel(q_ref, k_ref, v_ref, cum_ref, o_ref, kx_scr, m_scr, acc_scr, p_scr, a_scr, *, tq):
    pair = pl.program_id(1)
    masks = _head_masks()
    lane = lax.broadcasted_iota(jnp.int32, (1, PAIR), 1)
    k = k_ref[0]
    cum = cum_ref[0]
    lane_s = lax.broadcasted_iota(jnp.int32, cum.shape, 1)
    for hh in range(2):
        f_s = jnp.sum(jnp.where(lane_s == 2 * pair + hh, cum, 0.0), axis=-1, keepdims=True)
        feats = _place(lane, _feat_base(hh), _three_bf16(-f_s))
        kx_scr[hh] = jnp.where(masks[hh], k, feats.astype(BF16))

    causal = (lax.broadcasted_iota(jnp.int32, (tq, tq), 1)
              <= lax.broadcasted_iota(jnp.int32, (tq, tq), 0))
    for c in range(q_ref.shape[1] // tq):
        q = q_ref[0, c * tq:(c + 1) * tq, :]
        qx = []
        for hh in range(2):
            ones = (lane >= _feat_base(hh)) & (lane < _feat_base(hh) + 3)
            qx.append(jnp.where(masks[hh], q, jnp.where(ones, 1.0, 0.0).astype(BF16)))
        _flash_pair(c, qx, kx_scr, v_ref, o_ref, causal, False, m_scr, acc_scr, p_scr, a_scr, tq)


def _fox(q, k, v, cum, col0, n_pairs, tq):
    b, seq, _ = q.shape
    head_cols = pl.BlockSpec((1, seq, PAIR), lambda bi, p: (bi, 0, col0 + p))
    return pl.pallas_call(
        functools.partial(_fox_kernel, tq=tq),
        grid=(b, n_pairs),
        in_specs=[head_cols, head_cols, head_cols,
                  pl.BlockSpec((1, seq, LANES), lambda bi, p: (bi, 0, 0))],
        out_specs=pl.BlockSpec((1, seq, PAIR), lambda bi, p: (bi, 0, p)),
        out_shape=jax.ShapeDtypeStruct((b, seq, n_pairs * PAIR), BF16),
        scratch_shapes=[pltpu.VMEM((2, seq, PAIR), BF16)] + _stat_scratch(tq),
        compiler_params=_params("arbitrary", "arbitrary"),
        name="fox_attn",
    )(q, k, v, cum)


MOBA_NBLK = 8
ALIBI_PARTS = 5


def _moba_feature_tables(slopes, seq):
    pos = np.arange(seq, dtype=np.float64)

    def parts(x):
        out, rest = [], x.copy()
        for _ in range(ALIBI_PARTS):
            p = rest.astype(np.float32).astype(jnp.bfloat16).astype(np.float64)
            out.append(p)
            rest = rest - p
        assert not rest.any()
        return out

    n_heads = len(slopes)
    qf = np.zeros((n_heads // 2, 2, seq, PAIR), np.float32)
    kf = np.zeros((n_heads // 2, 2, seq, PAIR), np.float32)
    for h, slope in enumerate(slopes):
        base = _feat_base(h % 2)
        q_t, k_t = qf[h // 2, h % 2], kf[h // 2, h % 2]
        k_t[np.arange(seq), base + np.arange(seq) // MOBA_BLOCK] = 1.0
        lo, mid, hi = base + MOBA_NBLK, base + MOBA_NBLK + ALIBI_PARTS, base + MOBA_NBLK + 2 * ALIBI_PARTS
        q_t[:, lo:mid] = 1.0
        k_t[:, lo:mid] = np.stack(parts(np.float64(slope) * pos), axis=1)
        q_t[:, mid:hi] = np.stack(parts(-np.float64(slope) * pos), axis=1)
        k_t[:, mid:hi] = 1.0
    return jnp.asarray(qf, BF16), jnp.asarray(kf, BF16)


def _moba_kernel(q_ref, k_ref, v_ref, qf_ref, kf_ref, o_ref, km_scr, kx_scr, m_scr, acc_scr, p_scr,
                 a_scr, *, tq):
    blk = MOBA_BLOCK
    nbp = MOBA_NBLK
    n_blk = k_ref.shape[1] // blk
    per = tq // blk
    masks = _head_masks()

    km_scr[...] = jnp.zeros(km_scr.shape, F32)
    for n in range(n_blk):
        kb = k_ref[0, n * blk:(n + 1) * blk, :].astype(F32)
        km_scr[n:n + 1, :] = jnp.sum(kb, axis=0, keepdims=True) * (1.0 / blk)
    for hh in range(2):
        kx_scr[hh] = jnp.where(masks[hh], k_ref[0], kf_ref[0, hh])

    blk_n = lax.broadcasted_iota(jnp.int32, (nbp, tq), 0)
    lane = lax.broadcasted_iota(jnp.int32, (1, PAIR), 1)
    r = lax.broadcasted_iota(jnp.int32, (tq, tq), 0)
    s = lax.broadcasted_iota(jnp.int32, (tq, tq), 1)
    keep_own = (r // blk != s // blk) | (s <= r)

    for c in range(q_ref.shape[1] // tq):
        q = q_ref[0, c * tq:(c + 1) * tq, :]
        qs = _split_heads(q)
        q_blk = c * per + lax.broadcasted_iota(jnp.int32, (nbp, tq), 1) // blk
        qx = []
        for hh in range(2):
            base = _feat_base(hh)
            g = lax.dot_general(km_scr[0:nbp, :], qs[hh].astype(F32), _NT,
                                precision=lax.Precision.HIGHEST, preferred_element_type=F32)
            cnt = jnp.zeros((nbp, tq), F32)
            for m in range(n_blk):
                gm = g[m:m + 1, :]
                beats = ((gm > g) | ((gm == g) & (blk_n > m))) & (q_blk > m)
                cnt = cnt + jnp.where(beats, 1.0, 0.0)
            visible = ((blk_n < q_blk) & (cnt < float(MOBA_TOPK))) | (blk_n == q_blk)
            hide = jnp.where(visible, 0.0, NEG_INF)
            pads = [jnp.zeros((n, tq), F32) for n in (base, LANES - nbp - base)]
            hide = jnp.concatenate([a for a in (pads[0], hide, pads[1]) if a.shape[0]], axis=0)
            feats = jnp.where((lane >= base) & (lane < base + nbp), hide.T.astype(BF16),
                              qf_ref[0, hh, c * tq:(c + 1) * tq, :])
            qx.append(jnp.where(masks[hh], q, feats))
        _flash_pair(c, qx, kx_scr, v_ref, o_ref, keep_own, True, m_scr, acc_scr, p_scr, a_scr, tq)


def _moba(q, k, v, col0, n_pairs, tq):
    b, seq, _ = q.shape
    blk = MOBA_BLOCK
    assert seq % tq == 0 and tq % blk == 0 and seq // blk <= MOBA_NBLK
    slopes = _alibi_slopes(N_ALIBI)[MOBA_SLOPE_OFFSET:MOBA_SLOPE_OFFSET + 2 * n_pairs]
    q_feats, k_feats = _moba_feature_tables(slopes, seq)
    head_cols = pl.BlockSpec((1, seq, PAIR), lambda bi, p: (bi, 0, col0 + p))
    table = pl.BlockSpec((1, 2, seq, PAIR), lambda bi, p: (p, 0, 0, 0))
    return pl.pallas_call(
        functools.partial(_moba_kernel, tq=tq),
        grid=(b, n_pairs),
        in_specs=[head_cols, head_cols, head_cols, table, table],
        out_specs=pl.BlockSpec((1, seq, PAIR), lambda bi, p: (bi, 0, p)),
        out_shape=jax.ShapeDtypeStruct((b, seq, n_pairs * PAIR), BF16),
        scratch_shapes=[pltpu.VMEM((LANES, PAIR), F32), pltpu.VMEM((2, seq, PAIR), BF16)]
        + _stat_scratch(tq),
        compiler_params=_params("arbitrary", "arbitrary"),
        name="moba_attn",
    )(q, k, v, q_feats, k_feats)


def _dil_kernel(q_ref, k_ref, v_ref, o_ref, lse_ref, *, dil, slopes, n_q):
    wb = WIN_BLOCK
    seq = q_ref.shape[1]
    nb = seq // wb
    steps = (lax.broadcasted_iota(jnp.int32, (wb, 2 * wb), 0) + wb
             - lax.broadcasted_iota(jnp.int32, (wb, 2 * wb), 1))
    band_ok = (steps >= 0) & (steps <= wb)
    steps_f = (steps * dil).astype(F32)
    blk = lax.broadcasted_iota(jnp.int32, (nb, 1, 2 * wb), 0)
    key = lax.broadcasted_iota(jnp.int32, (nb, 1, 2 * wb), 2)
    no_prev = jnp.where((blk % n_q == 0) & (key < wb), -NEG_INF, 0.0)
    first = _head_masks()[0]

    def band(ref, cols):
        cur = ref[0, :, cols]
        prev = jnp.concatenate([cur[0:wb], cur[0:seq - wb]], axis=0)
        return jnp.concatenate([prev.reshape(nb, wb, PAIR), cur.reshape(nb, wb, PAIR)], axis=1)

    for pair in range(2):
        cols = slice(pair * PAIR, (pair + 1) * PAIR)
        qs = [qh.reshape(nb, wb, PAIR) for qh in _split_heads(q_ref[0, :, cols])]
        kband, vband = band(k_ref, cols), band(v_ref, cols)
        outs, lses = [], []
        for hh in range(2):
            bias = jnp.where(band_ok, slopes[2 * pair + hh] * steps_f, -NEG_INF)
            s = jnp.einsum("bqd,bkd->bqk", qs[hh], kband, preferred_element_type=F32)
            s = s - bias[None] - no_prev
            m = jnp.max(s, axis=-1, keepdims=True)
            p = jnp.exp(s - m)
            l = jnp.sum(p, axis=-1, keepdims=True)
            o = jnp.einsum("bqk,bkd->bqd", p.astype(BF16), vband, preferred_element_type=F32)
            outs.append(o / l)
            lses.append(jnp.broadcast_to(m + jnp.log(l), (nb, wb, PAIR)))
        o_ref[0, :, cols] = jnp.where(first, outs[0], outs[1]).reshape(seq, PAIR).astype(BF16)
        lse_ref[0, :, cols] = jnp.where(first, lses[0], lses[1]).reshape(seq, PAIR)


def _dilated(q, k, v, group):
    window, dil = DILATED_PAIRS[group]
    assert window // dil == WIN_BLOCK
    b, _, l_sub, w = q.shape
    seq = dil * l_sub
    so = DIL_SLOPE_OFFSETS[group]
    slopes = tuple(float(s) for s in _alibi_slopes(N_ALIBI)[so:so + N_SLOTS_DIL])
    spec = pl.BlockSpec((1, seq, w), lambda bi: (bi, 0, 0))
    flat = lambda a: a.reshape(b, seq, w)
    o, lse = pl.pallas_call(
        functools.partial(_dil_kernel, dil=dil, slopes=slopes, n_q=l_sub // WIN_BLOCK),
        grid=(b,),
        in_specs=[spec, spec, spec],
        out_specs=[spec, spec],
        out_shape=[jax.ShapeDtypeStruct((b, seq, w), BF16), jax.ShapeDtypeStruct((b, seq, w), F32)],
        compiler_params=_params("arbitrary"),
        name=f"dilated_attn_{dil}",
    )(flat(q), flat(k), flat(v))
    return o.reshape(q.shape), lse.reshape(q.shape)


def _outproj_kernel(x_ref, mod_ref, g_ref, of_ref, om_ref, o1, l1, o2, l2, o3, l3, gate_ref,
                    wf_ref, wm_ref, wd_ref, wo_ref, x1_ref, h2_ref, perm_scr):
    d = x_ref.shape[1]

    def natural(ref):
        dil, n = ref.shape[1], ref.shape[2]
        if dil == 1:
            return ref[0, 0].astype(F32)
        halves = ref.shape[3] // LANES
        for r in range(dil):
            blk = ref[0, r].astype(F32)
            for c in range(halves):
                perm_scr[c, pl.ds(r, n, stride=dil), :] = blk[:, c * LANES:(c + 1) * LANES]
        return jnp.concatenate([perm_scr[c] for c in range(halves)], axis=1)

    lses = [natural(l) for l in (l1, l2, l3)]
    lmax = jnp.maximum(jnp.maximum(lses[0], lses[1]), lses[2])
    e1, e2, e3 = [jnp.exp(l - lmax) for l in lses]
    den = e1 + e2 + e3
    o_dil = (e1 / den) * natural(o1) + (e2 / den) * natural(o2) + (e3 / den) * natural(o3)
    y = (gate_ref[:, 0:d].astype(F32) * jnp.dot(of_ref[...], wf_ref[...], preferred_element_type=F32)
         + gate_ref[:, d:2 * d].astype(F32) * jnp.dot(om_ref[...], wm_ref[...], preferred_element_type=F32)
         + gate_ref[:, 2 * d:3 * d].astype(F32)
         * jnp.dot(o_dil.astype(BF16), wd_ref[...], preferred_element_type=F32))
    out = jnp.dot(y.astype(BF16), wo_ref[...], preferred_element_type=F32)
    x1 = x_ref[...] + mod_ref[0, 2:3, :] * out
    x1_ref[...] = x1
    h2_ref[...] = _mod_norm(x1, g_ref[...], mod_ref[0, 4:5, :], mod_ref[0, 3:4, :]).astype(BF16)


def _outproj(x2, mod_l, g_norm, o_fox, o_moba, dil_outs, gates, w_f, w_m, w_d, w_o, seq, tm):
    t, d = x2.shape
    per_b = seq // tm
    row = lambda i: (i, 0)
    fix = lambda i: (0, 0)
    dil_args, dil_specs = [], []
    for o, lse in dil_outs:
        _, dil, _, w = o.shape
        dil_args += [o, lse]
        dil_specs += [pl.BlockSpec((1, dil, tm // dil, w), lambda i: (i // per_b, 0, i % per_b, 0))] * 2
    return pl.pallas_call(
        _outproj_kernel,
        grid=(t // tm,),
        in_specs=[pl.BlockSpec((tm, d), row),
                  pl.BlockSpec((1, 6, d), lambda i: (i // per_b, 0, 0)),
                  pl.BlockSpec((1, d), fix),
                  pl.BlockSpec((tm, o_fox.shape[1]), row),
                  pl.BlockSpec((tm, o_moba.shape[1]), row)] + dil_specs + [
                  pl.BlockSpec((tm, N_BRANCH * d), row),
                  pl.BlockSpec(w_f.shape, fix), pl.BlockSpec(w_m.shape, fix),
                  pl.BlockSpec(w_d.shape, fix), pl.BlockSpec(w_o.shape, fix)],
        out_specs=[pl.BlockSpec((tm, d), row), pl.BlockSpec((tm, d), row)],
        out_shape=[jax.ShapeDtypeStruct((t, d), F32), jax.ShapeDtypeStruct((t, d), BF16)],
        scratch_shapes=[pltpu.VMEM((N_SLOTS_DIL * HEAD_DIM // LANES, tm, LANES), F32)],
        compiler_params=_params("arbitrary"),
        name="outproj",
    )(x2, mod_l, g_norm, o_fox, o_moba, *dil_args, gates, w_f, w_m, w_d, w_o)


def _router_kernel(h_ref, w_ref, b_ref, o_ref, cnt_ref, *, n_exp):
    logits = jnp.dot(h_ref[...], w_ref[...], preferred_element_type=F32) + b_ref[...]
    lane = lax.broadcasted_iota(jnp.int32, logits.shape, 1)
    logits = jnp.where(lane < n_exp, logits, -jnp.inf)

    def top(vals):
        m = jnp.max(vals, axis=-1, keepdims=True)
        idx = jnp.min(jnp.where(vals == m, lane, LANES), axis=-1, keepdims=True)
        return m, lane == idx

    m1, hot1 = top(logits)
    m2, hot2 = top(jnp.where(hot1, -jnp.inf, logits))
    e2 = jnp.exp(m2 - m1)
    cw = jnp.where(hot1, 1.0 / (1.0 + e2), 0.0) + jnp.where(hot2, e2 / (1.0 + e2), 0.0)
    o_ref[...] = cw
    sub = cnt_ref.shape[0] // 8
    rows = cw.shape[0] // sub
    for s in range(sub):
        n_tok = jnp.sum(jnp.where(cw[s * rows:(s + 1) * rows] > 0.0, 1.0, 0.0), axis=0, keepdims=True)
        cnt_ref[s * 8:(s + 1) * 8, :] = jnp.broadcast_to(n_tok, (8, LANES)).astype(jnp.int32)


def _router(h2, w_r, b_r, n_exp, tile):
    t, d = h2.shape
    tm = 4 * tile
    cw, cnt = pl.pallas_call(
        functools.partial(_router_kernel, n_exp=n_exp),
        grid=(t // tm,),
        in_specs=[pl.BlockSpec((tm, d), lambda i: (i, 0)),
                  pl.BlockSpec((d, LANES), lambda i: (0, 0)),
                  pl.BlockSpec((1, LANES), lambda i: (0, 0))],
        out_specs=[pl.BlockSpec((tm, LANES), lambda i: (i, 0)),
                   pl.BlockSpec((8 * tm // tile, LANES), lambda i: (i, 0))],
        out_shape=[jax.ShapeDtypeStruct((t, LANES), F32),
                   jax.ShapeDtypeStruct((t // tile * 8, LANES), jnp.int32)],
        compiler_params=_params("arbitrary"),
        name="router",
    )(h2, w_r, b_r)
    return cw, cnt.reshape(t // tile, 8, LANES)[:, 0, :n_exp]


MOE_TILE = 512
MOE_ROWS = 512
MOE_PUT = 128
MOE_GET = 256
MOE_ALIGN = 16


def _moe_layout(cnt, n_rows_static):
    n_tiles, n_exp = cnt.shape
    padded = (cnt + MOE_ALIGN - 1) // MOE_ALIGN * MOE_ALIGN
    length = jnp.sum(padded, axis=0)
    span = (length + (MOE_PUT - MOE_ALIGN) + MOE_ROWS - 1) // MOE_ROWS * MOE_ROWS
    start = jnp.cumsum(span) - span
    off = start[None, :] + jnp.cumsum(padded, axis=0) - padded
    n_steps = n_rows_static // MOE_ROWS
    first = start // MOE_ROWS
    step = jnp.arange(n_steps, dtype=jnp.int32)
    expert = jnp.sum((first[None, :] <= step[:, None]).astype(jnp.int32), axis=1) - 1
    active = step < (first + (length + MOE_ROWS - 1) // MOE_ROWS)[expert]
    return (off.reshape(-1).astype(jnp.int32), cnt.reshape(-1).astype(jnp.int32),
            expert.astype(jnp.int32), active.astype(jnp.int32))


def _dispatch_kernel(off_ref, cnt_ref, h_ref, cw_ref, u_ref, xs_in, xs_ref, stage, sems, *, n_exp):
    del xs_in
    i = pl.program_id(0)
    n_put = MOE_TILE // MOE_PUT
    routed = jnp.where(cw_ref[...].T[0:8] > 0.0, 1.0, 0.0)
    pos = jnp.dot(routed.astype(BF16), u_ref[...], preferred_element_type=F32)
    row = lax.broadcasted_iota(jnp.int32, (MOE_PUT, MOE_TILE), 0).astype(F32)
    h = h_ref[...]

    def copy(e, s):
        slot = e * n_put + s
        dst = pl.multiple_of(off_ref[i * n_exp + e] + s * MOE_PUT, MOE_ALIGN)
        return pltpu.make_async_copy(stage.at[slot], xs_ref.at[pl.ds(dst, MOE_PUT), :], sems.at[slot])

    for e in range(n_exp):
        for s in range(n_put):
            @pl.when(s * MOE_PUT < cnt_ref[i * n_exp + e])
            def _():
                take = (routed[e:e + 1, :] > 0.5) & (pos[e:e + 1, :] == row + float(s * MOE_PUT))
                onehot = jnp.where(take, 1.0, 0.0).astype(BF16)
                stage[e * n_put + s] = jnp.dot(onehot, h, preferred_element_type=F32).astype(BF16)
                copy(e, s).start()

    for e in range(n_exp):
        for s in range(n_put):
            @pl.when(s * MOE_PUT < cnt_ref[i * n_exp + e])
            def _():
                copy(e, s).wait()


def _dispatch(h2, cw, off, cnt, n_exp, n_rows):
    t, d = h2.shape
    u = jnp.asarray(np.arange(MOE_TILE)[:, None] < np.arange(MOE_TILE)[None, :], BF16)
    n_slots = n_exp * (MOE_TILE // MOE_PUT)
    return pl.pallas_call(
        functools.partial(_dispatch_kernel, n_exp=n_exp),
        grid_spec=pltpu.PrefetchScalarGridSpec(
            num_scalar_prefetch=2,
            grid=(t // MOE_TILE,),
            in_specs=[pl.BlockSpec((MOE_TILE, d), lambda i, o, c: (i, 0)),
                      pl.BlockSpec((MOE_TILE, LANES), lambda i, o, c: (i, 0)),
                      pl.BlockSpec((MOE_TILE, MOE_TILE), lambda i, o, c: (0, 0)),
                      pl.BlockSpec(memory_space=pl.ANY)],
            out_specs=pl.BlockSpec(memory_space=pl.ANY),
            scratch_shapes=[pltpu.VMEM((n_slots, MOE_PUT, d), BF16),
                            pltpu.SemaphoreType.DMA((n_slots,))]),
        out_shape=jax.ShapeDtypeStruct((n_rows, d), BF16),
        input_output_aliases={5: 0},
        compiler_params=_params("arbitrary"),
        name="moe_dispatch",
    )(off, cnt, h2, cw, u, jnp.zeros((n_rows, d), BF16))


FF_CHUNK = 768


def _swiglu(x, wg_ref, wu_ref, wd_ref):
    ff = wg_ref.shape[2]
    y = None
    for c0 in range(0, ff, FF_CHUNK):
        c1 = min(c0 + FF_CHUNK, ff)
        gate = jnp.dot(x, wg_ref[0, :, c0:c1], preferred_element_type=F32)
        up = jnp.dot(x, wu_ref[0, :, c0:c1], preferred_element_type=F32)
        a = ((gate * _sigmoid(gate)) * up).astype(BF16)
        part = jnp.dot(a, wd_ref[0, c0:c1, :], preferred_element_type=F32)
        y = part if y is None else y + part
    return y


def _experts_kernel(exp_ref, act_ref, xs_ref, wg_ref, wu_ref, wd_ref, ys_ref):
    g = pl.program_id(0)

    @pl.when(act_ref[g] > 0)
    def _():
        ys_ref[...] = _swiglu(xs_ref[...], wg_ref, wu_ref, wd_ref).astype(BF16)

    @pl.when(act_ref[g] == 0)
    def _():
        ys_ref[...] = jnp.zeros(ys_ref.shape, BF16)


def _experts(xs, expert, active, w_g, w_u, w_d):
    n_rows, d = xs.shape
    _, _, ff = w_g.shape
    wmap = lambda g, ex, ac: (ex[g], 0, 0)
    return pl.pallas_call(
        _experts_kernel,
        grid_spec=pltpu.PrefetchScalarGridSpec(
            num_scalar_prefetch=2,
            grid=(n_rows // MOE_ROWS,),
            in_specs=[pl.BlockSpec((MOE_ROWS, d), lambda g, ex, ac: (g, 0)),
                      pl.BlockSpec((1, d, ff), wmap), pl.BlockSpec((1, d, ff), wmap),
                      pl.BlockSpec((1, ff, d), wmap)],
            out_specs=pl.BlockSpec((MOE_ROWS, d), lambda g, ex, ac: (g, 0))),
        out_shape=jax.ShapeDtypeStruct((n_rows, d), BF16),
        compiler_params=_params("arbitrary"),
        name="moe_experts",
    )(expert, active, xs, w_g, w_u, w_d)


def _combine_kernel(off_ref, cnt_ref, x_ref, mod_ref, cw_ref, l_ref, ys_ref, o_ref, buf, sems, acc_scr,
                    *, n_exp):
    i = pl.program_id(0)
    n_get = MOE_TILE // MOE_GET

    def copy(e, s):
        slot = e * n_get + s
        src = pl.multiple_of(off_ref[i * n_exp + e] + s * MOE_GET, MOE_ALIGN)
        return pltpu.make_async_copy(ys_ref.at[pl.ds(src, MOE_GET), :], buf.at[slot], sems.at[slot])

    for e in range(n_exp):
        for s in range(n_get):
            @pl.when(s * MOE_GET < cnt_ref[i * n_exp + e])
            def _():
                copy(e, s).start()

    cw = cw_ref[...]
    routed = jnp.where(cw > 0.0, 1.0, 0.0)
    pos = jnp.dot(l_ref[...], routed.astype(BF16), preferred_element_type=F32)
    col = lax.broadcasted_iota(jnp.int32, (MOE_TILE, MOE_GET), 1).astype(F32)
    acc_scr[...] = jnp.zeros(acc_scr.shape, F32)
    for e in range(n_exp):
        for s in range(n_get):
            @pl.when(s * MOE_GET < cnt_ref[i * n_exp + e])
            def _():
                copy(e, s).wait()
                take = (cw[:, e:e + 1] > 0.0) & (pos[:, e:e + 1] == col + float(s * MOE_GET))
                onehot = jnp.where(take, 1.0, 0.0).astype(BF16)
                acc_scr[...] += cw[:, e:e + 1] * jnp.dot(onehot, buf[e * n_get + s],
                                                         preferred_element_type=F32)
    o_ref[...] = x_ref[...] + mod_ref[0, 5:6, :] * acc_scr[...]


def _combine(x1, mod_l, cw, ys, off, cnt, n_exp, seq):
    t, d = x1.shape
    per_b = seq // MOE_TILE
    low = jnp.asarray(np.arange(MOE_TILE)[:, None] > np.arange(MOE_TILE)[None, :], BF16)
    n_slots = n_exp * (MOE_TILE // MOE_GET)
    return pl.pallas_call(
        functools.partial(_combine_kernel, n_exp=n_exp),
        grid_spec=pltpu.PrefetchScalarGridSpec(
            num_scalar_prefetch=2,
            grid=(t // MOE_TILE,),
            in_specs=[pl.BlockSpec((MOE_TILE, d), lambda i, o, c: (i, 0)),
                      pl.BlockSpec((1, 6, d), lambda i, o, c: (i // per_b, 0, 0)),
                      pl.BlockSpec((MOE_TILE, LANES), lambda i, o, c: (i, 0)),
                      pl.BlockSpec((MOE_TILE, MOE_TILE), lambda i, o, c: (0, 0)),
                      pl.BlockSpec(memory_space=pl.ANY)],
            out_specs=pl.BlockSpec((MOE_TILE, d), lambda i, o, c: (i, 0)),
            scratch_shapes=[pltpu.VMEM((n_slots, MOE_GET, d), BF16),
                            pltpu.SemaphoreType.DMA((n_slots,)),
                            pltpu.VMEM((MOE_TILE, d), F32)]),
        out_shape=jax.ShapeDtypeStruct((t, d), F32),
        compiler_params=_params("arbitrary"),
        name="moe_combine",
    )(off, cnt, x1, mod_l, cw, low, ys)


def _moe(x1, h2, mod_l, w_r, b_r, w_g, w_u, w_d, seq):
    t, d = x1.shape
    n_exp = w_g.shape[0]
    n_tiles = t // MOE_TILE
    bound = (TOP_K * t + n_tiles * n_exp * (MOE_ALIGN - 1)
             + n_exp * (MOE_PUT - MOE_ALIGN + MOE_ROWS - 1))
    n_rows = (bound + MOE_ROWS - 1) // MOE_ROWS * MOE_ROWS + MOE_ROWS
    cw, cnt = _router(h2, w_r, b_r, n_exp, MOE_TILE)
    off, cnt, expert, active = _moe_layout(cnt, n_rows)
    xs = _dispatch(h2, cw, off, cnt, n_exp, n_rows)
    ys = _experts(xs, expert, active, w_g, w_u, w_d)
    return _combine(x1, mod_l, cw, ys, off, cnt, n_exp, seq)


def _ffn_kernel(x_ref, h_ref, mod_ref, wg_ref, wu_ref, wd_ref, o_ref):
    o_ref[...] = x_ref[...] + mod_ref[0, 5:6, :] * _swiglu(h_ref[...], wg_ref, wu_ref, wd_ref)


def _ffn(x1, h2, mod_l, w_g, w_u, w_d, seq, tm):
    t, d = x1.shape
    per_b = seq // tm
    row = lambda i: (i, 0)
    whole = lambda w: pl.BlockSpec(w.shape, lambda i: (0, 0, 0), pipeline_mode=pl.Buffered(1))
    return pl.pallas_call(
        _ffn_kernel,
        grid=(t // tm,),
        in_specs=[pl.BlockSpec((tm, d), row), pl.BlockSpec((tm, d), row),
                  pl.BlockSpec((1, 6, d), lambda i: (i // per_b, 0, 0)),
                  whole(w_g), whole(w_u), whole(w_d)],
        out_specs=pl.BlockSpec((tm, d), row),
        out_shape=jax.ShapeDtypeStruct((t, d), F32),
        compiler_params=_params("arbitrary"),
        name="swiglu",
    )(x1, h2, mod_l, w_g, w_u, w_d)


def _pad_cols(a, width):
    return jnp.pad(a, ((0, 0), (0, width - a.shape[1])))


def kernel(x, c, w_ada, b_ada, norm_mix, norm_ffn, w_in, b_fgate, q_gain, k_gain, w_br_fox, w_br_moba,
           w_br_dil, w_out, w_ffn_gate, w_ffn_up, w_ffn_down, w_router, b_router, w_exp_gate,
           w_exp_up, w_exp_down):
    b, seq, d = x.shape
    depth = w_ada.shape[0]
    t = b * seq
    tm, tq = TOKEN_TILE, ATTN_CHUNK
    n_pairs_fox = N_HEADS_FOX // 2
    n_pairs_moba = N_HEADS_MOBA // 2

    mod = _adaln(c, w_ada, b_ada).reshape(depth, b, 6, d)
    x2 = x.reshape(t, d)
    w_in_t = jnp.transpose(w_in, (2, 0, 1))
    for l in range(depth):
        wl = w_in_t[:, l, :].astype(BF16)
        f0 = 3 * MIX_WIDTH
        w_qkv, w_gate = wl[:f0], wl[f0 + N_HEADS_FOX:]
        w_f = jnp.pad(wl[f0:f0 + N_HEADS_FOX], ((0, LANES - N_HEADS_FOX), (0, 0)))
        b_f = _pad_cols(b_fgate[l].reshape(1, -1), LANES)
        outs = _inproj(x2, mod[l], norm_mix[l].reshape(1, d), w_qkv, w_gate, w_f, q_gain[l].reshape(1, -1),
                       k_gain[l].reshape(1, -1), b_f, seq, tm)
        q_m, k_m, v_m = (o.reshape(b, seq, -1) for o in outs[0:3])
        gates, lf = outs[12], outs[13]

        cum = _decay(lf.reshape(b, seq, LANES))
        o_fox = _fox(q_m, k_m, v_m, cum, 0, n_pairs_fox, tq).reshape(t, -1)
        o_moba = _moba(q_m, k_m, v_m, n_pairs_fox, n_pairs_moba, tq).reshape(t, -1)
        dil_outs = [_dilated(*outs[3 + 3 * g:6 + 3 * g], g) for g in range(len(DILATED_PAIRS))]

        x1, h2 = _outproj(x2, mod[l], norm_ffn[l].reshape(1, d), o_fox, o_moba, dil_outs, gates,
                          w_br_fox[l].astype(BF16), w_br_moba[l].astype(BF16),
                          w_br_dil[l].astype(BF16), w_out[l].astype(BF16), seq, tm)
        i = l // 2
        if l % 2 == 0:
            x2 = _ffn(x1, h2, mod[l], w_ffn_gate[i:i + 1].astype(BF16),
                      w_ffn_up[i:i + 1].astype(BF16), w_ffn_down[i:i + 1].astype(BF16), seq, FFN_TILE)
        else:
            x2 = _moe(x1, h2, mod[l], _pad_cols(w_router[i], LANES).astype(BF16),
                      _pad_cols(b_router[i].reshape(1, -1), LANES), w_exp_gate[i].astype(BF16),
                      w_exp_up[i].astype(BF16), w_exp_down[i].astype(BF16), seq)
    return x2.reshape(b, seq, d)
```

```python
import functools

import numpy as np
import jax
import jax.numpy as jnp
from jax import lax
from jax.experimental import pallas as pl
from jax.experimental.pallas import tpu as pltpu

HEAD_DIM = 64
N_HEADS_FOX = 6
N_HEADS_MOBA = 6
DILATED_PAIRS = ((128, 1), (512, 4), (2048, 16))
N_SLOTS_DIL = 4
N_HEADS_DIL = N_SLOTS_DIL * len(DILATED_PAIRS)
N_HEADS = N_HEADS_FOX + N_HEADS_MOBA + N_HEADS_DIL
MIX_WIDTH = N_HEADS * HEAD_DIM
N_BRANCH = 3
MOBA_BLOCK = 256
MOBA_TOPK = 3
WIN_BLOCK = 128
N_ALIBI = N_HEADS_MOBA + N_HEADS_DIL
DIL_SLOPE_OFFSETS = (0, N_SLOTS_DIL, 2 * N_SLOTS_DIL + N_HEADS_MOBA)
MOBA_SLOPE_OFFSET = 2 * N_SLOTS_DIL
TOP_K = 2
RMS_EPS = 1e-6
NEG_INF = -1e30
LOG2E = 1.4426950408889634

LANES = 128
PAIR = 2 * HEAD_DIM
VMEM_LIMIT = 56 * 1024 * 1024
TOKEN_TILE = 512
ATTN_CHUNK = 512
FFN_TILE = 1024

F32 = jnp.float32
BF16 = jnp.bfloat16
_NT = (((1,), (1,)), ((), ()))


def _alibi_slopes(n):
    return (2.0 ** (-8.0 * np.arange(1, n + 1) / n)).astype(np.float32)


def _sigmoid(x):
    return 1.0 / (1.0 + jnp.exp(-x))


def _params(*sem):
    return pltpu.CompilerParams(dimension_semantics=sem, vmem_limit_bytes=VMEM_LIMIT)


def _adaln_kernel(c_ref, w_ref, b_ref, o_ref):
    c = c_ref[...]
    cond = c * _sigmoid(c)
    o_ref[0] = jnp.dot(cond, w_ref[0], precision=lax.Precision.HIGHEST,
                       preferred_element_type=F32) + b_ref[0]


def _adaln(c, w_ada, b_ada):
    depth, d, n = w_ada.shape
    b = c.shape[0]
    tn = 1536
    return pl.pallas_call(
        _adaln_kernel,
        grid=(depth, n // tn),
        in_specs=[pl.BlockSpec((b, d), lambda l, j: (0, 0)),
                  pl.BlockSpec((1, d, tn), lambda l, j: (l, 0, j)),
                  pl.BlockSpec((1, 1, tn), lambda l, j: (l, 0, j))],
        out_specs=pl.BlockSpec((1, b, tn), lambda l, j: (l, 0, j)),
        out_shape=jax.ShapeDtypeStruct((depth, b, n), F32),
        compiler_params=_params("arbitrary", "arbitrary"),
        name="adaln",
    )(c, w_ada, b_ada.reshape(depth, 1, n))


def _mod_norm(x, g, scale, shift):
    ms = jnp.mean(x * x, axis=-1, keepdims=True)
    return (x * lax.rsqrt(ms + RMS_EPS) * g) * (1.0 + scale) + shift


def _inproj_kernel(x_ref, mod_ref, g_ref, w_ref, wg_ref, wf_ref, qg_ref, kg_ref, bf_ref, e_ref,
                   q_ref, k_ref, v_ref, qd1, kd1, vd1, qd2, kd2, vd2, qd3, kd3, vd3,
                   gate_ref, lf_ref, perm_scr, *, n_main):
    h = _mod_norm(x_ref[...], g_ref[...], mod_ref[0, 1:2, :], mod_ref[0, 0:1, :]).astype(BF16)
    tm = x_ref.shape[0]
    qk_chunk = 4 * HEAD_DIM
    q_outs = (q_ref, qd1, qd2, qd3)
    k_outs = (k_ref, kd1, kd2, kd3)
    v_outs = (v_ref, vd1, vd2, vd3)

    def put(outs, col, val):
        if col < n_main:
            outs[0][:, col:col + qk_chunk] = val.astype(BF16)
            return
        g = (col - n_main) // qk_chunk
        dil = DILATED_PAIRS[g][1]
        if dil == 1:
            outs[1 + g][0, 0] = val.astype(BF16)
            return
        halves = qk_chunk // LANES
        for c in range(halves):
            perm_scr[c] = val[:, c * LANES:(c + 1) * LANES]
        for r in range(dil):
            outs[1 + g][0, r] = jnp.concatenate(
                [perm_scr[c, pl.ds(r, tm // dil, stride=dil), :] for c in range(halves)],
                axis=1).astype(BF16)

    def proj(wt_ref, col0, width):
        return lax.dot_general(h, wt_ref[col0:col0 + width, :], _NT, preferred_element_type=F32)

    wide = MIX_WIDTH // 2
    for half in range(2):
        yq = proj(w_ref, half * wide, wide)
        yk = proj(w_ref, MIX_WIDTH + half * wide, wide)
        yv = proj(w_ref, 2 * MIX_WIDTH + half * wide, wide)
        for c in range(wide // qk_chunk):
            col = half * wide + c * qk_chunk
            yq_c = yq[:, c * qk_chunk:(c + 1) * qk_chunk]
            yk_c = yk[:, c * qk_chunk:(c + 1) * qk_chunk]
            sq = jnp.concatenate([yq_c * yq_c, yk_c * yk_c], axis=0).astype(BF16)
            ss = jnp.dot(sq, e_ref[...], preferred_element_type=F32)
            r = lax.rsqrt(ss * (1.0 / HEAD_DIM) + RMS_EPS)
            put(q_outs, col, (yq_c * r[0:tm] * qg_ref[:, col:col + qk_chunk])
                * (HEAD_DIM ** -0.5 * LOG2E))
            put(k_outs, col, yk_c * r[tm:2 * tm] * kg_ref[:, col:col + qk_chunk])
            put(v_outs, col, yv[:, c * qk_chunk:(c + 1) * qk_chunk])

    gchunk = 1024
    for c in range(wg_ref.shape[0] // gchunk):
        y = proj(wg_ref, c * gchunk, gchunk)
        gate_ref[:, c * gchunk:(c + 1) * gchunk] = _sigmoid(y).astype(BF16)

    f = proj(wf_ref, 0, LANES) + bf_ref[...]
    lf_ref[...] = jnp.minimum(f, 0.0) - jnp.log(1.0 + jnp.exp(-jnp.abs(f)))


def _inproj(x2, mod_l, g_norm, w_qkv, w_gate, w_f, q_gain, k_gain, b_f, seq, tm):
    t, d = x2.shape
    n_main = (N_HEADS_FOX + N_HEADS_MOBA) * HEAD_DIM
    dil_w = N_SLOTS_DIL * HEAD_DIM
    e = (np.arange(dil_w)[:, None] // HEAD_DIM == np.arange(dil_w)[None, :] // HEAD_DIM)
    e = jnp.asarray(e, BF16)
    per_b = seq // tm
    row = lambda i: (i, 0)
    fix = lambda i: (0, 0)
    qkv_shapes = [jax.ShapeDtypeStruct((t, n_main), BF16)] * 3
    qkv_specs = [pl.BlockSpec((tm, n_main), row)] * 3
    for _, dil in DILATED_PAIRS:
        assert tm % (16 * dil) == 0
        qkv_shapes += [jax.ShapeDtypeStruct((t // seq, dil, seq // dil, dil_w), BF16)] * 3
        qkv_specs += [pl.BlockSpec((1, dil, tm // dil, dil_w),
                                   lambda i: (i // per_b, 0, i % per_b, 0))] * 3
    out_shapes = qkv_shapes + [jax.ShapeDtypeStruct((t, N_BRANCH * d), BF16),
                               jax.ShapeDtypeStruct((t, LANES), F32)]
    out_specs = qkv_specs + [pl.BlockSpec((tm, N_BRANCH * d), row), pl.BlockSpec((tm, LANES), row)]
    return pl.pallas_call(
        functools.partial(_inproj_kernel, n_main=n_main),
        grid=(t // tm,),
        in_specs=[pl.BlockSpec((tm, d), row),
                  pl.BlockSpec((1, 6, d), lambda i: (i // per_b, 0, 0)),
                  pl.BlockSpec((1, d), fix),
                  pl.BlockSpec(w_qkv.shape, fix, pipeline_mode=pl.Buffered(1)),
                  pl.BlockSpec(w_gate.shape, fix, pipeline_mode=pl.Buffered(1)),
                  pl.BlockSpec(w_f.shape, fix),
                  pl.BlockSpec((1, MIX_WIDTH), fix),
                  pl.BlockSpec((1, MIX_WIDTH), fix),
                  pl.BlockSpec((1, LANES), fix),
                  pl.BlockSpec(e.shape, fix)],
        out_specs=out_specs,
        out_shape=out_shapes,
        scratch_shapes=[pltpu.VMEM((dil_w // LANES, tm, LANES), F32)],
        compiler_params=_params("arbitrary"),
        name="inproj",
    )(x2, mod_l, g_norm, w_qkv, w_gate, w_f, q_gain, k_gain, b_f, e)


def _decay_kernel(lf_ref, tri_ref, o_ref, *, blk):
    carry = jnp.zeros((1, LANES), F32)
    for j in range(lf_ref.shape[1] // blk):
        c = jnp.dot(tri_ref[...], lf_ref[0, j * blk:(j + 1) * blk, :],
                    precision=lax.Precision.HIGHEST, preferred_element_type=F32) + carry
        o_ref[0, j * blk:(j + 1) * blk, :] = c
        carry = c[blk - 1:blk, :]


def _decay(lf, blk=256):
    b, seq, _ = lf.shape
    tri = jnp.asarray(np.arange(blk)[:, None] >= np.arange(blk)[None, :], F32)
    return pl.pallas_call(
        functools.partial(_decay_kernel, blk=blk),
        grid=(b,),
        in_specs=[pl.BlockSpec((1, seq, LANES), lambda i: (i, 0, 0)),
                  pl.BlockSpec((blk, blk), lambda i: (0, 0))],
        out_specs=pl.BlockSpec((1, seq, LANES), lambda i: (i, 0, 0)),
        out_shape=jax.ShapeDtypeStruct((b, seq, LANES), F32),
        compiler_params=_params("arbitrary"),
        name="fox_decay",
    )(lf, tri)


def _head_masks():
    lane = lax.broadcasted_iota(jnp.int32, (1, PAIR), 1)
    return lane < HEAD_DIM, lane >= HEAD_DIM


def _split_heads(q):
    return [jnp.where(m, q, jnp.zeros_like(q)) for m in _head_masks()]


def _feat_base(hh):
    return HEAD_DIM * (1 - hh)


def _three_bf16(x):
    hi = x.astype(BF16).astype(F32)
    mid = (x - hi).astype(BF16).astype(F32)
    lo = (x - hi - mid).astype(BF16).astype(F32)
    return hi, mid, lo


def _place(idx, base, parts):
    out = jnp.zeros(jnp.broadcast_shapes(idx.shape, jnp.shape(parts[0])), F32)
    for r, part in enumerate(parts):
        out = jnp.where(idx == base + r, part, out)
    return out


def _flash_pair(c, qx, kx_scr, v_ref, o_ref, keep_own, own_first, m_scr, acc_scr, p_scr, a_scr, tq):
    _reset(m_scr, acc_scr)

    def scores(j, own, slot):
        for hh in range(2):
            s = lax.dot_general(qx[hh], kx_scr[hh, j * tq:(j + 1) * tq, :], _NT,
                                preferred_element_type=F32)
            if own:
                s = jnp.where(keep_own, s, NEG_INF)
            _softmax_stage(s, m_scr.at[hh], p_scr.at[slot, hh], a_scr.at[slot, hh])

    def values(j, slot):
        vs = _with_ones(v_ref[0, j * tq:(j + 1) * tq, :])
        for hh in range(2):
            _value_stage(vs[hh], acc_scr.at[hh], p_scr.at[slot, hh], a_scr.at[slot, hh])

    order = [c] + list(range(c)) if own_first else list(range(c + 1))
    scores(order[0], order[0] == c, 0)
    for n in range(1, len(order)):
        scores(order[n], order[n] == c, n % 2)
        values(order[n - 1], (n - 1) % 2)
    values(order[-1], (len(order) - 1) % 2)
    o_ref[0, c * tq:(c + 1) * tq, :] = _merge_pair(acc_scr)


def _with_ones(v):
    return [jnp.where(m, v, jnp.ones_like(v)) for m in _head_masks()]


def _softmax_stage(s, m_ref, p_ref, a_ref):
    m_prev = m_ref[...]
    m_new = jnp.maximum(m_prev, jnp.max(s, axis=-1, keepdims=True))
    p = jnp.exp2(s - jnp.concatenate([m_new] * (s.shape[1] // LANES), axis=1))
    m_ref[...] = m_new
    p_ref[...] = p.astype(BF16)
    a_ref[...] = jnp.exp2(m_prev - m_new)


def _value_stage(v_ones, acc_ref, p_ref, a_ref):
    acc_ref[...] = a_ref[...] * acc_ref[...] + jnp.dot(p_ref[...], v_ones, preferred_element_type=F32)


def _reset(m_scr, acc_scr):
    m_scr[...] = jnp.full(m_scr.shape, NEG_INF, F32)
    acc_scr[...] = jnp.zeros(acc_scr.shape, F32)


def _merge_pair(acc_scr):
    first = _head_masks()[0]
    a0, a1 = acc_scr[0], acc_scr[1]
    return jnp.where(first, a0 / a0[:, HEAD_DIM:HEAD_DIM + 1], a1 / a1[:, 0:1]).astype(BF16)


def _stat_scratch(tq):
    return [pltpu.VMEM((2, tq, LANES), F32), pltpu.VMEM((2, tq, PAIR), F32),
            pltpu.VMEM((2, 2, tq, tq), BF16), pltpu.VMEM((2, 2, tq, LANES), F32)]


def _fox_kernel(q_ref, k_ref, v_ref, cum_ref, o_ref, kx_scr, m_scr, acc_scr, p_scr, a_scr, *, tq):
    pair = pl.program_id(1)
    masks = _head_masks()
    lane = lax.broadcasted_iota(jnp.int32, (1, PAIR), 1)
    k = k_ref[0]
    cum = cum_ref[0]
    lane_s = lax.broadcasted_iota(jnp.int32, cum.shape, 1)
    for hh in range(2):
        f_s = jnp.sum(jnp.where(lane_s == 2 * pair + hh, cum, 0.0), axis=-1, keepdims=True)
        feats = _place(lane, _feat_base(hh), _three_bf16(-LOG2E * f_s))
        kx_scr[hh] = jnp.where(masks[hh], k, feats.astype(BF16))

    causal = (lax.broadcasted_iota(jnp.int32, (tq, tq), 1)
              <= lax.broadcasted_iota(jnp.int32, (tq, tq), 0))
    for c in range(q_ref.shape[1] // tq):
        q = q_ref[0, c * tq:(c + 1) * tq, :]
        qx = []
        for hh in range(2):
            ones = (lane >= _feat_base(hh)) & (lane < _feat_base(hh) + 3)
            qx.append(jnp.where(masks[hh], q, jnp.where(ones, 1.0, 0.0).astype(BF16)))
        _flash_pair(c, qx, kx_scr, v_ref, o_ref, causal, False, m_scr, acc_scr, p_scr, a_scr, tq)


def _fox(q, k, v, cum, col0, n_pairs, tq):
    b, seq, _ = q.shape
    head_cols = pl.BlockSpec((1, seq, PAIR), lambda bi, p: (bi, 0, col0 + p))
    return pl.pallas_call(
        functools.partial(_fox_kernel, tq=tq),
        grid=(b, n_pairs),
        in_specs=[head_cols, head_cols, head_cols,
                  pl.BlockSpec((1, seq, LANES), lambda bi, p: (bi, 0, 0))],
        out_specs=pl.BlockSpec((1, seq, PAIR), lambda bi, p: (bi, 0, p)),
        out_shape=jax.ShapeDtypeStruct((b, seq, n_pairs * PAIR), BF16),
        scratch_shapes=[pltpu.VMEM((2, seq, PAIR), BF16)] + _stat_scratch(tq),
        compiler_params=_params("arbitrary", "arbitrary"),
        name="fox_attn",
    )(q, k, v, cum)


MOBA_NBLK = 8
ALIBI_PARTS = 5


def _moba_feature_tables(slopes, seq):
    pos = np.arange(seq, dtype=np.float64)

    def parts(x):
        out, rest = [], x.copy()
        for _ in range(ALIBI_PARTS):
            p = rest.astype(np.float32).astype(jnp.bfloat16).astype(np.float64)
            out.append(p)
            rest = rest - p
        assert not rest.any()
        return out

    n_heads = len(slopes)
    qf = np.zeros((n_heads // 2, 2, seq, PAIR), np.float32)
    kf = np.zeros((n_heads // 2, 2, seq, PAIR), np.float32)
    for h, slope in enumerate(slopes):
        base = _feat_base(h % 2)
        q_t, k_t = qf[h // 2, h % 2], kf[h // 2, h % 2]
        k_t[np.arange(seq), base + np.arange(seq) // MOBA_BLOCK] = 1.0
        lo, mid, hi = base + MOBA_NBLK, base + MOBA_NBLK + ALIBI_PARTS, base + MOBA_NBLK + 2 * ALIBI_PARTS
        q_t[:, lo:mid] = 1.0
        slope2 = np.float64(np.float32(slope * LOG2E))
        k_t[:, lo:mid] = np.stack(parts(slope2 * pos), axis=1)
        q_t[:, mid:hi] = np.stack(parts(-slope2 * pos), axis=1)
        k_t[:, mid:hi] = 1.0
    return jnp.asarray(qf, BF16), jnp.asarray(kf, BF16)


def _moba_kernel(q_ref, k_ref, v_ref, qf_ref, kf_ref, o_ref, km_scr, kx_scr, m_scr, acc_scr, p_scr,
                 a_scr, *, tq):
    blk = MOBA_BLOCK
    nbp = MOBA_NBLK
    n_blk = k_ref.shape[1] // blk
    per = tq // blk
    masks = _head_masks()

    km_scr[...] = jnp.zeros(km_scr.shape, F32)
    for n in range(n_blk):
        kb = k_ref[0, n * blk:(n + 1) * blk, :].astype(F32)
        km_scr[n:n + 1, :] = jnp.sum(kb, axis=0, keepdims=True) * (1.0 / blk)
    for hh in range(2):
        kx_scr[hh] = jnp.where(masks[hh], k_ref[0], kf_ref[0, hh])

    blk_n = lax.broadcasted_iota(jnp.int32, (nbp, tq), 0)
    lane = lax.broadcasted_iota(jnp.int32, (1, PAIR), 1)
    r = lax.broadcasted_iota(jnp.int32, (tq, tq), 0)
    s = lax.broadcasted_iota(jnp.int32, (tq, tq), 1)
    keep_own = (r // blk != s // blk) | (s <= r)

    for c in range(q_ref.shape[1] // tq):
        q = q_ref[0, c * tq:(c + 1) * tq, :]
        qs = _split_heads(q)
        q_blk = c * per + lax.broadcasted_iota(jnp.int32, (nbp, tq), 1) // blk
        qx = []
        for hh in range(2):
            base = _feat_base(hh)
            g = lax.dot_general(km_scr[0:nbp, :], qs[hh].astype(F32), _NT,
                                precision=lax.Precision.HIGHEST, preferred_element_type=F32)
            cnt = jnp.zeros((nbp, tq), F32)
            for m in range(n_blk):
                gm = g[m:m + 1, :]
                beats = ((gm > g) | ((gm == g) & (blk_n > m))) & (q_blk > m)
                cnt = cnt + jnp.where(beats, 1.0, 0.0)
            visible = ((blk_n < q_blk) & (cnt < float(MOBA_TOPK))) | (blk_n == q_blk)
            hide = jnp.where(visible, 0.0, NEG_INF)
            pads = [jnp.zeros((n, tq), F32) for n in (base, LANES - nbp - base)]
            hide = jnp.concatenate([a for a in (pads[0], hide, pads[1]) if a.shape[0]], axis=0)
            feats = jnp.where((lane >= base) & (lane < base + nbp), hide.T.astype(BF16),
                              qf_ref[0, hh, c * tq:(c + 1) * tq, :])
            qx.append(jnp.where(masks[hh], q, feats))
        _flash_pair(c, qx, kx_scr, v_ref, o_ref, keep_own, True, m_scr, acc_scr, p_scr, a_scr, tq)


def _moba(q, k, v, col0, n_pairs, tq):
    b, seq, _ = q.shape
    blk = MOBA_BLOCK
    assert seq % tq == 0 and tq % blk == 0 and seq // blk <= MOBA_NBLK
    slopes = _alibi_slopes(N_ALIBI)[MOBA_SLOPE_OFFSET:MOBA_SLOPE_OFFSET + 2 * n_pairs]
    q_feats, k_feats = _moba_feature_tables(slopes, seq)
    head_cols = pl.BlockSpec((1, seq, PAIR), lambda bi, p: (bi, 0, col0 + p))
    table = pl.BlockSpec((1, 2, seq, PAIR), lambda bi, p: (p, 0, 0, 0))
    return pl.pallas_call(
        functools.partial(_moba_kernel, tq=tq),
        grid=(b, n_pairs),
        in_specs=[head_cols, head_cols, head_cols, table, table],
        out_specs=pl.BlockSpec((1, seq, PAIR), lambda bi, p: (bi, 0, p)),
        out_shape=jax.ShapeDtypeStruct((b, seq, n_pairs * PAIR), BF16),
        scratch_shapes=[pltpu.VMEM((LANES, PAIR), F32), pltpu.VMEM((2, seq, PAIR), BF16)]
        + _stat_scratch(tq),
        compiler_params=_params("arbitrary", "arbitrary"),
        name="moba_attn",
    )(q, k, v, q_feats, k_feats)


def _dil_kernel(q_ref, k_ref, v_ref, o_ref, lse_ref, *, dil, slopes, n_q):
    wb = WIN_BLOCK
    seq = q_ref.shape[1]
    nb = seq // wb
    steps = (lax.broadcasted_iota(jnp.int32, (wb, 2 * wb), 0) + wb
             - lax.broadcasted_iota(jnp.int32, (wb, 2 * wb), 1))
    band_ok = (steps >= 0) & (steps <= wb)
    steps_f = (steps * dil).astype(F32)
    blk = lax.broadcasted_iota(jnp.int32, (nb, 1, 2 * wb), 0)
    key = lax.broadcasted_iota(jnp.int32, (nb, 1, 2 * wb), 2)
    no_prev = jnp.where((blk % n_q == 0) & (key < wb), -NEG_INF, 0.0)
    first = _head_masks()[0]

    def band(ref, cols):
        cur = ref[0, :, cols]
        prev = jnp.concatenate([cur[0:wb], cur[0:seq - wb]], axis=0)
        return jnp.concatenate([prev.reshape(nb, wb, PAIR), cur.reshape(nb, wb, PAIR)], axis=1)

    for pair in range(2):
        cols = slice(pair * PAIR, (pair + 1) * PAIR)
        qs = [qh.reshape(nb, wb, PAIR) for qh in _split_heads(q_ref[0, :, cols])]
        kband, vband = band(k_ref, cols), band(v_ref, cols)
        outs, lses = [], []
        for hh in range(2):
            bias = jnp.where(band_ok, (slopes[2 * pair + hh] * LOG2E) * steps_f, -NEG_INF)
            s = jnp.einsum("bqd,bkd->bqk", qs[hh], kband, preferred_element_type=F32)
            s = s - bias[None] - no_prev
            m = jnp.max(s, axis=-1, keepdims=True)
            p = jnp.exp2(s - m)
            l = jnp.sum(p, axis=-1, keepdims=True)
            o = jnp.einsum("bqk,bkd->bqd", p.astype(BF16), vband, preferred_element_type=F32)
            outs.append(o / l)
            lses.append(jnp.broadcast_to(m + LOG2E * jnp.log(l), (nb, wb, PAIR)))
        o_ref[0, :, cols] = jnp.where(first, outs[0], outs[1]).reshape(seq, PAIR).astype(BF16)
        lse_ref[0, :, cols] = jnp.where(first, lses[0], lses[1]).reshape(seq, PAIR)


def _dilated(q, k, v, group):
    window, dil = DILATED_PAIRS[group]
    assert window // dil == WIN_BLOCK
    b, _, l_sub, w = q.shape
    seq = dil * l_sub
    so = DIL_SLOPE_OFFSETS[group]
    slopes = tuple(float(s) for s in _alibi_slopes(N_ALIBI)[so:so + N_SLOTS_DIL])
    spec = pl.BlockSpec((1, seq, w), lambda bi: (bi, 0, 0))
    flat = lambda a: a.reshape(b, seq, w)
    o, lse = pl.pallas_call(
        functools.partial(_dil_kernel, dil=dil, slopes=slopes, n_q=l_sub // WIN_BLOCK),
        grid=(b,),
        in_specs=[spec, spec, spec],
        out_specs=[spec, spec],
        out_shape=[jax.ShapeDtypeStruct((b, seq, w), BF16), jax.ShapeDtypeStruct((b, seq, w), F32)],
        compiler_params=_params("arbitrary"),
        name=f"dilated_attn_{dil}",
    )(flat(q), flat(k), flat(v))
    return o.reshape(q.shape), lse.reshape(q.shape)


def _outproj_kernel(x_ref, mod_ref, g_ref, of_ref, om_ref, o1, l1, o2, l2, o3, l3, gate_ref,
                    wf_ref, wm_ref, wd_ref, wo_ref, x1_ref, h2_ref, perm_scr):
    d = x_ref.shape[1]

    def natural(ref):
        dil, n = ref.shape[1], ref.shape[2]
        if dil == 1:
            return ref[0, 0].astype(F32)
        halves = ref.shape[3] // LANES
        for r in range(dil):
            blk = ref[0, r].astype(F32)
            for c in range(halves):
                perm_scr[c, pl.ds(r, n, stride=dil), :] = blk[:, c * LANES:(c + 1) * LANES]
        return jnp.concatenate([perm_scr[c] for c in range(halves)], axis=1)

    lses = [natural(l) for l in (l1, l2, l3)]
    lmax = jnp.maximum(jnp.maximum(lses[0], lses[1]), lses[2])
    e1, e2, e3 = [jnp.exp2(l - lmax) for l in lses]
    den = e1 + e2 + e3
    o_dil = (e1 / den) * natural(o1) + (e2 / den) * natural(o2) + (e3 / den) * natural(o3)
    y = (gate_ref[:, 0:d].astype(F32) * jnp.dot(of_ref[...], wf_ref[...], preferred_element_type=F32)
         + gate_ref[:, d:2 * d].astype(F32) * jnp.dot(om_ref[...], wm_ref[...], preferred_element_type=F32)
         + gate_ref[:, 2 * d:3 * d].astype(F32)
         * jnp.dot(o_dil.astype(BF16), wd_ref[...], preferred_element_type=F32))
    out = jnp.dot(y.astype(BF16), wo_ref[...], preferred_element_type=F32)
    x1 = x_ref[...] + mod_ref[0, 2:3, :] * out
    x1_ref[...] = x1
    h2_ref[...] = _mod_norm(x1, g_ref[...], mod_ref[0, 4:5, :], mod_ref[0, 3:4, :]).astype(BF16)


def _outproj(x2, mod_l, g_norm, o_fox, o_moba, dil_outs, gates, w_f, w_m, w_d, w_o, seq, tm):
    t, d = x2.shape
    per_b = seq // tm
    row = lambda i: (i, 0)
    fix = lambda i: (0, 0)
    dil_args, dil_specs = [], []
    for o, lse in dil_outs:
        _, dil, _, w = o.shape
        dil_args += [o, lse]
        dil_specs += [pl.BlockSpec((1, dil, tm // dil, w), lambda i: (i // per_b, 0, i % per_b, 0))] * 2
    return pl.pallas_call(
        _outproj_kernel,
        grid=(t // tm,),
        in_specs=[pl.BlockSpec((tm, d), row),
                  pl.BlockSpec((1, 6, d), lambda i: (i // per_b, 0, 0)),
                  pl.BlockSpec((1, d), fix),
                  pl.BlockSpec((tm, o_fox.shape[1]), row),
                  pl.BlockSpec((tm, o_moba.shape[1]), row)] + dil_specs + [
                  pl.BlockSpec((tm, N_BRANCH * d), row),
                  pl.BlockSpec(w_f.shape, fix), pl.BlockSpec(w_m.shape, fix),
                  pl.BlockSpec(w_d.shape, fix), pl.BlockSpec(w_o.shape, fix)],
        out_specs=[pl.BlockSpec((tm, d), row), pl.BlockSpec((tm, d), row)],
        out_shape=[jax.ShapeDtypeStruct((t, d), F32), jax.ShapeDtypeStruct((t, d), BF16)],
        scratch_shapes=[pltpu.VMEM((N_SLOTS_DIL * HEAD_DIM // LANES, tm, LANES), F32)],
        compiler_params=_params("arbitrary"),
        name="outproj",
    )(x2, mod_l, g_norm, o_fox, o_moba, *dil_args, gates, w_f, w_m, w_d, w_o)


def _router_kernel(h_ref, w_ref, b_ref, o_ref, cnt_ref, *, n_exp):
    logits = jnp.dot(h_ref[...], w_ref[...], preferred_element_type=F32) + b_ref[...]
    lane = lax.broadcasted_iota(jnp.int32, logits.shape, 1)
    logits = jnp.where(lane < n_exp, logits, -jnp.inf)

    def top(vals):
        m = jnp.max(vals, axis=-1, keepdims=True)
        idx = jnp.min(jnp.where(vals == m, lane, LANES), axis=-1, keepdims=True)
        return m, lane == idx

    m1, hot1 = top(logits)
    m2, hot2 = top(jnp.where(hot1, -jnp.inf, logits))
    e2 = jnp.exp(m2 - m1)
    cw = jnp.where(hot1, 1.0 / (1.0 + e2), 0.0) + jnp.where(hot2, e2 / (1.0 + e2), 0.0)
    o_ref[...] = cw
    sub = cnt_ref.shape[0] // 8
    rows = cw.shape[0] // sub
    for s in range(sub):
        n_tok = jnp.sum(jnp.where(cw[s * rows:(s + 1) * rows] > 0.0, 1.0, 0.0), axis=0, keepdims=True)
        cnt_ref[s * 8:(s + 1) * 8, :] = jnp.broadcast_to(n_tok, (8, LANES)).astype(jnp.int32)


def _router(h2, w_r, b_r, n_exp, tile):
    t, d = h2.shape
    tm = 4 * tile
    cw, cnt = pl.pallas_call(
        functools.partial(_router_kernel, n_exp=n_exp),
        grid=(t // tm,),
        in_specs=[pl.BlockSpec((tm, d), lambda i: (i, 0)),
                  pl.BlockSpec((d, LANES), lambda i: (0, 0)),
                  pl.BlockSpec((1, LANES), lambda i: (0, 0))],
        out_specs=[pl.BlockSpec((tm, LANES), lambda i: (i, 0)),
                   pl.BlockSpec((8 * tm // tile, LANES), lambda i: (i, 0))],
        out_shape=[jax.ShapeDtypeStruct((t, LANES), F32),
                   jax.ShapeDtypeStruct((t // tile * 8, LANES), jnp.int32)],
        compiler_params=_params("arbitrary"),
        name="router",
    )(h2, w_r, b_r)
    return cw, cnt.reshape(t // tile, 8, LANES)[:, 0, :n_exp]


MOE_TILE = 512
MOE_ROWS = 512
MOE_PUT = 128
MOE_GET = 256
MOE_ALIGN = 16


def _moe_layout(cnt, n_rows_static):
    n_tiles, n_exp = cnt.shape
    padded = (cnt + MOE_ALIGN - 1) // MOE_ALIGN * MOE_ALIGN
    length = jnp.sum(padded, axis=0)
    span = (length + (MOE_PUT - MOE_ALIGN) + MOE_ROWS - 1) // MOE_ROWS * MOE_ROWS
    start = jnp.cumsum(span) - span
    off = start[None, :] + jnp.cumsum(padded, axis=0) - padded
    n_steps = n_rows_static // MOE_ROWS
    first = start // MOE_ROWS
    step = jnp.arange(n_steps, dtype=jnp.int32)
    expert = jnp.sum((first[None, :] <= step[:, None]).astype(jnp.int32), axis=1) - 1
    active = step < (first + (length + MOE_ROWS - 1) // MOE_ROWS)[expert]
    return (off.reshape(-1).astype(jnp.int32), cnt.reshape(-1).astype(jnp.int32),
            expert.astype(jnp.int32), active.astype(jnp.int32))


def _dispatch_kernel(off_ref, cnt_ref, h_ref, cw_ref, u_ref, xs_in, xs_ref, stage, sems, *, n_exp):
    del xs_in
    i = pl.program_id(0)
    n_put = MOE_TILE // MOE_PUT
    routed = jnp.where(cw_ref[...].T[0:8] > 0.0, 1.0, 0.0)
    pos = jnp.dot(routed.astype(BF16), u_ref[...], preferred_element_type=F32)
    row = lax.broadcasted_iota(jnp.int32, (MOE_PUT, MOE_TILE), 0).astype(F32)
    h = h_ref[...]

    def copy(e, s):
        slot = e * n_put + s
        dst = pl.multiple_of(off_ref[i * n_exp + e] + s * MOE_PUT, MOE_ALIGN)
        return pltpu.make_async_copy(stage.at[slot], xs_ref.at[pl.ds(dst, MOE_PUT), :], sems.at[slot])

    for e in range(n_exp):
        for s in range(n_put):
            @pl.when(s * MOE_PUT < cnt_ref[i * n_exp + e])
            def _():
                take = (routed[e:e + 1, :] > 0.5) & (pos[e:e + 1, :] == row + float(s * MOE_PUT))
                onehot = jnp.where(take, 1.0, 0.0).astype(BF16)
                stage[e * n_put + s] = jnp.dot(onehot, h, preferred_element_type=F32).astype(BF16)
                copy(e, s).start()

    for e in range(n_exp):
        for s in range(n_put):
            @pl.when(s * MOE_PUT < cnt_ref[i * n_exp + e])
            def _():
                copy(e, s).wait()


def _dispatch(h2, cw, off, cnt, n_exp, n_rows):
    t, d = h2.shape
    u = jnp.asarray(np.arange(MOE_TILE)[:, None] < np.arange(MOE_TILE)[None, :], BF16)
    n_slots = n_exp * (MOE_TILE // MOE_PUT)
    return pl.pallas_call(
        functools.partial(_dispatch_kernel, n_exp=n_exp),
        grid_spec=pltpu.PrefetchScalarGridSpec(
            num_scalar_prefetch=2,
            grid=(t // MOE_TILE,),
            in_specs=[pl.BlockSpec((MOE_TILE, d), lambda i, o, c: (i, 0)),
                      pl.BlockSpec((MOE_TILE, LANES), lambda i, o, c: (i, 0)),
                      pl.BlockSpec((MOE_TILE, MOE_TILE), lambda i, o, c: (0, 0)),
                      pl.BlockSpec(memory_space=pl.ANY)],
            out_specs=pl.BlockSpec(memory_space=pl.ANY),
            scratch_shapes=[pltpu.VMEM((n_slots, MOE_PUT, d), BF16),
                            pltpu.SemaphoreType.DMA((n_slots,))]),
        out_shape=jax.ShapeDtypeStruct((n_rows, d), BF16),
        input_output_aliases={5: 0},
        compiler_params=_params("arbitrary"),
        name="moe_dispatch",
    )(off, cnt, h2, cw, u, jnp.zeros((n_rows, d), BF16))


FF_CHUNK = 768


def _swiglu(x, wg_ref, wu_ref, wd_ref):
    ff = wg_ref.shape[2]
    y = None
    for c0 in range(0, ff, FF_CHUNK):
        c1 = min(c0 + FF_CHUNK, ff)
        gate = jnp.dot(x, wg_ref[0, :, c0:c1], preferred_element_type=F32)
        up = jnp.dot(x, wu_ref[0, :, c0:c1], preferred_element_type=F32)
        a = ((gate * _sigmoid(gate)) * up).astype(BF16)
        part = jnp.dot(a, wd_ref[0, c0:c1, :], preferred_element_type=F32)
        y = part if y is None else y + part
    return y


def _experts_kernel(exp_ref, act_ref, xs_ref, wg_ref, wu_ref, wd_ref, ys_ref):
    g = pl.program_id(0)

    @pl.when(act_ref[g] > 0)
    def _():
        ys_ref[...] = _swiglu(xs_ref[...], wg_ref, wu_ref, wd_ref).astype(BF16)

    @pl.when(act_ref[g] == 0)
    def _():
        ys_ref[...] = jnp.zeros(ys_ref.shape, BF16)


def _experts(xs, expert, active, w_g, w_u, w_d):
    n_rows, d = xs.shape
    _, _, ff = w_g.shape
    wmap = lambda g, ex, ac: (ex[g], 0, 0)
    return pl.pallas_call(
        _experts_kernel,
        grid_spec=pltpu.PrefetchScalarGridSpec(
            num_scalar_prefetch=2,
            grid=(n_rows // MOE_ROWS,),
            in_specs=[pl.BlockSpec((MOE_ROWS, d), lambda g, ex, ac: (g, 0)),
                      pl.BlockSpec((1, d, ff), wmap), pl.BlockSpec((1, d, ff), wmap),
                      pl.BlockSpec((1, ff, d), wmap)],
            out_specs=pl.BlockSpec((MOE_ROWS, d), lambda g, ex, ac: (g, 0))),
        out_shape=jax.ShapeDtypeStruct((n_rows, d), BF16),
        compiler_params=_params("arbitrary"),
        name="moe_experts",
    )(expert, active, xs, w_g, w_u, w_d)


def _combine_kernel(off_ref, cnt_ref, x_ref, mod_ref, cw_ref, l_ref, ys_ref, o_ref, buf, sems, acc_scr,
                    *, n_exp):
    i = pl.program_id(0)
    n_get = MOE_TILE // MOE_GET

    def copy(e, s):
        slot = e * n_get + s
        src = pl.multiple_of(off_ref[i * n_exp + e] + s * MOE_GET, MOE_ALIGN)
        return pltpu.make_async_copy(ys_ref.at[pl.ds(src, MOE_GET), :], buf.at[slot], sems.at[slot])

    for e in range(n_exp):
        for s in range(n_get):
            @pl.when(s * MOE_GET < cnt_ref[i * n_exp + e])
            def _():
                copy(e, s).start()

    cw = cw_ref[...]
    routed = jnp.where(cw > 0.0, 1.0, 0.0)
    pos = jnp.dot(l_ref[...], routed.astype(BF16), preferred_element_type=F32)
    col = lax.broadcasted_iota(jnp.int32, (MOE_TILE, MOE_GET), 1).astype(F32)
    acc_scr[...] = jnp.zeros(acc_scr.shape, F32)
    for e in range(n_exp):
        for s in range(n_get):
            @pl.when(s * MOE_GET < cnt_ref[i * n_exp + e])
            def _():
                copy(e, s).wait()
                take = (cw[:, e:e + 1] > 0.0) & (pos[:, e:e + 1] == col + float(s * MOE_GET))
                onehot = jnp.where(take, 1.0, 0.0).astype(BF16)
                acc_scr[...] += cw[:, e:e + 1] * jnp.dot(onehot, buf[e * n_get + s],
                                                         preferred_element_type=F32)
    o_ref[...] = x_ref[...] + mod_ref[0, 5:6, :] * acc_scr[...]


def _combine(x1, mod_l, cw, ys, off, cnt, n_exp, seq):
    t, d = x1.shape
    per_b = seq // MOE_TILE
    low = jnp.asarray(np.arange(MOE_TILE)[:, None] > np.arange(MOE_TILE)[None, :], BF16)
    n_slots = n_exp * (MOE_TILE // MOE_GET)
    return pl.pallas_call(
        functools.partial(_combine_kernel, n_exp=n_exp),
        grid_spec=pltpu.PrefetchScalarGridSpec(
            num_scalar_prefetch=2,
            grid=(t // MOE_TILE,),
            in_specs=[pl.BlockSpec((MOE_TILE, d), lambda i, o, c: (i, 0)),
                      pl.BlockSpec((1, 6, d), lambda i, o, c: (i // per_b, 0, 0)),
                      pl.BlockSpec((MOE_TILE, LANES), lambda i, o, c: (i, 0)),
                      pl.BlockSpec((MOE_TILE, MOE_TILE), lambda i, o, c: (0, 0)),
                      pl.BlockSpec(memory_space=pl.ANY)],
            out_specs=pl.BlockSpec((MOE_TILE, d), lambda i, o, c: (i, 0)),
            scratch_shapes=[pltpu.VMEM((n_slots, MOE_GET, d), BF16),
                            pltpu.SemaphoreType.DMA((n_slots,)),
                            pltpu.VMEM((MOE_TILE, d), F32)]),
        out_shape=jax.ShapeDtypeStruct((t, d), F32),
        compiler_params=_params("arbitrary"),
        name="moe_combine",
    )(off, cnt, x1, mod_l, cw, low, ys)


def _moe(x1, h2, mod_l, w_r, b_r, w_g, w_u, w_d, seq):
    t, d = x1.shape
    n_exp = w_g.shape[0]
    n_tiles = t // MOE_TILE
    bound = (TOP_K * t + n_tiles * n_exp * (MOE_ALIGN - 1)
             + n_exp * (MOE_PUT - MOE_ALIGN + MOE_ROWS - 1))
    n_rows = (bound + MOE_ROWS - 1) // MOE_ROWS * MOE_ROWS + MOE_ROWS
    cw, cnt = _router(h2, w_r, b_r, n_exp, MOE_TILE)
    off, cnt, expert, active = _moe_layout(cnt, n_rows)
    xs = _dispatch(h2, cw, off, cnt, n_exp, n_rows)
    ys = _experts(xs, expert, active, w_g, w_u, w_d)
    return _combine(x1, mod_l, cw, ys, off, cnt, n_exp, seq)


def _ffn_kernel(x_ref, h_ref, mod_ref, wg_ref, wu_ref, wd_ref, o_ref):
    o_ref[...] = x_ref[...] + mod_ref[0, 5:6, :] * _swiglu(h_ref[...], wg_ref, wu_ref, wd_ref)


def _ffn(x1, h2, mod_l, w_g, w_u, w_d, seq, tm):
    t, d = x1.shape
    per_b = seq // tm
    row = lambda i: (i, 0)
    whole = lambda w: pl.BlockSpec(w.shape, lambda i: (0, 0, 0), pipeline_mode=pl.Buffered(1))
    return pl.pallas_call(
        _ffn_kernel,
        grid=(t // tm,),
        in_specs=[pl.BlockSpec((tm, d), row), pl.BlockSpec((tm, d), row),
                  pl.BlockSpec((1, 6, d), lambda i: (i // per_b, 0, 0)),
                  whole(w_g), whole(w_u), whole(w_d)],
        out_specs=pl.BlockSpec((tm, d), row),
        out_shape=jax.ShapeDtypeStruct((t, d), F32),
        compiler_params=_params("arbitrary"),
        name="swiglu",
    )(x1, h2, mod_l, w_g, w_u, w_d)


def _pad_cols(a, width):
    return jnp.pad(a, ((0, 0), (0, width - a.shape[1])))


def kernel(x, c, w_ada, b_ada, norm_mix, norm_ffn, w_in, b_fgate, q_gain, k_gain, w_br_fox, w_br_moba,
           w_br_dil, w_out, w_ffn_gate, w_ffn_up, w_ffn_down, w_router, b_router, w_exp_gate,
           w_exp_up, w_exp_down):
    b, seq, d = x.shape
    depth = w_ada.shape[0]
    t = b * seq
    tm, tq = TOKEN_TILE, ATTN_CHUNK
    n_pairs_fox = N_HEADS_FOX // 2
    n_pairs_moba = N_HEADS_MOBA // 2

    mod = _adaln(c, w_ada, b_ada).reshape(depth, b, 6, d)
    x2 = x.reshape(t, d)
    w_in_t = jnp.transpose(w_in, (2, 0, 1))
    for l in range(depth):
        wl = w_in_t[:, l, :].astype(BF16)
        f0 = 3 * MIX_WIDTH
        w_qkv, w_gate = wl[:f0], wl[f0 + N_HEADS_FOX:]
        w_f = jnp.pad(wl[f0:f0 + N_HEADS_FOX], ((0, LANES - N_HEADS_FOX), (0, 0)))
        b_f = _pad_cols(b_fgate[l].reshape(1, -1), LANES)
        outs = _inproj(x2, mod[l], norm_mix[l].reshape(1, d), w_qkv, w_gate, w_f, q_gain[l].reshape(1, -1),
                       k_gain[l].reshape(1, -1), b_f, seq, tm)
        q_m, k_m, v_m = (o.reshape(b, seq, -1) for o in outs[0:3])
        gates, lf = outs[12], outs[13]

        cum = _decay(lf.reshape(b, seq, LANES))
        o_fox = _fox(q_m, k_m, v_m, cum, 0, n_pairs_fox, tq).reshape(t, -1)
        o_moba = _moba(q_m, k_m, v_m, n_pairs_fox, n_pairs_moba, tq).reshape(t, -1)
        dil_outs = [_dilated(*outs[3 + 3 * g:6 + 3 * g], g) for g in range(len(DILATED_PAIRS))]

        x1, h2 = _outproj(x2, mod[l], norm_ffn[l].reshape(1, d), o_fox, o_moba, dil_outs, gates,
                          w_br_fox[l].astype(BF16), w_br_moba[l].astype(BF16),
                          w_br_dil[l].astype(BF16), w_out[l].astype(BF16), seq, tm)
        i = l // 2
        if l % 2 == 0:
            x2 = _ffn(x1, h2, mod[l], w_ffn_gate[i:i + 1].astype(BF16),
                      w_ffn_up[i:i + 1].astype(BF16), w_ffn_down[i:i + 1].astype(BF16), seq, FFN_TILE)
        else:
            x2 = _moe(x1, h2, mod[l], _pad_cols(w_router[i], LANES).astype(BF16),
                      _pad_cols(b_router[i].reshape(1, -1), LANES), w_exp_gate[i].astype(BF16),
                      w_exp_up[i].astype(BF16), w_exp_down[i].astype(BF16), seq)
    return x2.reshape(b, seq, d)
```

```python
import functools

import numpy as np
import jax
import jax.numpy as jnp
from jax import lax
from jax.experimental import pallas as pl
from jax.experimental.pallas import tpu as pltpu

HEAD_DIM = 64
N_HEADS_FOX = 6
N_HEADS_MOBA = 6
DILATED_PAIRS = ((128, 1), (512, 4), (2048, 16))
N_SLOTS_DIL = 4
N_HEADS_DIL = N_SLOTS_DIL * len(DILATED_PAIRS)
N_HEADS = N_HEADS_FOX + N_HEADS_MOBA + N_HEADS_DIL
MIX_WIDTH = N_HEADS * HEAD_DIM
N_BRANCH = 3
MOBA_BLOCK = 256
MOBA_TOPK = 3
WIN_BLOCK = 128
N_ALIBI = N_HEADS_MOBA + N_HEADS_DIL
DIL_SLOPE_OFFSETS = (0, N_SLOTS_DIL, 2 * N_SLOTS_DIL + N_HEADS_MOBA)
MOBA_SLOPE_OFFSET = 2 * N_SLOTS_DIL
TOP_K = 2
RMS_EPS = 1e-6
NEG_INF = -1e30
LOG2E = 1.4426950408889634

LANES = 128
PAIR = 2 * HEAD_DIM
VMEM_LIMIT = 56 * 1024 * 1024
TOKEN_TILE = 512
ATTN_CHUNK = 512
FFN_TILE = 1024

F32 = jnp.float32
BF16 = jnp.bfloat16
_NT = (((1,), (1,)), ((), ()))


def _alibi_slopes(n):
    return (2.0 ** (-8.0 * np.arange(1, n + 1) / n)).astype(np.float32)


def _sigmoid(x):
    return 1.0 / (1.0 + jnp.exp(-x))


def _params(*sem):
    return pltpu.CompilerParams(dimension_semantics=sem, vmem_limit_bytes=VMEM_LIMIT)


def _adaln_kernel(c_ref, w_ref, b_ref, o_ref):
    c = c_ref[...]
    cond = c * _sigmoid(c)
    o_ref[0] = jnp.dot(cond, w_ref[0], precision=lax.Precision.HIGHEST,
                       preferred_element_type=F32) + b_ref[0]


def _adaln(c, w_ada, b_ada):
    depth, d, n = w_ada.shape
    b = c.shape[0]
    tn = 1536
    return pl.pallas_call(
        _adaln_kernel,
        grid=(depth, n // tn),
        in_specs=[pl.BlockSpec((b, d), lambda l, j: (0, 0)),
                  pl.BlockSpec((1, d, tn), lambda l, j: (l, 0, j)),
                  pl.BlockSpec((1, 1, tn), lambda l, j: (l, 0, j))],
        out_specs=pl.BlockSpec((1, b, tn), lambda l, j: (l, 0, j)),
        out_shape=jax.ShapeDtypeStruct((depth, b, n), F32),
        compiler_params=_params("arbitrary", "arbitrary"),
        name="adaln",
    )(c, w_ada, b_ada.reshape(depth, 1, n))


def _mod_norm(x, g, scale, shift):
    ms = jnp.mean(x * x, axis=-1, keepdims=True)
    return (x * lax.rsqrt(ms + RMS_EPS) * g) * (1.0 + scale) + shift


def _inproj_kernel(x_ref, mod_ref, g_ref, w_ref, wg_ref, wf_ref, qg_ref, kg_ref, bf_ref, e_ref,
                   q_ref, k_ref, v_ref, qd1, kd1, vd1, qd2, kd2, vd2, qd3, kd3, vd3,
                   gate_ref, lf_ref, perm_scr, *, n_main):
    h = _mod_norm(x_ref[...], g_ref[...], mod_ref[0, 1:2, :], mod_ref[0, 0:1, :]).astype(BF16)
    tm = x_ref.shape[0]
    qk_chunk = 4 * HEAD_DIM
    q_outs = (q_ref, qd1, qd2, qd3)
    k_outs = (k_ref, kd1, kd2, kd3)
    v_outs = (v_ref, vd1, vd2, vd3)

    def put(outs, col, val):
        if col < n_main:
            outs[0][:, col:col + qk_chunk] = val.astype(BF16)
            return
        g = (col - n_main) // qk_chunk
        dil = DILATED_PAIRS[g][1]
        if dil == 1:
            outs[1 + g][0, 0] = val.astype(BF16)
            return
        halves = qk_chunk // LANES
        for c in range(halves):
            perm_scr[c] = val[:, c * LANES:(c + 1) * LANES]
        for r in range(dil):
            outs[1 + g][0, r] = jnp.concatenate(
                [perm_scr[c, pl.ds(r, tm // dil, stride=dil), :] for c in range(halves)],
                axis=1).astype(BF16)

    def proj(wt_ref, col0, width):
        return lax.dot_general(h, wt_ref[col0:col0 + width, :], _NT, preferred_element_type=F32)

    wide = MIX_WIDTH // 2
    for half in range(2):
        yq = proj(w_ref, half * wide, wide)
        yk = proj(w_ref, MIX_WIDTH + half * wide, wide)
        yv = proj(w_ref, 2 * MIX_WIDTH + half * wide, wide)
        for c in range(wide // qk_chunk):
            col = half * wide + c * qk_chunk
            yq_c = yq[:, c * qk_chunk:(c + 1) * qk_chunk]
            yk_c = yk[:, c * qk_chunk:(c + 1) * qk_chunk]
            sq = jnp.concatenate([yq_c * yq_c, yk_c * yk_c], axis=0).astype(BF16)
            ss = jnp.dot(sq, e_ref[...], preferred_element_type=F32)
            r = lax.rsqrt(ss * (1.0 / HEAD_DIM) + RMS_EPS)
            put(q_outs, col, (yq_c * r[0:tm] * qg_ref[:, col:col + qk_chunk])
                * (HEAD_DIM ** -0.5 * LOG2E))
            put(k_outs, col, yk_c * r[tm:2 * tm] * kg_ref[:, col:col + qk_chunk])
            put(v_outs, col, yv[:, c * qk_chunk:(c + 1) * qk_chunk])

    gchunk = 1024
    for c in range(wg_ref.shape[0] // gchunk):
        y = proj(wg_ref, c * gchunk, gchunk)
        gate_ref[:, c * gchunk:(c + 1) * gchunk] = _sigmoid(y).astype(BF16)

    f = proj(wf_ref, 0, LANES) + bf_ref[...]
    lf_ref[...] = jnp.minimum(f, 0.0) - jnp.log(1.0 + jnp.exp(-jnp.abs(f)))


def _inproj(x2, mod_l, g_norm, w_qkv, w_gate, w_f, q_gain, k_gain, b_f, seq, tm):
    t, d = x2.shape
    n_main = (N_HEADS_FOX + N_HEADS_MOBA) * HEAD_DIM
    dil_w = N_SLOTS_DIL * HEAD_DIM
    e = (np.arange(dil_w)[:, None] // HEAD_DIM == np.arange(dil_w)[None, :] // HEAD_DIM)
    e = jnp.asarray(e, BF16)
    per_b = seq // tm
    row = lambda i: (i, 0)
    fix = lambda i: (0, 0)
    qkv_shapes = [jax.ShapeDtypeStruct((t, n_main), BF16)] * 3
    qkv_specs = [pl.BlockSpec((tm, n_main), row)] * 3
    for _, dil in DILATED_PAIRS:
        assert tm % (16 * dil) == 0
        qkv_shapes += [jax.ShapeDtypeStruct((t // seq, dil, seq // dil, dil_w), BF16)] * 3
        qkv_specs += [pl.BlockSpec((1, dil, tm // dil, dil_w),
                                   lambda i: (i // per_b, 0, i % per_b, 0))] * 3
    out_shapes = qkv_shapes + [jax.ShapeDtypeStruct((t, N_BRANCH * d), BF16),
                               jax.ShapeDtypeStruct((t, LANES), F32)]
    out_specs = qkv_specs + [pl.BlockSpec((tm, N_BRANCH * d), row), pl.BlockSpec((tm, LANES), row)]
    return pl.pallas_call(
        functools.partial(_inproj_kernel, n_main=n_main),
        grid=(t // tm,),
        in_specs=[pl.BlockSpec((tm, d), row),
                  pl.BlockSpec((1, 6, d), lambda i: (i // per_b, 0, 0)),
                  pl.BlockSpec((1, d), fix),
                  pl.BlockSpec(w_qkv.shape, fix, pipeline_mode=pl.Buffered(1)),
                  pl.BlockSpec(w_gate.shape, fix, pipeline_mode=pl.Buffered(1)),
                  pl.BlockSpec(w_f.shape, fix),
                  pl.BlockSpec((1, MIX_WIDTH), fix),
                  pl.BlockSpec((1, MIX_WIDTH), fix),
                  pl.BlockSpec((1, LANES), fix),
                  pl.BlockSpec(e.shape, fix)],
        out_specs=out_specs,
        out_shape=out_shapes,
        scratch_shapes=[pltpu.VMEM((dil_w // LANES, tm, LANES), F32)],
        compiler_params=_params("arbitrary"),
        name="inproj",
    )(x2, mod_l, g_norm, w_qkv, w_gate, w_f, q_gain, k_gain, b_f, e)


def _decay_kernel(lf_ref, tri_ref, o_ref, *, blk):
    carry = jnp.zeros((1, LANES), F32)
    for j in range(lf_ref.shape[1] // blk):
        c = jnp.dot(tri_ref[...], lf_ref[0, j * blk:(j + 1) * blk, :],
                    precision=lax.Precision.HIGHEST, preferred_element_type=F32) + carry
        o_ref[0, j * blk:(j + 1) * blk, :] = c
        carry = c[blk - 1:blk, :]


def _decay(lf, blk=256):
    b, seq, _ = lf.shape
    tri = jnp.asarray(np.arange(blk)[:, None] >= np.arange(blk)[None, :], F32)
    return pl.pallas_call(
        functools.partial(_decay_kernel, blk=blk),
        grid=(b,),
        in_specs=[pl.BlockSpec((1, seq, LANES), lambda i: (i, 0, 0)),
                  pl.BlockSpec((blk, blk), lambda i: (0, 0))],
        out_specs=pl.BlockSpec((1, seq, LANES), lambda i: (i, 0, 0)),
        out_shape=jax.ShapeDtypeStruct((b, seq, LANES), F32),
        compiler_params=_params("arbitrary"),
        name="fox_decay",
    )(lf, tri)


def _head_masks():
    lane = lax.broadcasted_iota(jnp.int32, (1, PAIR), 1)
    return lane < HEAD_DIM, lane >= HEAD_DIM


def _split_heads(q):
    return [jnp.where(m, q, jnp.zeros_like(q)) for m in _head_masks()]


def _feat_base(hh):
    return HEAD_DIM * (1 - hh)


def _three_bf16(x):
    hi = x.astype(BF16).astype(F32)
    mid = (x - hi).astype(BF16).astype(F32)
    lo = (x - hi - mid).astype(BF16).astype(F32)
    return hi, mid, lo


def _place(idx, base, parts):
    out = jnp.zeros(jnp.broadcast_shapes(idx.shape, jnp.shape(parts[0])), F32)
    for r, part in enumerate(parts):
        out = jnp.where(idx == base + r, part, out)
    return out


def _flash_pair(c, qx, kx_scr, v_ref, o_ref, keep_own, own_first, m_scr, acc_scr, p_scr, a_scr, tq):
    _reset(m_scr, acc_scr)

    def scores(j, own, slot):
        for hh in range(2):
            s = lax.dot_general(qx[hh], kx_scr[hh, j * tq:(j + 1) * tq, :], _NT,
                                preferred_element_type=F32)
            if own:
                s = jnp.where(keep_own, s, NEG_INF)
            _softmax_stage(s, m_scr.at[hh], p_scr.at[slot, hh], a_scr.at[slot, hh])

    def values(j, slot):
        vs = _with_ones(v_ref[0, j * tq:(j + 1) * tq, :])
        for hh in range(2):
            _value_stage(vs[hh], acc_scr.at[hh], p_scr.at[slot, hh], a_scr.at[slot, hh])

    order = [c] + list(range(c)) if own_first else list(range(c + 1))
    scores(order[0], order[0] == c, 0)
    for n in range(1, len(order)):
        scores(order[n], order[n] == c, n % 2)
        values(order[n - 1], (n - 1) % 2)
    values(order[-1], (len(order) - 1) % 2)
    o_ref[0, c * tq:(c + 1) * tq, :] = _merge_pair(acc_scr)


def _with_ones(v):
    return [jnp.where(m, v, jnp.ones_like(v)) for m in _head_masks()]


def _softmax_stage(s, m_ref, p_ref, a_ref):
    m_prev = m_ref[...]
    m_new = jnp.maximum(m_prev, jnp.max(s, axis=-1, keepdims=True))
    p = jnp.exp2(s - jnp.concatenate([m_new] * (s.shape[1] // LANES), axis=1))
    m_ref[...] = m_new
    p_ref[...] = p.astype(BF16)
    a_ref[...] = jnp.exp2(m_prev - m_new)


def _value_stage(v_ones, acc_ref, p_ref, a_ref):
    acc_ref[...] = a_ref[...] * acc_ref[...] + jnp.dot(p_ref[...], v_ones, preferred_element_type=F32)


def _reset(m_scr, acc_scr):
    m_scr[...] = jnp.full(m_scr.shape, NEG_INF, F32)
    acc_scr[...] = jnp.zeros(acc_scr.shape, F32)


def _merge_pair(acc_scr):
    first = _head_masks()[0]
    a0, a1 = acc_scr[0], acc_scr[1]
    return jnp.where(first, a0 / a0[:, HEAD_DIM:HEAD_DIM + 1], a1 / a1[:, 0:1]).astype(BF16)


def _stat_scratch(tq):
    return [pltpu.VMEM((2, tq, LANES), F32), pltpu.VMEM((2, tq, PAIR), F32),
            pltpu.VMEM((2, 2, tq, tq), BF16), pltpu.VMEM((2, 2, tq, LANES), F32)]


def _fox_kernel(q_ref, k_ref, v_ref, cum_ref, o_ref, kx_scr, m_scr, acc_scr, p_scr, a_scr, *, tq):
    pair = pl.program_id(1)
    masks = _head_masks()
    lane = lax.broadcasted_iota(jnp.int32, (1, PAIR), 1)
    k = k_ref[0]
    cum = cum_ref[0]
    lane_s = lax.broadcasted_iota(jnp.int32, cum.shape, 1)
    for hh in range(2):
        f_s = jnp.sum(jnp.where(lane_s == 2 * pair + hh, cum, 0.0), axis=-1, keepdims=True)
        feats = _place(lane, _feat_base(hh), _three_bf16(-LOG2E * f_s))
        kx_scr[hh] = jnp.where(masks[hh], k, feats.astype(BF16))

    causal = (lax.broadcasted_iota(jnp.int32, (tq, tq), 1)
              <= lax.broadcasted_iota(jnp.int32, (tq, tq), 0))
    for c in range(q_ref.shape[1] // tq):
        q = q_ref[0, c * tq:(c + 1) * tq, :]
        qx = []
        for hh in range(2):
            ones = (lane >= _feat_base(hh)) & (lane < _feat_base(hh) + 3)
            qx.append(jnp.where(masks[hh], q, jnp.where(ones, 1.0, 0.0).astype(BF16)))
        _flash_pair(c, qx, kx_scr, v_ref, o_ref, causal, False, m_scr, acc_scr, p_scr, a_scr, tq)


def _fox(q, k, v, cum, col0, n_pairs, tq):
    b, seq, _ = q.shape
    head_cols = pl.BlockSpec((1, seq, PAIR), lambda bi, p: (bi, 0, col0 + p))
    return pl.pallas_call(
        functools.partial(_fox_kernel, tq=tq),
        grid=(b, n_pairs),
        in_specs=[head_cols, head_cols, head_cols,
                  pl.BlockSpec((1, seq, LANES), lambda bi, p: (bi, 0, 0))],
        out_specs=pl.BlockSpec((1, seq, PAIR), lambda bi, p: (bi, 0, p)),
        out_shape=jax.ShapeDtypeStruct((b, seq, n_pairs * PAIR), BF16),
        scratch_shapes=[pltpu.VMEM((2, seq, PAIR), BF16)] + _stat_scratch(tq),
        compiler_params=_params("arbitrary", "arbitrary"),
        name="fox_attn",
    )(q, k, v, cum)


MOBA_NBLK = 8
ALIBI_PARTS = 5


def _moba_feature_tables(slopes, seq):
    pos = np.arange(seq, dtype=np.float64)

    def parts(x):
        out, rest = [], x.copy()
        for _ in range(ALIBI_PARTS):
            p = rest.astype(np.float32).astype(jnp.bfloat16).astype(np.float64)
            out.append(p)
            rest = rest - p
        assert not rest.any()
        return out

    n_heads = len(slopes)
    qf = np.zeros((n_heads // 2, 2, seq, PAIR), np.float32)
    kf = np.zeros((n_heads // 2, 2, seq, PAIR), np.float32)
    for h, slope in enumerate(slopes):
        base = _feat_base(h % 2)
        q_t, k_t = qf[h // 2, h % 2], kf[h // 2, h % 2]
        k_t[np.arange(seq), base + np.arange(seq) // MOBA_BLOCK] = 1.0
        lo, mid, hi = base + MOBA_NBLK, base + MOBA_NBLK + ALIBI_PARTS, base + MOBA_NBLK + 2 * ALIBI_PARTS
        q_t[:, lo:mid] = 1.0
        slope2 = np.float64(np.float32(slope * LOG2E))
        k_t[:, lo:mid] = np.stack(parts(slope2 * pos), axis=1)
        q_t[:, mid:hi] = np.stack(parts(-slope2 * pos), axis=1)
        k_t[:, mid:hi] = 1.0
    return jnp.asarray(qf, BF16), jnp.asarray(kf, BF16)


def _moba_kernel(q_ref, k_ref, v_ref, qf_ref, kf_ref, o_ref, km_scr, kx_scr, m_scr, acc_scr, p_scr,
                 a_scr, *, tq):
    blk = MOBA_BLOCK
    nbp = MOBA_NBLK
    n_blk = k_ref.shape[1] // blk
    per = tq // blk
    masks = _head_masks()

    km_scr[...] = jnp.zeros(km_scr.shape, F32)
    for n in range(n_blk):
        kb = k_ref[0, n * blk:(n + 1) * blk, :].astype(F32)
        km_scr[n:n + 1, :] = jnp.sum(kb, axis=0, keepdims=True) * (1.0 / blk)
    for hh in range(2):
        kx_scr[hh] = jnp.where(masks[hh], k_ref[0], kf_ref[0, hh])

    blk_n = lax.broadcasted_iota(jnp.int32, (nbp, tq), 0)
    lane = lax.broadcasted_iota(jnp.int32, (1, PAIR), 1)
    r = lax.broadcasted_iota(jnp.int32, (tq, tq), 0)
    s = lax.broadcasted_iota(jnp.int32, (tq, tq), 1)
    keep_own = (r // blk != s // blk) | (s <= r)

    for c in range(q_ref.shape[1] // tq):
        q = q_ref[0, c * tq:(c + 1) * tq, :]
        qs = _split_heads(q)
        q_blk = c * per + lax.broadcasted_iota(jnp.int32, (nbp, tq), 1) // blk
        qx = []
        for hh in range(2):
            base = _feat_base(hh)
            g = lax.dot_general(km_scr[0:nbp, :], qs[hh].astype(F32), _NT,
                                precision=lax.Precision.HIGHEST, preferred_element_type=F32)
            cnt = jnp.zeros((nbp, tq), F32)
            for m in range(n_blk):
                gm = g[m:m + 1, :]
                beats = ((gm > g) | ((gm == g) & (blk_n > m))) & (q_blk > m)
                cnt = cnt + jnp.where(beats, 1.0, 0.0)
            visible = ((blk_n < q_blk) & (cnt < float(MOBA_TOPK))) | (blk_n == q_blk)
            hide = jnp.where(visible, 0.0, NEG_INF)
            pads = [jnp.zeros((n, tq), F32) for n in (base, LANES - nbp - base)]
            hide = jnp.concatenate([a for a in (pads[0], hide, pads[1]) if a.shape[0]], axis=0)
            feats = jnp.where((lane >= base) & (lane < base + nbp), hide.T.astype(BF16),
                              qf_ref[0, hh, c * tq:(c + 1) * tq, :])
            qx.append(jnp.where(masks[hh], q, feats))
        _flash_pair(c, qx, kx_scr, v_ref, o_ref, keep_own, True, m_scr, acc_scr, p_scr, a_scr, tq)


def _moba(q, k, v, col0, n_pairs, tq):
    b, seq, _ = q.shape
    blk = MOBA_BLOCK
    assert seq % tq == 0 and tq % blk == 0 and seq // blk <= MOBA_NBLK
    slopes = _alibi_slopes(N_ALIBI)[MOBA_SLOPE_OFFSET:MOBA_SLOPE_OFFSET + 2 * n_pairs]
    q_feats, k_feats = _moba_feature_tables(slopes, seq)
    head_cols = pl.BlockSpec((1, seq, PAIR), lambda bi, p: (bi, 0, col0 + p))
    table = pl.BlockSpec((1, 2, seq, PAIR), lambda bi, p: (p, 0, 0, 0))
    return pl.pallas_call(
        functools.partial(_moba_kernel, tq=tq),
        grid=(b, n_pairs),
        in_specs=[head_cols, head_cols, head_cols, table, table],
        out_specs=pl.BlockSpec((1, seq, PAIR), lambda bi, p: (bi, 0, p)),
        out_shape=jax.ShapeDtypeStruct((b, seq, n_pairs * PAIR), BF16),
        scratch_shapes=[pltpu.VMEM((LANES, PAIR), F32), pltpu.VMEM((2, seq, PAIR), BF16)]
        + _stat_scratch(tq),
        compiler_params=_params("arbitrary", "arbitrary"),
        name="moba_attn",
    )(q, k, v, q_feats, k_feats)


def _dil_kernel(q_ref, k_ref, v_ref, o_ref, lse_ref, *, dil, slopes, n_q):
    wb = WIN_BLOCK
    seq = q_ref.shape[1]
    nb = seq // wb
    steps = (lax.broadcasted_iota(jnp.int32, (wb, 2 * wb), 0) + wb
             - lax.broadcasted_iota(jnp.int32, (wb, 2 * wb), 1))
    band_ok = (steps >= 0) & (steps <= wb)
    steps_f = (steps * dil).astype(F32)
    blk = lax.broadcasted_iota(jnp.int32, (nb, 1, 2 * wb), 0)
    key = lax.broadcasted_iota(jnp.int32, (nb, 1, 2 * wb), 2)
    no_prev = jnp.where((blk % n_q == 0) & (key < wb), -NEG_INF, 0.0)
    first = _head_masks()[0]

    def band(ref, cols):
        cur = ref[0, :, cols]
        prev = jnp.concatenate([cur[0:wb], cur[0:seq - wb]], axis=0)
        return jnp.concatenate([prev.reshape(nb, wb, PAIR), cur.reshape(nb, wb, PAIR)], axis=1)

    for pair in range(2):
        cols = slice(pair * PAIR, (pair + 1) * PAIR)
        qs = [qh.reshape(nb, wb, PAIR) for qh in _split_heads(q_ref[0, :, cols])]
        kband, vband = band(k_ref, cols), band(v_ref, cols)
        outs, lses = [], []
        for hh in range(2):
            bias = jnp.where(band_ok, (slopes[2 * pair + hh] * LOG2E) * steps_f, -NEG_INF)
            s = jnp.einsum("bqd,bkd->bqk", qs[hh], kband, preferred_element_type=F32)
            s = s - bias[None] - no_prev
            m = jnp.max(s, axis=-1, keepdims=True)
            p = jnp.exp2(s - m)
            l = jnp.sum(p, axis=-1, keepdims=True)
            o = jnp.einsum("bqk,bkd->bqd", p.astype(BF16), vband, preferred_element_type=F32)
            outs.append(o / l)
            lses.append(jnp.broadcast_to(m + LOG2E * jnp.log(l), (nb, wb, PAIR)))
        o_ref[0, :, cols] = jnp.where(first, outs[0], outs[1]).reshape(seq, PAIR).astype(BF16)
        lse_ref[0, :, cols] = jnp.where(first, lses[0], lses[1]).reshape(seq, PAIR)


def _dilated(q, k, v, group):
    window, dil = DILATED_PAIRS[group]
    assert window // dil == WIN_BLOCK
    b, _, l_sub, w = q.shape
    seq = dil * l_sub
    so = DIL_SLOPE_OFFSETS[group]
    slopes = tuple(float(s) for s in _alibi_slopes(N_ALIBI)[so:so + N_SLOTS_DIL])
    spec = pl.BlockSpec((1, seq, w), lambda bi: (bi, 0, 0))
    flat = lambda a: a.reshape(b, seq, w)
    o, lse = pl.pallas_call(
        functools.partial(_dil_kernel, dil=dil, slopes=slopes, n_q=l_sub // WIN_BLOCK),
        grid=(b,),
        in_specs=[spec, spec, spec],
        out_specs=[spec, spec],
        out_shape=[jax.ShapeDtypeStruct((b, seq, w), BF16), jax.ShapeDtypeStruct((b, seq, w), F32)],
        compiler_params=_params("arbitrary"),
        name=f"dilated_attn_{dil}",
    )(flat(q), flat(k), flat(v))
    return o.reshape(q.shape), lse.reshape(q.shape)


def _outproj_kernel(x_ref, mod_ref, g_ref, of_ref, om_ref, o1, l1, o2, l2, o3, l3, gate_ref,
                    wf_ref, wm_ref, wd_ref, wo_ref, x1_ref, h2_ref, perm_scr):
    d = x_ref.shape[1]

    def natural(ref):
        dil, n = ref.shape[1], ref.shape[2]
        if dil == 1:
            return ref[0, 0].astype(F32)
        halves = ref.shape[3] // LANES
        for r in range(dil):
            blk = ref[0, r].astype(F32)
            for c in range(halves):
                perm_scr[c, pl.ds(r, n, stride=dil), :] = blk[:, c * LANES:(c + 1) * LANES]
        return jnp.concatenate([perm_scr[c] for c in range(halves)], axis=1)

    lses = [natural(l) for l in (l1, l2, l3)]
    lmax = jnp.maximum(jnp.maximum(lses[0], lses[1]), lses[2])
    e1, e2, e3 = [jnp.exp2(l - lmax) for l in lses]
    den = e1 + e2 + e3
    o_dil = (e1 / den) * natural(o1) + (e2 / den) * natural(o2) + (e3 / den) * natural(o3)
    y = (gate_ref[:, 0:d].astype(F32) * jnp.dot(of_ref[...], wf_ref[...], preferred_element_type=F32)
         + gate_ref[:, d:2 * d].astype(F32) * jnp.dot(om_ref[...], wm_ref[...], preferred_element_type=F32)
         + gate_ref[:, 2 * d:3 * d].astype(F32)
         * jnp.dot(o_dil.astype(BF16), wd_ref[...], preferred_element_type=F32))
    out = jnp.dot(y.astype(BF16), wo_ref[...], preferred_element_type=F32)
    x1 = x_ref[...] + mod_ref[0, 2:3, :] * out
    x1_ref[...] = x1
    h2_ref[...] = _mod_norm(x1, g_ref[...], mod_ref[0, 4:5, :], mod_ref[0, 3:4, :]).astype(BF16)


def _outproj(x2, mod_l, g_norm, o_fox, o_moba, dil_outs, gates, w_f, w_m, w_d, w_o, seq, tm):
    t, d = x2.shape
    per_b = seq // tm
    row = lambda i: (i, 0)
    fix = lambda i: (0, 0)
    dil_args, dil_specs = [], []
    for o, lse in dil_outs:
        _, dil, _, w = o.shape
        dil_args += [o, lse]
        dil_specs += [pl.BlockSpec((1, dil, tm // dil, w), lambda i: (i // per_b, 0, i % per_b, 0))] * 2
    return pl.pallas_call(
        _outproj_kernel,
        grid=(t // tm,),
        in_specs=[pl.BlockSpec((tm, d), row),
                  pl.BlockSpec((1, 6, d), lambda i: (i // per_b, 0, 0)),
                  pl.BlockSpec((1, d), fix),
                  pl.BlockSpec((tm, o_fox.shape[1]), row),
                  pl.BlockSpec((tm, o_moba.shape[1]), row)] + dil_specs + [
                  pl.BlockSpec((tm, N_BRANCH * d), row),
                  pl.BlockSpec(w_f.shape, fix), pl.BlockSpec(w_m.shape, fix),
                  pl.BlockSpec(w_d.shape, fix), pl.BlockSpec(w_o.shape, fix)],
        out_specs=[pl.BlockSpec((tm, d), row), pl.BlockSpec((tm, d), row)],
        out_shape=[jax.ShapeDtypeStruct((t, d), F32), jax.ShapeDtypeStruct((t, d), BF16)],
        scratch_shapes=[pltpu.VMEM((N_SLOTS_DIL * HEAD_DIM // LANES, tm, LANES), F32)],
        compiler_params=_params("arbitrary"),
        name="outproj",
    )(x2, mod_l, g_norm, o_fox, o_moba, *dil_args, gates, w_f, w_m, w_d, w_o)


def _router_kernel(h_ref, w_ref, b_ref, o_ref, cnt_ref, *, n_exp):
    logits = jnp.dot(h_ref[...], w_ref[...], preferred_element_type=F32) + b_ref[...]
    lane = lax.broadcasted_iota(jnp.int32, logits.shape, 1)
    logits = jnp.where(lane < n_exp, logits, -jnp.inf)

    def top(vals):
        m = jnp.max(vals, axis=-1, keepdims=True)
        idx = jnp.min(jnp.where(vals == m, lane, LANES), axis=-1, keepdims=True)
        return m, lane == idx

    m1, hot1 = top(logits)
    m2, hot2 = top(jnp.where(hot1, -jnp.inf, logits))
    e2 = jnp.exp(m2 - m1)
    cw = jnp.where(hot1, 1.0 / (1.0 + e2), 0.0) + jnp.where(hot2, e2 / (1.0 + e2), 0.0)
    o_ref[...] = cw
    sub = cnt_ref.shape[0] // 8
    rows = cw.shape[0] // sub
    for s in range(sub):
        n_tok = jnp.sum(jnp.where(cw[s * rows:(s + 1) * rows] > 0.0, 1.0, 0.0), axis=0, keepdims=True)
        cnt_ref[s * 8:(s + 1) * 8, :] = jnp.broadcast_to(n_tok, (8, LANES)).astype(jnp.int32)


def _router(h2, w_r, b_r, n_exp, tile):
    t, d = h2.shape
    tm = 4 * tile
    cw, cnt = pl.pallas_call(
        functools.partial(_router_kernel, n_exp=n_exp),
        grid=(t // tm,),
        in_specs=[pl.BlockSpec((tm, d), lambda i: (i, 0)),
                  pl.BlockSpec((d, LANES), lambda i: (0, 0)),
                  pl.BlockSpec((1, LANES), lambda i: (0, 0))],
        out_specs=[pl.BlockSpec((tm, LANES), lambda i: (i, 0)),
                   pl.BlockSpec((8 * tm // tile, LANES), lambda i: (i, 0))],
        out_shape=[jax.ShapeDtypeStruct((t, LANES), F32),
                   jax.ShapeDtypeStruct((t // tile * 8, LANES), jnp.int32)],
        compiler_params=_params("arbitrary"),
        name="router",
    )(h2, w_r, b_r)
    return cw, cnt.reshape(t // tile, 8, LANES)[:, 0, :n_exp]


MOE_TILE = 512
MOE_ROWS = 512
MOE_PUT = 128
MOE_GET = 256
MOE_ALIGN = 16


def _moe_layout(cnt, n_rows_static):
    n_tiles, n_exp = cnt.shape
    padded = (cnt + MOE_ALIGN - 1) // MOE_ALIGN * MOE_ALIGN
    length = jnp.sum(padded, axis=0)
    span = (length + (MOE_PUT - MOE_ALIGN) + MOE_ROWS - 1) // MOE_ROWS * MOE_ROWS
    start = jnp.cumsum(span) - span
    off = start[None, :] + jnp.cumsum(padded, axis=0) - padded
    n_steps = n_rows_static // MOE_ROWS
    first = start // MOE_ROWS
    step = jnp.arange(n_steps, dtype=jnp.int32)
    expert = jnp.sum((first[None, :] <= step[:, None]).astype(jnp.int32), axis=1) - 1
    active = step < (first + (length + MOE_ROWS - 1) // MOE_ROWS)[expert]
    return (off.reshape(-1).astype(jnp.int32), cnt.reshape(-1).astype(jnp.int32),
            expert.astype(jnp.int32), active.astype(jnp.int32))


def _dispatch_kernel(off_ref, cnt_ref, h_ref, cw_ref, u_ref, xs_in, xs_ref, stage, sems, *, n_exp):
    del xs_in
    i = pl.program_id(0)
    n_put = MOE_TILE // MOE_PUT
    routed = jnp.where(cw_ref[...].T[0:8] > 0.0, 1.0, 0.0)
    pos = jnp.dot(routed.astype(BF16), u_ref[...], preferred_element_type=F32)
    row = lax.broadcasted_iota(jnp.int32, (MOE_PUT, MOE_TILE), 0).astype(F32)
    h = h_ref[...]

    def copy(e, s):
        slot = e * n_put + s
        dst = pl.multiple_of(off_ref[i * n_exp + e] + s * MOE_PUT, MOE_ALIGN)
        return pltpu.make_async_copy(stage.at[slot], xs_ref.at[pl.ds(dst, MOE_PUT), :], sems.at[slot])

    for e in range(n_exp):
        for s in range(n_put):
            @pl.when(s * MOE_PUT < cnt_ref[i * n_exp + e])
            def _():
                take = (routed[e:e + 1, :] > 0.5) & (pos[e:e + 1, :] == row + float(s * MOE_PUT))
                onehot = jnp.where(take, 1.0, 0.0).astype(BF16)
                stage[e * n_put + s] = jnp.dot(onehot, h, preferred_element_type=F32).astype(BF16)
                copy(e, s).start()

    for e in range(n_exp):
        for s in range(n_put):
            @pl.when(s * MOE_PUT < cnt_ref[i * n_exp + e])
            def _():
                copy(e, s).wait()


def _dispatch(h2, cw, off, cnt, n_exp, n_rows):
    t, d = h2.shape
    u = jnp.asarray(np.arange(MOE_TILE)[:, None] < np.arange(MOE_TILE)[None, :], BF16)
    n_slots = n_exp * (MOE_TILE // MOE_PUT)
    return pl.pallas_call(
        functools.partial(_dispatch_kernel, n_exp=n_exp),
        grid_spec=pltpu.PrefetchScalarGridSpec(
            num_scalar_prefetch=2,
            grid=(t // MOE_TILE,),
            in_specs=[pl.BlockSpec((MOE_TILE, d), lambda i, o, c: (i, 0)),
                      pl.BlockSpec((MOE_TILE, LANES), lambda i, o, c: (i, 0)),
                      pl.BlockSpec((MOE_TILE, MOE_TILE), lambda i, o, c: (0, 0)),
                      pl.BlockSpec(memory_space=pl.ANY)],
            out_specs=pl.BlockSpec(memory_space=pl.ANY),
            scratch_shapes=[pltpu.VMEM((n_slots, MOE_PUT, d), BF16),
                            pltpu.SemaphoreType.DMA((n_slots,))]),
        out_shape=jax.ShapeDtypeStruct((n_rows, d), BF16),
        input_output_aliases={5: 0},
        compiler_params=_params("arbitrary"),
        name="moe_dispatch",
    )(off, cnt, h2, cw, u, jnp.zeros((n_rows, d), BF16))


FF_CHUNK = 768


def _swiglu(x, wg_ref, wu_ref, wd_ref):
    ff = wg_ref.shape[2]
    y = None
    for c0 in range(0, ff, FF_CHUNK):
        c1 = min(c0 + FF_CHUNK, ff)
        gate = jnp.dot(x, wg_ref[0, :, c0:c1].astype(BF16), preferred_element_type=F32)
        up = jnp.dot(x, wu_ref[0, :, c0:c1].astype(BF16), preferred_element_type=F32)
        a = ((gate * _sigmoid(gate)) * up).astype(BF16)
        part = jnp.dot(a, wd_ref[0, c0:c1, :].astype(BF16), preferred_element_type=F32)
        y = part if y is None else y + part
    return y


def _experts_kernel(exp_ref, act_ref, xs_ref, wg_ref, wu_ref, wd_ref, ys_ref):
    g = pl.program_id(0)

    @pl.when(act_ref[g] > 0)
    def _():
        ys_ref[...] = _swiglu(xs_ref[...], wg_ref, wu_ref, wd_ref).astype(BF16)

    @pl.when(act_ref[g] == 0)
    def _():
        ys_ref[...] = jnp.zeros(ys_ref.shape, BF16)


def _experts(xs, expert, active, w_g, w_u, w_d):
    n_rows, d = xs.shape
    _, _, ff = w_g.shape
    wmap = lambda g, ex, ac: (ex[g], 0, 0)
    single = lambda shape: pl.BlockSpec(shape, wmap, pipeline_mode=pl.Buffered(1))
    return pl.pallas_call(
        _experts_kernel,
        grid_spec=pltpu.PrefetchScalarGridSpec(
            num_scalar_prefetch=2,
            grid=(n_rows // MOE_ROWS,),
            in_specs=[pl.BlockSpec((MOE_ROWS, d), lambda g, ex, ac: (g, 0)),
                      single((1, d, ff)), single((1, d, ff)), pl.BlockSpec((1, ff, d), wmap)],
            out_specs=pl.BlockSpec((MOE_ROWS, d), lambda g, ex, ac: (g, 0))),
        out_shape=jax.ShapeDtypeStruct((n_rows, d), BF16),
        compiler_params=_params("arbitrary"),
        name="moe_experts",
    )(expert, active, xs, w_g, w_u, w_d)


def _combine_kernel(off_ref, cnt_ref, x_ref, mod_ref, cw_ref, l_ref, ys_ref, o_ref, buf, sems, acc_scr,
                    *, n_exp):
    i = pl.program_id(0)
    n_get = MOE_TILE // MOE_GET

    def copy(e, s):
        slot = e * n_get + s
        src = pl.multiple_of(off_ref[i * n_exp + e] + s * MOE_GET, MOE_ALIGN)
        return pltpu.make_async_copy(ys_ref.at[pl.ds(src, MOE_GET), :], buf.at[slot], sems.at[slot])

    for e in range(n_exp):
        for s in range(n_get):
            @pl.when(s * MOE_GET < cnt_ref[i * n_exp + e])
            def _():
                copy(e, s).start()

    cw = cw_ref[...]
    routed = jnp.where(cw > 0.0, 1.0, 0.0)
    pos = jnp.dot(l_ref[...], routed.astype(BF16), preferred_element_type=F32)
    col = lax.broadcasted_iota(jnp.int32, (MOE_TILE, MOE_GET), 1).astype(F32)
    acc_scr[...] = jnp.zeros(acc_scr.shape, F32)
    for e in range(n_exp):
        for s in range(n_get):
            @pl.when(s * MOE_GET < cnt_ref[i * n_exp + e])
            def _():
                copy(e, s).wait()
                take = (cw[:, e:e + 1] > 0.0) & (pos[:, e:e + 1] == col + float(s * MOE_GET))
                onehot = jnp.where(take, 1.0, 0.0).astype(BF16)
                acc_scr[...] += cw[:, e:e + 1] * jnp.dot(onehot, buf[e * n_get + s],
                                                         preferred_element_type=F32)
    o_ref[...] = x_ref[...] + mod_ref[0, 5:6, :] * acc_scr[...]


def _combine(x1, mod_l, cw, ys, off, cnt, n_exp, seq):
    t, d = x1.shape
    per_b = seq // MOE_TILE
    low = jnp.asarray(np.arange(MOE_TILE)[:, None] > np.arange(MOE_TILE)[None, :], BF16)
    n_slots = n_exp * (MOE_TILE // MOE_GET)
    return pl.pallas_call(
        functools.partial(_combine_kernel, n_exp=n_exp),
        grid_spec=pltpu.PrefetchScalarGridSpec(
            num_scalar_prefetch=2,
            grid=(t // MOE_TILE,),
            in_specs=[pl.BlockSpec((MOE_TILE, d), lambda i, o, c: (i, 0)),
                      pl.BlockSpec((1, 6, d), lambda i, o, c: (i // per_b, 0, 0)),
                      pl.BlockSpec((MOE_TILE, LANES), lambda i, o, c: (i, 0)),
                      pl.BlockSpec((MOE_TILE, MOE_TILE), lambda i, o, c: (0, 0)),
                      pl.BlockSpec(memory_space=pl.ANY)],
            out_specs=pl.BlockSpec((MOE_TILE, d), lambda i, o, c: (i, 0)),
            scratch_shapes=[pltpu.VMEM((n_slots, MOE_GET, d), BF16),
                            pltpu.SemaphoreType.DMA((n_slots,)),
                            pltpu.VMEM((MOE_TILE, d), F32)]),
        out_shape=jax.ShapeDtypeStruct((t, d), F32),
        compiler_params=_params("arbitrary"),
        name="moe_combine",
    )(off, cnt, x1, mod_l, cw, low, ys)


def _moe(x1, h2, mod_l, w_r, b_r, w_g, w_u, w_d, seq):
    t, d = x1.shape
    n_exp = w_g.shape[0]
    n_tiles = t // MOE_TILE
    bound = (TOP_K * t + n_tiles * n_exp * (MOE_ALIGN - 1)
             + n_exp * (MOE_PUT - MOE_ALIGN + MOE_ROWS - 1))
    n_rows = (bound + MOE_ROWS - 1) // MOE_ROWS * MOE_ROWS + MOE_ROWS
    cw, cnt = _router(h2, w_r, b_r, n_exp, MOE_TILE)
    off, cnt, expert, active = _moe_layout(cnt, n_rows)
    xs = _dispatch(h2, cw, off, cnt, n_exp, n_rows)
    ys = _experts(xs, expert, active, w_g, w_u, w_d)
    return _combine(x1, mod_l, cw, ys, off, cnt, n_exp, seq)


def _ffn_kernel(x_ref, h_ref, mod_ref, wg_ref, wu_ref, wd_ref, o_ref):
    o_ref[...] = x_ref[...] + mod_ref[0, 5:6, :] * _swiglu(h_ref[...], wg_ref, wu_ref, wd_ref)


def _ffn(x1, h2, mod_l, w_g, w_u, w_d, seq, tm):
    t, d = x1.shape
    per_b = seq // tm
    row = lambda i: (i, 0)
    whole = lambda w: pl.BlockSpec(w.shape, lambda i: (0, 0, 0), pipeline_mode=pl.Buffered(1))
    return pl.pallas_call(
        _ffn_kernel,
        grid=(t // tm,),
        in_specs=[pl.BlockSpec((tm, d), row), pl.BlockSpec((tm, d), row),
                  pl.BlockSpec((1, 6, d), lambda i: (i // per_b, 0, 0)),
                  whole(w_g), whole(w_u), whole(w_d)],
        out_specs=pl.BlockSpec((tm, d), row),
        out_shape=jax.ShapeDtypeStruct((t, d), F32),
        compiler_params=_params("arbitrary"),
        name="swiglu",
    )(x1, h2, mod_l, w_g, w_u, w_d)


def _pad_cols(a, width):
    return jnp.pad(a, ((0, 0), (0, width - a.shape[1])))


def kernel(x, c, w_ada, b_ada, norm_mix, norm_ffn, w_in, b_fgate, q_gain, k_gain, w_br_fox, w_br_moba,
           w_br_dil, w_out, w_ffn_gate, w_ffn_up, w_ffn_down, w_router, b_router, w_exp_gate,
           w_exp_up, w_exp_down):
    b, seq, d = x.shape
    depth = w_ada.shape[0]
    t = b * seq
    tm, tq = TOKEN_TILE, ATTN_CHUNK
    n_pairs_fox = N_HEADS_FOX // 2
    n_pairs_moba = N_HEADS_MOBA // 2

    mod = _adaln(c, w_ada, b_ada).reshape(depth, b, 6, d)
    x2 = x.reshape(t, d)
    w_in_t = jnp.transpose(w_in, (2, 0, 1))
    for l in range(depth):
        wl = w_in_t[:, l, :].astype(BF16)
        f0 = 3 * MIX_WIDTH
        w_qkv, w_gate = wl[:f0], wl[f0 + N_HEADS_FOX:]
        w_f = jnp.pad(wl[f0:f0 + N_HEADS_FOX], ((0, LANES - N_HEADS_FOX), (0, 0)))
        b_f = _pad_cols(b_fgate[l].reshape(1, -1), LANES)
        outs = _inproj(x2, mod[l], norm_mix[l].reshape(1, d), w_qkv, w_gate, w_f, q_gain[l].reshape(1, -1),
                       k_gain[l].reshape(1, -1), b_f, seq, tm)
        q_m, k_m, v_m = (o.reshape(b, seq, -1) for o in outs[0:3])
        gates, lf = outs[12], outs[13]

        cum = _decay(lf.reshape(b, seq, LANES))
        o_fox = _fox(q_m, k_m, v_m, cum, 0, n_pairs_fox, tq).reshape(t, -1)
        o_moba = _moba(q_m, k_m, v_m, n_pairs_fox, n_pairs_moba, tq).reshape(t, -1)
        dil_outs = [_dilated(*outs[3 + 3 * g:6 + 3 * g], g) for g in range(len(DILATED_PAIRS))]

        x1, h2 = _outproj(x2, mod[l], norm_ffn[l].reshape(1, d), o_fox, o_moba, dil_outs, gates,
                          w_br_fox[l].astype(BF16), w_br_moba[l].astype(BF16),
                          w_br_dil[l].astype(BF16), w_out[l].astype(BF16), seq, tm)
        i = l // 2
        if l % 2 == 0:
            x2 = _ffn(x1, h2, mod[l], w_ffn_gate[i:i + 1].astype(BF16),
                      w_ffn_up[i:i + 1].astype(BF16), w_ffn_down[i:i + 1].astype(BF16), seq, FFN_TILE)
        else:
            x2 = _moe(x1, h2, mod[l], _pad_cols(w_router[i], LANES).astype(BF16),
                      _pad_cols(b_router[i].reshape(1, -1), LANES), w_exp_gate[i], w_exp_up[i],
                      w_exp_down[i], seq)
    return x2.reshape(b, seq, d)
```

```python
import functools

import numpy as np
import jax
import jax.numpy as jnp
from jax import lax
from jax.experimental import pallas as pl
from jax.experimental.pallas import tpu as pltpu

HEAD_DIM = 64
N_HEADS_FOX = 6
N_HEADS_MOBA = 6
DILATED_PAIRS = ((128, 1), (512, 4), (2048, 16))
N_SLOTS_DIL = 4
N_HEADS_DIL = N_SLOTS_DIL * len(DILATED_PAIRS)
N_HEADS = N_HEADS_FOX + N_HEADS_MOBA + N_HEADS_DIL
MIX_WIDTH = N_HEADS * HEAD_DIM
N_BRANCH = 3
MOBA_BLOCK = 256
MOBA_TOPK = 3
WIN_BLOCK = 128
N_ALIBI = N_HEADS_MOBA + N_HEADS_DIL
DIL_SLOPE_OFFSETS = (0, N_SLOTS_DIL, 2 * N_SLOTS_DIL + N_HEADS_MOBA)
MOBA_SLOPE_OFFSET = 2 * N_SLOTS_DIL
TOP_K = 2
RMS_EPS = 1e-6
NEG_INF = -1e30
LOG2E = 1.4426950408889634

LANES = 128
PAIR = 2 * HEAD_DIM
VMEM_LIMIT = 56 * 1024 * 1024
TOKEN_TILE = 512
ATTN_CHUNK = 512
FFN_TILE = 1024

F32 = jnp.float32
BF16 = jnp.bfloat16
_NT = (((1,), (1,)), ((), ()))


def _alibi_slopes(n):
    return (2.0 ** (-8.0 * np.arange(1, n + 1) / n)).astype(np.float32)


def _sigmoid(x):
    return 1.0 / (1.0 + jnp.exp(-x))


def _params(*sem):
    return pltpu.CompilerParams(dimension_semantics=sem, vmem_limit_bytes=VMEM_LIMIT)


def _adaln_kernel(c_ref, w_ref, b_ref, o_ref):
    c = c_ref[...]
    cond = c * _sigmoid(c)
    o_ref[0] = jnp.dot(cond, w_ref[0], precision=lax.Precision.HIGHEST,
                       preferred_element_type=F32) + b_ref[0]


def _adaln(c, w_ada, b_ada):
    depth, d, n = w_ada.shape
    b = c.shape[0]
    tn = 1536
    return pl.pallas_call(
        _adaln_kernel,
        grid=(depth, n // tn),
        in_specs=[pl.BlockSpec((b, d), lambda l, j: (0, 0)),
                  pl.BlockSpec((1, d, tn), lambda l, j: (l, 0, j)),
                  pl.BlockSpec((1, 1, tn), lambda l, j: (l, 0, j))],
        out_specs=pl.BlockSpec((1, b, tn), lambda l, j: (l, 0, j)),
        out_shape=jax.ShapeDtypeStruct((depth, b, n), F32),
        compiler_params=_params("arbitrary", "arbitrary"),
        name="adaln",
    )(c, w_ada, b_ada.reshape(depth, 1, n))


def _mod_norm(x, g, scale, shift):
    ms = jnp.mean(x * x, axis=-1, keepdims=True)
    return (x * lax.rsqrt(ms + RMS_EPS) * g) * (1.0 + scale) + shift


def _inproj_kernel(x_ref, mod_ref, g_ref, w_ref, wg_ref, qg_ref, kg_ref, bf_ref, e_ref,
                   q_ref, k_ref, v_ref, qd1, kd1, vd1, qd2, kd2, vd2, qd3, kd3, vd3,
                   gate_ref, lf_ref, perm_scr, *, n_main):
    h = _mod_norm(x_ref[...], g_ref[...], mod_ref[0, 1:2, :], mod_ref[0, 0:1, :]).astype(BF16)
    tm = x_ref.shape[0]
    qk_chunk = 4 * HEAD_DIM
    q_outs = (q_ref, qd1, qd2, qd3)
    k_outs = (k_ref, kd1, kd2, kd3)
    v_outs = (v_ref, vd1, vd2, vd3)

    def put(outs, col, val):
        if col < n_main:
            outs[0][:, col:col + qk_chunk] = val.astype(BF16)
            return
        g = (col - n_main) // qk_chunk
        dil = DILATED_PAIRS[g][1]
        if dil == 1:
            outs[1 + g][0, 0] = val.astype(BF16)
            return
        halves = qk_chunk // LANES
        for c in range(halves):
            perm_scr[c] = val[:, c * LANES:(c + 1) * LANES]
        for r in range(dil):
            outs[1 + g][0, r] = jnp.concatenate(
                [perm_scr[c, pl.ds(r, tm // dil, stride=dil), :] for c in range(halves)],
                axis=1).astype(BF16)

    def proj(wt_ref, col0, width):
        return lax.dot_general(h, wt_ref[col0:col0 + width, :], _NT, preferred_element_type=F32)

    wide = MIX_WIDTH // 2
    for half in range(2):
        yq = proj(w_ref, half * wide, wide)
        yk = proj(w_ref, MIX_WIDTH + half * wide, wide)
        yv = proj(w_ref, 2 * MIX_WIDTH + half * wide, wide)
        for c in range(wide // qk_chunk):
            col = half * wide + c * qk_chunk
            yq_c = yq[:, c * qk_chunk:(c + 1) * qk_chunk]
            yk_c = yk[:, c * qk_chunk:(c + 1) * qk_chunk]
            sq = jnp.concatenate([yq_c * yq_c, yk_c * yk_c], axis=0).astype(BF16)
            ss = jnp.dot(sq, e_ref[...], preferred_element_type=F32)
            r = lax.rsqrt(ss * (1.0 / HEAD_DIM) + RMS_EPS)
            put(q_outs, col, (yq_c * r[0:tm] * qg_ref[:, col:col + qk_chunk])
                * (HEAD_DIM ** -0.5 * LOG2E))
            put(k_outs, col, yk_c * r[tm:2 * tm] * kg_ref[:, col:col + qk_chunk])
            put(v_outs, col, yv[:, c * qk_chunk:(c + 1) * qk_chunk])

    gchunk = 1024
    for c in range(wg_ref.shape[0] // gchunk):
        y = proj(wg_ref, c * gchunk, gchunk)
        gate_ref[:, c * gchunk:(c + 1) * gchunk] = _sigmoid(y).astype(BF16)

    f = proj(w_ref, 3 * MIX_WIDTH, LANES) + bf_ref[...]
    lf_ref[...] = jnp.minimum(f, 0.0) - jnp.log(1.0 + jnp.exp(-jnp.abs(f)))


def _inproj(x2, mod_l, g_norm, w_t, w_gate, q_gain, k_gain, b_f, seq, tm):
    t, d = x2.shape
    n_main = (N_HEADS_FOX + N_HEADS_MOBA) * HEAD_DIM
    dil_w = N_SLOTS_DIL * HEAD_DIM
    e = (np.arange(dil_w)[:, None] // HEAD_DIM == np.arange(dil_w)[None, :] // HEAD_DIM)
    e = jnp.asarray(e, BF16)
    per_b = seq // tm
    row = lambda i: (i, 0)
    fix = lambda i: (0, 0)
    qkv_shapes = [jax.ShapeDtypeStruct((t, n_main), BF16)] * 3
    qkv_specs = [pl.BlockSpec((tm, n_main), row)] * 3
    for _, dil in DILATED_PAIRS:
        assert tm % (16 * dil) == 0
        qkv_shapes += [jax.ShapeDtypeStruct((t // seq, dil, seq // dil, dil_w), BF16)] * 3
        qkv_specs += [pl.BlockSpec((1, dil, tm // dil, dil_w),
                                   lambda i: (i // per_b, 0, i % per_b, 0))] * 3
    out_shapes = qkv_shapes + [jax.ShapeDtypeStruct((t, N_BRANCH * d), BF16),
                               jax.ShapeDtypeStruct((t, LANES), F32)]
    out_specs = qkv_specs + [pl.BlockSpec((tm, N_BRANCH * d), row), pl.BlockSpec((tm, LANES), row)]
    return pl.pallas_call(
        functools.partial(_inproj_kernel, n_main=n_main),
        grid=(t // tm,),
        in_specs=[pl.BlockSpec((tm, d), row),
                  pl.BlockSpec((1, 6, d), lambda i: (i // per_b, 0, 0)),
                  pl.BlockSpec((1, d), fix),
                  pl.BlockSpec(w_t.shape, fix, pipeline_mode=pl.Buffered(1)),
                  pl.BlockSpec(w_gate.shape, fix, pipeline_mode=pl.Buffered(1)),
                  pl.BlockSpec((1, MIX_WIDTH), fix),
                  pl.BlockSpec((1, MIX_WIDTH), fix),
                  pl.BlockSpec((1, LANES), fix),
                  pl.BlockSpec(e.shape, fix)],
        out_specs=out_specs,
        out_shape=out_shapes,
        scratch_shapes=[pltpu.VMEM((dil_w // LANES, tm, LANES), F32)],
        compiler_params=_params("arbitrary"),
        name="inproj",
    )(x2, mod_l, g_norm, w_t, w_gate, q_gain, k_gain, b_f, e)


def _decay_kernel(lf_ref, tri_ref, o_ref, *, blk):
    carry = jnp.zeros((1, LANES), F32)
    for j in range(lf_ref.shape[1] // blk):
        c = jnp.dot(tri_ref[...], lf_ref[0, j * blk:(j + 1) * blk, :],
                    precision=lax.Precision.HIGHEST, preferred_element_type=F32) + carry
        o_ref[0, j * blk:(j + 1) * blk, :] = c
        carry = c[blk - 1:blk, :]


def _decay(lf, blk=256):
    b, seq, _ = lf.shape
    tri = jnp.asarray(np.arange(blk)[:, None] >= np.arange(blk)[None, :], F32)
    return pl.pallas_call(
        functools.partial(_decay_kernel, blk=blk),
        grid=(b,),
        in_specs=[pl.BlockSpec((1, seq, LANES), lambda i: (i, 0, 0)),
                  pl.BlockSpec((blk, blk), lambda i: (0, 0))],
        out_specs=pl.BlockSpec((1, seq, LANES), lambda i: (i, 0, 0)),
        out_shape=jax.ShapeDtypeStruct((b, seq, LANES), F32),
        compiler_params=_params("arbitrary"),
        name="fox_decay",
    )(lf, tri)


def _head_masks():
    lane = lax.broadcasted_iota(jnp.int32, (1, PAIR), 1)
    return lane < HEAD_DIM, lane >= HEAD_DIM


def _split_heads(q):
    return [jnp.where(m, q, jnp.zeros_like(q)) for m in _head_masks()]


def _feat_base(hh):
    return HEAD_DIM * (1 - hh)


def _three_bf16(x):
    hi = x.astype(BF16).astype(F32)
    mid = (x - hi).astype(BF16).astype(F32)
    lo = (x - hi - mid).astype(BF16).astype(F32)
    return hi, mid, lo


def _place(idx, base, parts):
    out = jnp.zeros(jnp.broadcast_shapes(idx.shape, jnp.shape(parts[0])), F32)
    for r, part in enumerate(parts):
        out = jnp.where(idx == base + r, part, out)
    return out


def _flash_pair(c, qx, kx_scr, v_ref, o_ref, keep_own, own_first, m_scr, acc_scr, p_scr, a_scr, tq):
    _reset(m_scr, acc_scr)

    def scores(j, own, slot):
        for hh in range(2):
            s = lax.dot_general(qx[hh], kx_scr[hh, j * tq:(j + 1) * tq, :], _NT,
                                preferred_element_type=F32)
            if own:
                s = jnp.where(keep_own, s, NEG_INF)
            _softmax_stage(s, m_scr.at[hh], p_scr.at[slot, hh], a_scr.at[slot, hh])

    def values(j, slot):
        vs = _with_ones(v_ref[0, j * tq:(j + 1) * tq, :])
        for hh in range(2):
            _value_stage(vs[hh], acc_scr.at[hh], p_scr.at[slot, hh], a_scr.at[slot, hh])

    order = [c] + list(range(c)) if own_first else list(range(c + 1))
    scores(order[0], order[0] == c, 0)
    for n in range(1, len(order)):
        scores(order[n], order[n] == c, n % 2)
        values(order[n - 1], (n - 1) % 2)
    values(order[-1], (len(order) - 1) % 2)
    o_ref[0, c * tq:(c + 1) * tq, :] = _merge_pair(acc_scr)


def _with_ones(v):
    return [jnp.where(m, v, jnp.ones_like(v)) for m in _head_masks()]


def _softmax_stage(s, m_ref, p_ref, a_ref):
    m_prev = m_ref[...]
    m_new = jnp.maximum(m_prev, jnp.max(s, axis=-1, keepdims=True))
    p = jnp.exp2(s - jnp.concatenate([m_new] * (s.shape[1] // LANES), axis=1))
    m_ref[...] = m_new
    p_ref[...] = p.astype(BF16)
    a_ref[...] = jnp.exp2(m_prev - m_new)


def _value_stage(v_ones, acc_ref, p_ref, a_ref):
    acc_ref[...] = a_ref[...] * acc_ref[...] + jnp.dot(p_ref[...], v_ones, preferred_element_type=F32)


def _reset(m_scr, acc_scr):
    m_scr[...] = jnp.full(m_scr.shape, NEG_INF, F32)
    acc_scr[...] = jnp.zeros(acc_scr.shape, F32)


def _merge_pair(acc_scr):
    first = _head_masks()[0]
    a0, a1 = acc_scr[0], acc_scr[1]
    return jnp.where(first, a0 / a0[:, HEAD_DIM:HEAD_DIM + 1], a1 / a1[:, 0:1]).astype(BF16)


def _stat_scratch(tq):
    return [pltpu.VMEM((2, tq, LANES), F32), pltpu.VMEM((2, tq, PAIR), F32),
            pltpu.VMEM((2, 2, tq, tq), BF16), pltpu.VMEM((2, 2, tq, LANES), F32)]


def _fox_kernel(q_ref, k_ref, v_ref, cum_ref, o_ref, kx_scr, m_scr, acc_scr, p_scr, a_scr, *, tq):
    pair = pl.program_id(1)
    masks = _head_masks()
    lane = lax.broadcasted_iota(jnp.int32, (1, PAIR), 1)
    k = k_ref[0]
    cum = cum_ref[0]
    lane_s = lax.broadcasted_iota(jnp.int32, cum.shape, 1)
    for hh in range(2):
        f_s = jnp.sum(jnp.where(lane_s == 2 * pair + hh, cum, 0.0), axis=-1, keepdims=True)
        feats = _place(lane, _feat_base(hh), _three_bf16(-LOG2E * f_s))
        kx_scr[hh] = jnp.where(masks[hh], k, feats.astype(BF16))

    causal = (lax.broadcasted_iota(jnp.int32, (tq, tq), 1)
              <= lax.broadcasted_iota(jnp.int32, (tq, tq), 0))
    for c in range(q_ref.shape[1] // tq):
        q = q_ref[0, c * tq:(c + 1) * tq, :]
        qx = []
        for hh in range(2):
            ones = (lane >= _feat_base(hh)) & (lane < _feat_base(hh) + 3)
            qx.append(jnp.where(masks[hh], q, jnp.where(ones, 1.0, 0.0).astype(BF16)))
        _flash_pair(c, qx, kx_scr, v_ref, o_ref, causal, False, m_scr, acc_scr, p_scr, a_scr, tq)


def _fox(q, k, v, cum, col0, n_pairs, tq):
    b, seq, _ = q.shape
    head_cols = pl.BlockSpec((1, seq, PAIR), lambda bi, p: (bi, 0, col0 + p))
    return pl.pallas_call(
        functools.partial(_fox_kernel, tq=tq),
        grid=(b, n_pairs),
        in_specs=[head_cols, head_cols, head_cols,
                  pl.BlockSpec((1, seq, LANES), lambda bi, p: (bi, 0, 0))],
        out_specs=pl.BlockSpec((1, seq, PAIR), lambda bi, p: (bi, 0, p)),
        out_shape=jax.ShapeDtypeStruct((b, seq, n_pairs * PAIR), BF16),
        scratch_shapes=[pltpu.VMEM((2, seq, PAIR), BF16)] + _stat_scratch(tq),
        compiler_params=_params("arbitrary", "arbitrary"),
        name="fox_attn",
    )(q, k, v, cum)


MOBA_NBLK = 8
ALIBI_PARTS = 5


def _moba_feature_tables(slopes, seq):
    pos = np.arange(seq, dtype=np.float64)

    def parts(x):
        out, rest = [], x.copy()
        for _ in range(ALIBI_PARTS):
            p = rest.astype(np.float32).astype(jnp.bfloat16).astype(np.float64)
            out.append(p)
            rest = rest - p
        assert not rest.any()
        return out

    n_heads = len(slopes)
    qf = np.zeros((n_heads // 2, 2, seq, PAIR), np.float32)
    kf = np.zeros((n_heads // 2, 2, seq, PAIR), np.float32)
    for h, slope in enumerate(slopes):
        base = _feat_base(h % 2)
        q_t, k_t = qf[h // 2, h % 2], kf[h // 2, h % 2]
        k_t[np.arange(seq), base + np.arange(seq) // MOBA_BLOCK] = 1.0
        lo, mid, hi = base + MOBA_NBLK, base + MOBA_NBLK + ALIBI_PARTS, base + MOBA_NBLK + 2 * ALIBI_PARTS
        q_t[:, lo:mid] = 1.0
        slope2 = np.float64(np.float32(slope * LOG2E))
        k_t[:, lo:mid] = np.stack(parts(slope2 * pos), axis=1)
        q_t[:, mid:hi] = np.stack(parts(-slope2 * pos), axis=1)
        k_t[:, mid:hi] = 1.0
    return jnp.asarray(qf, BF16), jnp.asarray(kf, BF16)


def _moba_kernel(q_ref, k_ref, v_ref, qf_ref, kf_ref, o_ref, km_scr, kx_scr, m_scr, acc_scr, p_scr,
                 a_scr, *, tq):
    blk = MOBA_BLOCK
    nbp = MOBA_NBLK
    n_blk = k_ref.shape[1] // blk
    per = tq // blk
    masks = _head_masks()

    km_scr[...] = jnp.zeros(km_scr.shape, F32)
    for n in range(n_blk):
        kb = k_ref[0, n * blk:(n + 1) * blk, :].astype(F32)
        km_scr[n:n + 1, :] = jnp.sum(kb, axis=0, keepdims=True) * (1.0 / blk)
    for hh in range(2):
        kx_scr[hh] = jnp.where(masks[hh], k_ref[0], kf_ref[0, hh])

    blk_n = lax.broadcasted_iota(jnp.int32, (nbp, tq), 0)
    lane = lax.broadcasted_iota(jnp.int32, (1, PAIR), 1)
    r = lax.broadcasted_iota(jnp.int32, (tq, tq), 0)
    s = lax.broadcasted_iota(jnp.int32, (tq, tq), 1)
    keep_own = (r // blk != s // blk) | (s <= r)

    for c in range(q_ref.shape[1] // tq):
        q = q_ref[0, c * tq:(c + 1) * tq, :]
        qs = _split_heads(q)
        q_blk = c * per + lax.broadcasted_iota(jnp.int32, (nbp, tq), 1) // blk
        qx = []
        for hh in range(2):
            base = _feat_base(hh)
            g = lax.dot_general(km_scr[0:nbp, :], qs[hh].astype(F32), _NT,
                                precision=lax.Precision.HIGHEST, preferred_element_type=F32)
            cnt = jnp.zeros((nbp, tq), F32)
            for m in range(n_blk):
                gm = g[m:m + 1, :]
                beats = ((gm > g) | ((gm == g) & (blk_n > m))) & (q_blk > m)
                cnt = cnt + jnp.where(beats, 1.0, 0.0)
            visible = ((blk_n < q_blk) & (cnt < float(MOBA_TOPK))) | (blk_n == q_blk)
            hide = jnp.where(visible, 0.0, NEG_INF)
            pads = [jnp.zeros((n, tq), F32) for n in (base, LANES - nbp - base)]
            hide = jnp.concatenate([a for a in (pads[0], hide, pads[1]) if a.shape[0]], axis=0)
            feats = jnp.where((lane >= base) & (lane < base + nbp), hide.T.astype(BF16),
                              qf_ref[0, hh, c * tq:(c + 1) * tq, :])
            qx.append(jnp.where(masks[hh], q, feats))
        _flash_pair(c, qx, kx_scr, v_ref, o_ref, keep_own, True, m_scr, acc_scr, p_scr, a_scr, tq)


def _moba(q, k, v, col0, n_pairs, tq):
    b, seq, _ = q.shape
    blk = MOBA_BLOCK
    assert seq % tq == 0 and tq % blk == 0 and seq // blk <= MOBA_NBLK
    slopes = _alibi_slopes(N_ALIBI)[MOBA_SLOPE_OFFSET:MOBA_SLOPE_OFFSET + 2 * n_pairs]
    q_feats, k_feats = _moba_feature_tables(slopes, seq)
    head_cols = pl.BlockSpec((1, seq, PAIR), lambda bi, p: (bi, 0, col0 + p))
    table = pl.BlockSpec((1, 2, seq, PAIR), lambda bi, p: (p, 0, 0, 0))
    return pl.pallas_call(
        functools.partial(_moba_kernel, tq=tq),
        grid=(b, n_pairs),
        in_specs=[head_cols, head_cols, head_cols, table, table],
        out_specs=pl.BlockSpec((1, seq, PAIR), lambda bi, p: (bi, 0, p)),
        out_shape=jax.ShapeDtypeStruct((b, seq, n_pairs * PAIR), BF16),
        scratch_shapes=[pltpu.VMEM((LANES, PAIR), F32), pltpu.VMEM((2, seq, PAIR), BF16)]
        + _stat_scratch(tq),
        compiler_params=_params("arbitrary", "arbitrary"),
        name="moba_attn",
    )(q, k, v, q_feats, k_feats)


def _dil_kernel(q_ref, k_ref, v_ref, o_ref, lse_ref, *, dil, slopes, n_q):
    wb = WIN_BLOCK
    seq = q_ref.shape[1]
    nb = seq // wb
    steps = (lax.broadcasted_iota(jnp.int32, (wb, 2 * wb), 0) + wb
             - lax.broadcasted_iota(jnp.int32, (wb, 2 * wb), 1))
    band_ok = (steps >= 0) & (steps <= wb)
    steps_f = (steps * dil).astype(F32)
    blk = lax.broadcasted_iota(jnp.int32, (nb, 1, 2 * wb), 0)
    key = lax.broadcasted_iota(jnp.int32, (nb, 1, 2 * wb), 2)
    no_prev = jnp.where((blk % n_q == 0) & (key < wb), -NEG_INF, 0.0)
    first = _head_masks()[0]

    def band(ref, cols):
        cur = ref[0, :, cols]
        prev = jnp.concatenate([cur[0:wb], cur[0:seq - wb]], axis=0)
        return jnp.concatenate([prev.reshape(nb, wb, PAIR), cur.reshape(nb, wb, PAIR)], axis=1)

    for pair in range(2):
        cols = slice(pair * PAIR, (pair + 1) * PAIR)
        qs = [qh.reshape(nb, wb, PAIR) for qh in _split_heads(q_ref[0, :, cols])]
        kband, vband = band(k_ref, cols), band(v_ref, cols)
        outs, lses = [], []
        for hh in range(2):
            bias = jnp.where(band_ok, (slopes[2 * pair + hh] * LOG2E) * steps_f, -NEG_INF)
            s = jnp.einsum("bqd,bkd->bqk", qs[hh], kband, preferred_element_type=F32)
            s = s - bias[None] - no_prev
            m = jnp.max(s, axis=-1, keepdims=True)
            p = jnp.exp2(s - m)
            l = jnp.sum(p, axis=-1, keepdims=True)
            o = jnp.einsum("bqk,bkd->bqd", p.astype(BF16), vband, preferred_element_type=F32)
            outs.append(o / l)
            lses.append(jnp.broadcast_to(m + LOG2E * jnp.log(l), (nb, wb, PAIR)))
        o_ref[0, :, cols] = jnp.where(first, outs[0], outs[1]).reshape(seq, PAIR).astype(BF16)
        lse_ref[0, :, cols] = jnp.where(first, lses[0], lses[1]).reshape(seq, PAIR)


def _dilated(q, k, v, group):
    window, dil = DILATED_PAIRS[group]
    assert window // dil == WIN_BLOCK
    b, _, l_sub, w = q.shape
    seq = dil * l_sub
    so = DIL_SLOPE_OFFSETS[group]
    slopes = tuple(float(s) for s in _alibi_slopes(N_ALIBI)[so:so + N_SLOTS_DIL])
    spec = pl.BlockSpec((1, seq, w), lambda bi: (bi, 0, 0))
    flat = lambda a: a.reshape(b, seq, w)
    o, lse = pl.pallas_call(
        functools.partial(_dil_kernel, dil=dil, slopes=slopes, n_q=l_sub // WIN_BLOCK),
        grid=(b,),
        in_specs=[spec, spec, spec],
        out_specs=[spec, spec],
        out_shape=[jax.ShapeDtypeStruct((b, seq, w), BF16), jax.ShapeDtypeStruct((b, seq, w), F32)],
        compiler_params=_params("arbitrary"),
        name=f"dilated_attn_{dil}",
    )(flat(q), flat(k), flat(v))
    return o.reshape(q.shape), lse.reshape(q.shape)


def _outproj_kernel(x_ref, mod_ref, g_ref, of_ref, om_ref, o1, l1, o2, l2, o3, l3, gate_ref,
                    wf_ref, wm_ref, wd_ref, wo_ref, x1_ref, h2_ref, perm_scr):
    d = x_ref.shape[1]

    def natural(ref):
        dil, n = ref.shape[1], ref.shape[2]
        if dil == 1:
            return ref[0, 0].astype(F32)
        halves = ref.shape[3] // LANES
        for r in range(dil):
            blk = ref[0, r].astype(F32)
            for c in range(halves):
                perm_scr[c, pl.ds(r, n, stride=dil), :] = blk[:, c * LANES:(c + 1) * LANES]
        return jnp.concatenate([perm_scr[c] for c in range(halves)], axis=1)

    lses = [natural(l) for l in (l1, l2, l3)]
    lmax = jnp.maximum(jnp.maximum(lses[0], lses[1]), lses[2])
    e1, e2, e3 = [jnp.exp2(l - lmax) for l in lses]
    den = e1 + e2 + e3
    o_dil = (e1 / den) * natural(o1) + (e2 / den) * natural(o2) + (e3 / den) * natural(o3)
    y = (gate_ref[:, 0:d].astype(F32) * jnp.dot(of_ref[...], wf_ref[...], preferred_element_type=F32)
         + gate_ref[:, d:2 * d].astype(F32) * jnp.dot(om_ref[...], wm_ref[...], preferred_element_type=F32)
         + gate_ref[:, 2 * d:3 * d].astype(F32)
         * jnp.dot(o_dil.astype(BF16), wd_ref[...], preferred_element_type=F32))
    out = jnp.dot(y.astype(BF16), wo_ref[...], preferred_element_type=F32)
    x1 = x_ref[...] + mod_ref[0, 2:3, :] * out
    x1_ref[...] = x1
    h2_ref[...] = _mod_norm(x1, g_ref[...], mod_ref[0, 4:5, :], mod_ref[0, 3:4, :]).astype(BF16)


def _outproj(x2, mod_l, g_norm, o_fox, o_moba, dil_outs, gates, w_f, w_m, w_d, w_o, seq, tm):
    t, d = x2.shape
    per_b = seq // tm
    row = lambda i: (i, 0)
    fix = lambda i: (0, 0)
    dil_args, dil_specs = [], []
    for o, lse in dil_outs:
        _, dil, _, w = o.shape
        dil_args += [o, lse]
        dil_specs += [pl.BlockSpec((1, dil, tm // dil, w), lambda i: (i // per_b, 0, i % per_b, 0))] * 2
    return pl.pallas_call(
        _outproj_kernel,
        grid=(t // tm,),
        in_specs=[pl.BlockSpec((tm, d), row),
                  pl.BlockSpec((1, 6, d), lambda i: (i // per_b, 0, 0)),
                  pl.BlockSpec((1, d), fix),
                  pl.BlockSpec((tm, o_fox.shape[1]), row),
                  pl.BlockSpec((tm, o_moba.shape[1]), row)] + dil_specs + [
                  pl.BlockSpec((tm, N_BRANCH * d), row),
                  pl.BlockSpec(w_f.shape, fix), pl.BlockSpec(w_m.shape, fix),
                  pl.BlockSpec(w_d.shape, fix), pl.BlockSpec(w_o.shape, fix)],
        out_specs=[pl.BlockSpec((tm, d), row), pl.BlockSpec((tm, d), row)],
        out_shape=[jax.ShapeDtypeStruct((t, d), F32), jax.ShapeDtypeStruct((t, d), BF16)],
        scratch_shapes=[pltpu.VMEM((N_SLOTS_DIL * HEAD_DIM // LANES, tm, LANES), F32)],
        compiler_params=_params("arbitrary"),
        name="outproj",
    )(x2, mod_l, g_norm, o_fox, o_moba, *dil_args, gates, w_f, w_m, w_d, w_o)


def _router_kernel(h_ref, w_ref, b_ref, o_ref, cnt_ref, *, n_exp):
    logits = jnp.dot(h_ref[...], w_ref[...], preferred_element_type=F32) + b_ref[...]
    lane = lax.broadcasted_iota(jnp.int32, logits.shape, 1)
    logits = jnp.where(lane < n_exp, logits, -jnp.inf)

    def top(vals):
        m = jnp.max(vals, axis=-1, keepdims=True)
        idx = jnp.min(jnp.where(vals == m, lane, LANES), axis=-1, keepdims=True)
        return m, lane == idx

    m1, hot1 = top(logits)
    m2, hot2 = top(jnp.where(hot1, -jnp.inf, logits))
    e2 = jnp.exp(m2 - m1)
    cw = jnp.where(hot1, 1.0 / (1.0 + e2), 0.0) + jnp.where(hot2, e2 / (1.0 + e2), 0.0)
    o_ref[...] = cw
    sub = cnt_ref.shape[0] // 8
    rows = cw.shape[0] // sub
    for s in range(sub):
        n_tok = jnp.sum(jnp.where(cw[s * rows:(s + 1) * rows] > 0.0, 1.0, 0.0), axis=0, keepdims=True)
        cnt_ref[s * 8:(s + 1) * 8, :] = jnp.broadcast_to(n_tok, (8, LANES)).astype(jnp.int32)


def _router(h2, w_r, b_r, n_exp, tile):
    t, d = h2.shape
    tm = 4 * tile
    cw, cnt = pl.pallas_call(
        functools.partial(_router_kernel, n_exp=n_exp),
        grid=(t // tm,),
        in_specs=[pl.BlockSpec((tm, d), lambda i: (i, 0)),
                  pl.BlockSpec((d, LANES), lambda i: (0, 0)),
                  pl.BlockSpec((1, LANES), lambda i: (0, 0))],
        out_specs=[pl.BlockSpec((tm, LANES), lambda i: (i, 0)),
                   pl.BlockSpec((8 * tm // tile, LANES), lambda i: (i, 0))],
        out_shape=[jax.ShapeDtypeStruct((t, LANES), F32),
                   jax.ShapeDtypeStruct((t // tile * 8, LANES), jnp.int32)],
        compiler_params=_params("arbitrary"),
        name="router",
    )(h2, w_r, b_r)
    return cw, cnt.reshape(t // tile, 8, LANES)[:, 0, :n_exp]


MOE_TILE = 512
MOE_ROWS = 512
MOE_PUT = 128
MOE_GET = 256
MOE_ALIGN = 16


def _moe_layout(cnt, n_rows_static):
    n_tiles, n_exp = cnt.shape
    padded = (cnt + MOE_ALIGN - 1) // MOE_ALIGN * MOE_ALIGN
    length = jnp.sum(padded, axis=0)
    span = (length + (MOE_PUT - MOE_ALIGN) + MOE_ROWS - 1) // MOE_ROWS * MOE_ROWS
    start = jnp.cumsum(span) - span
    off = start[None, :] + jnp.cumsum(padded, axis=0) - padded
    n_steps = n_rows_static // MOE_ROWS
    first = start // MOE_ROWS
    step = jnp.arange(n_steps, dtype=jnp.int32)
    expert = jnp.sum((first[None, :] <= step[:, None]).astype(jnp.int32), axis=1) - 1
    active = step < (first + (length + MOE_ROWS - 1) // MOE_ROWS)[expert]
    return (off.reshape(-1).astype(jnp.int32), cnt.reshape(-1).astype(jnp.int32),
            expert.astype(jnp.int32), active.astype(jnp.int32))


def _dispatch_kernel(off_ref, cnt_ref, h_ref, cw_ref, u_ref, xs_in, xs_ref, stage, sems, *, n_exp):
    del xs_in
    i = pl.program_id(0)
    n_put = MOE_TILE // MOE_PUT
    routed = jnp.where(cw_ref[...].T[0:8] > 0.0, 1.0, 0.0)
    pos = jnp.dot(routed.astype(BF16), u_ref[...], preferred_element_type=F32)
    row = lax.broadcasted_iota(jnp.int32, (MOE_PUT, MOE_TILE), 0).astype(F32)
    h = h_ref[...]

    def copy(e, s):
        slot = e * n_put + s
        dst = pl.multiple_of(off_ref[i * n_exp + e] + s * MOE_PUT, MOE_ALIGN)
        return pltpu.make_async_copy(stage.at[slot], xs_ref.at[pl.ds(dst, MOE_PUT), :], sems.at[slot])

    for e in range(n_exp):
        for s in range(n_put):
            @pl.when(s * MOE_PUT < cnt_ref[i * n_exp + e])
            def _():
                take = (routed[e:e + 1, :] > 0.5) & (pos[e:e + 1, :] == row + float(s * MOE_PUT))
                onehot = jnp.where(take, 1.0, 0.0).astype(BF16)
                stage[e * n_put + s] = jnp.dot(onehot, h, preferred_element_type=F32).astype(BF16)
                copy(e, s).start()

    for e in range(n_exp):
        for s in range(n_put):
            @pl.when(s * MOE_PUT < cnt_ref[i * n_exp + e])
            def _():
                copy(e, s).wait()


def _dispatch(h2, cw, off, cnt, n_exp, n_rows):
    t, d = h2.shape
    u = jnp.asarray(np.arange(MOE_TILE)[:, None] < np.arange(MOE_TILE)[None, :], BF16)
    n_slots = n_exp * (MOE_TILE // MOE_PUT)
    return pl.pallas_call(
        functools.partial(_dispatch_kernel, n_exp=n_exp),
        grid_spec=pltpu.PrefetchScalarGridSpec(
            num_scalar_prefetch=2,
            grid=(t // MOE_TILE,),
            in_specs=[pl.BlockSpec((MOE_TILE, d), lambda i, o, c: (i, 0)),
                      pl.BlockSpec((MOE_TILE, LANES), lambda i, o, c: (i, 0)),
                      pl.BlockSpec((MOE_TILE, MOE_TILE), lambda i, o, c: (0, 0)),
                      pl.BlockSpec(memory_space=pl.ANY)],
            out_specs=pl.BlockSpec(memory_space=pl.ANY),
            scratch_shapes=[pltpu.VMEM((n_slots, MOE_PUT, d), BF16),
                            pltpu.SemaphoreType.DMA((n_slots,))]),
        out_shape=jax.ShapeDtypeStruct((n_rows, d), BF16),
        input_output_aliases={5: 0},
        compiler_params=_params("arbitrary"),
        name="moe_dispatch",
    )(off, cnt, h2, cw, u, jnp.zeros((n_rows, d), BF16))


FF_CHUNK = 768


def _swiglu(x, wg_ref, wu_ref, wd_ref):
    ff = wg_ref.shape[2]
    y = None
    for c0 in range(0, ff, FF_CHUNK):
        c1 = min(c0 + FF_CHUNK, ff)
        gate = jnp.dot(x, wg_ref[0, :, c0:c1].astype(BF16), preferred_element_type=F32)
        up = jnp.dot(x, wu_ref[0, :, c0:c1].astype(BF16), preferred_element_type=F32)
        a = ((gate * _sigmoid(gate)) * up).astype(BF16)
        part = jnp.dot(a, wd_ref[0, c0:c1, :].astype(BF16), preferred_element_type=F32)
        y = part if y is None else y + part
    return y


def _experts_kernel(exp_ref, act_ref, xs_ref, wg_ref, wu_ref, wd_ref, ys_ref):
    g = pl.program_id(0)

    @pl.when(act_ref[g] > 0)
    def _():
        ys_ref[...] = _swiglu(xs_ref[...], wg_ref, wu_ref, wd_ref).astype(BF16)

    @pl.when(act_ref[g] == 0)
    def _():
        ys_ref[...] = jnp.zeros(ys_ref.shape, BF16)


def _experts(xs, expert, active, w_g, w_u, w_d):
    n_rows, d = xs.shape
    _, _, ff = w_g.shape
    wmap = lambda g, ex, ac: (ex[g], 0, 0)
    single = lambda shape: pl.BlockSpec(shape, wmap, pipeline_mode=pl.Buffered(1))
    return pl.pallas_call(
        _experts_kernel,
        grid_spec=pltpu.PrefetchScalarGridSpec(
            num_scalar_prefetch=2,
            grid=(n_rows // MOE_ROWS,),
            in_specs=[pl.BlockSpec((MOE_ROWS, d), lambda g, ex, ac: (g, 0)),
                      single((1, d, ff)), single((1, d, ff)), pl.BlockSpec((1, ff, d), wmap)],
            out_specs=pl.BlockSpec((MOE_ROWS, d), lambda g, ex, ac: (g, 0))),
        out_shape=jax.ShapeDtypeStruct((n_rows, d), BF16),
        compiler_params=_params("arbitrary"),
        name="moe_experts",
    )(expert, active, xs, w_g, w_u, w_d)


def _combine_kernel(off_ref, cnt_ref, x_ref, mod_ref, cw_ref, l_ref, ys_ref, o_ref, buf, sems, acc_scr,
                    *, n_exp):
    i = pl.program_id(0)
    n_get = MOE_TILE // MOE_GET

    def copy(e, s):
        slot = e * n_get + s
        src = pl.multiple_of(off_ref[i * n_exp + e] + s * MOE_GET, MOE_ALIGN)
        return pltpu.make_async_copy(ys_ref.at[pl.ds(src, MOE_GET), :], buf.at[slot], sems.at[slot])

    for e in range(n_exp):
        for s in range(n_get):
            @pl.when(s * MOE_GET < cnt_ref[i * n_exp + e])
            def _():
                copy(e, s).start()

    cw = cw_ref[...]
    routed = jnp.where(cw > 0.0, 1.0, 0.0)
    pos = jnp.dot(l_ref[...], routed.astype(BF16), preferred_element_type=F32)
    col = lax.broadcasted_iota(jnp.int32, (MOE_TILE, MOE_GET), 1).astype(F32)
    acc_scr[...] = jnp.zeros(acc_scr.shape, F32)
    for e in range(n_exp):
        for s in range(n_get):
            @pl.when(s * MOE_GET < cnt_ref[i * n_exp + e])
            def _():
                copy(e, s).wait()
                take = (cw[:, e:e + 1] > 0.0) & (pos[:, e:e + 1] == col + float(s * MOE_GET))
                onehot = jnp.where(take, 1.0, 0.0).astype(BF16)
                acc_scr[...] += cw[:, e:e + 1] * jnp.dot(onehot, buf[e * n_get + s],
                                                         preferred_element_type=F32)
    o_ref[...] = x_ref[...] + mod_ref[0, 5:6, :] * acc_scr[...]


def _combine(x1, mod_l, cw, ys, off, cnt, n_exp, seq):
    t, d = x1.shape
    per_b = seq // MOE_TILE
    low = jnp.asarray(np.arange(MOE_TILE)[:, None] > np.arange(MOE_TILE)[None, :], BF16)
    n_slots = n_exp * (MOE_TILE // MOE_GET)
    return pl.pallas_call(
        functools.partial(_combine_kernel, n_exp=n_exp),
        grid_spec=pltpu.PrefetchScalarGridSpec(
            num_scalar_prefetch=2,
            grid=(t // MOE_TILE,),
            in_specs=[pl.BlockSpec((MOE_TILE, d), lambda i, o, c: (i, 0)),
                      pl.BlockSpec((1, 6, d), lambda i, o, c: (i // per_b, 0, 0)),
                      pl.BlockSpec((MOE_TILE, LANES), lambda i, o, c: (i, 0)),
                      pl.BlockSpec((MOE_TILE, MOE_TILE), lambda i, o, c: (0, 0)),
                      pl.BlockSpec(memory_space=pl.ANY)],
            out_specs=pl.BlockSpec((MOE_TILE, d), lambda i, o, c: (i, 0)),
            scratch_shapes=[pltpu.VMEM((n_slots, MOE_GET, d), BF16),
                            pltpu.SemaphoreType.DMA((n_slots,)),
                            pltpu.VMEM((MOE_TILE, d), F32)]),
        out_shape=jax.ShapeDtypeStruct((t, d), F32),
        compiler_params=_params("arbitrary"),
        name="moe_combine",
    )(off, cnt, x1, mod_l, cw, low, ys)


def _moe(x1, h2, mod_l, w_r, b_r, w_g, w_u, w_d, seq):
    t, d = x1.shape
    n_exp = w_g.shape[0]
    n_tiles = t // MOE_TILE
    bound = (TOP_K * t + n_tiles * n_exp * (MOE_ALIGN - 1)
             + n_exp * (MOE_PUT - MOE_ALIGN + MOE_ROWS - 1))
    n_rows = (bound + MOE_ROWS - 1) // MOE_ROWS * MOE_ROWS + MOE_ROWS
    cw, cnt = _router(h2, w_r, b_r, n_exp, MOE_TILE)
    off, cnt, expert, active = _moe_layout(cnt, n_rows)
    xs = _dispatch(h2, cw, off, cnt, n_exp, n_rows)
    ys = _experts(xs, expert, active, w_g, w_u, w_d)
    return _combine(x1, mod_l, cw, ys, off, cnt, n_exp, seq)


def _ffn_kernel(x_ref, h_ref, mod_ref, wg_ref, wu_ref, wd_ref, o_ref):
    o_ref[...] = x_ref[...] + mod_ref[0, 5:6, :] * _swiglu(h_ref[...], wg_ref, wu_ref, wd_ref)


def _ffn(x1, h2, mod_l, w_g, w_u, w_d, seq, tm):
    t, d = x1.shape
    per_b = seq // tm
    row = lambda i: (i, 0)
    whole = lambda w: pl.BlockSpec(w.shape, lambda i: (0, 0, 0), pipeline_mode=pl.Buffered(1))
    return pl.pallas_call(
        _ffn_kernel,
        grid=(t // tm,),
        in_specs=[pl.BlockSpec((tm, d), row), pl.BlockSpec((tm, d), row),
                  pl.BlockSpec((1, 6, d), lambda i: (i // per_b, 0, 0)),
                  whole(w_g), whole(w_u), whole(w_d)],
        out_specs=pl.BlockSpec((tm, d), row),
        out_shape=jax.ShapeDtypeStruct((t, d), F32),
        compiler_params=_params("arbitrary"),
        name="swiglu",
    )(x1, h2, mod_l, w_g, w_u, w_d)


def _pad_cols(a, width):
    return jnp.pad(a, ((0, 0), (0, width - a.shape[1])))


def kernel(x, c, w_ada, b_ada, norm_mix, norm_ffn, w_in, b_fgate, q_gain, k_gain, w_br_fox, w_br_moba,
           w_br_dil, w_out, w_ffn_gate, w_ffn_up, w_ffn_down, w_router, b_router, w_exp_gate,
           w_exp_up, w_exp_down):
    b, seq, d = x.shape
    depth = w_ada.shape[0]
    t = b * seq
    tm, tq = TOKEN_TILE, ATTN_CHUNK
    n_pairs_fox = N_HEADS_FOX // 2
    n_pairs_moba = N_HEADS_MOBA // 2

    mod = _adaln(c, w_ada, b_ada).reshape(depth, b, 6, d)
    x2 = x.reshape(t, d)
    w_in_t = jnp.transpose(w_in, (2, 0, 1))
    for l in range(depth):
        wl = w_in_t[:, l, :].astype(BF16)
        w_gate = wl[3 * MIX_WIDTH + N_HEADS_FOX:]
        b_f = _pad_cols(b_fgate[l].reshape(1, -1), LANES)
        outs = _inproj(x2, mod[l], norm_mix[l].reshape(1, d), wl, w_gate, q_gain[l].reshape(1, -1),
                       k_gain[l].reshape(1, -1), b_f, seq, tm)
        q_m, k_m, v_m = (o.reshape(b, seq, -1) for o in outs[0:3])
        gates, lf = outs[12], outs[13]

        cum = _decay(lf.reshape(b, seq, LANES))
        o_fox = _fox(q_m, k_m, v_m, cum, 0, n_pairs_fox, tq).reshape(t, -1)
        o_moba = _moba(q_m, k_m, v_m, n_pairs_fox, n_pairs_moba, tq).reshape(t, -1)
        dil_outs = [_dilated(*outs[3 + 3 * g:6 + 3 * g], g) for g in range(len(DILATED_PAIRS))]

        x1, h2 = _outproj(x2, mod[l], norm_ffn[l].reshape(1, d), o_fox, o_moba, dil_outs, gates,
                          w_br_fox[l].astype(BF16), w_br_moba[l].astype(BF16),
                          w_br_dil[l].astype(BF16), w_out[l].astype(BF16), seq, tm)
        i = l // 2
        if l % 2 == 0:
            x2 = _ffn(x1, h2, mod[l], w_ffn_gate[i:i + 1].astype(BF16),
                      w_ffn_up[i:i + 1].astype(BF16), w_ffn_down[i:i + 1].astype(BF16), seq, FFN_TILE)
        else:
            x2 = _moe(x1, h2, mod[l], _pad_cols(w_router[i], LANES).astype(BF16),
                      _pad_cols(b_router[i].reshape(1, -1), LANES), w_exp_gate[i], w_exp_up[i],
                      w_exp_down[i], seq)
    return x2.reshape(b, seq, d)
```

```python
import functools

import numpy as np
import jax
import jax.numpy as jnp
from jax import lax
from jax.experimental import pallas as pl
from jax.experimental.pallas import tpu as pltpu

HEAD_DIM = 64
N_HEADS_FOX = 6
N_HEADS_MOBA = 6
DILATED_PAIRS = ((128, 1), (512, 4), (2048, 16))
N_SLOTS_DIL = 4
N_HEADS_DIL = N_SLOTS_DIL * len(DILATED_PAIRS)
N_HEADS = N_HEADS_FOX + N_HEADS_MOBA + N_HEADS_DIL
MIX_WIDTH = N_HEADS * HEAD_DIM
N_BRANCH = 3
MOBA_BLOCK = 256
MOBA_TOPK = 3
WIN_BLOCK = 128
N_ALIBI = N_HEADS_MOBA + N_HEADS_DIL
DIL_SLOPE_OFFSETS = (0, N_SLOTS_DIL, 2 * N_SLOTS_DIL + N_HEADS_MOBA)
MOBA_SLOPE_OFFSET = 2 * N_SLOTS_DIL
TOP_K = 2
RMS_EPS = 1e-6
NEG_INF = -1e30
LOG2E = 1.4426950408889634

LANES = 128
PAIR = 2 * HEAD_DIM
VMEM_LIMIT = 56 * 1024 * 1024
TOKEN_TILE = 512
ATTN_CHUNK = 512
FFN_TILE = 1024

F32 = jnp.float32
BF16 = jnp.bfloat16
_NT = (((1,), (1,)), ((), ()))


def _alibi_slopes(n):
    return (2.0 ** (-8.0 * np.arange(1, n + 1) / n)).astype(np.float32)


def _sigmoid(x):
    return 1.0 / (1.0 + jnp.exp(-x))


def _params(*sem):
    return pltpu.CompilerParams(dimension_semantics=sem, vmem_limit_bytes=VMEM_LIMIT)


def _adaln_kernel(c_ref, w_ref, b_ref, o_ref):
    c = c_ref[...]
    cond = c * _sigmoid(c)
    o_ref[0] = jnp.dot(cond, w_ref[0], precision=lax.Precision.HIGHEST,
                       preferred_element_type=F32) + b_ref[0]


def _adaln(c, w_ada, b_ada):
    depth, d, n = w_ada.shape
    b = c.shape[0]
    tn = 1536
    return pl.pallas_call(
        _adaln_kernel,
        grid=(depth, n // tn),
        in_specs=[pl.BlockSpec((b, d), lambda l, j: (0, 0)),
                  pl.BlockSpec((1, d, tn), lambda l, j: (l, 0, j)),
                  pl.BlockSpec((1, 1, tn), lambda l, j: (l, 0, j))],
        out_specs=pl.BlockSpec((1, b, tn), lambda l, j: (l, 0, j)),
        out_shape=jax.ShapeDtypeStruct((depth, b, n), F32),
        compiler_params=_params("arbitrary", "arbitrary"),
        name="adaln",
    )(c, w_ada, b_ada.reshape(depth, 1, n))


def _mod_norm(x, g, scale, shift):
    ms = jnp.mean(x * x, axis=-1, keepdims=True)
    return (x * lax.rsqrt(ms + RMS_EPS) * g) * (1.0 + scale) + shift


def _inproj_kernel(x_ref, mod_ref, g_ref, w_ref, wg_ref, qg_ref, kg_ref, bf_ref, e_ref,
                   q_ref, k_ref, v_ref, qd1, kd1, vd1, qd2, kd2, vd2, qd3, kd3, vd3,
                   gate_ref, lf_ref, perm_scr, *, n_main):
    h = _mod_norm(x_ref[...], g_ref[...], mod_ref[0, 1:2, :], mod_ref[0, 0:1, :]).astype(BF16)
    tm = x_ref.shape[0]
    qk_chunk = 4 * HEAD_DIM
    q_outs = (q_ref, qd1, qd2, qd3)
    k_outs = (k_ref, kd1, kd2, kd3)
    v_outs = (v_ref, vd1, vd2, vd3)

    def put(outs, col, val):
        if col < n_main:
            outs[0][:, col:col + qk_chunk] = val.astype(BF16)
            return
        g = (col - n_main) // qk_chunk
        dil = DILATED_PAIRS[g][1]
        if dil == 1:
            outs[1 + g][0, 0] = val.astype(BF16)
            return
        halves = qk_chunk // LANES
        for c in range(halves):
            perm_scr[c] = val[:, c * LANES:(c + 1) * LANES]
        for r in range(dil):
            outs[1 + g][0, r] = jnp.concatenate(
                [perm_scr[c, pl.ds(r, tm // dil, stride=dil), :] for c in range(halves)],
                axis=1).astype(BF16)

    def proj(wt_ref, col0, width):
        return lax.dot_general(h, wt_ref[col0:col0 + width, :], _NT, preferred_element_type=F32)

    wide = MIX_WIDTH // 2
    for half in range(2):
        yq = proj(w_ref, half * wide, wide)
        yk = proj(w_ref, MIX_WIDTH + half * wide, wide)
        yv = proj(w_ref, 2 * MIX_WIDTH + half * wide, wide)
        for c in range(wide // qk_chunk):
            col = half * wide + c * qk_chunk
            yq_c = yq[:, c * qk_chunk:(c + 1) * qk_chunk]
            yk_c = yk[:, c * qk_chunk:(c + 1) * qk_chunk]
            sq = jnp.concatenate([yq_c * yq_c, yk_c * yk_c], axis=0).astype(BF16)
            ss = jnp.dot(sq, e_ref[...], preferred_element_type=F32)
            r = lax.rsqrt(ss * (1.0 / HEAD_DIM) + RMS_EPS)
            put(q_outs, col, (yq_c * r[0:tm] * qg_ref[:, col:col + qk_chunk])
                * (HEAD_DIM ** -0.5 * LOG2E))
            put(k_outs, col, yk_c * r[tm:2 * tm] * kg_ref[:, col:col + qk_chunk])
            put(v_outs, col, yv[:, c * qk_chunk:(c + 1) * qk_chunk])

    gchunk = 1024
    for c in range(wg_ref.shape[0] // gchunk):
        y = proj(wg_ref, c * gchunk, gchunk)
        gate_ref[:, c * gchunk:(c + 1) * gchunk] = _sigmoid(y).astype(BF16)

    f = proj(w_ref, 3 * MIX_WIDTH, LANES) + bf_ref[...]
    lf_ref[...] = jnp.minimum(f, 0.0) - jnp.log(1.0 + jnp.exp(-jnp.abs(f)))


def _inproj(x2, mod_l, g_norm, w_t, w_gate, q_gain, k_gain, b_f, seq, tm):
    t, d = x2.shape
    n_main = (N_HEADS_FOX + N_HEADS_MOBA) * HEAD_DIM
    dil_w = N_SLOTS_DIL * HEAD_DIM
    e = (np.arange(dil_w)[:, None] // HEAD_DIM == np.arange(dil_w)[None, :] // HEAD_DIM)
    e = jnp.asarray(e, BF16)
    per_b = seq // tm
    row = lambda i: (i, 0)
    fix = lambda i: (0, 0)
    qkv_shapes = [jax.ShapeDtypeStruct((t, n_main), BF16)] * 3
    qkv_specs = [pl.BlockSpec((tm, n_main), row)] * 3
    for _, dil in DILATED_PAIRS:
        assert tm % (16 * dil) == 0
        qkv_shapes += [jax.ShapeDtypeStruct((t // seq, dil, seq // dil, dil_w), BF16)] * 3
        qkv_specs += [pl.BlockSpec((1, dil, tm // dil, dil_w),
                                   lambda i: (i // per_b, 0, i % per_b, 0))] * 3
    out_shapes = qkv_shapes + [jax.ShapeDtypeStruct((t, N_BRANCH * d), BF16),
                               jax.ShapeDtypeStruct((t, LANES), F32)]
    out_specs = qkv_specs + [pl.BlockSpec((tm, N_BRANCH * d), row), pl.BlockSpec((tm, LANES), row)]
    return pl.pallas_call(
        functools.partial(_inproj_kernel, n_main=n_main),
        grid=(t // tm,),
        in_specs=[pl.BlockSpec((tm, d), row),
                  pl.BlockSpec((1, 6, d), lambda i: (i // per_b, 0, 0)),
                  pl.BlockSpec((1, d), fix),
                  pl.BlockSpec(w_t.shape, fix, pipeline_mode=pl.Buffered(1)),
                  pl.BlockSpec(w_gate.shape, fix, pipeline_mode=pl.Buffered(1)),
                  pl.BlockSpec((1, MIX_WIDTH), fix),
                  pl.BlockSpec((1, MIX_WIDTH), fix),
                  pl.BlockSpec((1, LANES), fix),
                  pl.BlockSpec(e.shape, fix)],
        out_specs=out_specs,
        out_shape=out_shapes,
        scratch_shapes=[pltpu.VMEM((dil_w // LANES, tm, LANES), F32)],
        compiler_params=_params("arbitrary"),
        name="inproj",
    )(x2, mod_l, g_norm, w_t, w_gate, q_gain, k_gain, b_f, e)


def _decay_kernel(lf_ref, tri_ref, o_ref, *, blk):
    carry = jnp.zeros((1, LANES), F32)
    for j in range(lf_ref.shape[1] // blk):
        c = jnp.dot(tri_ref[...], lf_ref[0, j * blk:(j + 1) * blk, :],
                    precision=lax.Precision.HIGHEST, preferred_element_type=F32) + carry
        o_ref[0, j * blk:(j + 1) * blk, :] = c
        carry = c[blk - 1:blk, :]


def _decay(lf, blk=256):
    b, seq, _ = lf.shape
    tri = jnp.asarray(np.arange(blk)[:, None] >= np.arange(blk)[None, :], F32)
    return pl.pallas_call(
        functools.partial(_decay_kernel, blk=blk),
        grid=(b,),
        in_specs=[pl.BlockSpec((1, seq, LANES), lambda i: (i, 0, 0)),
                  pl.BlockSpec((blk, blk), lambda i: (0, 0))],
        out_specs=pl.BlockSpec((1, seq, LANES), lambda i: (i, 0, 0)),
        out_shape=jax.ShapeDtypeStruct((b, seq, LANES), F32),
        compiler_params=_params("arbitrary"),
        name="fox_decay",
    )(lf, tri)


def _head_masks():
    lane = lax.broadcasted_iota(jnp.int32, (1, PAIR), 1)
    return lane < HEAD_DIM, lane >= HEAD_DIM


def _split_heads(q):
    return [jnp.where(m, q, jnp.zeros_like(q)) for m in _head_masks()]


def _feat_base(hh):
    return HEAD_DIM * (1 - hh)


def _three_bf16(x):
    hi = x.astype(BF16).astype(F32)
    mid = (x - hi).astype(BF16).astype(F32)
    lo = (x - hi - mid).astype(BF16).astype(F32)
    return hi, mid, lo


def _place(idx, base, parts):
    out = jnp.zeros(jnp.broadcast_shapes(idx.shape, jnp.shape(parts[0])), F32)
    for r, part in enumerate(parts):
        out = jnp.where(idx == base + r, part, out)
    return out


def _flash_pair(c, qx, kx_scr, v_ref, o_ref, keep_own, own_first, m_scr, acc_scr, p_scr, a_scr, tq):
    _reset(m_scr, acc_scr)

    def scores(j, own, slot):
        for hh in range(2):
            s = lax.dot_general(qx[hh], kx_scr[hh, j * tq:(j + 1) * tq, :], _NT,
                                preferred_element_type=F32)
            if own:
                s = jnp.where(keep_own, s, NEG_INF)
            _softmax_stage(s, m_scr.at[hh], p_scr.at[slot, hh], a_scr.at[slot, hh])

    def values(j, slot):
        vs = _with_ones(v_ref[0, j * tq:(j + 1) * tq, :])
        for hh in range(2):
            _value_stage(vs[hh], acc_scr.at[hh], p_scr.at[slot, hh], a_scr.at[slot, hh])

    order = [c] + list(range(c)) if own_first else list(range(c + 1))
    scores(order[0], order[0] == c, 0)
    for n in range(1, len(order)):
        scores(order[n], order[n] == c, n % 2)
        values(order[n - 1], (n - 1) % 2)
    values(order[-1], (len(order) - 1) % 2)
    o_ref[0, c * tq:(c + 1) * tq, :] = _merge_pair(acc_scr)


def _with_ones(v):
    return [jnp.where(m, v, jnp.ones_like(v)) for m in _head_masks()]


def _softmax_stage(s, m_ref, p_ref, a_ref):
    m_prev = m_ref[...]
    m_new = jnp.maximum(m_prev, jnp.max(s, axis=-1, keepdims=True))
    p = jnp.exp2(s - jnp.concatenate([m_new] * (s.shape[1] // LANES), axis=1))
    m_ref[...] = m_new
    p_ref[...] = p.astype(BF16)
    a_ref[...] = jnp.exp2(m_prev - m_new)


def _value_stage(v_ones, acc_ref, p_ref, a_ref):
    acc_ref[...] = a_ref[...] * acc_ref[...] + jnp.dot(p_ref[...], v_ones, preferred_element_type=F32)


def _reset(m_scr, acc_scr):
    m_scr[...] = jnp.full(m_scr.shape, NEG_INF, F32)
    acc_scr[...] = jnp.zeros(acc_scr.shape, F32)


def _merge_pair(acc_scr):
    first = _head_masks()[0]
    a0, a1 = acc_scr[0], acc_scr[1]
    return jnp.where(first, a0 / a0[:, HEAD_DIM:HEAD_DIM + 1], a1 / a1[:, 0:1]).astype(BF16)


def _stat_scratch(tq):
    return [pltpu.VMEM((2, tq, LANES), F32), pltpu.VMEM((2, tq, PAIR), F32),
            pltpu.VMEM((2, 2, tq, tq), BF16), pltpu.VMEM((2, 2, tq, LANES), F32)]


def _fox_kernel(q_ref, k_ref, v_ref, cum_ref, o_ref, kx_scr, m_scr, acc_scr, p_scr, a_scr, *, tq):
    pair = pl.program_id(1)
    masks = _head_masks()
    lane = lax.broadcasted_iota(jnp.int32, (1, PAIR), 1)
    k = k_ref[0]
    cum = cum_ref[0]
    lane_s = lax.broadcasted_iota(jnp.int32, cum.shape, 1)
    for hh in range(2):
        f_s = jnp.sum(jnp.where(lane_s == 2 * pair + hh, cum, 0.0), axis=-1, keepdims=True)
        feats = _place(lane, _feat_base(hh), _three_bf16(-LOG2E * f_s))
        kx_scr[hh] = jnp.where(masks[hh], k, feats.astype(BF16))

    causal = (lax.broadcasted_iota(jnp.int32, (tq, tq), 1)
              <= lax.broadcasted_iota(jnp.int32, (tq, tq), 0))
    for c in range(q_ref.shape[1] // tq):
        q = q_ref[0, c * tq:(c + 1) * tq, :]
        qx = []
        for hh in range(2):
            ones = (lane >= _feat_base(hh)) & (lane < _feat_base(hh) + 3)
            qx.append(jnp.where(masks[hh], q, jnp.where(ones, 1.0, 0.0).astype(BF16)))
        _flash_pair(c, qx, kx_scr, v_ref, o_ref, causal, False, m_scr, acc_scr, p_scr, a_scr, tq)


def _fox(q, k, v, cum, col0, n_pairs, tq):
    b, seq, _ = q.shape
    head_cols = pl.BlockSpec((1, seq, PAIR), lambda bi, p: (bi, 0, col0 + p))
    return pl.pallas_call(
        functools.partial(_fox_kernel, tq=tq),
        grid=(b, n_pairs),
        in_specs=[head_cols, head_cols, head_cols,
                  pl.BlockSpec((1, seq, LANES), lambda bi, p: (bi, 0, 0))],
        out_specs=pl.BlockSpec((1, seq, PAIR), lambda bi, p: (bi, 0, p)),
        out_shape=jax.ShapeDtypeStruct((b, seq, n_pairs * PAIR), BF16),
        scratch_shapes=[pltpu.VMEM((2, seq, PAIR), BF16)] + _stat_scratch(tq),
        compiler_params=_params("arbitrary", "arbitrary"),
        name="fox_attn",
    )(q, k, v, cum)


MOBA_NBLK = 8
ALIBI_PARTS = 5


def _moba_feature_tables(slopes, seq):
    pos = np.arange(seq, dtype=np.float64)

    def parts(x):
        out, rest = [], x.copy()
        for _ in range(ALIBI_PARTS):
            p = rest.astype(np.float32).astype(jnp.bfloat16).astype(np.float64)
            out.append(p)
            rest = rest - p
        assert not rest.any()
        return out

    n_heads = len(slopes)
    qf = np.zeros((n_heads // 2, 2, seq, PAIR), np.float32)
    kf = np.zeros((n_heads // 2, 2, seq, PAIR), np.float32)
    for h, slope in enumerate(slopes):
        base = _feat_base(h % 2)
        q_t, k_t = qf[h // 2, h % 2], kf[h // 2, h % 2]
        k_t[np.arange(seq), base + np.arange(seq) // MOBA_BLOCK] = 1.0
        lo, mid, hi = base + MOBA_NBLK, base + MOBA_NBLK + ALIBI_PARTS, base + MOBA_NBLK + 2 * ALIBI_PARTS
        q_t[:, lo:mid] = 1.0
        slope2 = np.float64(np.float32(slope * LOG2E))
        k_t[:, lo:mid] = np.stack(parts(slope2 * pos), axis=1)
        q_t[:, mid:hi] = np.stack(parts(-slope2 * pos), axis=1)
        k_t[:, mid:hi] = 1.0
    return jnp.asarray(qf, BF16), jnp.asarray(kf, BF16)


def _moba_kernel(q_ref, k_ref, v_ref, qf_ref, kf_ref, o_ref, km_scr, kx_scr, m_scr, acc_scr, p_scr,
                 a_scr, *, tq):
    blk = MOBA_BLOCK
    nbp = MOBA_NBLK
    n_blk = k_ref.shape[1] // blk
    per = tq // blk
    masks = _head_masks()

    km_scr[...] = jnp.zeros(km_scr.shape, F32)
    for n in range(n_blk):
        kb = k_ref[0, n * blk:(n + 1) * blk, :].astype(F32)
        km_scr[n:n + 1, :] = jnp.sum(kb, axis=0, keepdims=True) * (1.0 / blk)
    for hh in range(2):
        kx_scr[hh] = jnp.where(masks[hh], k_ref[0], kf_ref[0, hh])
    km = km_scr[0:nbp, :]
    km3 = jnp.concatenate(list(_three_bf16(km)) + [jnp.zeros_like(km)], axis=0).astype(BF16)

    blk_n = lax.broadcasted_iota(jnp.int32, (nbp, tq), 0)
    lane = lax.broadcasted_iota(jnp.int32, (1, PAIR), 1)
    r = lax.broadcasted_iota(jnp.int32, (tq, tq), 0)
    s = lax.broadcasted_iota(jnp.int32, (tq, tq), 1)
    keep_own = (r // blk != s // blk) | (s <= r)

    for c in range(q_ref.shape[1] // tq):
        q = q_ref[0, c * tq:(c + 1) * tq, :]
        qs = _split_heads(q)
        q_blk = c * per + lax.broadcasted_iota(jnp.int32, (nbp, tq), 1) // blk
        qx = []
        for hh in range(2):
            base = _feat_base(hh)
            g3 = lax.dot_general(km3, qs[hh], _NT, preferred_element_type=F32)
            g = g3[0:nbp] + g3[nbp:2 * nbp] + g3[2 * nbp:3 * nbp]
            cnt = jnp.zeros((nbp, tq), F32)
            for m in range(n_blk):
                gm = g[m:m + 1, :]
                beats = ((gm > g) | ((gm == g) & (blk_n > m))) & (q_blk > m)
                cnt = cnt + jnp.where(beats, 1.0, 0.0)
            visible = ((blk_n < q_blk) & (cnt < float(MOBA_TOPK))) | (blk_n == q_blk)
            hide = jnp.where(visible, 0.0, NEG_INF)
            pads = [jnp.zeros((n, tq), F32) for n in (base, LANES - nbp - base)]
            hide = jnp.concatenate([a for a in (pads[0], hide, pads[1]) if a.shape[0]], axis=0)
            feats = jnp.where((lane >= base) & (lane < base + nbp), hide.T.astype(BF16),
                              qf_ref[0, hh, c * tq:(c + 1) * tq, :])
            qx.append(jnp.where(masks[hh], q, feats))
        _flash_pair(c, qx, kx_scr, v_ref, o_ref, keep_own, True, m_scr, acc_scr, p_scr, a_scr, tq)


def _moba(q, k, v, col0, n_pairs, tq):
    b, seq, _ = q.shape
    blk = MOBA_BLOCK
    assert seq % tq == 0 and tq % blk == 0 and seq // blk <= MOBA_NBLK
    slopes = _alibi_slopes(N_ALIBI)[MOBA_SLOPE_OFFSET:MOBA_SLOPE_OFFSET + 2 * n_pairs]
    q_feats, k_feats = _moba_feature_tables(slopes, seq)
    head_cols = pl.BlockSpec((1, seq, PAIR), lambda bi, p: (bi, 0, col0 + p))
    table = pl.BlockSpec((1, 2, seq, PAIR), lambda bi, p: (p, 0, 0, 0))
    return pl.pallas_call(
        functools.partial(_moba_kernel, tq=tq),
        grid=(b, n_pairs),
        in_specs=[head_cols, head_cols, head_cols, table, table],
        out_specs=pl.BlockSpec((1, seq, PAIR), lambda bi, p: (bi, 0, p)),
        out_shape=jax.ShapeDtypeStruct((b, seq, n_pairs * PAIR), BF16),
        scratch_shapes=[pltpu.VMEM((LANES, PAIR), F32), pltpu.VMEM((2, seq, PAIR), BF16)]
        + _stat_scratch(tq),
        compiler_params=_params("arbitrary", "arbitrary"),
        name="moba_attn",
    )(q, k, v, q_feats, k_feats)


def _dil_kernel(q_ref, k_ref, v_ref, o_ref, lse_ref, *, dil, slopes, n_q):
    wb = WIN_BLOCK
    seq = q_ref.shape[1]
    nb = seq // wb
    steps = (lax.broadcasted_iota(jnp.int32, (wb, 2 * wb), 0) + wb
             - lax.broadcasted_iota(jnp.int32, (wb, 2 * wb), 1))
    band_ok = (steps >= 0) & (steps <= wb)
    steps_f = (steps * dil).astype(F32)
    blk = lax.broadcasted_iota(jnp.int32, (nb, 1, 2 * wb), 0)
    key = lax.broadcasted_iota(jnp.int32, (nb, 1, 2 * wb), 2)
    no_prev = jnp.where((blk % n_q == 0) & (key < wb), -NEG_INF, 0.0)
    first = _head_masks()[0]

    def band(ref, cols):
        cur = ref[0, :, cols]
        prev = jnp.concatenate([cur[0:wb], cur[0:seq - wb]], axis=0)
        return jnp.concatenate([prev.reshape(nb, wb, PAIR), cur.reshape(nb, wb, PAIR)], axis=1)

    for pair in range(2):
        cols = slice(pair * PAIR, (pair + 1) * PAIR)
        qs = [qh.reshape(nb, wb, PAIR) for qh in _split_heads(q_ref[0, :, cols])]
        kband, vband = band(k_ref, cols), band(v_ref, cols)
        outs, lses = [], []
        for hh in range(2):
            bias = jnp.where(band_ok, (slopes[2 * pair + hh] * LOG2E) * steps_f, -NEG_INF)
            s = jnp.einsum("bqd,bkd->bqk", qs[hh], kband, preferred_element_type=F32)
            s = s - bias[None] - no_prev
            m = jnp.max(s, axis=-1, keepdims=True)
            p = jnp.exp2(s - m)
            l = jnp.sum(p, axis=-1, keepdims=True)
            o = jnp.einsum("bqk,bkd->bqd", p.astype(BF16), vband, preferred_element_type=F32)
            outs.append(o / l)
            lses.append(jnp.broadcast_to(m + LOG2E * jnp.log(l), (nb, wb, PAIR)))
        o_ref[0, :, cols] = jnp.where(first, outs[0], outs[1]).reshape(seq, PAIR).astype(BF16)
        lse_ref[0, :, cols] = jnp.where(first, lses[0], lses[1]).reshape(seq, PAIR)


def _dilated(q, k, v, group):
    window, dil = DILATED_PAIRS[group]
    assert window // dil == WIN_BLOCK
    b, _, l_sub, w = q.shape
    seq = dil * l_sub
    so = DIL_SLOPE_OFFSETS[group]
    slopes = tuple(float(s) for s in _alibi_slopes(N_ALIBI)[so:so + N_SLOTS_DIL])
    spec = pl.BlockSpec((1, seq, w), lambda bi: (bi, 0, 0))
    flat = lambda a: a.reshape(b, seq, w)
    o, lse = pl.pallas_call(
        functools.partial(_dil_kernel, dil=dil, slopes=slopes, n_q=l_sub // WIN_BLOCK),
        grid=(b,),
        in_specs=[spec, spec, spec],
        out_specs=[spec, spec],
        out_shape=[jax.ShapeDtypeStruct((b, seq, w), BF16), jax.ShapeDtypeStruct((b, seq, w), F32)],
        compiler_params=_params("arbitrary"),
        name=f"dilated_attn_{dil}",
    )(flat(q), flat(k), flat(v))
    return o.reshape(q.shape), lse.reshape(q.shape)


def _outproj_kernel(x_ref, mod_ref, g_ref, of_ref, om_ref, o1, l1, o2, l2, o3, l3, gate_ref,
                    wf_ref, wm_ref, wd_ref, wo_ref, x1_ref, h2_ref, perm_scr):
    d = x_ref.shape[1]

    def natural(ref):
        dil, n = ref.shape[1], ref.shape[2]
        if dil == 1:
            return ref[0, 0].astype(F32)
        halves = ref.shape[3] // LANES
        for r in range(dil):
            blk = ref[0, r].astype(F32)
            for c in range(halves):
                perm_scr[c, pl.ds(r, n, stride=dil), :] = blk[:, c * LANES:(c + 1) * LANES]
        return jnp.concatenate([perm_scr[c] for c in range(halves)], axis=1)

    lses = [natural(l) for l in (l1, l2, l3)]
    lmax = jnp.maximum(jnp.maximum(lses[0], lses[1]), lses[2])
    e1, e2, e3 = [jnp.exp2(l - lmax) for l in lses]
    den = e1 + e2 + e3
    o_dil = (e1 / den) * natural(o1) + (e2 / den) * natural(o2) + (e3 / den) * natural(o3)
    y = (gate_ref[:, 0:d].astype(F32) * jnp.dot(of_ref[...], wf_ref[...], preferred_element_type=F32)
         + gate_ref[:, d:2 * d].astype(F32) * jnp.dot(om_ref[...], wm_ref[...], preferred_element_type=F32)
         + gate_ref[:, 2 * d:3 * d].astype(F32)
         * jnp.dot(o_dil.astype(BF16), wd_ref[...], preferred_element_type=F32))
    out = jnp.dot(y.astype(BF16), wo_ref[...], preferred_element_type=F32)
    x1 = x_ref[...] + mod_ref[0, 2:3, :] * out
    x1_ref[...] = x1
    h2_ref[...] = _mod_norm(x1, g_ref[...], mod_ref[0, 4:5, :], mod_ref[0, 3:4, :]).astype(BF16)


def _outproj(x2, mod_l, g_norm, o_fox, o_moba, dil_outs, gates, w_f, w_m, w_d, w_o, seq, tm):
    t, d = x2.shape
    per_b = seq // tm
    row = lambda i: (i, 0)
    fix = lambda i: (0, 0)
    dil_args, dil_specs = [], []
    for o, lse in dil_outs:
        _, dil, _, w = o.shape
        dil_args += [o, lse]
        dil_specs += [pl.BlockSpec((1, dil, tm // dil, w), lambda i: (i // per_b, 0, i % per_b, 0))] * 2
    return pl.pallas_call(
        _outproj_kernel,
        grid=(t // tm,),
        in_specs=[pl.BlockSpec((tm, d), row),
                  pl.BlockSpec((1, 6, d), lambda i: (i // per_b, 0, 0)),
                  pl.BlockSpec((1, d), fix),
                  pl.BlockSpec((tm, o_fox.shape[1]), row),
                  pl.BlockSpec((tm, o_moba.shape[1]), row)] + dil_specs + [
                  pl.BlockSpec((tm, N_BRANCH * d), row),
                  pl.BlockSpec(w_f.shape, fix), pl.BlockSpec(w_m.shape, fix),
                  pl.BlockSpec(w_d.shape, fix), pl.BlockSpec(w_o.shape, fix)],
        out_specs=[pl.BlockSpec((tm, d), row), pl.BlockSpec((tm, d), row)],
        out_shape=[jax.ShapeDtypeStruct((t, d), F32), jax.ShapeDtypeStruct((t, d), BF16)],
        scratch_shapes=[pltpu.VMEM((N_SLOTS_DIL * HEAD_DIM // LANES, tm, LANES), F32)],
        compiler_params=_params("arbitrary"),
        name="outproj",
    )(x2, mod_l, g_norm, o_fox, o_moba, *dil_args, gates, w_f, w_m, w_d, w_o)


def _router_kernel(h_ref, w_ref, b_ref, o_ref, cnt_ref, *, n_exp):
    logits = jnp.dot(h_ref[...], w_ref[...], preferred_element_type=F32) + b_ref[...]
    lane = lax.broadcasted_iota(jnp.int32, logits.shape, 1)
    logits = jnp.where(lane < n_exp, logits, -jnp.inf)

    def top(vals):
        m = jnp.max(vals, axis=-1, keepdims=True)
        idx = jnp.min(jnp.where(vals == m, lane, LANES), axis=-1, keepdims=True)
        return m, lane == idx

    m1, hot1 = top(logits)
    m2, hot2 = top(jnp.where(hot1, -jnp.inf, logits))
    e2 = jnp.exp(m2 - m1)
    cw = jnp.where(hot1, 1.0 / (1.0 + e2), 0.0) + jnp.where(hot2, e2 / (1.0 + e2), 0.0)
    o_ref[...] = cw
    sub = cnt_ref.shape[0] // 8
    rows = cw.shape[0] // sub
    for s in range(sub):
        n_tok = jnp.sum(jnp.where(cw[s * rows:(s + 1) * rows] > 0.0, 1.0, 0.0), axis=0, keepdims=True)
        cnt_ref[s * 8:(s + 1) * 8, :] = jnp.broadcast_to(n_tok, (8, LANES)).astype(jnp.int32)


def _router(h2, w_r, b_r, n_exp, tile):
    t, d = h2.shape
    tm = 4 * tile
    cw, cnt = pl.pallas_call(
        functools.partial(_router_kernel, n_exp=n_exp),
        grid=(t // tm,),
        in_specs=[pl.BlockSpec((tm, d), lambda i: (i, 0)),
                  pl.BlockSpec((d, LANES), lambda i: (0, 0)),
                  pl.BlockSpec((1, LANES), lambda i: (0, 0))],
        out_specs=[pl.BlockSpec((tm, LANES), lambda i: (i, 0)),
                   pl.BlockSpec((8 * tm // tile, LANES), lambda i: (i, 0))],
        out_shape=[jax.ShapeDtypeStruct((t, LANES), F32),
                   jax.ShapeDtypeStruct((t // tile * 8, LANES), jnp.int32)],
        compiler_params=_params("arbitrary"),
        name="router",
    )(h2, w_r, b_r)
    return cw, cnt.reshape(t // tile, 8, LANES)[:, 0, :n_exp]


MOE_TILE = 512
MOE_ROWS = 512
MOE_PUT = 128
MOE_GET = 256
MOE_ALIGN = 16


def _moe_layout(cnt, n_rows_static):
    n_tiles, n_exp = cnt.shape
    padded = (cnt + MOE_ALIGN - 1) // MOE_ALIGN * MOE_ALIGN
    length = jnp.sum(padded, axis=0)
    span = (length + (MOE_PUT - MOE_ALIGN) + MOE_ROWS - 1) // MOE_ROWS * MOE_ROWS
    start = jnp.cumsum(span) - span
    off = start[None, :] + jnp.cumsum(padded, axis=0) - padded
    n_steps = n_rows_static // MOE_ROWS
    first = start // MOE_ROWS
    step = jnp.arange(n_steps, dtype=jnp.int32)
    expert = jnp.sum((first[None, :] <= step[:, None]).astype(jnp.int32), axis=1) - 1
    active = step < (first + (length + MOE_ROWS - 1) // MOE_ROWS)[expert]
    return (off.reshape(-1).astype(jnp.int32), cnt.reshape(-1).astype(jnp.int32),
            expert.astype(jnp.int32), active.astype(jnp.int32))


def _dispatch_kernel(off_ref, cnt_ref, h_ref, cw_ref, u_ref, xs_in, xs_ref, stage, sems, *, n_exp):
    del xs_in
    i = pl.program_id(0)
    n_put = MOE_TILE // MOE_PUT
    routed = jnp.where(cw_ref[...].T[0:8] > 0.0, 1.0, 0.0)
    pos = jnp.dot(routed.astype(BF16), u_ref[...], preferred_element_type=F32)
    row = lax.broadcasted_iota(jnp.int32, (MOE_PUT, MOE_TILE), 0).astype(F32)
    h = h_ref[...]

    def copy(e, s):
        slot = e * n_put + s
        dst = pl.multiple_of(off_ref[i * n_exp + e] + s * MOE_PUT, MOE_ALIGN)
        return pltpu.make_async_copy(stage.at[slot], xs_ref.at[pl.ds(dst, MOE_PUT), :], sems.at[slot])

    for e in range(n_exp):
        for s in range(n_put):
            @pl.when(s * MOE_PUT < cnt_ref[i * n_exp + e])
            def _():
                take = (routed[e:e + 1, :] > 0.5) & (pos[e:e + 1, :] == row + float(s * MOE_PUT))
                onehot = jnp.where(take, 1.0, 0.0).astype(BF16)
                stage[e * n_put + s] = jnp.dot(onehot, h, preferred_element_type=F32).astype(BF16)
                copy(e, s).start()

    for e in range(n_exp):
        for s in range(n_put):
            @pl.when(s * MOE_PUT < cnt_ref[i * n_exp + e])
            def _():
                copy(e, s).wait()


def _dispatch(h2, cw, off, cnt, n_exp, n_rows):
    t, d = h2.shape
    u = jnp.asarray(np.arange(MOE_TILE)[:, None] < np.arange(MOE_TILE)[None, :], BF16)
    n_slots = n_exp * (MOE_TILE // MOE_PUT)
    return pl.pallas_call(
        functools.partial(_dispatch_kernel, n_exp=n_exp),
        grid_spec=pltpu.PrefetchScalarGridSpec(
            num_scalar_prefetch=2,
            grid=(t // MOE_TILE,),
            in_specs=[pl.BlockSpec((MOE_TILE, d), lambda i, o, c: (i, 0)),
                      pl.BlockSpec((MOE_TILE, LANES), lambda i, o, c: (i, 0)),
                      pl.BlockSpec((MOE_TILE, MOE_TILE), lambda i, o, c: (0, 0)),
                      pl.BlockSpec(memory_space=pl.ANY)],
            out_specs=pl.BlockSpec(memory_space=pl.ANY),
            scratch_shapes=[pltpu.VMEM((n_slots, MOE_PUT, d), BF16),
                            pltpu.SemaphoreType.DMA((n_slots,))]),
        out_shape=jax.ShapeDtypeStruct((n_rows, d), BF16),
        input_output_aliases={5: 0},
        compiler_params=_params("arbitrary"),
        name="moe_dispatch",
    )(off, cnt, h2, cw, u, jnp.zeros((n_rows, d), BF16))


FF_CHUNK = 768


def _swiglu(x, wg_ref, wu_ref, wd_ref):
    ff = wg_ref.shape[2]
    y = None
    for c0 in range(0, ff, FF_CHUNK):
        c1 = min(c0 + FF_CHUNK, ff)
        gate = jnp.dot(x, wg_ref[0, :, c0:c1].astype(BF16), preferred_element_type=F32)
        up = jnp.dot(x, wu_ref[0, :, c0:c1].astype(BF16), preferred_element_type=F32)
        a = ((gate * _sigmoid(gate)) * up).astype(BF16)
        part = jnp.dot(a, wd_ref[0, c0:c1, :].astype(BF16), preferred_element_type=F32)
        y = part if y is None else y + part
    return y


def _experts_kernel(exp_ref, act_ref, xs_ref, wg_ref, wu_ref, wd_ref, ys_ref):
    g = pl.program_id(0)

    @pl.when(act_ref[g] > 0)
    def _():
        ys_ref[...] = _swiglu(xs_ref[...], wg_ref, wu_ref, wd_ref).astype(BF16)

    @pl.when(act_ref[g] == 0)
    def _():
        ys_ref[...] = jnp.zeros(ys_ref.shape, BF16)


def _experts(xs, expert, active, w_g, w_u, w_d):
    n_rows, d = xs.shape
    _, _, ff = w_g.shape
    wmap = lambda g, ex, ac: (ex[g], 0, 0)
    single = lambda shape: pl.BlockSpec(shape, wmap, pipeline_mode=pl.Buffered(1))
    return pl.pallas_call(
        _experts_kernel,
        grid_spec=pltpu.PrefetchScalarGridSpec(
            num_scalar_prefetch=2,
            grid=(n_rows // MOE_ROWS,),
            in_specs=[pl.BlockSpec((MOE_ROWS, d), lambda g, ex, ac: (g, 0)),
                      single((1, d, ff)), single((1, d, ff)), pl.BlockSpec((1, ff, d), wmap)],
            out_specs=pl.BlockSpec((MOE_ROWS, d), lambda g, ex, ac: (g, 0))),
        out_shape=jax.ShapeDtypeStruct((n_rows, d), BF16),
        compiler_params=_params("arbitrary"),
        name="moe_experts",
    )(expert, active, xs, w_g, w_u, w_d)


def _combine_kernel(off_ref, cnt_ref, x_ref, mod_ref, cw_ref, l_ref, ys_ref, o_ref, buf, sems, acc_scr,
                    *, n_exp):
    i = pl.program_id(0)
    n_get = MOE_TILE // MOE_GET

    def copy(e, s):
        slot = e * n_get + s
        src = pl.multiple_of(off_ref[i * n_exp + e] + s * MOE_GET, MOE_ALIGN)
        return pltpu.make_async_copy(ys_ref.at[pl.ds(src, MOE_GET), :], buf.at[slot], sems.at[slot])

    for e in range(n_exp):
        for s in range(n_get):
            @pl.when(s * MOE_GET < cnt_ref[i * n_exp + e])
            def _():
                copy(e, s).start()

    cw = cw_ref[...]
    routed = jnp.where(cw > 0.0, 1.0, 0.0)
    pos = jnp.dot(l_ref[...], routed.astype(BF16), preferred_element_type=F32)
    col = lax.broadcasted_iota(jnp.int32, (MOE_TILE, MOE_GET), 1).astype(F32)
    acc_scr[...] = jnp.zeros(acc_scr.shape, F32)
    for e in range(n_exp):
        for s in range(n_get):
            @pl.when(s * MOE_GET < cnt_ref[i * n_exp + e])
            def _():
                copy(e, s).wait()
                take = (cw[:, e:e + 1] > 0.0) & (pos[:, e:e + 1] == col + float(s * MOE_GET))
                onehot = jnp.where(take, 1.0, 0.0).astype(BF16)
                acc_scr[...] += cw[:, e:e + 1] * jnp.dot(onehot, buf[e * n_get + s],
                                                         preferred_element_type=F32)
    o_ref[...] = x_ref[...] + mod_ref[0, 5:6, :] * acc_scr[...]


def _combine(x1, mod_l, cw, ys, off, cnt, n_exp, seq):
    t, d = x1.shape
    per_b = seq // MOE_TILE
    low = jnp.asarray(np.arange(MOE_TILE)[:, None] > np.arange(MOE_TILE)[None, :], BF16)
    n_slots = n_exp * (MOE_TILE // MOE_GET)
    return pl.pallas_call(
        functools.partial(_combine_kernel, n_exp=n_exp),
        grid_spec=pltpu.PrefetchScalarGridSpec(
            num_scalar_prefetch=2,
            grid=(t // MOE_TILE,),
            in_specs=[pl.BlockSpec((MOE_TILE, d), lambda i, o, c: (i, 0)),
                      pl.BlockSpec((1, 6, d), lambda i, o, c: (i // per_b, 0, 0)),
                      pl.BlockSpec((MOE_TILE, LANES), lambda i, o, c: (i, 0)),
                      pl.BlockSpec((MOE_TILE, MOE_TILE), lambda i, o, c: (0, 0)),
                      pl.BlockSpec(memory_space=pl.ANY)],
            out_specs=pl.BlockSpec((MOE_TILE, d), lambda i, o, c: (i, 0)),
            scratch_shapes=[pltpu.VMEM((n_slots, MOE_GET, d), BF16),
                            pltpu.SemaphoreType.DMA((n_slots,)),
                            pltpu.VMEM((MOE_TILE, d), F32)]),
        out_shape=jax.ShapeDtypeStruct((t, d), F32),
        compiler_params=_params("arbitrary"),
        name="moe_combine",
    )(off, cnt, x1, mod_l, cw, low, ys)


def _moe(x1, h2, mod_l, w_r, b_r, w_g, w_u, w_d, seq):
    t, d = x1.shape
    n_exp = w_g.shape[0]
    n_tiles = t // MOE_TILE
    bound = (TOP_K * t + n_tiles * n_exp * (MOE_ALIGN - 1)
             + n_exp * (MOE_PUT - MOE_ALIGN + MOE_ROWS - 1))
    n_rows = (bound + MOE_ROWS - 1) // MOE_ROWS * MOE_ROWS + MOE_ROWS
    cw, cnt = _router(h2, w_r, b_r, n_exp, MOE_TILE)
    off, cnt, expert, active = _moe_layout(cnt, n_rows)
    xs = _dispatch(h2, cw, off, cnt, n_exp, n_rows)
    ys = _experts(xs, expert, active, w_g, w_u, w_d)
    return _combine(x1, mod_l, cw, ys, off, cnt, n_exp, seq)


def _ffn_kernel(x_ref, h_ref, mod_ref, wg_ref, wu_ref, wd_ref, o_ref):
    o_ref[...] = x_ref[...] + mod_ref[0, 5:6, :] * _swiglu(h_ref[...], wg_ref, wu_ref, wd_ref)


def _ffn(x1, h2, mod_l, w_g, w_u, w_d, seq, tm):
    t, d = x1.shape
    per_b = seq // tm
    row = lambda i: (i, 0)
    whole = lambda w: pl.BlockSpec(w.shape, lambda i: (0, 0, 0), pipeline_mode=pl.Buffered(1))
    return pl.pallas_call(
        _ffn_kernel,
        grid=(t // tm,),
        in_specs=[pl.BlockSpec((tm, d), row), pl.BlockSpec((tm, d), row),
                  pl.BlockSpec((1, 6, d), lambda i: (i // per_b, 0, 0)),
                  whole(w_g), whole(w_u), whole(w_d)],
        out_specs=pl.BlockSpec((tm, d), row),
        out_shape=jax.ShapeDtypeStruct((t, d), F32),
        compiler_params=_params("arbitrary"),
        name="swiglu",
    )(x1, h2, mod_l, w_g, w_u, w_d)


def _pad_cols(a, width):
    return jnp.pad(a, ((0, 0), (0, width - a.shape[1])))


def kernel(x, c, w_ada, b_ada, norm_mix, norm_ffn, w_in, b_fgate, q_gain, k_gain, w_br_fox, w_br_moba,
           w_br_dil, w_out, w_ffn_gate, w_ffn_up, w_ffn_down, w_router, b_router, w_exp_gate,
           w_exp_up, w_exp_down):
    b, seq, d = x.shape
    depth = w_ada.shape[0]
    t = b * seq
    tm, tq = TOKEN_TILE, ATTN_CHUNK
    n_pairs_fox = N_HEADS_FOX // 2
    n_pairs_moba = N_HEADS_MOBA // 2

    mod = _adaln(c, w_ada, b_ada).reshape(depth, b, 6, d)
    x2 = x.reshape(t, d)
    w_in_t = jnp.transpose(w_in, (2, 0, 1))
    for l in range(depth):
        wl = w_in_t[:, l, :].astype(BF16)
        w_gate = wl[3 * MIX_WIDTH + N_HEADS_FOX:]
        b_f = _pad_cols(b_fgate[l].reshape(1, -1), LANES)
        outs = _inproj(x2, mod[l], norm_mix[l].reshape(1, d), wl, w_gate, q_gain[l].reshape(1, -1),
                       k_gain[l].reshape(1, -1), b_f, seq, tm)
        q_m, k_m, v_m = (o.reshape(b, seq, -1) for o in outs[0:3])
        gates, lf = outs[12], outs[13]

        cum = _decay(lf.reshape(b, seq, LANES))
        o_fox = _fox(q_m, k_m, v_m, cum, 0, n_pairs_fox, tq).reshape(t, -1)
        o_moba = _moba(q_m, k_m, v_m, n_pairs_fox, n_pairs_moba, tq).reshape(t, -1)
        dil_outs = [_dilated(*outs[3 + 3 * g:6 + 3 * g], g) for g in range(len(DILATED_PAIRS))]

        x1, h2 = _outproj(x2, mod[l], norm_ffn[l].reshape(1, d), o_fox, o_moba, dil_outs, gates,
                          w_br_fox[l].astype(BF16), w_br_moba[l].astype(BF16),
                          w_br_dil[l].astype(BF16), w_out[l].astype(BF16), seq, tm)
        i = l // 2
        if l % 2 == 0:
            x2 = _ffn(x1, h2, mod[l], w_ffn_gate[i:i + 1].astype(BF16),
                      w_ffn_up[i:i + 1].astype(BF16), w_ffn_down[i:i + 1].astype(BF16), seq, FFN_TILE)
        else:
            x2 = _moe(x1, h2, mod[l], _pad_cols(w_router[i], LANES).astype(BF16),
                      _pad_cols(b_router[i].reshape(1, -1), LANES), w_exp_gate[i], w_exp_up[i],
                      w_exp_down[i], seq)
    return x2.reshape(b, seq, d)
```

```python
import functools

import numpy as np
import jax
import jax.numpy as jnp
from jax import lax
from jax.experimental import pallas as pl
from jax.experimental.pallas import tpu as pltpu

HEAD_DIM = 64
N_HEADS_FOX = 6
N_HEADS_MOBA = 6
DILATED_PAIRS = ((128, 1), (512, 4), (2048, 16))
N_SLOTS_DIL = 4
N_HEADS_DIL = N_SLOTS_DIL * len(DILATED_PAIRS)
N_HEADS = N_HEADS_FOX + N_HEADS_MOBA + N_HEADS_DIL
MIX_WIDTH = N_HEADS * HEAD_DIM
N_BRANCH = 3
MOBA_BLOCK = 256
MOBA_TOPK = 3
WIN_BLOCK = 128
N_ALIBI = N_HEADS_MOBA + N_HEADS_DIL
DIL_SLOPE_OFFSETS = (0, N_SLOTS_DIL, 2 * N_SLOTS_DIL + N_HEADS_MOBA)
MOBA_SLOPE_OFFSET = 2 * N_SLOTS_DIL
TOP_K = 2
RMS_EPS = 1e-6
NEG_INF = -1e30
LOG2E = 1.4426950408889634

LANES = 128
PAIR = 2 * HEAD_DIM
VMEM_LIMIT = 56 * 1024 * 1024
TOKEN_TILE = 512
ATTN_CHUNK = 512
FFN_TILE = 1024

F32 = jnp.float32
BF16 = jnp.bfloat16
_NT = (((1,), (1,)), ((), ()))


def _alibi_slopes(n):
    return (2.0 ** (-8.0 * np.arange(1, n + 1) / n)).astype(np.float32)


def _sigmoid(x):
    return 1.0 / (1.0 + jnp.exp(-x))


def _params(*sem):
    return pltpu.CompilerParams(dimension_semantics=sem, vmem_limit_bytes=VMEM_LIMIT)


def _adaln_kernel(c_ref, w_ref, b_ref, o_ref):
    c = c_ref[...]
    cond = c * _sigmoid(c)
    o_ref[0] = jnp.dot(cond, w_ref[0], precision=lax.Precision.HIGHEST,
                       preferred_element_type=F32) + b_ref[0]


def _adaln(c, w_ada, b_ada):
    depth, d, n = w_ada.shape
    b = c.shape[0]
    tn = 1536
    return pl.pallas_call(
        _adaln_kernel,
        grid=(depth, n // tn),
        in_specs=[pl.BlockSpec((b, d), lambda l, j: (0, 0)),
                  pl.BlockSpec((1, d, tn), lambda l, j: (l, 0, j)),
                  pl.BlockSpec((1, 1, tn), lambda l, j: (l, 0, j))],
        out_specs=pl.BlockSpec((1, b, tn), lambda l, j: (l, 0, j)),
        out_shape=jax.ShapeDtypeStruct((depth, b, n), F32),
        compiler_params=_params("arbitrary", "arbitrary"),
        name="adaln",
    )(c, w_ada, b_ada.reshape(depth, 1, n))


def _mod_norm(x, g, scale, shift):
    ms = jnp.mean(x * x, axis=-1, keepdims=True)
    return (x * lax.rsqrt(ms + RMS_EPS) * g) * (1.0 + scale) + shift


def _inproj_kernel(x_ref, mod_ref, g_ref, w_ref, wg_ref, qg_ref, kg_ref, bf_ref, e_ref,
                   q_ref, k_ref, v_ref, qd1, kd1, vd1, qd2, kd2, vd2, qd3, kd3, vd3,
                   gate_ref, lf_ref, perm_scr, *, n_main):
    h = _mod_norm(x_ref[...], g_ref[...], mod_ref[0, 1:2, :], mod_ref[0, 0:1, :]).astype(BF16)
    tm = x_ref.shape[0]
    qk_chunk = 4 * HEAD_DIM
    q_outs = (q_ref, qd1, qd2, qd3)
    k_outs = (k_ref, kd1, kd2, kd3)
    v_outs = (v_ref, vd1, vd2, vd3)

    def put(outs, col, val):
        if col < n_main:
            outs[0][:, col:col + qk_chunk] = val.astype(BF16)
            return
        g = (col - n_main) // qk_chunk
        dil = DILATED_PAIRS[g][1]
        if dil == 1:
            outs[1 + g][0, 0] = val.astype(BF16)
            return
        halves = qk_chunk // LANES
        for c in range(halves):
            perm_scr[c] = val[:, c * LANES:(c + 1) * LANES]
        for r in range(dil):
            outs[1 + g][0, r] = jnp.concatenate(
                [perm_scr[c, pl.ds(r, tm // dil, stride=dil), :] for c in range(halves)],
                axis=1).astype(BF16)

    def proj(wt_ref, col0, width):
        return lax.dot_general(h, wt_ref[col0:col0 + width, :], _NT, preferred_element_type=F32)

    wide = MIX_WIDTH // 2
    for half in range(2):
        yq = proj(w_ref, half * wide, wide)
        yk = proj(w_ref, MIX_WIDTH + half * wide, wide)
        yv = proj(w_ref, 2 * MIX_WIDTH + half * wide, wide)
        for c in range(wide // qk_chunk):
            col = half * wide + c * qk_chunk
            yq_c = yq[:, c * qk_chunk:(c + 1) * qk_chunk]
            yk_c = yk[:, c * qk_chunk:(c + 1) * qk_chunk]
            sq = jnp.concatenate([yq_c * yq_c, yk_c * yk_c], axis=0).astype(BF16)
            ss = jnp.dot(sq, e_ref[...], preferred_element_type=F32)
            r = lax.rsqrt(ss * (1.0 / HEAD_DIM) + RMS_EPS)
            put(q_outs, col, (yq_c * r[0:tm] * qg_ref[:, col:col + qk_chunk])
                * (HEAD_DIM ** -0.5 * LOG2E))
            put(k_outs, col, yk_c * r[tm:2 * tm] * kg_ref[:, col:col + qk_chunk])
            put(v_outs, col, yv[:, c * qk_chunk:(c + 1) * qk_chunk])

    gchunk = 1024
    for c in range(wg_ref.shape[0] // gchunk):
        y = proj(wg_ref, c * gchunk, gchunk)
        gate_ref[:, c * gchunk:(c + 1) * gchunk] = _sigmoid(y).astype(BF16)

    f = proj(w_ref, 3 * MIX_WIDTH, LANES) + bf_ref[...]
    lf_ref[...] = jnp.minimum(f, 0.0) - jnp.log(1.0 + jnp.exp(-jnp.abs(f)))


def _inproj(x2, mod_l, g_norm, w_t, w_gate, q_gain, k_gain, b_f, seq, tm):
    t, d = x2.shape
    n_main = (N_HEADS_FOX + N_HEADS_MOBA) * HEAD_DIM
    dil_w = N_SLOTS_DIL * HEAD_DIM
    e = (np.arange(dil_w)[:, None] // HEAD_DIM == np.arange(dil_w)[None, :] // HEAD_DIM)
    e = jnp.asarray(e, BF16)
    per_b = seq // tm
    row = lambda i: (i, 0)
    fix = lambda i: (0, 0)
    qkv_shapes = [jax.ShapeDtypeStruct((t, n_main), BF16)] * 3
    qkv_specs = [pl.BlockSpec((tm, n_main), row)] * 3
    for _, dil in DILATED_PAIRS:
        assert tm % (16 * dil) == 0
        qkv_shapes += [jax.ShapeDtypeStruct((t // seq, dil, seq // dil, dil_w), BF16)] * 3
        qkv_specs += [pl.BlockSpec((1, dil, tm // dil, dil_w),
                                   lambda i: (i // per_b, 0, i % per_b, 0))] * 3
    out_shapes = qkv_shapes + [jax.ShapeDtypeStruct((t, N_BRANCH * d), BF16),
                               jax.ShapeDtypeStruct((t, LANES), F32)]
    out_specs = qkv_specs + [pl.BlockSpec((tm, N_BRANCH * d), row), pl.BlockSpec((tm, LANES), row)]
    return pl.pallas_call(
        functools.partial(_inproj_kernel, n_main=n_main),
        grid=(t // tm,),
        in_specs=[pl.BlockSpec((tm, d), row),
                  pl.BlockSpec((1, 6, d), lambda i: (i // per_b, 0, 0)),
                  pl.BlockSpec((1, d), fix),
                  pl.BlockSpec(w_t.shape, fix, pipeline_mode=pl.Buffered(1)),
                  pl.BlockSpec(w_gate.shape, fix, pipeline_mode=pl.Buffered(1)),
                  pl.BlockSpec((1, MIX_WIDTH), fix),
                  pl.BlockSpec((1, MIX_WIDTH), fix),
                  pl.BlockSpec((1, LANES), fix),
                  pl.BlockSpec(e.shape, fix)],
        out_specs=out_specs,
        out_shape=out_shapes,
        scratch_shapes=[pltpu.VMEM((dil_w // LANES, tm, LANES), F32)],
        compiler_params=_params("arbitrary"),
        name="inproj",
    )(x2, mod_l, g_norm, w_t, w_gate, q_gain, k_gain, b_f, e)


def _decay_kernel(lf_ref, tri_ref, o_ref, *, blk):
    carry = jnp.zeros((1, LANES), F32)
    for j in range(lf_ref.shape[1] // blk):
        c = jnp.dot(tri_ref[...], lf_ref[0, j * blk:(j + 1) * blk, :],
                    precision=lax.Precision.HIGHEST, preferred_element_type=F32) + carry
        o_ref[0, j * blk:(j + 1) * blk, :] = c
        carry = c[blk - 1:blk, :]


def _decay(lf, blk=256):
    b, seq, _ = lf.shape
    tri = jnp.asarray(np.arange(blk)[:, None] >= np.arange(blk)[None, :], F32)
    return pl.pallas_call(
        functools.partial(_decay_kernel, blk=blk),
        grid=(b,),
        in_specs=[pl.BlockSpec((1, seq, LANES), lambda i: (i, 0, 0)),
                  pl.BlockSpec((blk, blk), lambda i: (0, 0))],
        out_specs=pl.BlockSpec((1, seq, LANES), lambda i: (i, 0, 0)),
        out_shape=jax.ShapeDtypeStruct((b, seq, LANES), F32),
        compiler_params=_params("arbitrary"),
        name="fox_decay",
    )(lf, tri)


def _head_masks():
    lane = lax.broadcasted_iota(jnp.int32, (1, PAIR), 1)
    return lane < HEAD_DIM, lane >= HEAD_DIM


def _split_heads(q):
    return [jnp.where(m, q, jnp.zeros_like(q)) for m in _head_masks()]


def _feat_base(hh):
    return HEAD_DIM * (1 - hh)


def _three_bf16(x):
    hi = x.astype(BF16).astype(F32)
    mid = (x - hi).astype(BF16).astype(F32)
    lo = (x - hi - mid).astype(BF16).astype(F32)
    return hi, mid, lo


def _place(idx, base, parts):
    out = jnp.zeros(jnp.broadcast_shapes(idx.shape, jnp.shape(parts[0])), F32)
    for r, part in enumerate(parts):
        out = jnp.where(idx == base + r, part, out)
    return out


def _flash_pair(c, qx, kx_scr, v_ref, o_ref, keep_own, own_first, m_scr, acc_scr, p_scr, a_scr, tq):
    _reset(m_scr, acc_scr)

    def scores(j, own, slot):
        for hh in range(2):
            s = lax.dot_general(qx[hh], kx_scr[hh, j * tq:(j + 1) * tq, :], _NT,
                                preferred_element_type=F32)
            if own:
                s = jnp.where(keep_own, s, NEG_INF)
            _softmax_stage(s, m_scr.at[hh], p_scr.at[slot, hh], a_scr.at[slot, hh])

    def values(j, slot):
        vs = _with_ones(v_ref[0, j * tq:(j + 1) * tq, :])
        for hh in range(2):
            _value_stage(vs[hh], acc_scr.at[hh], p_scr.at[slot, hh], a_scr.at[slot, hh])

    order = [c] + list(range(c)) if own_first else list(range(c + 1))
    scores(order[0], order[0] == c, 0)
    for n in range(1, len(order)):
        scores(order[n], order[n] == c, n % 2)
        values(order[n - 1], (n - 1) % 2)
    values(order[-1], (len(order) - 1) % 2)
    o_ref[0, c * tq:(c + 1) * tq, :] = _merge_pair(acc_scr)


def _with_ones(v):
    return [jnp.where(m, v, jnp.ones_like(v)) for m in _head_masks()]


def _softmax_stage(s, m_ref, p_ref, a_ref):
    m_prev = m_ref[...]
    m_new = jnp.maximum(m_prev, jnp.max(s, axis=-1, keepdims=True))
    p = jnp.exp2(s - jnp.concatenate([m_new] * (s.shape[1] // LANES), axis=1))
    m_ref[...] = m_new
    p_ref[...] = p.astype(BF16)
    a_ref[...] = jnp.exp2(m_prev - m_new)


def _value_stage(v_ones, acc_ref, p_ref, a_ref):
    acc_ref[...] = a_ref[...] * acc_ref[...] + jnp.dot(p_ref[...], v_ones, preferred_element_type=F32)


def _reset(m_scr, acc_scr):
    m_scr[...] = jnp.full(m_scr.shape, NEG_INF, F32)
    acc_scr[...] = jnp.zeros(acc_scr.shape, F32)


def _merge_pair(acc_scr):
    first = _head_masks()[0]
    a0, a1 = acc_scr[0], acc_scr[1]
    return jnp.where(first, a0 / a0[:, HEAD_DIM:HEAD_DIM + 1], a1 / a1[:, 0:1]).astype(BF16)


def _stat_scratch(tq):
    return [pltpu.VMEM((2, tq, LANES), F32), pltpu.VMEM((2, tq, PAIR), F32),
            pltpu.VMEM((2, 2, tq, tq), BF16), pltpu.VMEM((2, 2, tq, LANES), F32)]


def _fox_kernel(q_ref, k_ref, v_ref, cum_ref, o_ref, kx_scr, m_scr, acc_scr, p_scr, a_scr, *, tq):
    pair = pl.program_id(1)
    masks = _head_masks()
    lane = lax.broadcasted_iota(jnp.int32, (1, PAIR), 1)
    k = k_ref[0]
    cum = cum_ref[0]
    lane_s = lax.broadcasted_iota(jnp.int32, cum.shape, 1)
    for hh in range(2):
        f_s = jnp.sum(jnp.where(lane_s == 2 * pair + hh, cum, 0.0), axis=-1, keepdims=True)
        feats = _place(lane, _feat_base(hh), _three_bf16(-LOG2E * f_s))
        kx_scr[hh] = jnp.where(masks[hh], k, feats.astype(BF16))

    causal = (lax.broadcasted_iota(jnp.int32, (tq, tq), 1)
              <= lax.broadcasted_iota(jnp.int32, (tq, tq), 0))
    for c in range(q_ref.shape[1] // tq):
        q = q_ref[0, c * tq:(c + 1) * tq, :]
        qx = []
        for hh in range(2):
            ones = (lane >= _feat_base(hh)) & (lane < _feat_base(hh) + 3)
            qx.append(jnp.where(masks[hh], q, jnp.where(ones, 1.0, 0.0).astype(BF16)))
        _flash_pair(c, qx, kx_scr, v_ref, o_ref, causal, False, m_scr, acc_scr, p_scr, a_scr, tq)


def _fox(q, k, v, cum, col0, n_pairs, tq):
    b, seq, _ = q.shape
    head_cols = pl.BlockSpec((1, seq, PAIR), lambda bi, p: (bi, 0, col0 + p))
    return pl.pallas_call(
        functools.partial(_fox_kernel, tq=tq),
        grid=(b, n_pairs),
        in_specs=[head_cols, head_cols, head_cols,
                  pl.BlockSpec((1, seq, LANES), lambda bi, p: (bi, 0, 0))],
        out_specs=pl.BlockSpec((1, seq, PAIR), lambda bi, p: (bi, 0, p)),
        out_shape=jax.ShapeDtypeStruct((b, seq, n_pairs * PAIR), BF16),
        scratch_shapes=[pltpu.VMEM((2, seq, PAIR), BF16)] + _stat_scratch(tq),
        compiler_params=_params("arbitrary", "arbitrary"),
        name="fox_attn",
    )(q, k, v, cum)


MOBA_NBLK = 8
ALIBI_PARTS = 5


def _moba_feature_tables(slopes, seq):
    pos = np.arange(seq, dtype=np.float64)

    def parts(x):
        out, rest = [], x.copy()
        for _ in range(ALIBI_PARTS):
            p = rest.astype(np.float32).astype(jnp.bfloat16).astype(np.float64)
            out.append(p)
            rest = rest - p
        assert not rest.any()
        return out

    n_heads = len(slopes)
    qf = np.zeros((n_heads // 2, 2, seq, PAIR), np.float32)
    kf = np.zeros((n_heads // 2, 2, seq, PAIR), np.float32)
    for h, slope in enumerate(slopes):
        base = _feat_base(h % 2)
        q_t, k_t = qf[h // 2, h % 2], kf[h // 2, h % 2]
        k_t[np.arange(seq), base + np.arange(seq) // MOBA_BLOCK] = 1.0
        lo, mid, hi = base + MOBA_NBLK, base + MOBA_NBLK + ALIBI_PARTS, base + MOBA_NBLK + 2 * ALIBI_PARTS
        q_t[:, lo:mid] = 1.0
        slope2 = np.float64(np.float32(slope * LOG2E))
        k_t[:, lo:mid] = np.stack(parts(slope2 * pos), axis=1)
        q_t[:, mid:hi] = np.stack(parts(-slope2 * pos), axis=1)
        k_t[:, mid:hi] = 1.0
    return jnp.asarray(qf, BF16), jnp.asarray(kf, BF16)


def _moba_kernel(q_ref, k_ref, v_ref, qf_ref, kf_ref, o_ref, km_scr, kx_scr, m_scr, acc_scr, p_scr,
                 a_scr, *, tq):
    blk = MOBA_BLOCK
    nbp = MOBA_NBLK
    n_blk = k_ref.shape[1] // blk
    per = tq // blk
    masks = _head_masks()

    km_scr[...] = jnp.zeros(km_scr.shape, F32)
    for n in range(n_blk):
        kb = k_ref[0, n * blk:(n + 1) * blk, :].astype(F32)
        km_scr[n:n + 1, :] = jnp.sum(kb, axis=0, keepdims=True) * (1.0 / blk)
    for hh in range(2):
        kx_scr[hh] = jnp.where(masks[hh], k_ref[0], kf_ref[0, hh])
    km = km_scr[0:nbp, :]
    km3 = jnp.concatenate(list(_three_bf16(km)) + [jnp.zeros_like(km)], axis=0).astype(BF16)

    blk_n = lax.broadcasted_iota(jnp.int32, (nbp, tq), 0)
    lane = lax.broadcasted_iota(jnp.int32, (1, PAIR), 1)
    r = lax.broadcasted_iota(jnp.int32, (tq, tq), 0)
    s = lax.broadcasted_iota(jnp.int32, (tq, tq), 1)
    keep_own = (r // blk != s // blk) | (s <= r)

    for c in range(q_ref.shape[1] // tq):
        q = q_ref[0, c * tq:(c + 1) * tq, :]
        qs = _split_heads(q)
        q_blk = c * per + lax.broadcasted_iota(jnp.int32, (nbp, tq), 1) // blk
        qx = []
        for hh in range(2):
            base = _feat_base(hh)
            g3 = lax.dot_general(km3, qs[hh], _NT, preferred_element_type=F32)
            g = g3[0:nbp] + g3[nbp:2 * nbp] + g3[2 * nbp:3 * nbp]
            cnt = jnp.zeros((nbp, tq), F32)
            for m in range(n_blk):
                gm = g[m:m + 1, :]
                beats = ((gm > g) | ((gm == g) & (blk_n > m))) & (q_blk > m)
                cnt = cnt + jnp.where(beats, 1.0, 0.0)
            visible = ((blk_n < q_blk) & (cnt < float(MOBA_TOPK))) | (blk_n == q_blk)
            hide = jnp.where(visible, 0.0, NEG_INF)
            pads = [jnp.zeros((n, tq), F32) for n in (base, LANES - nbp - base)]
            hide = jnp.concatenate([a for a in (pads[0], hide, pads[1]) if a.shape[0]], axis=0)
            feats = jnp.where((lane >= base) & (lane < base + nbp), hide.T.astype(BF16),
                              qf_ref[0, hh, c * tq:(c + 1) * tq, :])
            qx.append(jnp.where(masks[hh], q, feats))
        _flash_pair(c, qx, kx_scr, v_ref, o_ref, keep_own, True, m_scr, acc_scr, p_scr, a_scr, tq)


def _moba(q, k, v, col0, n_pairs, tq):
    b, seq, _ = q.shape
    blk = MOBA_BLOCK
    assert seq % tq == 0 and tq % blk == 0 and seq // blk <= MOBA_NBLK
    slopes = _alibi_slopes(N_ALIBI)[MOBA_SLOPE_OFFSET:MOBA_SLOPE_OFFSET + 2 * n_pairs]
    q_feats, k_feats = _moba_feature_tables(slopes, seq)
    head_cols = pl.BlockSpec((1, seq, PAIR), lambda bi, p: (bi, 0, col0 + p))
    table = pl.BlockSpec((1, 2, seq, PAIR), lambda bi, p: (p, 0, 0, 0))
    return pl.pallas_call(
        functools.partial(_moba_kernel, tq=tq),
        grid=(b, n_pairs),
        in_specs=[head_cols, head_cols, head_cols, table, table],
        out_specs=pl.BlockSpec((1, seq, PAIR), lambda bi, p: (bi, 0, p)),
        out_shape=jax.ShapeDtypeStruct((b, seq, n_pairs * PAIR), BF16),
        scratch_shapes=[pltpu.VMEM((LANES, PAIR), F32), pltpu.VMEM((2, seq, PAIR), BF16)]
        + _stat_scratch(tq),
        compiler_params=_params("arbitrary", "arbitrary"),
        name="moba_attn",
    )(q, k, v, q_feats, k_feats)


def _dil_kernel(q_ref, k_ref, v_ref, o_ref, lse_ref, *, dil, slopes, n_q):
    wb = WIN_BLOCK
    seq = q_ref.shape[1]
    nb = seq // wb
    steps = (lax.broadcasted_iota(jnp.int32, (wb, 2 * wb), 0) + wb
             - lax.broadcasted_iota(jnp.int32, (wb, 2 * wb), 1))
    band_ok = (steps >= 0) & (steps <= wb)
    steps_f = (steps * dil).astype(F32)
    blk = lax.broadcasted_iota(jnp.int32, (nb, 1, 2 * wb), 0)
    key = lax.broadcasted_iota(jnp.int32, (nb, 1, 2 * wb), 2)
    no_prev = jnp.where((blk % n_q == 0) & (key < wb), -NEG_INF, 0.0)
    first = _head_masks()[0]

    def band(ref, cols):
        cur = ref[0, :, cols]
        prev = jnp.concatenate([cur[0:wb], cur[0:seq - wb]], axis=0)
        return jnp.concatenate([prev.reshape(nb, wb, PAIR), cur.reshape(nb, wb, PAIR)], axis=1)

    for pair in range(2):
        cols = slice(pair * PAIR, (pair + 1) * PAIR)
        qs = [qh.reshape(nb, wb, PAIR) for qh in _split_heads(q_ref[0, :, cols])]
        kband, vband = band(k_ref, cols), band(v_ref, cols)
        outs, lses = [], []
        for hh in range(2):
            bias = jnp.where(band_ok, (slopes[2 * pair + hh] * LOG2E) * steps_f, -NEG_INF)
            s = jnp.einsum("bqd,bkd->bqk", qs[hh], kband, preferred_element_type=F32)
            s = s - bias[None] - no_prev
            m = jnp.max(s, axis=-1, keepdims=True)
            p = jnp.exp2(s - m)
            l = jnp.sum(p, axis=-1, keepdims=True)
            o = jnp.einsum("bqk,bkd->bqd", p.astype(BF16), vband, preferred_element_type=F32)
            outs.append(o / l)
            lses.append(jnp.broadcast_to(m + LOG2E * jnp.log(l), (nb, wb, PAIR)))
        o_ref[0, :, cols] = jnp.where(first, outs[0], outs[1]).reshape(seq, PAIR).astype(BF16)
        lse_ref[0, :, cols] = jnp.where(first, lses[0], lses[1]).reshape(seq, PAIR)


def _dilated(q, k, v, group):
    window, dil = DILATED_PAIRS[group]
    assert window // dil == WIN_BLOCK
    b, _, l_sub, w = q.shape
    seq = dil * l_sub
    so = DIL_SLOPE_OFFSETS[group]
    slopes = tuple(float(s) for s in _alibi_slopes(N_ALIBI)[so:so + N_SLOTS_DIL])
    spec = pl.BlockSpec((1, seq, w), lambda bi: (bi, 0, 0))
    flat = lambda a: a.reshape(b, seq, w)
    o, lse = pl.pallas_call(
        functools.partial(_dil_kernel, dil=dil, slopes=slopes, n_q=l_sub // WIN_BLOCK),
        grid=(b,),
        in_specs=[spec, spec, spec],
        out_specs=[spec, spec],
        out_shape=[jax.ShapeDtypeStruct((b, seq, w), BF16), jax.ShapeDtypeStruct((b, seq, w), F32)],
        compiler_params=_params("arbitrary"),
        name=f"dilated_attn_{dil}",
    )(flat(q), flat(k), flat(v))
    return o.reshape(q.shape), lse.reshape(q.shape)


def _outproj_kernel(x_ref, mod_ref, g_ref, of_ref, om_ref, o1, l1, o2, l2, o3, l3, gate_ref,
                    wf_ref, wm_ref, wd_ref, wo_ref, x1_ref, h2_ref, perm_scr):
    d = x_ref.shape[1]

    def natural(ref):
        dil, n = ref.shape[1], ref.shape[2]
        if dil == 1:
            return ref[0, 0].astype(F32)
        halves = ref.shape[3] // LANES
        for r in range(dil):
            blk = ref[0, r].astype(F32)
            for c in range(halves):
                perm_scr[c, pl.ds(r, n, stride=dil), :] = blk[:, c * LANES:(c + 1) * LANES]
        return jnp.concatenate([perm_scr[c] for c in range(halves)], axis=1)

    lses = [natural(l) for l in (l1, l2, l3)]
    lmax = jnp.maximum(jnp.maximum(lses[0], lses[1]), lses[2])
    e1, e2, e3 = [jnp.exp2(l - lmax) for l in lses]
    den = e1 + e2 + e3
    o_dil = (e1 / den) * natural(o1) + (e2 / den) * natural(o2) + (e3 / den) * natural(o3)
    y = (gate_ref[:, 0:d].astype(F32) * jnp.dot(of_ref[...], wf_ref[...], preferred_element_type=F32)
         + gate_ref[:, d:2 * d].astype(F32) * jnp.dot(om_ref[...], wm_ref[...], preferred_element_type=F32)
         + gate_ref[:, 2 * d:3 * d].astype(F32)
         * jnp.dot(o_dil.astype(BF16), wd_ref[...], preferred_element_type=F32))
    out = jnp.dot(y.astype(BF16), wo_ref[...], preferred_element_type=F32)
    x1 = x_ref[...] + mod_ref[0, 2:3, :] * out
    x1_ref[...] = x1
    h2_ref[...] = _mod_norm(x1, g_ref[...], mod_ref[0, 4:5, :], mod_ref[0, 3:4, :]).astype(BF16)


def _outproj(x2, mod_l, g_norm, o_fox, o_moba, dil_outs, gates, w_f, w_m, w_d, w_o, seq, tm):
    t, d = x2.shape
    per_b = seq // tm
    row = lambda i: (i, 0)
    fix = lambda i: (0, 0)
    dil_args, dil_specs = [], []
    for o, lse in dil_outs:
        _, dil, _, w = o.shape
        dil_args += [o, lse]
        dil_specs += [pl.BlockSpec((1, dil, tm // dil, w), lambda i: (i // per_b, 0, i % per_b, 0))] * 2
    return pl.pallas_call(
        _outproj_kernel,
        grid=(t // tm,),
        in_specs=[pl.BlockSpec((tm, d), row),
                  pl.BlockSpec((1, 6, d), lambda i: (i // per_b, 0, 0)),
                  pl.BlockSpec((1, d), fix),
                  pl.BlockSpec((tm, o_fox.shape[1]), row),
                  pl.BlockSpec((tm, o_moba.shape[1]), row)] + dil_specs + [
                  pl.BlockSpec((tm, N_BRANCH * d), row),
                  pl.BlockSpec(w_f.shape, fix), pl.BlockSpec(w_m.shape, fix),
                  pl.BlockSpec(w_d.shape, fix), pl.BlockSpec(w_o.shape, fix)],
        out_specs=[pl.BlockSpec((tm, d), row), pl.BlockSpec((tm, d), row)],
        out_shape=[jax.ShapeDtypeStruct((t, d), F32), jax.ShapeDtypeStruct((t, d), BF16)],
        scratch_shapes=[pltpu.VMEM((N_SLOTS_DIL * HEAD_DIM // LANES, tm, LANES), F32)],
        compiler_params=_params("arbitrary"),
        name="outproj",
    )(x2, mod_l, g_norm, o_fox, o_moba, *dil_args, gates, w_f, w_m, w_d, w_o)


def _router_kernel(h_ref, w_ref, b_ref, o_ref, cnt_ref, *, n_exp):
    logits = jnp.dot(h_ref[...], w_ref[...], preferred_element_type=F32) + b_ref[...]
    lane = lax.broadcasted_iota(jnp.int32, logits.shape, 1)
    logits = jnp.where(lane < n_exp, logits, -jnp.inf)

    def top(vals):
        m = jnp.max(vals, axis=-1, keepdims=True)
        idx = jnp.min(jnp.where(vals == m, lane, LANES), axis=-1, keepdims=True)
        return m, lane == idx

    m1, hot1 = top(logits)
    m2, hot2 = top(jnp.where(hot1, -jnp.inf, logits))
    e2 = jnp.exp(m2 - m1)
    cw = jnp.where(hot1, 1.0 / (1.0 + e2), 0.0) + jnp.where(hot2, e2 / (1.0 + e2), 0.0)
    o_ref[...] = cw
    sub = cnt_ref.shape[0] // 8
    rows = cw.shape[0] // sub
    for s in range(sub):
        n_tok = jnp.sum(jnp.where(cw[s * rows:(s + 1) * rows] > 0.0, 1.0, 0.0), axis=0, keepdims=True)
        cnt_ref[s * 8:(s + 1) * 8, :] = jnp.broadcast_to(n_tok, (8, LANES)).astype(jnp.int32)


def _router(h2, w_r, b_r, n_exp, tile):
    t, d = h2.shape
    tm = 4 * tile
    cw, cnt = pl.pallas_call(
        functools.partial(_router_kernel, n_exp=n_exp),
        grid=(t // tm,),
        in_specs=[pl.BlockSpec((tm, d), lambda i: (i, 0)),
                  pl.BlockSpec((d, LANES), lambda i: (0, 0)),
                  pl.BlockSpec((1, LANES), lambda i: (0, 0))],
        out_specs=[pl.BlockSpec((tm, LANES), lambda i: (i, 0)),
                   pl.BlockSpec((8 * tm // tile, LANES), lambda i: (i, 0))],
        out_shape=[jax.ShapeDtypeStruct((t, LANES), F32),
                   jax.ShapeDtypeStruct((t // tile * 8, LANES), jnp.int32)],
        compiler_params=_params("arbitrary"),
        name="router",
    )(h2, w_r, b_r)
    return cw, cnt.reshape(t // tile, 8, LANES)[:, 0, :n_exp]


MOE_TILE = 512
MOE_ROWS = 512
MOE_PUT = 128
MOE_GET = 256
MOE_ALIGN = 16


def _moe_layout(cnt, n_rows_static):
    n_tiles, n_exp = cnt.shape
    padded = (cnt + MOE_ALIGN - 1) // MOE_ALIGN * MOE_ALIGN
    length = jnp.sum(padded, axis=0)
    span = (length + (MOE_PUT - MOE_ALIGN) + MOE_ROWS - 1) // MOE_ROWS * MOE_ROWS
    start = jnp.cumsum(span) - span
    off = start[None, :] + jnp.cumsum(padded, axis=0) - padded
    n_steps = n_rows_static // MOE_ROWS
    first = start // MOE_ROWS
    step = jnp.arange(n_steps, dtype=jnp.int32)
    expert = jnp.sum((first[None, :] <= step[:, None]).astype(jnp.int32), axis=1) - 1
    active = step < (first + (length + MOE_ROWS - 1) // MOE_ROWS)[expert]
    return (off.reshape(-1).astype(jnp.int32), cnt.reshape(-1).astype(jnp.int32),
            expert.astype(jnp.int32), active.astype(jnp.int32))


def _dispatch_kernel(off_ref, cnt_ref, h_ref, cw_ref, u_ref, xs_in, xs_ref, stage, sems, *, n_exp):
    del xs_in
    i = pl.program_id(0)
    n_put = MOE_TILE // MOE_PUT
    routed = jnp.where(cw_ref[...].T[0:8] > 0.0, 1.0, 0.0)
    pos = jnp.dot(routed.astype(BF16), u_ref[...], preferred_element_type=F32)
    row = lax.broadcasted_iota(jnp.int32, (MOE_PUT, MOE_TILE), 0).astype(F32)
    h = h_ref[...]

    def copy(e, s):
        slot = e * n_put + s
        dst = pl.multiple_of(off_ref[i * n_exp + e] + s * MOE_PUT, MOE_ALIGN)
        return pltpu.make_async_copy(stage.at[slot], xs_ref.at[pl.ds(dst, MOE_PUT), :], sems.at[slot])

    for e in range(n_exp):
        for s in range(n_put):
            @pl.when(s * MOE_PUT < cnt_ref[i * n_exp + e])
            def _():
                take = (routed[e:e + 1, :] > 0.5) & (pos[e:e + 1, :] == row + float(s * MOE_PUT))
                onehot = jnp.where(take, 1.0, 0.0).astype(BF16)
                stage[e * n_put + s] = jnp.dot(onehot, h, preferred_element_type=F32).astype(BF16)
                copy(e, s).start()

    for e in range(n_exp):
        for s in range(n_put):
            @pl.when(s * MOE_PUT < cnt_ref[i * n_exp + e])
            def _():
                copy(e, s).wait()


def _dispatch(h2, cw, off, cnt, n_exp, n_rows):
    t, d = h2.shape
    u = jnp.asarray(np.arange(MOE_TILE)[:, None] < np.arange(MOE_TILE)[None, :], BF16)
    n_slots = n_exp * (MOE_TILE // MOE_PUT)
    return pl.pallas_call(
        functools.partial(_dispatch_kernel, n_exp=n_exp),
        grid_spec=pltpu.PrefetchScalarGridSpec(
            num_scalar_prefetch=2,
            grid=(t // MOE_TILE,),
            in_specs=[pl.BlockSpec((MOE_TILE, d), lambda i, o, c: (i, 0)),
                      pl.BlockSpec((MOE_TILE, LANES), lambda i, o, c: (i, 0)),
                      pl.BlockSpec((MOE_TILE, MOE_TILE), lambda i, o, c: (0, 0)),
                      pl.BlockSpec(memory_space=pl.ANY)],
            out_specs=pl.BlockSpec(memory_space=pl.ANY),
            scratch_shapes=[pltpu.VMEM((n_slots, MOE_PUT, d), BF16),
                            pltpu.SemaphoreType.DMA((n_slots,))]),
        out_shape=jax.ShapeDtypeStruct((n_rows, d), BF16),
        input_output_aliases={5: 0},
        compiler_params=_params("arbitrary"),
        name="moe_dispatch",
    )(off, cnt, h2, cw, u, jnp.zeros((n_rows, d), BF16))


FF_CHUNK = 768


def _swiglu(x, wg_ref, wu_ref, wd_ref):
    ff = wg_ref.shape[2]
    y = None
    for c0 in range(0, ff, FF_CHUNK):
        c1 = min(c0 + FF_CHUNK, ff)
        gate = jnp.dot(x, wg_ref[0, :, c0:c1].astype(BF16), preferred_element_type=F32)
        up = jnp.dot(x, wu_ref[0, :, c0:c1].astype(BF16), preferred_element_type=F32)
        a = ((gate * _sigmoid(gate)) * up).astype(BF16)
        part = jnp.dot(a, wd_ref[0, c0:c1, :].astype(BF16), preferred_element_type=F32)
        y = part if y is None else y + part
    return y


def _experts_kernel(exp_ref, act_ref, xs_ref, wg_ref, wu_ref, wd_ref, ys_ref):
    g = pl.program_id(0)

    @pl.when(act_ref[g] > 0)
    def _():
        ys_ref[...] = _swiglu(xs_ref[...], wg_ref, wu_ref, wd_ref).astype(BF16)

    @pl.when(act_ref[g] == 0)
    def _():
        ys_ref[...] = jnp.zeros(ys_ref.shape, BF16)


def _experts(xs, expert, active, w_g, w_u, w_d):
    n_rows, d = xs.shape
    _, _, ff = w_g.shape
    wmap = lambda g, ex, ac: (ex[g], 0, 0)
    single = lambda shape: pl.BlockSpec(shape, wmap, pipeline_mode=pl.Buffered(1))
    return pl.pallas_call(
        _experts_kernel,
        grid_spec=pltpu.PrefetchScalarGridSpec(
            num_scalar_prefetch=2,
            grid=(n_rows // MOE_ROWS,),
            in_specs=[pl.BlockSpec((MOE_ROWS, d), lambda g, ex, ac: (g, 0)),
                      single((1, d, ff)), single((1, d, ff)), pl.BlockSpec((1, ff, d), wmap)],
            out_specs=pl.BlockSpec((MOE_ROWS, d), lambda g, ex, ac: (g, 0))),
        out_shape=jax.ShapeDtypeStruct((n_rows, d), BF16),
        compiler_params=_params("arbitrary"),
        name="moe_experts",
    )(expert, active, xs, w_g, w_u, w_d)


def _combine_kernel(off_ref, cnt_ref, x_ref, mod_ref, cw_ref, l_ref, ys_ref, o_ref, buf, sems, acc_scr,
                    *, n_exp):
    i = pl.program_id(0)
    n_tiles = pl.num_programs(0)
    n_get = MOE_TILE // MOE_GET

    def copy(tile, e, s):
        slot = e * n_get + s
        src = pl.multiple_of(off_ref[tile * n_exp + e] + s * MOE_GET, MOE_ALIGN)
        return pltpu.make_async_copy(ys_ref.at[pl.ds(src, MOE_GET), :], buf.at[tile % 2, slot],
                                     sems.at[tile % 2, slot])

    def start_all(tile, extra):
        for e in range(n_exp):
            for s in range(n_get):
                @pl.when(extra & (s * MOE_GET < cnt_ref[tile * n_exp + e]))
                def _():
                    copy(tile, e, s).start()

    start_all(i, i == 0)
    nxt = jnp.minimum(i + 1, n_tiles - 1)
    start_all(nxt, i + 1 < n_tiles)

    cw = cw_ref[...]
    routed = jnp.where(cw > 0.0, 1.0, 0.0)
    pos = jnp.dot(l_ref[...], routed.astype(BF16), preferred_element_type=F32)
    col = lax.broadcasted_iota(jnp.int32, (MOE_TILE, MOE_GET), 1).astype(F32)
    acc_scr[...] = jnp.zeros(acc_scr.shape, F32)
    for e in range(n_exp):
        for s in range(n_get):
            @pl.when(s * MOE_GET < cnt_ref[i * n_exp + e])
            def _():
                copy(i, e, s).wait()
                take = (cw[:, e:e + 1] > 0.0) & (pos[:, e:e + 1] == col + float(s * MOE_GET))
                onehot = jnp.where(take, 1.0, 0.0).astype(BF16)
                acc_scr[...] += cw[:, e:e + 1] * jnp.dot(onehot, buf[i % 2, e * n_get + s],
                                                         preferred_element_type=F32)
    o_ref[...] = x_ref[...] + mod_ref[0, 5:6, :] * acc_scr[...]


def _combine(x1, mod_l, cw, ys, off, cnt, n_exp, seq):
    t, d = x1.shape
    per_b = seq // MOE_TILE
    low = jnp.asarray(np.arange(MOE_TILE)[:, None] > np.arange(MOE_TILE)[None, :], BF16)
    n_slots = n_exp * (MOE_TILE // MOE_GET)
    return pl.pallas_call(
        functools.partial(_combine_kernel, n_exp=n_exp),
        grid_spec=pltpu.PrefetchScalarGridSpec(
            num_scalar_prefetch=2,
            grid=(t // MOE_TILE,),
            in_specs=[pl.BlockSpec((MOE_TILE, d), lambda i, o, c: (i, 0)),
                      pl.BlockSpec((1, 6, d), lambda i, o, c: (i // per_b, 0, 0)),
                      pl.BlockSpec((MOE_TILE, LANES), lambda i, o, c: (i, 0)),
                      pl.BlockSpec((MOE_TILE, MOE_TILE), lambda i, o, c: (0, 0)),
                      pl.BlockSpec(memory_space=pl.ANY)],
            out_specs=pl.BlockSpec((MOE_TILE, d), lambda i, o, c: (i, 0)),
            scratch_shapes=[pltpu.VMEM((2, n_slots, MOE_GET, d), BF16),
                            pltpu.SemaphoreType.DMA((2, n_slots)),
                            pltpu.VMEM((MOE_TILE, d), F32)]),
        out_shape=jax.ShapeDtypeStruct((t, d), F32),
        compiler_params=_params("arbitrary"),
        name="moe_combine",
    )(off, cnt, x1, mod_l, cw, low, ys)


def _moe(x1, h2, mod_l, w_r, b_r, w_g, w_u, w_d, seq):
    t, d = x1.shape
    n_exp = w_g.shape[0]
    n_tiles = t // MOE_TILE
    bound = (TOP_K * t + n_tiles * n_exp * (MOE_ALIGN - 1)
             + n_exp * (MOE_PUT - MOE_ALIGN + MOE_ROWS - 1))
    n_rows = (bound + MOE_ROWS - 1) // MOE_ROWS * MOE_ROWS + MOE_ROWS
    cw, cnt = _router(h2, w_r, b_r, n_exp, MOE_TILE)
    off, cnt, expert, active = _moe_layout(cnt, n_rows)
    xs = _dispatch(h2, cw, off, cnt, n_exp, n_rows)
    ys = _experts(xs, expert, active, w_g, w_u, w_d)
    return _combine(x1, mod_l, cw, ys, off, cnt, n_exp, seq)


def _ffn_kernel(x_ref, h_ref, mod_ref, wg_ref, wu_ref, wd_ref, o_ref):
    o_ref[...] = x_ref[...] + mod_ref[0, 5:6, :] * _swiglu(h_ref[...], wg_ref, wu_ref, wd_ref)


def _ffn(x1, h2, mod_l, w_g, w_u, w_d, seq, tm):
    t, d = x1.shape
    per_b = seq // tm
    row = lambda i: (i, 0)
    whole = lambda w: pl.BlockSpec(w.shape, lambda i: (0, 0, 0), pipeline_mode=pl.Buffered(1))
    return pl.pallas_call(
        _ffn_kernel,
        grid=(t // tm,),
        in_specs=[pl.BlockSpec((tm, d), row), pl.BlockSpec((tm, d), row),
                  pl.BlockSpec((1, 6, d), lambda i: (i // per_b, 0, 0)),
                  whole(w_g), whole(w_u), whole(w_d)],
        out_specs=pl.BlockSpec((tm, d), row),
        out_shape=jax.ShapeDtypeStruct((t, d), F32),
        compiler_params=_params("arbitrary"),
        name="swiglu",
    )(x1, h2, mod_l, w_g, w_u, w_d)


def _pad_cols(a, width):
    return jnp.pad(a, ((0, 0), (0, width - a.shape[1])))


def kernel(x, c, w_ada, b_ada, norm_mix, norm_ffn, w_in, b_fgate, q_gain, k_gain, w_br_fox, w_br_moba,
           w_br_dil, w_out, w_ffn_gate, w_ffn_up, w_ffn_down, w_router, b_router, w_exp_gate,
           w_exp_up, w_exp_down):
    b, seq, d = x.shape
    depth = w_ada.shape[0]
    t = b * seq
    tm, tq = TOKEN_TILE, ATTN_CHUNK
    n_pairs_fox = N_HEADS_FOX // 2
    n_pairs_moba = N_HEADS_MOBA // 2

    mod = _adaln(c, w_ada, b_ada).reshape(depth, b, 6, d)
    x2 = x.reshape(t, d)
    w_in_t = jnp.transpose(w_in, (2, 0, 1))
    for l in range(depth):
        wl = w_in_t[:, l, :].astype(BF16)
        w_gate = wl[3 * MIX_WIDTH + N_HEADS_FOX:]
        b_f = _pad_cols(b_fgate[l].reshape(1, -1), LANES)
        outs = _inproj(x2, mod[l], norm_mix[l].reshape(1, d), wl, w_gate, q_gain[l].reshape(1, -1),
                       k_gain[l].reshape(1, -1), b_f, seq, tm)
        q_m, k_m, v_m = (o.reshape(b, seq, -1) for o in outs[0:3])
        gates, lf = outs[12], outs[13]

        cum = _decay(lf.reshape(b, seq, LANES))
        o_fox = _fox(q_m, k_m, v_m, cum, 0, n_pairs_fox, tq).reshape(t, -1)
        o_moba = _moba(q_m, k_m, v_m, n_pairs_fox, n_pairs_moba, tq).reshape(t, -1)
        dil_outs = [_dilated(*outs[3 + 3 * g:6 + 3 * g], g) for g in range(len(DILATED_PAIRS))]

        x1, h2 = _outproj(x2, mod[l], norm_ffn[l].reshape(1, d), o_fox, o_moba, dil_outs, gates,
                          w_br_fox[l].astype(BF16), w_br_moba[l].astype(BF16),
                          w_br_dil[l].astype(BF16), w_out[l].astype(BF16), seq, tm)
        i = l // 2
        if l % 2 == 0:
            x2 = _ffn(x1, h2, mod[l], w_ffn_gate[i:i + 1].astype(BF16),
                      w_ffn_up[i:i + 1].astype(BF16), w_ffn_down[i:i + 1].astype(BF16), seq, FFN_TILE)
        else:
            x2 = _moe(x1, h2, mod[l], _pad_cols(w_router[i], LANES).astype(BF16),
                      _pad_cols(b_router[i].reshape(1, -1), LANES), w_exp_gate[i], w_exp_up[i],
                      w_exp_down[i], seq)
    return x2.reshape(b, seq, d)
```
